```python
import jax, jax.numpy as jnp
from jax import lax
import numpy as np

D_MODEL = 1024
BATCH = 32
SEQ = 2048
DEPTH = 2

CHUNK = 64

A_HEADS = 8
A_HEAD_DIM = 64
A_WIDTH = A_HEADS * A_HEAD_DIM
A_LEFT_CHUNKS = 8
A_BAND = (A_LEFT_CHUNKS + 1) * CHUNK
A_REL_CLIP = 256
A_N_REL = (CHUNK - 1) + A_REL_CLIP + 1

B_HEADS = 4
B_HEAD_DIM = 128
B_WIDTH = B_HEADS * B_HEAD_DIM
B_CONV = 4

C_WINDOWS = (2, 4, 8, 16)
C_GROUPS = 4
C_GROUP_DIM = 128
C_WIDTH = C_GROUPS * C_GROUP_DIM

N_BRANCH = 3
IN_SIZES = (A_WIDTH, A_WIDTH, A_WIDTH, 3 * B_WIDTH, B_HEADS, B_HEADS, B_WIDTH, C_WIDTH, N_BRANCH * D_MODEL)
D_IN = 3 * A_WIDTH + 4 * B_WIDTH + 2 * B_HEADS + C_WIDTH + N_BRANCH * D_MODEL

MOE_GROUPS = 4
MOE_PER_GROUP = 8
N_EXPERTS = MOE_GROUPS * MOE_PER_GROUP
MOE_TOPK = 2
MOE_FF = 512
MOE_BLOCK = 256

DN_ALPHA = (2 * DEPTH) ** 0.25
DN_BETA = (8 * DEPTH) ** -0.25
LN_EPS = 1e-5
RMS_EPS = 1e-6

kernel_name = 'hybrid_chunk_attn_gdn_pool_hmoe'


def layer_norm(x, g, b):
    xf = x.astype(jnp.float32)
    mu = jnp.mean(xf, axis=-1, keepdims=True)
    var = jnp.mean(jnp.square(xf - mu), axis=-1, keepdims=True)
    return ((xf - mu) * lax.rsqrt(var + LN_EPS) * g + b).astype(x.dtype)


def split_cols(t, sizes):
    outs, start = [], 0
    for s in sizes:
        outs.append(t[..., start:start + s])
        start += s
    return outs


def chunk_band_attention(q, k, v, rel_bias):
    b, l, h, dh = q.shape
    nc = l // CHUNK
    pad = A_LEFT_CHUNKS * CHUNK
    kp = jnp.pad(k, ((0, 0), (pad, 0), (0, 0), (0, 0)))
    vp = jnp.pad(v, ((0, 0), (pad, 0), (0, 0), (0, 0)))
    dist = jnp.arange(CHUNK)[:, None] + pad - jnp.arange(A_BAND)[None, :]
    idx = jnp.clip(dist, -(CHUNK - 1), A_REL_CLIP) + (CHUNK - 1)
    bias = rel_bias[:, idx].astype(jnp.float32)
    qc = jnp.moveaxis(q.reshape(b, nc, CHUNK, h, dh), 1, 0)
    scale = dh ** -0.5

    def one_chunk(args):
        qi, ci = args
        start = ci * CHUNK
        kb = lax.dynamic_slice_in_dim(kp, start, A_BAND, axis=1)
        vb = lax.dynamic_slice_in_dim(vp, start, A_BAND, axis=1)
        s = jnp.einsum('bqhd,bkhd->bhqk', qi, kb, preferred_element_type=jnp.float32) * scale + bias
        valid = (start - pad + jnp.arange(A_BAND)) >= 0
        s = jnp.where(valid, s, -jnp.inf)
        p = jax.nn.softmax(s, axis=-1).astype(vb.dtype)
        return jnp.einsum('bhqk,bkhd->bqhd', p, vb)

    o = lax.map(one_chunk, (qc, jnp.arange(nc)))
    return jnp.moveaxis(o, 0, 1).reshape(b, l, h * dh)


def causal_depthwise_conv(x, w):
    kw, c = w.shape
    return lax.conv_general_dilated(
        x, w[:, None, :].astype(x.dtype), window_strides=(1,), padding=[(kw - 1, 0)],
        dimension_numbers=('NWC', 'WIO', 'NWC'), feature_group_count=c)


def l2_normalize(t):
    return t * lax.rsqrt(jnp.sum(t * t, axis=-1, keepdims=True) + RMS_EPS)


def gated_delta_rule(q, k, v, beta, g):
    b, l, h, dk = q.shape
    dv = v.shape[-1]
    nc = l // CHUNK

    def to_chunks(t):
        t = t.reshape((b, nc, CHUNK, h) + t.shape[3:])
        return jnp.moveaxis(t, 3, 1)

    q = to_chunks(q) * (dk ** -0.5)
    k = to_chunks(k)
    v = to_chunks(v)
    beta = to_chunks(beta)
    gc = jnp.cumsum(to_chunks(g), axis=-1)
    tril = jnp.tril(jnp.ones((CHUNK, CHUNK), dtype=bool))
    strict = jnp.tril(jnp.ones((CHUNK, CHUNK), dtype=bool), -1)
    decay = jnp.exp(jnp.where(tril, gc[..., :, None] - gc[..., None, :], -jnp.inf))
    kbeta = k * beta[..., None]
    lower = jnp.where(strict, jnp.einsum('bhnid,bhnjd->bhnij', kbeta, k) * decay, 0.0)
    eye = jnp.eye(CHUNK, dtype=q.dtype)
    tmat = lax.linalg.triangular_solve(eye + lower, jnp.broadcast_to(eye, lower.shape),
                                       left_side=True, lower=True)
    u = tmat @ (v * beta[..., None])
    w = tmat @ (kbeta * jnp.exp(gc)[..., None])
    qk = jnp.einsum('bhnid,bhnjd->bhnij', q, k) * decay
    qg = q * jnp.exp(gc)[..., None]
    kd = k * jnp.exp(gc[..., -1:] - gc)[..., None]
    glast = jnp.exp(gc[..., -1])

    def step(state, inp):
        u_i, w_i, qk_i, qg_i, kd_i, gl_i = inp
        v_new = u_i - w_i @ state
        o_i = qg_i @ state + qk_i @ v_new
        state = state * gl_i[..., None, None] + jnp.einsum('bhck,bhcv->bhkv', kd_i, v_new)
        return state, o_i

    xs = tuple(jnp.moveaxis(t, 2, 0) for t in (u, w, qk, qg, kd, glast))
    s0 = jnp.zeros((b, h, dk, dv), q.dtype)
    _, o = lax.scan(step, s0, xs)
    return o.transpose(1, 0, 3, 2, 4).reshape(b, l, h, dv)


def multiscale_pool(u, pool_w, pool_scale):
    b, l, _ = u.shape
    uf = u.astype(jnp.float32)
    cs0 = jnp.pad(jnp.cumsum(uf, axis=1), ((0, 0), (1, 0), (0, 0)))
    pos = jnp.arange(1, l + 1, dtype=jnp.float32)
    groups = []
    for gi, win in enumerate(C_WINDOWS):
        lo_c, hi_c = gi * C_GROUP_DIM, (gi + 1) * C_GROUP_DIM
        hi = cs0[:, 1:, lo_c:hi_c]
        lo = jnp.pad(cs0[:, :, lo_c:hi_c], ((0, 0), (win - 1, 0), (0, 0)))[:, :l]
        cnt = jnp.minimum(pos, float(win))[None, :, None]
        groups.append((hi - lo) / cnt - uf[:, :, lo_c:hi_c])
    pooled = jnp.stack(groups, axis=2)
    mixed = jnp.einsum('blgc,gcd->blgd', pooled, pool_w.astype(jnp.float32))
    return (mixed.reshape(b, l, C_WIDTH) * pool_scale).astype(u.dtype)


def hybrid_mixer(x, w_in, rel_bias, conv_w, a_log, dt_bias, gdn_norm_g, pool_w, pool_scale,
                 w_br_a, w_br_b, w_br_c, gate_b, w_out):
    b, l, d = x.shape
    proj = x @ w_in
    qa, ka, va, qkv_b, beta_l, a_l, gate_bo, u_c, gate_l = split_cols(proj, IN_SIZES)
    shp_a = (b, l, A_HEADS, A_HEAD_DIM)
    o_a = chunk_band_attention(qa.reshape(shp_a), ka.reshape(shp_a), va.reshape(shp_a), rel_bias)
    qkv_b = jax.nn.silu(causal_depthwise_conv(qkv_b, conv_w)).astype(jnp.float32)
    shp_b = (b, l, B_HEADS, B_HEAD_DIM)
    qb, kb, vb = [t.reshape(shp_b) for t in split_cols(qkv_b, (B_WIDTH, B_WIDTH, B_WIDTH))]
    beta = jax.nn.sigmoid(beta_l.astype(jnp.float32))
    g = -jnp.exp(a_log.astype(jnp.float32)) * jax.nn.softplus(a_l.astype(jnp.float32) + dt_bias)
    o_b = gated_delta_rule(l2_normalize(qb), l2_normalize(kb), vb, beta, g)
    o_b = o_b * lax.rsqrt(jnp.mean(o_b * o_b, axis=-1, keepdims=True) + RMS_EPS) * gdn_norm_g
    o_b = o_b * jax.nn.silu(gate_bo.reshape(shp_b).astype(jnp.float32))
    o_b = o_b.reshape(b, l, B_WIDTH).astype(x.dtype)
    o_c = multiscale_pool(u_c, pool_w, pool_scale)
    gates = jax.nn.sigmoid(gate_l.reshape(b, l, N_BRANCH, d).astype(jnp.float32) + gate_b).astype(x.dtype)
    y = (gates[:, :, 0] * (o_a @ w_br_a) + gates[:, :, 1] * (o_b @ w_br_b)
         + gates[:, :, 2] * (o_c @ w_br_c))
    return y @ w_out


def hierarchical_moe(x, wg, bg, we, be, w1, w3, w2):
    b, l, d = x.shape
    xt = x.reshape(-1, d)
    t = xt.shape[0]
    lg = (xt @ wg).astype(jnp.float32) + bg
    pg = jax.nn.softmax(lg, axis=-1)
    gsel = jnp.argmax(lg, axis=-1)
    p_top = jnp.take_along_axis(pg, gsel[:, None], axis=-1)
    le = ((xt @ we).astype(jnp.float32) + be).reshape(t, MOE_GROUPS, MOE_PER_GROUP)
    le = jnp.take_along_axis(le, gsel[:, None, None], axis=1)[:, 0]
    top_l, top_i = lax.top_k(le, MOE_TOPK)
    wts = jax.nn.softmax(top_l, axis=-1) * p_top
    eid = gsel[:, None] * MOE_PER_GROUP + top_i
    m = t * MOE_TOPK
    e_flat = eid.reshape(-1).astype(jnp.int32)
    w_flat = wts.reshape(-1)
    tok_flat = jnp.repeat(jnp.arange(t, dtype=jnp.int32), MOE_TOPK)
    order = jnp.argsort(e_flat)
    e_sorted = e_flat[order]
    counts = jnp.bincount(e_flat, length=N_EXPERTS)
    padded = (counts + MOE_BLOCK - 1) // MOE_BLOCK * MOE_BLOCK
    pad_end = jnp.cumsum(padded)
    pad_start = pad_end - padded
    start = jnp.cumsum(counts) - counts
    dest = pad_start[e_sorted] + jnp.arange(m) - start[e_sorted]
    n_blocks = -(-m // MOE_BLOCK) + N_EXPERTS
    cap = n_blocks * MOE_BLOCK
    row_tok = jnp.zeros((cap,), jnp.int32).at[dest].set(tok_flat[order])
    row_w = jnp.zeros((cap,), jnp.float32).at[dest].set(w_flat[order])
    block_e = jnp.minimum(jnp.searchsorted(pad_end, jnp.arange(n_blocks) * MOE_BLOCK, side='right'),
                          N_EXPERTS - 1)
    xs = xt[row_tok].reshape(n_blocks, MOE_BLOCK, d)

    def expert_block(args):
        xb, e = args
        hdn = jax.nn.silu(xb @ w1[e]) * (xb @ w3[e])
        return hdn @ w2[e]

    yb = lax.map(expert_block, (xs, block_e)).reshape(cap, d)
    y = jnp.zeros((t, d), jnp.float32).at[row_tok].add(yb.astype(jnp.float32) * row_w[:, None])
    return y.astype(x.dtype).reshape(b, l, d)


def setup_inputs(seed: int = 0) -> dict:
    key = jax.random.key(seed)
    ks = iter(jax.random.split(key, 40))

    def nrm(shape, scale):
        return jax.random.normal(next(ks), shape, jnp.float32) * scale

    def unif(shape, lo, hi):
        return jax.random.uniform(next(ks), shape, jnp.float32, lo, hi)

    L = DEPTH
    d = D_MODEL
    dt = jnp.exp(unif((L, B_HEADS), float(np.log(1e-3)), float(np.log(1e-1))))
    return {
        'x': nrm((BATCH, SEQ, d), 1.0),
        'w_in': nrm((L, d, D_IN), d ** -0.5),
        'attn_rel_bias': nrm((L, A_HEADS, A_N_REL), 0.2),
        'gdn_conv_w': nrm((L, B_CONV, 3 * B_WIDTH), B_CONV ** -0.5),
        'gdn_a_log': jnp.log(unif((L, B_HEADS), 1.0, 16.0)),
        'gdn_dt_bias': dt + jnp.log(-jnp.expm1(-dt)),
        'gdn_norm_g': 1.0 + nrm((L, B_HEAD_DIM), 0.02),
        'pool_w': nrm((L, C_GROUPS, C_GROUP_DIM, C_GROUP_DIM), C_GROUP_DIM ** -0.5),
        'pool_scale': 1.0 + nrm((L, C_WIDTH), 0.02),
        'w_branch_a': nrm((L, A_WIDTH, d), A_WIDTH ** -0.5),
        'w_branch_b': nrm((L, B_WIDTH, d), B_WIDTH ** -0.5),
        'w_branch_c': nrm((L, C_WIDTH, d), C_WIDTH ** -0.5),
        'gate_b': nrm((L, N_BRANCH, d), 0.02),
        'w_out': nrm((L, d, d), d ** -0.5 * DN_BETA),
        'ln1_g': 1.0 + nrm((L, d), 0.02),
        'ln1_b': nrm((L, d), 0.02),
        'router_group_w': nrm((L, d, MOE_GROUPS), d ** -0.5),
        'router_group_b': nrm((L, MOE_GROUPS), 0.01),
        'router_expert_w': nrm((L, d, N_EXPERTS), d ** -0.5),
        'router_expert_b': nrm((L, N_EXPERTS), 0.01),
        'moe_w1': nrm((L, N_EXPERTS, d, MOE_FF), d ** -0.5),
        'moe_w3': nrm((L, N_EXPERTS, d, MOE_FF), d ** -0.5),
        'moe_w2': nrm((L, N_EXPERTS, MOE_FF, d), MOE_FF ** -0.5 * DN_BETA),
        'ln2_g': 1.0 + nrm((L, d), 0.02),
        'ln2_b': nrm((L, d), 0.02),
    }


def reference(x, w_in, attn_rel_bias, gdn_conv_w, gdn_a_log, gdn_dt_bias, gdn_norm_g, pool_w,
              pool_scale, w_branch_a, w_branch_b, w_branch_c, gate_b, w_out, ln1_g, ln1_b,
              router_group_w, router_group_b, router_expert_w, router_expert_b,
              moe_w1, moe_w3, moe_w2, ln2_g, ln2_b):
    for i in range(DEPTH):
        mix = hybrid_mixer(x, w_in[i], attn_rel_bias[i], gdn_conv_w[i], gdn_a_log[i], gdn_dt_bias[i],
                           gdn_norm_g[i], pool_w[i], pool_scale[i], w_branch_a[i], w_branch_b[i],
                           w_branch_c[i], gate_b[i], w_out[i])
        x = layer_norm(DN_ALPHA * x + mix, ln1_g[i], ln1_b[i])
        ffn = hierarchical_moe(x, router_group_w[i], router_group_b[i], router_expert_w[i],
                               router_expert_b[i], moe_w1[i], moe_w3[i], moe_w2[i])
        x = layer_norm(DN_ALPHA * x + ffn, ln2_g[i], ln2_b[i])
    return x
```

```python
import functools

import jax
import jax.numpy as jnp
from jax import lax
from jax.experimental import pallas as pl
from jax.experimental.pallas import tpu as pltpu

BF = jnp.bfloat16
F32 = jnp.float32
I32 = jnp.int32

D_MODEL = 1024
DEPTH = 2
CHUNK = 64

A_HEADS = 8
A_HEAD_DIM = 64
A_WIDTH = A_HEADS * A_HEAD_DIM
A_LEFT_CHUNKS = 8
A_REL_CLIP = 256

B_HEADS = 4
B_HEAD_DIM = 128
B_WIDTH = B_HEADS * B_HEAD_DIM
B_CONV = 4

C_WINDOWS = (2, 4, 8, 16)
C_GROUPS = 4
C_GROUP_DIM = 128
C_WIDTH = C_GROUPS * C_GROUP_DIM

N_BRANCH = 3
MOE_GROUPS = 4
MOE_PER_GROUP = 8
N_EXPERTS = MOE_GROUPS * MOE_PER_GROUP
MOE_TOPK = 2
MOE_FF = 512

DN_ALPHA = (2 * DEPTH) ** 0.25
LN_EPS = 1e-5
RMS_EPS = 1e-6

LANES = 128
SUBLANES = 8
VMEM_LIMIT_BYTES = 56 * 1024 * 1024

QBLOCK = 2 * CHUNK
KBAND = (A_LEFT_CHUNKS + 2) * CHUNK
NEG_BIG = -1e30

ROUTER_ROWS = 40
EXPERT_BLOCK = 512


def _cparams(*sem):
    return pltpu.CompilerParams(dimension_semantics=sem, vmem_limit_bytes=VMEM_LIMIT_BYTES)


def _mm(a, b):
    return jnp.dot(a.astype(BF), b.astype(BF), preferred_element_type=F32)


def _mm_nt(a, b):
    return lax.dot_general(a.astype(BF), b.astype(BF), (((1,), (1,)), ((), ())),
                           preferred_element_type=F32)


def _mm_tn(a, b):
    return lax.dot_general(a.astype(BF), b.astype(BF), (((0,), (0,)), ((), ())),
                           preferred_element_type=F32)


def _split_bf16(a):
    hi = a.astype(BF)
    lo = (a - hi.astype(F32)).astype(BF)
    return hi, lo


def _mm3(a, b):
    ah, al = _split_bf16(a)
    bh, bl = _split_bf16(b)
    return (jnp.dot(ah, bh, preferred_element_type=F32)
            + (jnp.dot(ah, bl, preferred_element_type=F32)
               + jnp.dot(al, bh, preferred_element_type=F32)))


def _sigmoid(x):
    return 1.0 / (1.0 + jnp.exp(-x))


def _silu(x):
    return x * _sigmoid(x)


def _layer_norm(z, g, b):
    mu = jnp.mean(z, axis=-1, keepdims=True)
    zc = z - mu
    var = jnp.mean(zc * zc, axis=-1, keepdims=True)
    return zc * lax.rsqrt(var + LN_EPS) * g + b


POOL_TAIL = 16


def _proj_kernel(x_ref, wa_ref, wb_ref, wbd_ref, wgo_ref, wc_ref, pw_ref, ps_ref,
                 qkva_ref, qkvb_ref, bd_ref, go_ref, oc_ref, tail_ref, *, tiles_per_seq, tm):
    it = pl.program_id(0) % tiles_per_seq
    xb = x_ref[...].astype(BF)
    qkva_ref[...] = jnp.dot(xb, wa_ref[...], preferred_element_type=F32).astype(BF)
    qkvb_ref[...] = jnp.dot(xb, wb_ref[...], preferred_element_type=F32).astype(BF)
    bd_ref[...] = jnp.dot(xb, wbd_ref[...], preferred_element_type=F32)
    go_ref[...] = jnp.dot(xb, wgo_ref[...], preferred_element_type=F32).astype(BF)
    u = jnp.dot(xb, wc_ref[...], preferred_element_type=F32)

    @pl.when(it == 0)
    def _():
        tail_ref[...] = jnp.zeros_like(tail_ref)

    ext = jnp.concatenate([tail_ref[...], u], axis=0)
    tail_ref[...] = u[tm - POOL_TAIL:, :]
    pos = (it * tm + 1 + lax.broadcasted_iota(I32, (tm, 1), 0)).astype(F32)
    s = ext
    for gi, win in enumerate(C_WINDOWS):
        if gi:
            s = s[:, C_GROUP_DIM:]
        s = s + pltpu.roll(s, win // 2, 0)
        lo_c, hi_c = gi * C_GROUP_DIM, (gi + 1) * C_GROUP_DIM
        cnt = jnp.minimum(pos, float(win))
        pooled = s[POOL_TAIL:, :C_GROUP_DIM] / cnt - u[:, lo_c:hi_c]
        mixed = jnp.dot(pooled.astype(BF), pw_ref[gi], preferred_element_type=F32)
        oc_ref[:, lo_c:hi_c] = (mixed * ps_ref[:, lo_c:hi_c]).astype(BF)


def _proj(x2d, wa, wb, wbd, wgo, wc, pw, ps, *, seq, tm):
    t = x2d.shape[0]
    full = lambda a: pl.BlockSpec(a.shape, lambda i: (0,) * a.ndim)
    row = lambda n: pl.BlockSpec((tm, n), lambda i: (i, 0))
    return pl.pallas_call(
        functools.partial(_proj_kernel, tiles_per_seq=seq // tm, tm=tm),
        grid=(t // tm,),
        in_specs=[row(D_MODEL), full(wa), full(wb), full(wbd), full(wgo), full(wc), full(pw), full(ps)],
        out_specs=[row(3 * A_WIDTH), row(3 * B_WIDTH), row(LANES), row(B_WIDTH), row(C_WIDTH)],
        out_shape=[jax.ShapeDtypeStruct((t, 3 * A_WIDTH), BF),
                   jax.ShapeDtypeStruct((t, 3 * B_WIDTH), BF),
                   jax.ShapeDtypeStruct((t, LANES), F32),
                   jax.ShapeDtypeStruct((t, B_WIDTH), BF),
                   jax.ShapeDtypeStruct((t, C_WIDTH), BF)],
        scratch_shapes=[pltpu.VMEM((POOL_TAIL, C_WIDTH), F32)],
        compiler_params=_cparams("arbitrary"),
        name="proj",
    )(x2d, wa, wb, wbd, wgo, wc, pw, ps)


def _attn_bias_table(rel_bias):
    pad = A_LEFT_CHUNKS * CHUNK
    width = KBAND + pad
    r = jnp.arange(QBLOCK)[:, None]
    m = jnp.arange(width)[None, :]
    idx = jnp.clip(r + pad - m, -(CHUNK - 1), A_REL_CLIP) + (CHUNK - 1)
    kc, qc = m // CHUNK, r // CHUNK
    valid = (kc >= qc) & (kc <= qc + A_LEFT_CHUNKS)
    tab = jnp.where(valid[None], rel_bias[:, idx].astype(F32), NEG_BIG)
    h = rel_bias.shape[0]
    return tab.reshape(h, QBLOCK, width // LANES, LANES).transpose(0, 2, 1, 3)


def _attn_kernel(q_ref, k_ref, v_ref, tab_ref, o_ref):
    j = pl.program_id(1)
    first_blk = A_LEFT_CHUNKS * CHUNK // QBLOCK
    kstart = pl.multiple_of(jnp.maximum(j - first_blk, 0) * QBLOCK, QBLOCK)
    offb = jnp.maximum(first_blk - j, 0)
    lane = lax.broadcasted_iota(I32, (QBLOCK, LANES), 1)
    lo = lane < A_HEAD_DIM
    nkb = KBAND // LANES
    for hp in range(A_HEADS // 2):
        cs = slice(hp * LANES, (hp + 1) * LANES)
        qp = q_ref[0, :, cs]
        kp = k_ref[0, pl.ds(kstart, KBAND), cs]
        vp = v_ref[0, pl.ds(kstart, KBAND), cs]
        outs = []
        for par in range(2):
            h = 2 * hp + par
            qm = jnp.where(lo if par == 0 else jnp.logical_not(lo), qp, jnp.zeros_like(qp))
            s = lax.dot_general(qm, kp, (((1,), (1,)), ((), ())), preferred_element_type=F32)
            bias = jnp.concatenate([tab_ref[h, offb + i] for i in range(nkb)], axis=1)
            s = s * (A_HEAD_DIM ** -0.5) + bias
            mx = jnp.max(s, axis=-1, keepdims=True)
            p = jnp.exp(s - mx)
            l = jnp.sum(p, axis=-1, keepdims=True)
            o = jnp.dot(p.astype(BF), vp, preferred_element_type=F32)
            outs.append(o / l)
        o_ref[0, :, cs] = jnp.where(lo, outs[0], outs[1]).astype(BF)


def _attn(qkva, tab):
    b, l, _ = qkva.shape
    return pl.pallas_call(
        _attn_kernel,
        grid=(b, l // QBLOCK),
        in_specs=[pl.BlockSpec((1, QBLOCK, A_WIDTH), lambda bi, j: (bi, j, 0)),
                  pl.BlockSpec((1, l, A_WIDTH), lambda bi, j: (bi, 0, 1)),
                  pl.BlockSpec((1, l, A_WIDTH), lambda bi, j: (bi, 0, 2)),
                  pl.BlockSpec(tab.shape, lambda bi, j: (0, 0, 0, 0))],
        out_specs=pl.BlockSpec((1, QBLOCK, A_WIDTH), lambda bi, j: (bi, j, 0)),
        out_shape=jax.ShapeDtypeStruct((b, l, A_WIDTH), BF),
        compiler_params=_cparams("arbitrary", "arbitrary"),
        name="attn",
    )(qkva, qkva, qkva, tab)


CONV_HIST = 8


def _gdn_kernel(qkv_ref, bd_ref, go_ref, cw_ref, alog_ref, dtb_ref, ng_ref, o_ref,
                state_ref, hist_ref, *, n_chunks):
    state_ref[...] = jnp.zeros_like(state_ref)
    hist_ref[...] = jnp.zeros_like(hist_ref)
    row = lax.broadcasted_iota(I32, (CHUNK, CHUNK), 0)
    col = lax.broadcasted_iota(I32, (CHUNK, CHUNK), 1)
    tril = row >= col
    strict = row > col
    trilf = tril.astype(F32)
    eye = (row == col).astype(F32)
    neg_rate = -jnp.exp(alog_ref[...])

    def body(c, carry):
        r0 = pl.multiple_of(c * CHUNK, CHUNK)
        cur = qkv_ref[0, pl.ds(r0, CHUNK), :].astype(F32)
        ext = jnp.concatenate([hist_ref[...], cur], axis=0)
        hist_ref[...] = cur[CHUNK - CONV_HIST:, :]
        acc = cur * cw_ref[B_CONV - 1:B_CONV, :]
        for back in range(1, B_CONV):
            tap = B_CONV - 1 - back
            acc = acc + pltpu.roll(ext, back, 0)[CONV_HIST:, :] * cw_ref[tap:tap + 1, :]
        act = _silu(acc)

        bd = bd_ref[0, pl.ds(r0, CHUNK), :]
        beta_all = _sigmoid(bd)
        z = bd + dtb_ref[...]
        softplus = jnp.maximum(z, 0.0) + jnp.log1p(jnp.exp(-jnp.abs(z)))
        g_all = neg_rate * softplus
        gc_all = jnp.dot(trilf, g_all, preferred_element_type=F32,
                         precision=lax.Precision.HIGHEST)
        gc_pad = jnp.concatenate([gc_all, jnp.zeros_like(gc_all)], axis=0)
        gc_t = gc_pad.T

        for h in range(B_HEADS):
            hs = slice(h * B_HEAD_DIM, (h + 1) * B_HEAD_DIM)
            q = act[:, hs]
            k = act[:, B_WIDTH + h * B_HEAD_DIM:B_WIDTH + (h + 1) * B_HEAD_DIM]
            v = act[:, 2 * B_WIDTH + h * B_HEAD_DIM:2 * B_WIDTH + (h + 1) * B_HEAD_DIM]
            q = q * lax.rsqrt(jnp.sum(q * q, axis=-1, keepdims=True) + RMS_EPS) * (B_HEAD_DIM ** -0.5)
            k = k * lax.rsqrt(jnp.sum(k * k, axis=-1, keepdims=True) + RMS_EPS)
            beta = beta_all[:, h:h + 1]
            gc = gc_all[:, B_HEADS + h:B_HEADS + h + 1]
            gcr = gc_t[B_HEADS + h:B_HEADS + h + 1, :CHUNK]
            glast = gc_all[CHUNK - 1:CHUNK, B_HEADS + h:B_HEADS + h + 1]
            decay = jnp.exp(jnp.where(tril, gc - gcr, -jnp.inf))
            egc = jnp.exp(gc)
            kbeta = k * beta
            lower = jnp.where(strict, _mm_nt(kbeta, k) * decay, 0.0)
            pw = -lower
            tmat = eye + pw
            for _ in range(5):
                pw = _mm3(pw, pw)
                tmat = tmat + _mm3(tmat, pw)
            u = _mm(tmat, v * beta)
            w = _mm(tmat, kbeta * egc)
            qk = _mm_nt(q, k) * decay
            st = state_ref[h]
            v_new = u - _mm(w, st)
            o = _mm(q * egc, st) + _mm(qk, v_new)
            kd = k * jnp.exp(glast - gc)
            state_ref[h] = st * jnp.exp(glast) + _mm_tn(kd, v_new)
            o = o * lax.rsqrt(jnp.mean(o * o, axis=-1, keepdims=True) + RMS_EPS) * ng_ref[...]
            gate = go_ref[0, pl.ds(r0, CHUNK), hs].astype(F32)
            o_ref[0, pl.ds(r0, CHUNK), hs] = (o * _silu(gate)).astype(BF)
        return carry

    lax.fori_loop(0, n_chunks, body, 0)


def _gdn(qkvb, bd, go, cw, alog, dtb, ng):
    b, l, _ = qkvb.shape
    full = lambda a: pl.BlockSpec(a.shape, lambda bi: (0,) * a.ndim)
    seq = lambda n: pl.BlockSpec((1, l, n), lambda bi: (bi, 0, 0))
    return pl.pallas_call(
        functools.partial(_gdn_kernel, n_chunks=l // CHUNK),
        grid=(b,),
        in_specs=[seq(3 * B_WIDTH), seq(LANES), seq(B_WIDTH), full(cw), full(alog), full(dtb), full(ng)],
        out_specs=seq(B_WIDTH),
        out_shape=jax.ShapeDtypeStruct((b, l, B_WIDTH), BF),
        scratch_shapes=[pltpu.VMEM((B_HEADS, B_HEAD_DIM, B_HEAD_DIM), F32),
                        pltpu.VMEM((CONV_HIST, 3 * B_WIDTH), F32)],
        compiler_params=_cparams("arbitrary"),
        name="gdn",
    )(qkvb, bd, go, cw, alog, dtb, ng)


def _merge_kernel(x_ref, oa_ref, ob_ref, oc_ref, wg_ref, gb_ref, wa_ref, wb_ref, wc_ref, wo_ref,
                  lg_ref, lb_ref, wrh_ref, wrl_ref, rb_ref, x1_ref, lgt_ref):
    x = x_ref[...]
    xb = x.astype(BF)
    y = None
    for i, (o_ref, w_ref) in enumerate(((oa_ref, wa_ref), (ob_ref, wb_ref), (oc_ref, wc_ref))):
        gl = jnp.dot(xb, wg_ref[:, i * D_MODEL:(i + 1) * D_MODEL], preferred_element_type=F32)
        gate = _sigmoid(gl + gb_ref[i:i + 1, :])
        br = gate * jnp.dot(o_ref[...], w_ref[...], preferred_element_type=F32)
        y = br if y is None else y + br
    mix = jnp.dot(y.astype(BF), wo_ref[...], preferred_element_type=F32)
    x1 = _layer_norm(DN_ALPHA * x + mix, lg_ref[...], lb_ref[...])
    x1_ref[...] = x1
    xh, xl = _split_bf16(x1)
    nt = lambda a, b: lax.dot_general(a, b, (((1,), (1,)), ((), ())), preferred_element_type=F32)
    lgt_ref[...] = nt(wrh_ref[...], xh) + (nt(wrh_ref[...], xl) + nt(wrl_ref[...], xh)) + rb_ref[...]


def _merge(x2d, oa, ob, oc, wg, gb, wa, wb, wc, wo, lg, lb, wrh, wrl, rb, *, tm):
    t = x2d.shape[0]
    full = lambda a: pl.BlockSpec(a.shape, lambda i: (0,) * a.ndim)
    row = lambda n: pl.BlockSpec((tm, n), lambda i: (i, 0))
    return pl.pallas_call(
        _merge_kernel,
        grid=(t // tm,),
        in_specs=[row(D_MODEL), row(A_WIDTH), row(B_WIDTH), row(C_WIDTH), full(wg), full(gb), full(wa),
                  full(wb), full(wc), full(wo), full(lg), full(lb), full(wrh), full(wrl), full(rb)],
        out_specs=[row(D_MODEL), pl.BlockSpec((ROUTER_ROWS, tm), lambda i: (0, i))],
        out_shape=[jax.ShapeDtypeStruct((t, D_MODEL), F32),
                   jax.ShapeDtypeStruct((ROUTER_ROWS, t), F32)],
        compiler_params=_cparams("arbitrary"),
        name="merge",
    )(x2d, oa, ob, oc, wg, gb, wa, wb, wc, wo, lg, lb, wrh, wrl, rb)


def _route_kernel(lgt_ref, eid_ref, wts_ref, rank_ref, cnt_ref, run_ref, *, tn):
    @pl.when(pl.program_id(0) == 0)
    def _():
        run_ref[...] = jnp.zeros_like(run_ref)

    le = lgt_ref[0:N_EXPERTS, :]
    lg = lgt_ref[N_EXPERTS:N_EXPERTS + MOE_GROUPS, :]
    gi = lax.broadcasted_iota(I32, (MOE_GROUPS, tn), 0).astype(F32)
    ei = lax.broadcasted_iota(I32, (N_EXPERTS, tn), 0).astype(F32)
    eg = jnp.right_shift(lax.broadcasted_iota(I32, (N_EXPERTS, tn), 0),
                         MOE_PER_GROUP.bit_length() - 1).astype(F32)
    mg = jnp.max(lg, axis=0, keepdims=True)
    gsel = jnp.min(jnp.where(lg == mg, gi, float(MOE_GROUPS)), axis=0, keepdims=True)
    p_top = 1.0 / jnp.sum(jnp.exp(lg - mg), axis=0, keepdims=True)
    l1 = jnp.where(eg == gsel, le, -jnp.inf)
    m1 = jnp.max(l1, axis=0, keepdims=True)
    i1 = jnp.min(jnp.where(l1 == m1, ei, float(N_EXPERTS)), axis=0, keepdims=True)
    l2 = jnp.where(ei == i1, -jnp.inf, l1)
    m2 = jnp.max(l2, axis=0, keepdims=True)
    i2 = jnp.min(jnp.where(l2 == m2, ei, float(N_EXPERTS)), axis=0, keepdims=True)
    e2 = jnp.exp(m2 - m1)
    den = 1.0 + e2
    eid_ref[...] = jnp.concatenate([i1, i2], axis=0).astype(I32)
    wts_ref[...] = jnp.concatenate([p_top / den, p_top * (e2 / den)], axis=0)

    before = (lax.broadcasted_iota(I32, (tn, tn), 0) < lax.broadcasted_iota(I32, (tn, tn), 1)).astype(BF)
    run = run_ref[...]
    ranks = []
    for ik in (i1, i2):
        onehot = (ei == ik)
        ohf = onehot.astype(F32)
        prefix = jnp.dot(ohf.astype(BF), before, preferred_element_type=F32)
        ranks.append(jnp.sum(jnp.where(onehot, prefix + run, 0.0), axis=0, keepdims=True))
        run = run + jnp.sum(ohf, axis=1, keepdims=True)
    run_ref[...] = run
    rank_ref[...] = jnp.concatenate(ranks, axis=0).astype(I32)
    cnt_ref[...] = jnp.broadcast_to(run, cnt_ref.shape)


def _route(lgt, *, tn):
    t = lgt.shape[1]
    tok = lambda: pl.BlockSpec((MOE_TOPK, tn), lambda i: (0, i))
    return pl.pallas_call(
        functools.partial(_route_kernel, tn=tn),
        grid=(t // tn,),
        in_specs=[pl.BlockSpec((ROUTER_ROWS, tn), lambda i: (0, i))],
        out_specs=[tok(), tok(), tok(), pl.BlockSpec((N_EXPERTS, LANES), lambda i: (0, 0))],
        out_shape=[jax.ShapeDtypeStruct((MOE_TOPK, t), I32),
                   jax.ShapeDtypeStruct((MOE_TOPK, t), F32),
                   jax.ShapeDtypeStruct((MOE_TOPK, t), I32),
                   jax.ShapeDtypeStruct((N_EXPERTS, LANES), F32)],
        scratch_shapes=[pltpu.VMEM((N_EXPERTS, 1), F32)],
        compiler_params=_cparams("arbitrary"),
        name="route",
    )(lgt)


def _dest_kernel(eid_ref, rank_ref, cnt_ref, dest_ref, bexp_ref, nused_ref, *, n_blocks):
    cnt = cnt_ref[...].astype(I32)
    shift = EXPERT_BLOCK.bit_length() - 1
    padded_blocks = jnp.right_shift(cnt + (EXPERT_BLOCK - 1), shift)
    er = lax.broadcasted_iota(I32, (N_EXPERTS, LANES), 0)
    start = jnp.zeros((N_EXPERTS, LANES), I32)
    for e in range(N_EXPERTS - 1):
        start = start + jnp.where(er > e, padded_blocks[e:e + 1, :], 0)
    end = start + padded_blocks
    eid = eid_ref[...]
    d = rank_ref[...]
    for e in range(N_EXPERTS):
        d = d + jnp.where(eid == e, start[e:e + 1, 0:1] * EXPERT_BLOCK, 0)
    dest_ref[...] = d

    @pl.when(pl.program_id(0) == 0)
    def _():
        blk = lax.broadcasted_iota(I32, (1, n_blocks), 1)
        owner = jnp.zeros((1, n_blocks), I32)
        for e in range(N_EXPERTS):
            owner = owner + (blk >= end[e:e + 1, 0:1]).astype(I32)
        bexp_ref[...] = jnp.minimum(owner, N_EXPERTS - 1)
        nused_ref[...] = end[N_EXPERTS - 1:N_EXPERTS, :]


def _dest(eid, rank, cnt, *, tn, n_blocks):
    t = eid.shape[1]
    tok = lambda: pl.BlockSpec((MOE_TOPK, tn), lambda i: (0, i))
    return pl.pallas_call(
        functools.partial(_dest_kernel, n_blocks=n_blocks),
        grid=(t // tn,),
        in_specs=[tok(), tok(), pl.BlockSpec(cnt.shape, lambda i: (0, 0))],
        out_specs=[tok(), pl.BlockSpec((1, n_blocks), lambda i: (0, 0)),
                   pl.BlockSpec((1, LANES), lambda i: (0, 0))],
        out_shape=[jax.ShapeDtypeStruct((MOE_TOPK, t), I32),
                   jax.ShapeDtypeStruct((1, n_blocks), I32),
                   jax.ShapeDtypeStruct((1, LANES), I32)],
        compiler_params=_cparams("arbitrary"),
        name="dest",
    )(eid, rank, cnt)


def _dispatch_kernel(dest_ref, x_ref, xs_in_ref, xs_ref, sem, *, td):
    del xs_in_ref

    def row_copy(t, d):
        return pltpu.make_async_copy(x_ref.at[pl.ds(t, 1), :], xs_ref.at[pl.ds(d, 1), :], sem)

    def issue(t, carry):
        for k in range(MOE_TOPK):
            row_copy(t, dest_ref[k, t]).start()
        return carry

    lax.fori_loop(0, td, issue, 0)

    def drain(t, carry):
        for k in range(MOE_TOPK):
            row_copy(t, dest_ref[k, t]).wait()
        return carry

    lax.fori_loop(0, td, drain, 0)


def _dispatch(dest, x1, xs_zero, *, td):
    t = x1.shape[0]
    return pl.pallas_call(
        functools.partial(_dispatch_kernel, td=td),
        grid=(t // td,),
        in_specs=[pl.BlockSpec((MOE_TOPK, td), lambda i: (0, i), memory_space=pltpu.SMEM),
                  pl.BlockSpec((td, D_MODEL), lambda i: (i, 0)),
                  pl.BlockSpec(memory_space=pl.ANY)],
        out_specs=pl.BlockSpec(memory_space=pl.ANY),
        out_shape=jax.ShapeDtypeStruct(xs_zero.shape, xs_zero.dtype),
        scratch_shapes=[pltpu.SemaphoreType.DMA(())],
        input_output_aliases={2: 0},
        compiler_params=_cparams("arbitrary"),
        name="dispatch",
    )(dest, x1, xs_zero)


def _expert_kernel(bexp_ref, nused_ref, xs_ref, w1_ref, w3_ref, w2_ref, y_ref):
    del bexp_ref
    i = pl.program_id(0)

    @pl.when(i < nused_ref[0])
    def _():
        xb = xs_ref[...].astype(BF)
        h1 = jnp.dot(xb, w1_ref[0], preferred_element_type=F32)
        h3 = jnp.dot(xb, w3_ref[0], preferred_element_type=F32)
        hid = (_silu(h1) * h3).astype(BF)
        y_ref[...] = jnp.dot(hid, w2_ref[0], preferred_element_type=F32)

    @pl.when(i >= nused_ref[0])
    def _():
        y_ref[...] = jnp.zeros_like(y_ref)


def _expert(bexp, nused, xs, w1, w3, w2):
    cap = xs.shape[0]
    nb = cap // EXPERT_BLOCK
    last = lambda i, nu: jnp.minimum(i, nu[0] - 1)
    grid_spec = pltpu.PrefetchScalarGridSpec(
        num_scalar_prefetch=2,
        grid=(nb,),
        in_specs=[pl.BlockSpec((EXPERT_BLOCK, D_MODEL), lambda i, be, nu: (last(i, nu), 0)),
                  pl.BlockSpec((1, D_MODEL, MOE_FF), lambda i, be, nu: (be[last(i, nu)], 0, 0)),
                  pl.BlockSpec((1, D_MODEL, MOE_FF), lambda i, be, nu: (be[last(i, nu)], 0, 0)),
                  pl.BlockSpec((1, MOE_FF, D_MODEL), lambda i, be, nu: (be[last(i, nu)], 0, 0))],
        out_specs=pl.BlockSpec((EXPERT_BLOCK, D_MODEL), lambda i, be, nu: (i, 0)),
    )
    return pl.pallas_call(
        _expert_kernel,
        grid_spec=grid_spec,
        out_shape=jax.ShapeDtypeStruct((cap, D_MODEL), F32),
        compiler_params=_cparams("arbitrary"),
        name="expert",
    )(bexp, nused, xs, w1, w3, w2)


def _combine_kernel(dest_ref, x_ref, w_ref, lg_ref, lb_ref, yb_ref, o_ref, buf_ref, sem, *, tc):
    def row_copy(t, k):
        return pltpu.make_async_copy(yb_ref.at[pl.ds(dest_ref[k, t], 1), :],
                                     buf_ref.at[k, pl.ds(t, 1), :], sem)

    def issue(t, carry):
        for k in range(MOE_TOPK):
            row_copy(t, k).start()
        return carry

    lax.fori_loop(0, tc, issue, 0)

    def drain(t, carry):
        for k in range(MOE_TOPK):
            row_copy(t, k).wait()
        return carry

    lax.fori_loop(0, tc, drain, 0)
    ffn = w_ref[:, 0:1] * buf_ref[0] + w_ref[:, 1:2] * buf_ref[1]
    o_ref[...] = _layer_norm(DN_ALPHA * x_ref[...] + ffn, lg_ref[...], lb_ref[...])


def _combine(dest, x1, wts_t, lg, lb, yb, *, tc):
    t = x1.shape[0]
    full = lambda a: pl.BlockSpec(a.shape, lambda i: (0,) * a.ndim)
    return pl.pallas_call(
        functools.partial(_combine_kernel, tc=tc),
        grid=(t // tc,),
        in_specs=[pl.BlockSpec((MOE_TOPK, tc), lambda i: (0, i), memory_space=pltpu.SMEM),
                  pl.BlockSpec((tc, D_MODEL), lambda i: (i, 0)),
                  pl.BlockSpec((tc, MOE_TOPK), lambda i: (i, 0)),
                  full(lg), full(lb),
                  pl.BlockSpec(memory_space=pl.ANY)],
        out_specs=pl.BlockSpec((tc, D_MODEL), lambda i: (i, 0)),
        out_shape=jax.ShapeDtypeStruct((t, D_MODEL), F32),
        scratch_shapes=[pltpu.VMEM((MOE_TOPK, tc, D_MODEL), F32), pltpu.SemaphoreType.DMA(())],
        compiler_params=_cparams("arbitrary"),
        name="combine",
    )(dest, x1, wts_t, lg, lb, yb)


def _tile(n, want):
    while n % want:
        want //= 2
    return want


def _mixer(x2d, b, l, w_in, rel_bias, conv_w, a_log, dt_bias, norm_g, pool_w, pool_scale,
           w_br_a, w_br_b, w_br_c, gate_b, w_out, ln_g, ln_b, wr, rb):
    t = x2d.shape[0]
    c0 = 3 * A_WIDTH
    c1 = c0 + 3 * B_WIDTH
    c2 = c1 + 2 * B_HEADS
    c3 = c2 + B_WIDTH
    c4 = c3 + C_WIDTH
    wbf = w_in.astype(BF)
    wbd = jnp.pad(wbf[:, c1:c2], ((0, 0), (0, LANES - 2 * B_HEADS)))
    qkva, qkvb, bd, go, oc = _proj(
        x2d, wbf[:, :c0], wbf[:, c0:c1], wbd, wbf[:, c2:c3], wbf[:, c3:c4],
        pool_w.astype(BF), pool_scale.reshape(1, C_WIDTH), seq=l, tm=_tile(l, 512))

    oa = _attn(qkva.reshape(b, l, 3 * A_WIDTH), _attn_bias_table(rel_bias)).reshape(t, A_WIDTH)

    lane_pad = lambda v: jnp.pad(v.reshape(1, B_HEADS), ((0, 0), (B_HEADS, LANES - 2 * B_HEADS)))
    ob = _gdn(qkvb.reshape(b, l, 3 * B_WIDTH), bd.reshape(b, l, LANES), go.reshape(b, l, B_WIDTH),
              conv_w, lane_pad(a_log), lane_pad(dt_bias), norm_g.reshape(1, B_HEAD_DIM)).reshape(t, B_WIDTH)

    wrh, wrl = _split_bf16(wr)
    return _merge(x2d, oa, ob, oc, wbf[:, c4:], gate_b, w_br_a.astype(BF), w_br_b.astype(BF),
                  w_br_c.astype(BF), w_out.astype(BF), ln_g.reshape(1, -1), ln_b.reshape(1, -1),
                  wrh, wrl, rb, tm=_tile(t, 512))


def _moe(x1, lgt, w1, w3, w2, ln_g, ln_b):
    t = x1.shape[0]
    n_blocks = -(-t * MOE_TOPK // EXPERT_BLOCK) + N_EXPERTS
    cap = n_blocks * EXPERT_BLOCK
    eid, wts, rank, cnt = _route(lgt, tn=_tile(t, 512))
    dest, bexp, nused = _dest(eid, rank, cnt, tn=_tile(t, 2048), n_blocks=n_blocks)
    xs = _dispatch(dest, x1, jnp.zeros((cap, D_MODEL), F32), td=_tile(t, 1024))
    yb = _expert(bexp.reshape(n_blocks), nused[0, :1], xs, w1.astype(BF), w3.astype(BF), w2.astype(BF))
    return _combine(dest, x1, wts.T, ln_g.reshape(1, -1), ln_b.reshape(1, -1), yb, tc=_tile(t, 512))


def kernel(x, w_in, attn_rel_bias, gdn_conv_w, gdn_a_log, gdn_dt_bias, gdn_norm_g, pool_w, pool_scale,
           w_branch_a, w_branch_b, w_branch_c, gate_b, w_out, ln1_g, ln1_b, router_group_w,
           router_group_b, router_expert_w, router_expert_b, moe_w1, moe_w3, moe_w2, ln2_g, ln2_b):
    b, l, d = x.shape
    x2d = x.reshape(b * l, d)
    for i in range(DEPTH):
        wr = jnp.concatenate([router_expert_w[i].T, router_group_w[i].T,
                              jnp.zeros((ROUTER_ROWS - N_EXPERTS - MOE_GROUPS, d), F32)], axis=0)
        rb = jnp.concatenate([router_expert_b[i], router_group_b[i],
                              jnp.zeros((ROUTER_ROWS - N_EXPERTS - MOE_GROUPS,), F32)]).reshape(ROUTER_ROWS, 1)
        x1, lgt = _mixer(x2d, b, l, w_in[i], attn_rel_bias[i], gdn_conv_w[i], gdn_a_log[i], gdn_dt_bias[i],
                         gdn_norm_g[i], pool_w[i], pool_scale[i], w_branch_a[i], w_branch_b[i],
                         w_branch_c[i], gate_b[i], w_out[i], ln1_g[i], ln1_b[i], wr, rb)
        x2d = _moe(x1, lgt, moe_w1[i], moe_w3[i], moe_w2[i], ln2_g[i], ln2_b[i])
    return x2d.reshape(b, l, d)
```

```python
import functools

import jax
import jax.numpy as jnp
from jax import lax
from jax.experimental import pallas as pl
from jax.experimental.pallas import tpu as pltpu

BF = jnp.bfloat16
F32 = jnp.float32
I32 = jnp.int32

D_MODEL = 1024
DEPTH = 2
CHUNK = 64

A_HEADS = 8
A_HEAD_DIM = 64
A_WIDTH = A_HEADS * A_HEAD_DIM
A_LEFT_CHUNKS = 8
A_REL_CLIP = 256

B_HEADS = 4
B_HEAD_DIM = 128
B_WIDTH = B_HEADS * B_HEAD_DIM
B_CONV = 4

C_WINDOWS = (2, 4, 8, 16)
C_GROUPS = 4
C_GROUP_DIM = 128
C_WIDTH = C_GROUPS * C_GROUP_DIM

N_BRANCH = 3
MOE_GROUPS = 4
MOE_PER_GROUP = 8
N_EXPERTS = MOE_GROUPS * MOE_PER_GROUP
MOE_TOPK = 2
MOE_FF = 512

DN_ALPHA = (2 * DEPTH) ** 0.25
LN_EPS = 1e-5
RMS_EPS = 1e-6

LANES = 128
SUBLANES = 8
VMEM_LIMIT_BYTES = 56 * 1024 * 1024

QBLOCK = 2 * CHUNK
KBAND = (A_LEFT_CHUNKS + 2) * CHUNK
NEG_BIG = -1e30

ROUTER_ROWS = 40
EXPERT_BLOCK = 512


def _cparams(*sem):
    return pltpu.CompilerParams(dimension_semantics=sem, vmem_limit_bytes=VMEM_LIMIT_BYTES)


def _mm(a, b):
    return jnp.dot(a.astype(BF), b.astype(BF), preferred_element_type=F32)


def _mm_nt(a, b):
    return lax.dot_general(a.astype(BF), b.astype(BF), (((1,), (1,)), ((), ())),
                           preferred_element_type=F32)


def _mm_tn(a, b):
    return lax.dot_general(a.astype(BF), b.astype(BF), (((0,), (0,)), ((), ())),
                           preferred_element_type=F32)


def _split_bf16(a):
    hi = a.astype(BF)
    lo = (a - hi.astype(F32)).astype(BF)
    return hi, lo


def _mm3(a, b):
    ah, al = _split_bf16(a)
    bh, bl = _split_bf16(b)
    return (jnp.dot(ah, bh, preferred_element_type=F32)
            + (jnp.dot(ah, bl, preferred_element_type=F32)
               + jnp.dot(al, bh, preferred_element_type=F32)))


def _sigmoid(x):
    return 1.0 / (1.0 + jnp.exp(-x))


def _silu(x):
    return x * _sigmoid(x)


def _layer_norm(z, g, b):
    mu = jnp.mean(z, axis=-1, keepdims=True)
    zc = z - mu
    var = jnp.mean(zc * zc, axis=-1, keepdims=True)
    return zc * lax.rsqrt(var + LN_EPS) * g + b


POOL_TAIL = 16


def _proj_kernel(x_ref, wa_ref, wb_ref, wbd_ref, wgo_ref, wc_ref, pw_ref, ps_ref,
                 qkva_ref, qkvb_ref, bd_ref, go_ref, oc_ref, tail_ref, *, tiles_per_seq, tm):
    it = pl.program_id(0) % tiles_per_seq
    xb = x_ref[...].astype(BF)
    qkva_ref[...] = jnp.dot(xb, wa_ref[...], preferred_element_type=F32).astype(BF)
    qkvb_ref[...] = jnp.dot(xb, wb_ref[...], preferred_element_type=F32).astype(BF)
    bd_ref[...] = jnp.dot(xb, wbd_ref[...], preferred_element_type=F32)
    go_ref[...] = jnp.dot(xb, wgo_ref[...], preferred_element_type=F32).astype(BF)
    u = jnp.dot(xb, wc_ref[...], preferred_element_type=F32)

    @pl.when(it == 0)
    def _():
        tail_ref[...] = jnp.zeros_like(tail_ref)

    ext = jnp.concatenate([tail_ref[...], u], axis=0)
    tail_ref[...] = u[tm - POOL_TAIL:, :]
    pos = (it * tm + 1 + lax.broadcasted_iota(I32, (tm, 1), 0)).astype(F32)
    s = ext
    for gi, win in enumerate(C_WINDOWS):
        if gi:
            s = s[:, C_GROUP_DIM:]
        s = s + pltpu.roll(s, win // 2, 0)
        lo_c, hi_c = gi * C_GROUP_DIM, (gi + 1) * C_GROUP_DIM
        cnt = jnp.minimum(pos, float(win))
        pooled = s[POOL_TAIL:, :C_GROUP_DIM] / cnt - u[:, lo_c:hi_c]
        mixed = jnp.dot(pooled.astype(BF), pw_ref[gi], preferred_element_type=F32)
        oc_ref[:, lo_c:hi_c] = (mixed * ps_ref[:, lo_c:hi_c]).astype(BF)


def _proj(x2d, wa, wb, wbd, wgo, wc, pw, ps, *, seq, tm):
    t = x2d.shape[0]
    full = lambda a: pl.BlockSpec(a.shape, lambda i: (0,) * a.ndim)
    row = lambda n: pl.BlockSpec((tm, n), lambda i: (i, 0))
    return pl.pallas_call(
        functools.partial(_proj_kernel, tiles_per_seq=seq // tm, tm=tm),
        grid=(t // tm,),
        in_specs=[row(D_MODEL), full(wa), full(wb), full(wbd), full(wgo), full(wc), full(pw), full(ps)],
        out_specs=[row(3 * A_WIDTH), row(3 * B_WIDTH), row(LANES), row(B_WIDTH), row(C_WIDTH)],
        out_shape=[jax.ShapeDtypeStruct((t, 3 * A_WIDTH), BF),
                   jax.ShapeDtypeStruct((t, 3 * B_WIDTH), BF),
                   jax.ShapeDtypeStruct((t, LANES), F32),
                   jax.ShapeDtypeStruct((t, B_WIDTH), BF),
                   jax.ShapeDtypeStruct((t, C_WIDTH), BF)],
        scratch_shapes=[pltpu.VMEM((POOL_TAIL, C_WIDTH), F32)],
        compiler_params=_cparams("arbitrary"),
        name="proj",
    )(x2d, wa, wb, wbd, wgo, wc, pw, ps)


def _attn_bias_table(rel_bias):
    h = rel_bias.shape[0]
    pad = A_LEFT_CHUNKS * CHUNK
    width = KBAND + pad
    n = width + QBLOCK - 1
    far = jnp.repeat(rel_bias[:, -1:], QBLOCK - 1 + pad - A_REL_CLIP, axis=1)
    near = jnp.repeat(rel_bias[:, :1], n - far.shape[1] - rel_bias.shape[1], axis=1)
    w = jnp.concatenate([far, rel_bias[:, ::-1], near], axis=1).astype(F32)
    w = jnp.roll(w, -(QBLOCK - 1), axis=1)
    toep = jnp.tile(w, (1, QBLOCK))[:, :QBLOCK * (n - 1)].reshape(h, QBLOCK, n - 1)[:, :, :width]
    r = jnp.arange(QBLOCK)[:, None]
    m = jnp.arange(width)[None, :]
    kc, qc = m // CHUNK, r // CHUNK
    valid = (kc >= qc) & (kc <= qc + A_LEFT_CHUNKS)
    tab = jnp.where(valid[None], toep, NEG_BIG)
    return tab.reshape(h, QBLOCK, width // LANES, LANES).transpose(0, 2, 1, 3)


def _attn_kernel(q_ref, k_ref, v_ref, tab_ref, o_ref):
    j = pl.program_id(1)
    first_blk = A_LEFT_CHUNKS * CHUNK // QBLOCK
    kstart = pl.multiple_of(jnp.maximum(j - first_blk, 0) * QBLOCK, QBLOCK)
    offb = jnp.maximum(first_blk - j, 0)
    lane = lax.broadcasted_iota(I32, (QBLOCK, LANES), 1)
    lo = lane < A_HEAD_DIM
    nkb = KBAND // LANES
    for hp in range(A_HEADS // 2):
        cs = slice(hp * LANES, (hp + 1) * LANES)
        qp = q_ref[0, :, cs]
        kp = k_ref[0, pl.ds(kstart, KBAND), cs]
        vp = v_ref[0, pl.ds(kstart, KBAND), cs]
        outs = []
        for par in range(2):
            h = 2 * hp + par
            qm = jnp.where(lo if par == 0 else jnp.logical_not(lo), qp, jnp.zeros_like(qp))
            s = lax.dot_general(qm, kp, (((1,), (1,)), ((), ())), preferred_element_type=F32)
            bias = jnp.concatenate([tab_ref[h, offb + i] for i in range(nkb)], axis=1)
            s = s * (A_HEAD_DIM ** -0.5) + bias
            mx = jnp.max(s, axis=-1, keepdims=True)
            p = jnp.exp(s - mx)
            l = jnp.sum(p, axis=-1, keepdims=True)
            o = jnp.dot(p.astype(BF), vp, preferred_element_type=F32)
            outs.append(o / l)
        o_ref[0, :, cs] = jnp.where(lo, outs[0], outs[1]).astype(BF)


def _attn(qkva, tab):
    b, l, _ = qkva.shape
    return pl.pallas_call(
        _attn_kernel,
        grid=(b, l // QBLOCK),
        in_specs=[pl.BlockSpec((1, QBLOCK, A_WIDTH), lambda bi, j: (bi, j, 0)),
                  pl.BlockSpec((1, l, A_WIDTH), lambda bi, j: (bi, 0, 1)),
                  pl.BlockSpec((1, l, A_WIDTH), lambda bi, j: (bi, 0, 2)),
                  pl.BlockSpec(tab.shape, lambda bi, j: (0, 0, 0, 0))],
        out_specs=pl.BlockSpec((1, QBLOCK, A_WIDTH), lambda bi, j: (bi, j, 0)),
        out_shape=jax.ShapeDtypeStruct((b, l, A_WIDTH), BF),
        compiler_params=_cparams("arbitrary", "arbitrary"),
        name="attn",
    )(qkva, qkva, qkva, tab)


CONV_HIST = 16
GDN_PREP_UNROLL = 2
PACKW = B_HEADS * CHUNK


def _gdn_kernel(qkv_ref, bd_ref, go_ref, cw_ref, alog_ref, dtb_ref, ng_ref, o_ref,
                u_ref, wq_ref, qkkd_ref, egl_ref, state_ref, *, n_chunks):
    f32 = lambda m: m.astype(F32)
    r_p = lax.broadcasted_iota(I32, (CHUNK, PACKW), 0)
    c_p = lax.broadcasted_iota(I32, (CHUNK, PACKW), 1)
    cin_p = jnp.bitwise_and(c_p, CHUNK - 1)
    grp_p = jnp.right_shift(c_p, CHUNK.bit_length() - 1)
    tril_p = r_p >= cin_p
    strict_p = r_p > cin_p
    eye_p = r_p == cin_p
    eye_pf = f32(eye_p)
    rt = lax.broadcasted_iota(I32, (CHUNK, CHUNK), 0)
    ct = lax.broadcasted_iota(I32, (CHUNK, CHUNK), 1)
    trilf = f32(rt >= ct)
    rb = jnp.right_shift(lax.broadcasted_iota(I32, (PACKW, PACKW), 0), CHUNK.bit_length() - 1)
    blk_pp = rb == jnp.right_shift(lax.broadcasted_iota(I32, (PACKW, PACKW), 1), CHUNK.bit_length() - 1)
    rb = jnp.right_shift(lax.broadcasted_iota(I32, (PACKW, B_WIDTH), 0), CHUNK.bit_length() - 1)
    blk_pw = rb == jnp.right_shift(lax.broadcasted_iota(I32, (PACKW, B_WIDTH), 1),
                                   B_HEAD_DIM.bit_length() - 1)
    eye_d = (lax.broadcasted_iota(I32, (B_HEAD_DIM, B_HEAD_DIM), 0)
             == lax.broadcasted_iota(I32, (B_HEAD_DIM, B_HEAD_DIM), 1)).astype(BF)
    neg_rate = -jnp.exp(alog_ref[...])

    def stack4(a, mask):
        return jnp.where(mask, jnp.concatenate([a] * B_HEADS, axis=0), 0.0).astype(BF)

    def heads(fn):
        return jnp.concatenate([fn(h, slice(h * B_HEAD_DIM, (h + 1) * B_HEAD_DIM)) for h in range(B_HEADS)],
                               axis=1)

    def prep(c):
        r0 = pl.multiple_of(c * CHUNK, CHUNK)
        cur = qkv_ref[0, pl.ds(r0, CHUNK), :].astype(F32)
        h0 = pl.multiple_of(jnp.maximum(r0 - CONV_HIST, 0), CONV_HIST)
        hist = qkv_ref[0, pl.ds(h0, CONV_HIST), :].astype(F32)
        hist = jnp.where(c > 0, hist, 0.0)
        ext = jnp.concatenate([hist, cur], axis=0)
        acc = cur * cw_ref[B_CONV - 1:B_CONV, :]
        for back in range(1, B_CONV):
            tap = B_CONV - 1 - back
            acc = acc + pltpu.roll(ext, back, 0)[CONV_HIST:, :] * cw_ref[tap:tap + 1, :]
        act = _silu(acc)
        q_all, k_all, v_all = act[:, :B_WIDTH], act[:, B_WIDTH:2 * B_WIDTH], act[:, 2 * B_WIDTH:]

        bd = bd_ref[0, pl.ds(r0, CHUNK), :]
        beta_all = _sigmoid(bd)
        z = bd + dtb_ref[...]
        softplus = jnp.maximum(z, 0.0) + jnp.log1p(jnp.exp(-jnp.abs(z)))
        g_all = neg_rate * softplus
        gc_all = jnp.dot(trilf, g_all, preferred_element_type=F32,
                         precision=lax.Precision.HIGHEST)
        egc_all = jnp.exp(gc_all)
        glast = gc_all[CHUNK - 1:CHUNK, :]
        ekd_all = jnp.exp(glast - gc_all)
        egl = jnp.exp(glast)
        col = lambda m, h: m[:, B_HEADS + h:B_HEADS + h + 1]

        def l2n(m):
            return m * lax.rsqrt(jnp.sum(m * m, axis=-1, keepdims=True) + RMS_EPS)

        qn = heads(lambda h, hs: l2n(q_all[:, hs]) * (B_HEAD_DIM ** -0.5))
        kn = heads(lambda h, hs: l2n(k_all[:, hs]))
        kbeta = heads(lambda h, hs: kn[:, hs] * beta_all[:, h:h + 1])
        vbeta = heads(lambda h, hs: v_all[:, hs] * beta_all[:, h:h + 1])
        kbe = heads(lambda h, hs: kbeta[:, hs] * col(egc_all, h))
        qe = heads(lambda h, hs: qn[:, hs] * col(egc_all, h))
        kd_rows = jnp.concatenate([(kn[:, h * B_HEAD_DIM:(h + 1) * B_HEAD_DIM] * col(ekd_all, h)).astype(BF)
                                   for h in range(B_HEADS)], axis=0)

        gcc = jnp.zeros((CHUNK, PACKW), F32)
        for h in range(B_HEADS):
            gcc = jnp.where(grp_p == h, col(gc_all, h), gcc)
        gcr = jnp.sum(gcc * eye_pf, axis=0, keepdims=True)
        decay = jnp.exp(jnp.where(tril_p, gcc - gcr, -jnp.inf))

        lhs = jnp.concatenate([kbeta, qn], axis=0).astype(BF)
        aq = lax.dot_general(lhs, stack4(kn, blk_pw), (((1,), (1,)), ((), ())), preferred_element_type=F32)
        lower = jnp.where(strict_p, aq[:CHUNK] * decay, 0.0)
        qk = aq[CHUNK:] * decay
        pw = -lower
        tmat = eye_pf + pw
        pw = jnp.dot(pw.astype(BF), stack4(pw, blk_pp), preferred_element_type=F32)
        n_sq = CHUNK.bit_length() - 2
        for lvl in range(n_sq):
            if lvl + 1 < n_sq:
                both = jnp.dot(jnp.concatenate([tmat, pw], axis=0).astype(BF), stack4(pw, blk_pp),
                               preferred_element_type=F32)
                tmat, pw = tmat + both[:CHUNK], both[CHUNK:]
            else:
                tmat = tmat + jnp.dot(tmat.astype(BF), stack4(pw, blk_pp), preferred_element_type=F32)
        rhs = jnp.concatenate([stack4(vbeta, blk_pw), stack4(kbe, blk_pw)], axis=1)
        uw = jnp.dot(tmat.astype(BF), rhs, preferred_element_type=F32)
        kdt = lax.dot_general(eye_d, kd_rows, (((1,), (1,)), ((), ())), preferred_element_type=F32)

        u_ref[c] = uw[:, :B_WIDTH]
        wq_ref[c, :CHUNK, :] = uw[:, B_WIDTH:].astype(BF)
        wq_ref[c, CHUNK:, :] = qe.astype(BF)
        qkkd_ref[c, :CHUNK, :] = qk.astype(BF)
        qkkd_ref[c, CHUNK:, :] = kdt.astype(BF)
        egl_ref[c] = heads(lambda h, hs: jnp.broadcast_to(col(egl, h), (1, B_HEAD_DIM)))

    def prep_body(i, carry):
        for g in range(GDN_PREP_UNROLL):
            prep(i * GDN_PREP_UNROLL + g)
        return carry

    lax.fori_loop(0, n_chunks // GDN_PREP_UNROLL, prep_body, 0)

    state_ref[...] = jnp.zeros_like(state_ref)

    def step(c, carry):
        r0 = pl.multiple_of(c * CHUNK, CHUNK)
        st = state_ref[...]
        stb = st.astype(BF)
        ws, qs = [], []
        for h in range(B_HEADS):
            hs = slice(h * B_HEAD_DIM, (h + 1) * B_HEAD_DIM)
            r1 = jnp.dot(wq_ref[c, :, hs], stb[:, hs], preferred_element_type=F32)
            ws.append(r1[:CHUNK])
            qs.append(r1[CHUNK:])
        v_new = u_ref[c] - jnp.concatenate(ws, axis=1)
        r2 = jnp.dot(qkkd_ref[c], stack4(v_new, blk_pw), preferred_element_type=F32)
        o_all = jnp.concatenate(qs, axis=1) + r2[:CHUNK]
        state_ref[...] = st * egl_ref[c] + r2[CHUNK:]
        gate = go_ref[0, pl.ds(r0, CHUNK), :].astype(F32)
        o_n = heads(lambda h, hs: o_all[:, hs]
                    * lax.rsqrt(jnp.mean(o_all[:, hs] * o_all[:, hs], axis=-1, keepdims=True) + RMS_EPS)
                    * ng_ref[...])
        o_ref[0, pl.ds(r0, CHUNK), :] = (o_n * _silu(gate)).astype(BF)
        return carry

    lax.fori_loop(0, n_chunks, step, 0)


def _gdn(qkvb, bd, go, cw, alog, dtb, ng):
    b, l, _ = qkvb.shape
    nc = l // CHUNK
    full = lambda a: pl.BlockSpec(a.shape, lambda bi: (0,) * a.ndim)
    seq = lambda n: pl.BlockSpec((1, l, n), lambda bi: (bi, 0, 0))
    return pl.pallas_call(
        functools.partial(_gdn_kernel, n_chunks=nc),
        grid=(b,),
        in_specs=[seq(3 * B_WIDTH), seq(LANES), seq(B_WIDTH), full(cw), full(alog), full(dtb), full(ng)],
        out_specs=seq(B_WIDTH),
        out_shape=jax.ShapeDtypeStruct((b, l, B_WIDTH), BF),
        scratch_shapes=[pltpu.VMEM((nc, CHUNK, B_WIDTH), F32),
                        pltpu.VMEM((nc, 2 * CHUNK, B_WIDTH), BF),
                        pltpu.VMEM((nc, CHUNK + B_HEAD_DIM, PACKW), BF),
                        pltpu.VMEM((nc, 1, B_WIDTH), F32),
                        pltpu.VMEM((B_HEAD_DIM, B_WIDTH), F32)],
        compiler_params=_cparams("arbitrary"),
        name="gdn",
    )(qkvb, bd, go, cw, alog, dtb, ng)


def _merge_kernel(x_ref, oa_ref, ob_ref, oc_ref, wg_ref, gb_ref, wa_ref, wb_ref, wc_ref, wo_ref,
                  lg_ref, lb_ref, wrh_ref, wrl_ref, rb_ref, x1_ref, lgt_ref):
    x = x_ref[...]
    xb = x.astype(BF)
    y = None
    for i, (o_ref, w_ref) in enumerate(((oa_ref, wa_ref), (ob_ref, wb_ref), (oc_ref, wc_ref))):
        gl = jnp.dot(xb, wg_ref[:, i * D_MODEL:(i + 1) * D_MODEL], preferred_element_type=F32)
        gate = _sigmoid(gl + gb_ref[i:i + 1, :])
        br = gate * jnp.dot(o_ref[...], w_ref[...], preferred_element_type=F32)
        y = br if y is None else y + br
    mix = jnp.dot(y.astype(BF), wo_ref[...], preferred_element_type=F32)
    x1 = _layer_norm(DN_ALPHA * x + mix, lg_ref[...], lb_ref[...])
    x1_ref[...] = x1
    xh, xl = _split_bf16(x1)
    nt = lambda a, b: lax.dot_general(a, b, (((1,), (1,)), ((), ())), preferred_element_type=F32)
    lgt_ref[...] = nt(wrh_ref[...], xh) + (nt(wrh_ref[...], xl) + nt(wrl_ref[...], xh)) + rb_ref[...]


def _merge(x2d, oa, ob, oc, wg, gb, wa, wb, wc, wo, lg, lb, wrh, wrl, rb, *, tm):
    t = x2d.shape[0]
    full = lambda a: pl.BlockSpec(a.shape, lambda i: (0,) * a.ndim)
    row = lambda n: pl.BlockSpec((tm, n), lambda i: (i, 0))
    return pl.pallas_call(
        _merge_kernel,
        grid=(t // tm,),
        in_specs=[row(D_MODEL), row(A_WIDTH), row(B_WIDTH), row(C_WIDTH), full(wg), full(gb), full(wa),
                  full(wb), full(wc), full(wo), full(lg), full(lb), full(wrh), full(wrl), full(rb)],
        out_specs=[row(D_MODEL), pl.BlockSpec((ROUTER_ROWS, tm), lambda i: (0, i))],
        out_shape=[jax.ShapeDtypeStruct((t, D_MODEL), F32),
                   jax.ShapeDtypeStruct((ROUTER_ROWS, t), F32)],
        compiler_params=_cparams("arbitrary"),
        name="merge",
    )(x2d, oa, ob, oc, wg, gb, wa, wb, wc, wo, lg, lb, wrh, wrl, rb)


def _route_kernel(lgt_ref, eid_ref, wts_ref, rank_ref, cnt_ref, run_ref, *, tn):
    @pl.when(pl.program_id(0) == 0)
    def _():
        run_ref[...] = jnp.zeros_like(run_ref)

    le = lgt_ref[0:N_EXPERTS, :]
    lg = lgt_ref[N_EXPERTS:N_EXPERTS + MOE_GROUPS, :]
    gi = lax.broadcasted_iota(I32, (MOE_GROUPS, tn), 0).astype(F32)
    ei = lax.broadcasted_iota(I32, (N_EXPERTS, tn), 0).astype(F32)
    eg = jnp.right_shift(lax.broadcasted_iota(I32, (N_EXPERTS, tn), 0),
                         MOE_PER_GROUP.bit_length() - 1).astype(F32)
    mg = jnp.max(lg, axis=0, keepdims=True)
    gsel = jnp.min(jnp.where(lg == mg, gi, float(MOE_GROUPS)), axis=0, keepdims=True)
    p_top = 1.0 / jnp.sum(jnp.exp(lg - mg), axis=0, keepdims=True)
    l1 = jnp.where(eg == gsel, le, -jnp.inf)
    m1 = jnp.max(l1, axis=0, keepdims=True)
    i1 = jnp.min(jnp.where(l1 == m1, ei, float(N_EXPERTS)), axis=0, keepdims=True)
    l2 = jnp.where(ei == i1, -jnp.inf, l1)
    m2 = jnp.max(l2, axis=0, keepdims=True)
    i2 = jnp.min(jnp.where(l2 == m2, ei, float(N_EXPERTS)), axis=0, keepdims=True)
    e2 = jnp.exp(m2 - m1)
    den = 1.0 + e2
    eid_ref[...] = jnp.concatenate([i1, i2], axis=0).astype(I32)
    wts_ref[...] = jnp.concatenate([p_top / den, p_top * (e2 / den)], axis=0)

    before = (lax.broadcasted_iota(I32, (tn, tn), 0) < lax.broadcasted_iota(I32, (tn, tn), 1)).astype(BF)
    run = run_ref[...]
    ranks = []
    for ik in (i1, i2):
        onehot = (ei == ik)
        ohf = onehot.astype(F32)
        prefix = jnp.dot(ohf.astype(BF), before, preferred_element_type=F32)
        ranks.append(jnp.sum(jnp.where(onehot, prefix + run, 0.0), axis=0, keepdims=True))
        run = run + jnp.sum(ohf, axis=1, keepdims=True)
    run_ref[...] = run
    rank_ref[...] = jnp.concatenate(ranks, axis=0).astype(I32)
    cnt_ref[...] = jnp.broadcast_to(run, cnt_ref.shape)


def _route(lgt, *, tn):
    t = lgt.shape[1]
    tok = lambda: pl.BlockSpec((MOE_TOPK, tn), lambda i: (0, i))
    return pl.pallas_call(
        functools.partial(_route_kernel, tn=tn),
        grid=(t // tn,),
        in_specs=[pl.BlockSpec((ROUTER_ROWS, tn), lambda i: (0, i))],
        out_specs=[tok(), tok(), tok(), pl.BlockSpec((N_EXPERTS, LANES), lambda i: (0, 0))],
        out_shape=[jax.ShapeDtypeStruct((MOE_TOPK, t), I32),
                   jax.ShapeDtypeStruct((MOE_TOPK, t), F32),
                   jax.ShapeDtypeStruct((MOE_TOPK, t), I32),
                   jax.ShapeDtypeStruct((N_EXPERTS, LANES), F32)],
        scratch_shapes=[pltpu.VMEM((N_EXPERTS, 1), F32)],
        compiler_params=_cparams("arbitrary"),
        name="route",
    )(lgt)


def _dest_kernel(eid_ref, rank_ref, cnt_ref, dest_ref, bexp_ref, nused_ref, *, n_blocks):
    cnt = cnt_ref[...].astype(I32)
    shift = EXPERT_BLOCK.bit_length() - 1
    padded_blocks = jnp.right_shift(cnt + (EXPERT_BLOCK - 1), shift)
    er = lax.broadcasted_iota(I32, (N_EXPERTS, LANES), 0)
    start = jnp.zeros((N_EXPERTS, LANES), I32)
    for e in range(N_EXPERTS - 1):
        start = start + jnp.where(er > e, padded_blocks[e:e + 1, :], 0)
    end = start + padded_blocks
    eid = eid_ref[...]
    d = rank_ref[...]
    for e in range(N_EXPERTS):
        d = d + jnp.where(eid == e, start[e:e + 1, 0:1] * EXPERT_BLOCK, 0)
    dest_ref[...] = d

    @pl.when(pl.program_id(0) == 0)
    def _():
        blk = lax.broadcasted_iota(I32, (1, n_blocks), 1)
        owner = jnp.zeros((1, n_blocks), I32)
        for e in range(N_EXPERTS):
            owner = owner + (blk >= end[e:e + 1, 0:1]).astype(I32)
        bexp_ref[...] = jnp.minimum(owner, N_EXPERTS - 1)
        nused_ref[...] = end[N_EXPERTS - 1:N_EXPERTS, :]


def _dest(eid, rank, cnt, *, tn, n_blocks):
    t = eid.shape[1]
    tok = lambda: pl.BlockSpec((MOE_TOPK, tn), lambda i: (0, i))
    return pl.pallas_call(
        functools.partial(_dest_kernel, n_blocks=n_blocks),
        grid=(t // tn,),
        in_specs=[tok(), tok(), pl.BlockSpec(cnt.shape, lambda i: (0, 0))],
        out_specs=[tok(), pl.BlockSpec((1, n_blocks), lambda i: (0, 0)),
                   pl.BlockSpec((1, LANES), lambda i: (0, 0))],
        out_shape=[jax.ShapeDtypeStruct((MOE_TOPK, t), I32),
                   jax.ShapeDtypeStruct((1, n_blocks), I32),
                   jax.ShapeDtypeStruct((1, LANES), I32)],
        compiler_params=_cparams("arbitrary"),
        name="dest",
    )(eid, rank, cnt)


def _dispatch_kernel(dest_ref, x_ref, xs_in_ref, xs_ref, sem, *, td):
    del xs_in_ref

    def row_copy(t, d):
        return pltpu.make_async_copy(x_ref.at[pl.ds(t, 1), :], xs_ref.at[pl.ds(d, 1), :], sem)

    def issue(t, carry):
        for k in range(MOE_TOPK):
            row_copy(t, dest_ref[k, t]).start()
        return carry

    lax.fori_loop(0, td, issue, 0)

    def drain(t, carry):
        for k in range(MOE_TOPK):
            row_copy(t, dest_ref[k, t]).wait()
        return carry

    lax.fori_loop(0, td, drain, 0)


def _dispatch(dest, x1, xs_zero, *, td):
    t = x1.shape[0]
    return pl.pallas_call(
        functools.partial(_dispatch_kernel, td=td),
        grid=(t // td,),
        in_specs=[pl.BlockSpec((MOE_TOPK, td), lambda i: (0, i), memory_space=pltpu.SMEM),
                  pl.BlockSpec((td, D_MODEL), lambda i: (i, 0)),
                  pl.BlockSpec(memory_space=pl.ANY)],
        out_specs=pl.BlockSpec(memory_space=pl.ANY),
        out_shape=jax.ShapeDtypeStruct(xs_zero.shape, xs_zero.dtype),
        scratch_shapes=[pltpu.SemaphoreType.DMA(())],
        input_output_aliases={2: 0},
        compiler_params=_cparams("arbitrary"),
        name="dispatch",
    )(dest, x1, xs_zero)


def _expert_kernel(bexp_ref, nused_ref, xs_ref, w1_ref, w3_ref, w2_ref, y_ref):
    del bexp_ref
    i = pl.program_id(0)

    @pl.when(i < nused_ref[0])
    def _():
        xb = xs_ref[...].astype(BF)
        h1 = jnp.dot(xb, w1_ref[0], preferred_element_type=F32)
        h3 = jnp.dot(xb, w3_ref[0], preferred_element_type=F32)
        hid = (_silu(h1) * h3).astype(BF)
        y_ref[...] = jnp.dot(hid, w2_ref[0], preferred_element_type=F32)

    @pl.when(i >= nused_ref[0])
    def _():
        y_ref[...] = jnp.zeros_like(y_ref)


def _expert(bexp, nused, xs, w1, w3, w2):
    cap = xs.shape[0]
    nb = cap // EXPERT_BLOCK
    last = lambda i, nu: jnp.minimum(i, nu[0] - 1)
    grid_spec = pltpu.PrefetchScalarGridSpec(
        num_scalar_prefetch=2,
        grid=(nb,),
        in_specs=[pl.BlockSpec((EXPERT_BLOCK, D_MODEL), lambda i, be, nu: (last(i, nu), 0)),
                  pl.BlockSpec((1, D_MODEL, MOE_FF), lambda i, be, nu: (be[last(i, nu)], 0, 0)),
                  pl.BlockSpec((1, D_MODEL, MOE_FF), lambda i, be, nu: (be[last(i, nu)], 0, 0)),
                  pl.BlockSpec((1, MOE_FF, D_MODEL), lambda i, be, nu: (be[last(i, nu)], 0, 0))],
        out_specs=pl.BlockSpec((EXPERT_BLOCK, D_MODEL), lambda i, be, nu: (i, 0)),
    )
    return pl.pallas_call(
        _expert_kernel,
        grid_spec=grid_spec,
        out_shape=jax.ShapeDtypeStruct((cap, D_MODEL), F32),
        compiler_params=_cparams("arbitrary"),
        name="expert",
    )(bexp, nused, xs, w1, w3, w2)


def _combine_kernel(dest_ref, x_ref, w_ref, lg_ref, lb_ref, yb_ref, o_ref, buf_ref, sem, *, tc):
    def row_copy(t, k):
        return pltpu.make_async_copy(yb_ref.at[pl.ds(dest_ref[k, t], 1), :],
                                     buf_ref.at[k, pl.ds(t, 1), :], sem)

    def issue(t, carry):
        for k in range(MOE_TOPK):
            row_copy(t, k).start()
        return carry

    lax.fori_loop(0, tc, issue, 0)

    def drain(t, carry):
        for k in range(MOE_TOPK):
            row_copy(t, k).wait()
        return carry

    lax.fori_loop(0, tc, drain, 0)
    ffn = w_ref[:, 0:1] * buf_ref[0] + w_ref[:, 1:2] * buf_ref[1]
    o_ref[...] = _layer_norm(DN_ALPHA * x_ref[...] + ffn, lg_ref[...], lb_ref[...])


def _combine(dest, x1, wts_t, lg, lb, yb, *, tc):
    t = x1.shape[0]
    full = lambda a: pl.BlockSpec(a.shape, lambda i: (0,) * a.ndim)
    return pl.pallas_call(
        functools.partial(_combine_kernel, tc=tc),
        grid=(t // tc,),
        in_specs=[pl.BlockSpec((MOE_TOPK, tc), lambda i: (0, i), memory_space=pltpu.SMEM),
                  pl.BlockSpec((tc, D_MODEL), lambda i: (i, 0)),
                  pl.BlockSpec((tc, MOE_TOPK), lambda i: (i, 0)),
                  full(lg), full(lb),
                  pl.BlockSpec(memory_space=pl.ANY)],
        out_specs=pl.BlockSpec((tc, D_MODEL), lambda i: (i, 0)),
        out_shape=jax.ShapeDtypeStruct((t, D_MODEL), F32),
        scratch_shapes=[pltpu.VMEM((MOE_TOPK, tc, D_MODEL), F32), pltpu.SemaphoreType.DMA(())],
        compiler_params=_cparams("arbitrary"),
        name="combine",
    )(dest, x1, wts_t, lg, lb, yb)


def _tile(n, want):
    while n % want:
        want //= 2
    return want


def _mixer(x2d, b, l, w_in, rel_bias, conv_w, a_log, dt_bias, norm_g, pool_w, pool_scale,
           w_br_a, w_br_b, w_br_c, gate_b, w_out, ln_g, ln_b, wr, rb):
    t = x2d.shape[0]
    c0 = 3 * A_WIDTH
    c1 = c0 + 3 * B_WIDTH
    c2 = c1 + 2 * B_HEADS
    c3 = c2 + B_WIDTH
    c4 = c3 + C_WIDTH
    wbf = w_in.astype(BF)
    wbd = jnp.pad(wbf[:, c1:c2], ((0, 0), (0, LANES - 2 * B_HEADS)))
    qkva, qkvb, bd, go, oc = _proj(
        x2d, wbf[:, :c0], wbf[:, c0:c1], wbd, wbf[:, c2:c3], wbf[:, c3:c4],
        pool_w.astype(BF), pool_scale.reshape(1, C_WIDTH), seq=l, tm=_tile(l, 512))

    oa = _attn(qkva.reshape(b, l, 3 * A_WIDTH), _attn_bias_table(rel_bias)).reshape(t, A_WIDTH)

    lane_pad = lambda v: jnp.pad(v.reshape(1, B_HEADS), ((0, 0), (B_HEADS, LANES - 2 * B_HEADS)))
    ob = _gdn(qkvb.reshape(b, l, 3 * B_WIDTH), bd.reshape(b, l, LANES), go.reshape(b, l, B_WIDTH),
              conv_w, lane_pad(a_log), lane_pad(dt_bias), norm_g.reshape(1, B_HEAD_DIM)).reshape(t, B_WIDTH)

    wrh, wrl = _split_bf16(wr)
    return _merge(x2d, oa, ob, oc, wbf[:, c4:], gate_b, w_br_a.astype(BF), w_br_b.astype(BF),
                  w_br_c.astype(BF), w_out.astype(BF), ln_g.reshape(1, -1), ln_b.reshape(1, -1),
                  wrh, wrl, rb, tm=_tile(t, 512))


def _moe(x1, lgt, w1, w3, w2, ln_g, ln_b):
    t = x1.shape[0]
    n_blocks = -(-t * MOE_TOPK // EXPERT_BLOCK) + N_EXPERTS
    cap = n_blocks * EXPERT_BLOCK
    eid, wts, rank, cnt = _route(lgt, tn=_tile(t, 512))
    dest, bexp, nused = _dest(eid, rank, cnt, tn=_tile(t, 2048), n_blocks=n_blocks)
    xs = _dispatch(dest, x1, jnp.zeros((cap, D_MODEL), F32), td=_tile(t, 1024))
    yb = _expert(bexp.reshape(n_blocks), nused[0, :1], xs, w1.astype(BF), w3.astype(BF), w2.astype(BF))
    return _combine(dest, x1, wts.T, ln_g.reshape(1, -1), ln_b.reshape(1, -1), yb, tc=_tile(t, 512))


def kernel(x, w_in, attn_rel_bias, gdn_conv_w, gdn_a_log, gdn_dt_bias, gdn_norm_g, pool_w, pool_scale,
           w_branch_a, w_branch_b, w_branch_c, gate_b, w_out, ln1_g, ln1_b, router_group_w,
           router_group_b, router_expert_w, router_expert_b, moe_w1, moe_w3, moe_w2, ln2_g, ln2_b):
    b, l, d = x.shape
    x2d = x.reshape(b * l, d)
    for i in range(DEPTH):
        wr = jnp.concatenate([router_expert_w[i].T, router_group_w[i].T,
                              jnp.zeros((ROUTER_ROWS - N_EXPERTS - MOE_GROUPS, d), F32)], axis=0)
        rb = jnp.concatenate([router_expert_b[i], router_group_b[i],
                              jnp.zeros((ROUTER_ROWS - N_EXPERTS - MOE_GROUPS,), F32)]).reshape(ROUTER_ROWS, 1)
        x1, lgt = _mixer(x2d, b, l, w_in[i], attn_rel_bias[i], gdn_conv_w[i], gdn_a_log[i], gdn_dt_bias[i],
                         gdn_norm_g[i], pool_w[i], pool_scale[i], w_branch_a[i], w_branch_b[i],
                         w_branch_c[i], gate_b[i], w_out[i], ln1_g[i], ln1_b[i], wr, rb)
        x2d = _moe(x1, lgt, moe_w1[i], moe_w3[i], moe_w2[i], ln2_g[i], ln2_b[i])
    return x2d.reshape(b, l, d)
```

```python
import functools

import jax
import jax.numpy as jnp
from jax import lax
from jax.experimental import pallas as pl
from jax.experimental.pallas import tpu as pltpu

BF = jnp.bfloat16
F32 = jnp.float32
I32 = jnp.int32

D_MODEL = 1024
DEPTH = 2
CHUNK = 64

A_HEADS = 8
A_HEAD_DIM = 64
A_WIDTH = A_HEADS * A_HEAD_DIM
A_LEFT_CHUNKS = 8
A_REL_CLIP = 256

B_HEADS = 4
B_HEAD_DIM = 128
B_WIDTH = B_HEADS * B_HEAD_DIM
B_CONV = 4

C_WINDOWS = (2, 4, 8, 16)
C_GROUPS = 4
C_GROUP_DIM = 128
C_WIDTH = C_GROUPS * C_GROUP_DIM

N_BRANCH = 3
MOE_GROUPS = 4
MOE_PER_GROUP = 8
N_EXPERTS = MOE_GROUPS * MOE_PER_GROUP
MOE_TOPK = 2
MOE_FF = 512

DN_ALPHA = (2 * DEPTH) ** 0.25
LN_EPS = 1e-5
RMS_EPS = 1e-6

LANES = 128
SUBLANES = 8
VMEM_LIMIT_BYTES = 56 * 1024 * 1024

QBLOCK = 2 * CHUNK
KBAND = (A_LEFT_CHUNKS + 2) * CHUNK
NEG_BIG = -1e30

ROUTER_ROWS = 40
EXPERT_BLOCK = 512
DMA_ISSUE_UNROLL = 8


def _cparams(*sem):
    return pltpu.CompilerParams(dimension_semantics=sem, vmem_limit_bytes=VMEM_LIMIT_BYTES)


def _mm(a, b):
    return jnp.dot(a.astype(BF), b.astype(BF), preferred_element_type=F32)


def _mm_nt(a, b):
    return lax.dot_general(a.astype(BF), b.astype(BF), (((1,), (1,)), ((), ())),
                           preferred_element_type=F32)


def _mm_tn(a, b):
    return lax.dot_general(a.astype(BF), b.astype(BF), (((0,), (0,)), ((), ())),
                           preferred_element_type=F32)


def _split_bf16(a):
    hi = a.astype(BF)
    lo = (a - hi.astype(F32)).astype(BF)
    return hi, lo


def _mm3(a, b):
    ah, al = _split_bf16(a)
    bh, bl = _split_bf16(b)
    return (jnp.dot(ah, bh, preferred_element_type=F32)
            + (jnp.dot(ah, bl, preferred_element_type=F32)
               + jnp.dot(al, bh, preferred_element_type=F32)))


def _sigmoid(x):
    return 1.0 / (1.0 + jnp.exp(-x))


def _silu(x):
    return x * _sigmoid(x)


def _layer_norm(z, g, b):
    mu = jnp.mean(z, axis=-1, keepdims=True)
    zc = z - mu
    var = jnp.mean(zc * zc, axis=-1, keepdims=True)
    return zc * lax.rsqrt(var + LN_EPS) * g + b


POOL_TAIL = 16


def _proj_kernel(x_ref, wa_ref, wb_ref, wbd_ref, wgo_ref, wc_ref, pw_ref, ps_ref,
                 qkva_ref, qkvb_ref, bd_ref, go_ref, oc_ref, tail_ref, *, tiles_per_seq, tm):
    it = pl.program_id(0) % tiles_per_seq
    xb = x_ref[...].astype(BF)
    qkva_ref[...] = jnp.dot(xb, wa_ref[...], preferred_element_type=F32).astype(BF)
    qkvb_ref[...] = jnp.dot(xb, wb_ref[...], preferred_element_type=F32).astype(BF)
    bd_ref[...] = jnp.dot(xb, wbd_ref[...], preferred_element_type=F32)
    go_ref[...] = jnp.dot(xb, wgo_ref[...], preferred_element_type=F32).astype(BF)
    u = jnp.dot(xb, wc_ref[...], preferred_element_type=F32)

    @pl.when(it == 0)
    def _():
        tail_ref[...] = jnp.zeros_like(tail_ref)

    ext = jnp.concatenate([tail_ref[...], u], axis=0)
    tail_ref[...] = u[tm - POOL_TAIL:, :]
    pos = (it * tm + 1 + lax.broadcasted_iota(I32, (tm, 1), 0)).astype(F32)
    s = ext
    for gi, win in enumerate(C_WINDOWS):
        if gi:
            s = s[:, C_GROUP_DIM:]
        s = s + pltpu.roll(s, win // 2, 0)
        lo_c, hi_c = gi * C_GROUP_DIM, (gi + 1) * C_GROUP_DIM
        cnt = jnp.minimum(pos, float(win))
        pooled = s[POOL_TAIL:, :C_GROUP_DIM] / cnt - u[:, lo_c:hi_c]
        mixed = jnp.dot(pooled.astype(BF), pw_ref[gi], preferred_element_type=F32)
        oc_ref[:, lo_c:hi_c] = (mixed * ps_ref[:, lo_c:hi_c]).astype(BF)


def _proj(x2d, wa, wb, wbd, wgo, wc, pw, ps, *, seq, tm):
    t = x2d.shape[0]
    full = lambda a: pl.BlockSpec(a.shape, lambda i: (0,) * a.ndim)
    row = lambda n: pl.BlockSpec((tm, n), lambda i: (i, 0))
    return pl.pallas_call(
        functools.partial(_proj_kernel, tiles_per_seq=seq // tm, tm=tm),
        grid=(t // tm,),
        in_specs=[row(D_MODEL), full(wa), full(wb), full(wbd), full(wgo), full(wc), full(pw), full(ps)],
        out_specs=[row(3 * A_WIDTH), row(3 * B_WIDTH), row(LANES), row(B_WIDTH), row(C_WIDTH)],
        out_shape=[jax.ShapeDtypeStruct((t, 3 * A_WIDTH), BF),
                   jax.ShapeDtypeStruct((t, 3 * B_WIDTH), BF),
                   jax.ShapeDtypeStruct((t, LANES), F32),
                   jax.ShapeDtypeStruct((t, B_WIDTH), BF),
                   jax.ShapeDtypeStruct((t, C_WIDTH), BF)],
        scratch_shapes=[pltpu.VMEM((POOL_TAIL, C_WIDTH), F32)],
        compiler_params=_cparams("arbitrary"),
        name="proj",
    )(x2d, wa, wb, wbd, wgo, wc, pw, ps)


def _attn_bias_table(rel_bias):
    h = rel_bias.shape[0]
    pad = A_LEFT_CHUNKS * CHUNK
    width = KBAND + pad
    n = width + QBLOCK - 1
    far = jnp.repeat(rel_bias[:, -1:], QBLOCK - 1 + pad - A_REL_CLIP, axis=1)
    near = jnp.repeat(rel_bias[:, :1], n - far.shape[1] - rel_bias.shape[1], axis=1)
    w = jnp.concatenate([far, rel_bias[:, ::-1], near], axis=1).astype(F32)
    w = jnp.roll(w, -(QBLOCK - 1), axis=1)
    toep = jnp.tile(w, (1, QBLOCK))[:, :QBLOCK * (n - 1)].reshape(h, QBLOCK, n - 1)[:, :, :width]
    r = jnp.arange(QBLOCK)[:, None]
    m = jnp.arange(width)[None, :]
    kc, qc = m // CHUNK, r // CHUNK
    valid = (kc >= qc) & (kc <= qc + A_LEFT_CHUNKS)
    tab = jnp.where(valid[None], toep, NEG_BIG)
    return tab.reshape(h, QBLOCK, width // LANES, LANES).transpose(0, 2, 1, 3)


def _attn_kernel(q_ref, k_ref, v_ref, tab_ref, o_ref):
    j = pl.program_id(1)
    first_blk = A_LEFT_CHUNKS * CHUNK // QBLOCK
    kstart = pl.multiple_of(jnp.maximum(j - first_blk, 0) * QBLOCK, QBLOCK)
    offb = jnp.maximum(first_blk - j, 0)
    lane = lax.broadcasted_iota(I32, (QBLOCK, LANES), 1)
    lo = lane < A_HEAD_DIM
    nkb = KBAND // LANES
    for hp in range(A_HEADS // 2):
        cs = slice(hp * LANES, (hp + 1) * LANES)
        qp = q_ref[0, :, cs]
        kp = k_ref[0, pl.ds(kstart, KBAND), cs]
        vp = v_ref[0, pl.ds(kstart, KBAND), cs]
        outs = []
        for par in range(2):
            h = 2 * hp + par
            qm = jnp.where(lo if par == 0 else jnp.logical_not(lo), qp, jnp.zeros_like(qp))
            s = lax.dot_general(qm, kp, (((1,), (1,)), ((), ())), preferred_element_type=F32)
            bias = jnp.concatenate([tab_ref[h, offb + i] for i in range(nkb)], axis=1)
            s = s * (A_HEAD_DIM ** -0.5) + bias
            mx = jnp.max(s, axis=-1, keepdims=True)
            p = jnp.exp(s - mx)
            l = jnp.sum(p, axis=-1, keepdims=True)
            o = jnp.dot(p.astype(BF), vp, preferred_element_type=F32)
            outs.append(o / l)
        o_ref[0, :, cs] = jnp.where(lo, outs[0], outs[1]).astype(BF)


def _attn(qkva, tab):
    b, l, _ = qkva.shape
    return pl.pallas_call(
        _attn_kernel,
        grid=(b, l // QBLOCK),
        in_specs=[pl.BlockSpec((1, QBLOCK, A_WIDTH), lambda bi, j: (bi, j, 0)),
                  pl.BlockSpec((1, l, A_WIDTH), lambda bi, j: (bi, 0, 1)),
                  pl.BlockSpec((1, l, A_WIDTH), lambda bi, j: (bi, 0, 2)),
                  pl.BlockSpec(tab.shape, lambda bi, j: (0, 0, 0, 0))],
        out_specs=pl.BlockSpec((1, QBLOCK, A_WIDTH), lambda bi, j: (bi, j, 0)),
        out_shape=jax.ShapeDtypeStruct((b, l, A_WIDTH), BF),
        compiler_params=_cparams("arbitrary", "arbitrary"),
        name="attn",
    )(qkva, qkva, qkva, tab)


CONV_HIST = 16
GDN_PREP_UNROLL = 2
PACKW = B_HEADS * CHUNK


def _gdn_kernel(qkv_ref, bd_ref, go_ref, cw_ref, alog_ref, dtb_ref, ng_ref, o_ref,
                u_ref, wq_ref, qkkd_ref, egl_ref, state_ref, *, n_chunks):
    f32 = lambda m: m.astype(F32)
    r_p = lax.broadcasted_iota(I32, (CHUNK, PACKW), 0)
    c_p = lax.broadcasted_iota(I32, (CHUNK, PACKW), 1)
    cin_p = jnp.bitwise_and(c_p, CHUNK - 1)
    grp_p = jnp.right_shift(c_p, CHUNK.bit_length() - 1)
    tril_p = r_p >= cin_p
    strict_p = r_p > cin_p
    eye_p = r_p == cin_p
    eye_pf = f32(eye_p)
    rt = lax.broadcasted_iota(I32, (CHUNK, CHUNK), 0)
    ct = lax.broadcasted_iota(I32, (CHUNK, CHUNK), 1)
    trilf = f32(rt >= ct)
    rb = jnp.right_shift(lax.broadcasted_iota(I32, (PACKW, PACKW), 0), CHUNK.bit_length() - 1)
    blk_pp = rb == jnp.right_shift(lax.broadcasted_iota(I32, (PACKW, PACKW), 1), CHUNK.bit_length() - 1)
    rb = jnp.right_shift(lax.broadcasted_iota(I32, (PACKW, B_WIDTH), 0), CHUNK.bit_length() - 1)
    blk_pw = rb == jnp.right_shift(lax.broadcasted_iota(I32, (PACKW, B_WIDTH), 1),
                                   B_HEAD_DIM.bit_length() - 1)
    eye_d = (lax.broadcasted_iota(I32, (B_HEAD_DIM, B_HEAD_DIM), 0)
             == lax.broadcasted_iota(I32, (B_HEAD_DIM, B_HEAD_DIM), 1)).astype(BF)
    neg_rate = -jnp.exp(alog_ref[...])

    def stack4(a, mask):
        return jnp.where(mask, jnp.concatenate([a] * B_HEADS, axis=0), 0.0).astype(BF)

    def heads(fn):
        return jnp.concatenate([fn(h, slice(h * B_HEAD_DIM, (h + 1) * B_HEAD_DIM)) for h in range(B_HEADS)],
                               axis=1)

    def prep(c):
        r0 = pl.multiple_of(c * CHUNK, CHUNK)
        cur = qkv_ref[0, pl.ds(r0, CHUNK), :].astype(F32)
        h0 = pl.multiple_of(jnp.maximum(r0 - CONV_HIST, 0), CONV_HIST)
        hist = qkv_ref[0, pl.ds(h0, CONV_HIST), :].astype(F32)
        hist = jnp.where(c > 0, hist, 0.0)
        ext = jnp.concatenate([hist, cur], axis=0)
        acc = cur * cw_ref[B_CONV - 1:B_CONV, :]
        for back in range(1, B_CONV):
            tap = B_CONV - 1 - back
            acc = acc + pltpu.roll(ext, back, 0)[CONV_HIST:, :] * cw_ref[tap:tap + 1, :]
        act = _silu(acc)
        q_all, k_all, v_all = act[:, :B_WIDTH], act[:, B_WIDTH:2 * B_WIDTH], act[:, 2 * B_WIDTH:]

        bd = bd_ref[0, pl.ds(r0, CHUNK), :]
        beta_all = _sigmoid(bd)
        z = bd + dtb_ref[...]
        softplus = jnp.maximum(z, 0.0) + jnp.log1p(jnp.exp(-jnp.abs(z)))
        g_all = neg_rate * softplus
        gc_all = jnp.dot(trilf, g_all, preferred_element_type=F32,
                         precision=lax.Precision.HIGHEST)
        egc_all = jnp.exp(gc_all)
        glast = gc_all[CHUNK - 1:CHUNK, :]
        ekd_all = jnp.exp(glast - gc_all)
        egl = jnp.exp(glast)
        col = lambda m, h: m[:, B_HEADS + h:B_HEADS + h + 1]

        def l2n(m):
            return m * lax.rsqrt(jnp.sum(m * m, axis=-1, keepdims=True) + RMS_EPS)

        qn = heads(lambda h, hs: l2n(q_all[:, hs]) * (B_HEAD_DIM ** -0.5))
        kn = heads(lambda h, hs: l2n(k_all[:, hs]))
        kbeta = heads(lambda h, hs: kn[:, hs] * beta_all[:, h:h + 1])
        vbeta = heads(lambda h, hs: v_all[:, hs] * beta_all[:, h:h + 1])
        kbe = heads(lambda h, hs: kbeta[:, hs] * col(egc_all, h))
        qe = heads(lambda h, hs: qn[:, hs] * col(egc_all, h))
        kd_rows = jnp.concatenate([(kn[:, h * B_HEAD_DIM:(h + 1) * B_HEAD_DIM] * col(ekd_all, h)).astype(BF)
                                   for h in range(B_HEADS)], axis=0)

        gcc = jnp.zeros((CHUNK, PACKW), F32)
        for h in range(B_HEADS):
            gcc = jnp.where(grp_p == h, col(gc_all, h), gcc)
        gcr = jnp.sum(gcc * eye_pf, axis=0, keepdims=True)
        decay = jnp.exp(jnp.where(tril_p, gcc - gcr, -jnp.inf))

        lhs = jnp.concatenate([kbeta, qn], axis=0).astype(BF)
        aq = lax.dot_general(lhs, stack4(kn, blk_pw), (((1,), (1,)), ((), ())), preferred_element_type=F32)
        lower = jnp.where(strict_p, aq[:CHUNK] * decay, 0.0)
        qk = aq[CHUNK:] * decay
        pw = -lower
        tmat = eye_pf + pw
        pw = jnp.dot(pw.astype(BF), stack4(pw, blk_pp), preferred_element_type=F32)
        n_sq = CHUNK.bit_length() - 2
        for lvl in range(n_sq):
            if lvl + 1 < n_sq:
                both = jnp.dot(jnp.concatenate([tmat, pw], axis=0).astype(BF), stack4(pw, blk_pp),
                               preferred_element_type=F32)
                tmat, pw = tmat + both[:CHUNK], both[CHUNK:]
            else:
                tmat = tmat + jnp.dot(tmat.astype(BF), stack4(pw, blk_pp), preferred_element_type=F32)
        rhs = jnp.concatenate([stack4(vbeta, blk_pw), stack4(kbe, blk_pw)], axis=1)
        uw = jnp.dot(tmat.astype(BF), rhs, preferred_element_type=F32)
        kdt = lax.dot_general(eye_d, kd_rows, (((1,), (1,)), ((), ())), preferred_element_type=F32)

        u_ref[c] = uw[:, :B_WIDTH]
        wq_ref[c, :CHUNK, :] = uw[:, B_WIDTH:].astype(BF)
        wq_ref[c, CHUNK:, :] = qe.astype(BF)
        qkkd_ref[c, :CHUNK, :] = qk.astype(BF)
        qkkd_ref[c, CHUNK:, :] = kdt.astype(BF)
        egl_ref[c] = heads(lambda h, hs: jnp.broadcast_to(col(egl, h), (1, B_HEAD_DIM)))

    def prep_body(i, carry):
        for g in range(GDN_PREP_UNROLL):
            prep(i * GDN_PREP_UNROLL + g)
        return carry

    lax.fori_loop(0, n_chunks // GDN_PREP_UNROLL, prep_body, 0)

    state_ref[...] = jnp.zeros_like(state_ref)

    def step(c, carry):
        r0 = pl.multiple_of(c * CHUNK, CHUNK)
        st = state_ref[...]
        stb = st.astype(BF)
        ws, qs = [], []
        for h in range(B_HEADS):
            hs = slice(h * B_HEAD_DIM, (h + 1) * B_HEAD_DIM)
            r1 = jnp.dot(wq_ref[c, :, hs], stb[:, hs], preferred_element_type=F32)
            ws.append(r1[:CHUNK])
            qs.append(r1[CHUNK:])
        v_new = u_ref[c] - jnp.concatenate(ws, axis=1)
        r2 = jnp.dot(qkkd_ref[c], stack4(v_new, blk_pw), preferred_element_type=F32)
        o_all = jnp.concatenate(qs, axis=1) + r2[:CHUNK]
        state_ref[...] = st * egl_ref[c] + r2[CHUNK:]
        gate = go_ref[0, pl.ds(r0, CHUNK), :].astype(F32)
        o_n = heads(lambda h, hs: o_all[:, hs]
                    * lax.rsqrt(jnp.mean(o_all[:, hs] * o_all[:, hs], axis=-1, keepdims=True) + RMS_EPS)
                    * ng_ref[...])
        o_ref[0, pl.ds(r0, CHUNK), :] = (o_n * _silu(gate)).astype(BF)
        return carry

    lax.fori_loop(0, n_chunks, step, 0)


def _gdn(qkvb, bd, go, cw, alog, dtb, ng):
    b, l, _ = qkvb.shape
    nc = l // CHUNK
    full = lambda a: pl.BlockSpec(a.shape, lambda bi: (0,) * a.ndim)
    seq = lambda n: pl.BlockSpec((1, l, n), lambda bi: (bi, 0, 0))
    return pl.pallas_call(
        functools.partial(_gdn_kernel, n_chunks=nc),
        grid=(b,),
        in_specs=[seq(3 * B_WIDTH), seq(LANES), seq(B_WIDTH), full(cw), full(alog), full(dtb), full(ng)],
        out_specs=seq(B_WIDTH),
        out_shape=jax.ShapeDtypeStruct((b, l, B_WIDTH), BF),
        scratch_shapes=[pltpu.VMEM((nc, CHUNK, B_WIDTH), F32),
                        pltpu.VMEM((nc, 2 * CHUNK, B_WIDTH), BF),
                        pltpu.VMEM((nc, CHUNK + B_HEAD_DIM, PACKW), BF),
                        pltpu.VMEM((nc, 1, B_WIDTH), F32),
                        pltpu.VMEM((B_HEAD_DIM, B_WIDTH), F32)],
        compiler_params=_cparams("arbitrary"),
        name="gdn",
    )(qkvb, bd, go, cw, alog, dtb, ng)


def _merge_kernel(x_ref, oa_ref, ob_ref, oc_ref, wg_ref, gb_ref, wa_ref, wb_ref, wc_ref, wo_ref,
                  lg_ref, lb_ref, wrh_ref, wrl_ref, rb_ref, x1_ref, lgt_ref):
    x = x_ref[...]
    xb = x.astype(BF)
    y = None
    for i, (o_ref, w_ref) in enumerate(((oa_ref, wa_ref), (ob_ref, wb_ref), (oc_ref, wc_ref))):
        gl = jnp.dot(xb, wg_ref[:, i * D_MODEL:(i + 1) * D_MODEL], preferred_element_type=F32)
        gate = _sigmoid(gl + gb_ref[i:i + 1, :])
        br = gate * jnp.dot(o_ref[...], w_ref[...], preferred_element_type=F32)
        y = br if y is None else y + br
    mix = jnp.dot(y.astype(BF), wo_ref[...], preferred_element_type=F32)
    x1 = _layer_norm(DN_ALPHA * x + mix, lg_ref[...], lb_ref[...])
    x1_ref[...] = x1
    xh, xl = _split_bf16(x1)
    nt = lambda a, b: lax.dot_general(a, b, (((1,), (1,)), ((), ())), preferred_element_type=F32)
    lgt_ref[...] = nt(wrh_ref[...], xh) + (nt(wrh_ref[...], xl) + nt(wrl_ref[...], xh)) + rb_ref[...]


def _merge(x2d, oa, ob, oc, wg, gb, wa, wb, wc, wo, lg, lb, wrh, wrl, rb, *, tm):
    t = x2d.shape[0]
    full = lambda a: pl.BlockSpec(a.shape, lambda i: (0,) * a.ndim)
    row = lambda n: pl.BlockSpec((tm, n), lambda i: (i, 0))
    return pl.pallas_call(
        _merge_kernel,
        grid=(t // tm,),
        in_specs=[row(D_MODEL), row(A_WIDTH), row(B_WIDTH), row(C_WIDTH), full(wg), full(gb), full(wa),
                  full(wb), full(wc), full(wo), full(lg), full(lb), full(wrh), full(wrl), full(rb)],
        out_specs=[row(D_MODEL), pl.BlockSpec((ROUTER_ROWS, tm), lambda i: (0, i))],
        out_shape=[jax.ShapeDtypeStruct((t, D_MODEL), F32),
                   jax.ShapeDtypeStruct((ROUTER_ROWS, t), F32)],
        compiler_params=_cparams("arbitrary"),
        name="merge",
    )(x2d, oa, ob, oc, wg, gb, wa, wb, wc, wo, lg, lb, wrh, wrl, rb)


def _route_kernel(lgt_ref, eid_ref, wts_ref, rank_ref, cnt_ref, run_ref, *, tn):
    @pl.when(pl.program_id(0) == 0)
    def _():
        run_ref[...] = jnp.zeros_like(run_ref)

    le = lgt_ref[0:N_EXPERTS, :]
    lg = lgt_ref[N_EXPERTS:N_EXPERTS + MOE_GROUPS, :]
    gi = lax.broadcasted_iota(I32, (MOE_GROUPS, tn), 0).astype(F32)
    ei = lax.broadcasted_iota(I32, (N_EXPERTS, tn), 0).astype(F32)
    eg = jnp.right_shift(lax.broadcasted_iota(I32, (N_EXPERTS, tn), 0),
                         MOE_PER_GROUP.bit_length() - 1).astype(F32)
    mg = jnp.max(lg, axis=0, keepdims=True)
    gsel = jnp.min(jnp.where(lg == mg, gi, float(MOE_GROUPS)), axis=0, keepdims=True)
    p_top = 1.0 / jnp.sum(jnp.exp(lg - mg), axis=0, keepdims=True)
    l1 = jnp.where(eg == gsel, le, -jnp.inf)
    m1 = jnp.max(l1, axis=0, keepdims=True)
    i1 = jnp.min(jnp.where(l1 == m1, ei, float(N_EXPERTS)), axis=0, keepdims=True)
    l2 = jnp.where(ei == i1, -jnp.inf, l1)
    m2 = jnp.max(l2, axis=0, keepdims=True)
    i2 = jnp.min(jnp.where(l2 == m2, ei, float(N_EXPERTS)), axis=0, keepdims=True)
    e2 = jnp.exp(m2 - m1)
    den = 1.0 + e2
    eid_ref[...] = jnp.concatenate([i1, i2], axis=0).astype(I32)
    wts_ref[...] = jnp.concatenate([p_top / den, p_top * (e2 / den)], axis=0)

    before = (lax.broadcasted_iota(I32, (tn, tn), 0) < lax.broadcasted_iota(I32, (tn, tn), 1)).astype(BF)
    run = run_ref[...]
    ranks = []
    for ik in (i1, i2):
        onehot = (ei == ik)
        ohf = onehot.astype(F32)
        prefix = jnp.dot(ohf.astype(BF), before, preferred_element_type=F32)
        ranks.append(jnp.sum(jnp.where(onehot, prefix + run, 0.0), axis=0, keepdims=True))
        run = run + jnp.sum(ohf, axis=1, keepdims=True)
    run_ref[...] = run
    rank_ref[...] = jnp.concatenate(ranks, axis=0).astype(I32)
    cnt_ref[...] = jnp.broadcast_to(run, cnt_ref.shape)


def _route(lgt, *, tn):
    t = lgt.shape[1]
    tok = lambda: pl.BlockSpec((MOE_TOPK, tn), lambda i: (0, i))
    return pl.pallas_call(
        functools.partial(_route_kernel, tn=tn),
        grid=(t // tn,),
        in_specs=[pl.BlockSpec((ROUTER_ROWS, tn), lambda i: (0, i))],
        out_specs=[tok(), tok(), tok(), pl.BlockSpec((N_EXPERTS, LANES), lambda i: (0, 0))],
        out_shape=[jax.ShapeDtypeStruct((MOE_TOPK, t), I32),
                   jax.ShapeDtypeStruct((MOE_TOPK, t), F32),
                   jax.ShapeDtypeStruct((MOE_TOPK, t), I32),
                   jax.ShapeDtypeStruct((N_EXPERTS, LANES), F32)],
        scratch_shapes=[pltpu.VMEM((N_EXPERTS, 1), F32)],
        compiler_params=_cparams("arbitrary"),
        name="route",
    )(lgt)


def _dest_kernel(eid_ref, rank_ref, cnt_ref, dest_ref, bexp_ref, nused_ref, *, n_blocks):
    cnt = cnt_ref[...].astype(I32)
    shift = EXPERT_BLOCK.bit_length() - 1
    padded_blocks = jnp.right_shift(cnt + (EXPERT_BLOCK - 1), shift)
    er = lax.broadcasted_iota(I32, (N_EXPERTS, LANES), 0)
    start = jnp.zeros((N_EXPERTS, LANES), I32)
    for e in range(N_EXPERTS - 1):
        start = start + jnp.where(er > e, padded_blocks[e:e + 1, :], 0)
    end = start + padded_blocks
    eid = eid_ref[...]
    d = rank_ref[...]
    for e in range(N_EXPERTS):
        d = d + jnp.where(eid == e, start[e:e + 1, 0:1] * EXPERT_BLOCK, 0)
    dest_ref[...] = d

    @pl.when(pl.program_id(0) == 0)
    def _():
        blk = lax.broadcasted_iota(I32, (1, n_blocks), 1)
        owner = jnp.zeros((1, n_blocks), I32)
        for e in range(N_EXPERTS):
            owner = owner + (blk >= end[e:e + 1, 0:1]).astype(I32)
        bexp_ref[...] = jnp.minimum(owner, N_EXPERTS - 1)
        nused_ref[...] = end[N_EXPERTS - 1:N_EXPERTS, :]


def _dest(eid, rank, cnt, *, tn, n_blocks):
    t = eid.shape[1]
    tok = lambda: pl.BlockSpec((MOE_TOPK, tn), lambda i: (0, i))
    return pl.pallas_call(
        functools.partial(_dest_kernel, n_blocks=n_blocks),
        grid=(t // tn,),
        in_specs=[tok(), tok(), pl.BlockSpec(cnt.shape, lambda i: (0, 0))],
        out_specs=[tok(), pl.BlockSpec((1, n_blocks), lambda i: (0, 0)),
                   pl.BlockSpec((1, LANES), lambda i: (0, 0))],
        out_shape=[jax.ShapeDtypeStruct((MOE_TOPK, t), I32),
                   jax.ShapeDtypeStruct((1, n_blocks), I32),
                   jax.ShapeDtypeStruct((1, LANES), I32)],
        compiler_params=_cparams("arbitrary"),
        name="dest",
    )(eid, rank, cnt)


def _dispatch_kernel(dest_ref, x_ref, xs_in_ref, xs_ref, sem, *, td):
    del xs_in_ref

    def issue(t, carry):
        for k in range(MOE_TOPK):
            pltpu.make_async_copy(x_ref.at[pl.ds(t, 1), :], xs_ref.at[pl.ds(dest_ref[k, t], 1), :], sem).start()
        return carry

    lax.fori_loop(0, td, issue, 0, unroll=DMA_ISSUE_UNROLL)
    all_rows = xs_ref.at[pl.ds(0, MOE_TOPK * td), :]
    pltpu.make_async_copy(all_rows, all_rows, sem).wait()


def _dispatch(dest, x1, xs_zero, *, td):
    t = x1.shape[0]
    return pl.pallas_call(
        functools.partial(_dispatch_kernel, td=td),
        grid=(t // td,),
        in_specs=[pl.BlockSpec((MOE_TOPK, td), lambda i: (0, i), memory_space=pltpu.SMEM),
                  pl.BlockSpec((td, D_MODEL), lambda i: (i, 0)),
                  pl.BlockSpec(memory_space=pl.ANY)],
        out_specs=pl.BlockSpec(memory_space=pl.ANY),
        out_shape=jax.ShapeDtypeStruct(xs_zero.shape, xs_zero.dtype),
        scratch_shapes=[pltpu.SemaphoreType.DMA(())],
        input_output_aliases={2: 0},
        compiler_params=_cparams("arbitrary"),
        name="dispatch",
    )(dest, x1, xs_zero)


def _expert_kernel(bexp_ref, nused_ref, xs_ref, w1_ref, w3_ref, w2_ref, y_ref):
    del bexp_ref
    i = pl.program_id(0)

    @pl.when(i < nused_ref[0])
    def _():
        xb = xs_ref[...].astype(BF)
        h1 = jnp.dot(xb, w1_ref[0], preferred_element_type=F32)
        h3 = jnp.dot(xb, w3_ref[0], preferred_element_type=F32)
        hid = (_silu(h1) * h3).astype(BF)
        y_ref[...] = jnp.dot(hid, w2_ref[0], preferred_element_type=F32)

    @pl.when(i >= nused_ref[0])
    def _():
        y_ref[...] = jnp.zeros_like(y_ref)


def _expert(bexp, nused, xs, w1, w3, w2):
    cap = xs.shape[0]
    nb = cap // EXPERT_BLOCK
    last = lambda i, nu: jnp.minimum(i, nu[0] - 1)
    grid_spec = pltpu.PrefetchScalarGridSpec(
        num_scalar_prefetch=2,
        grid=(nb,),
        in_specs=[pl.BlockSpec((EXPERT_BLOCK, D_MODEL), lambda i, be, nu: (last(i, nu), 0)),
                  pl.BlockSpec((1, D_MODEL, MOE_FF), lambda i, be, nu: (be[last(i, nu)], 0, 0)),
                  pl.BlockSpec((1, D_MODEL, MOE_FF), lambda i, be, nu: (be[last(i, nu)], 0, 0)),
                  pl.BlockSpec((1, MOE_FF, D_MODEL), lambda i, be, nu: (be[last(i, nu)], 0, 0))],
        out_specs=pl.BlockSpec((EXPERT_BLOCK, D_MODEL), lambda i, be, nu: (i, 0)),
    )
    return pl.pallas_call(
        _expert_kernel,
        grid_spec=grid_spec,
        out_shape=jax.ShapeDtypeStruct((cap, D_MODEL), F32),
        compiler_params=_cparams("arbitrary"),
        name="expert",
    )(bexp, nused, xs, w1, w3, w2)


def _combine_kernel(dest_ref, x_ref, w_ref, lg_ref, lb_ref, yb_ref, o_ref, buf_ref, sem, *, tc):
    def issue(t, carry):
        for k in range(MOE_TOPK):
            pltpu.make_async_copy(yb_ref.at[pl.ds(dest_ref[k, t], 1), :],
                                  buf_ref.at[k, pl.ds(t, 1), :], sem).start()
        return carry

    lax.fori_loop(0, tc, issue, 0, unroll=DMA_ISSUE_UNROLL)
    pltpu.make_async_copy(buf_ref, buf_ref, sem).wait()
    ffn = w_ref[:, 0:1] * buf_ref[0] + w_ref[:, 1:2] * buf_ref[1]
    o_ref[...] = _layer_norm(DN_ALPHA * x_ref[...] + ffn, lg_ref[...], lb_ref[...])


def _combine(dest, x1, wts_t, lg, lb, yb, *, tc):
    t = x1.shape[0]
    full = lambda a: pl.BlockSpec(a.shape, lambda i: (0,) * a.ndim)
    return pl.pallas_call(
        functools.partial(_combine_kernel, tc=tc),
        grid=(t // tc,),
        in_specs=[pl.BlockSpec((MOE_TOPK, tc), lambda i: (0, i), memory_space=pltpu.SMEM),
                  pl.BlockSpec((tc, D_MODEL), lambda i: (i, 0)),
                  pl.BlockSpec((tc, MOE_TOPK), lambda i: (i, 0)),
                  full(lg), full(lb),
                  pl.BlockSpec(memory_space=pl.ANY)],
        out_specs=pl.BlockSpec((tc, D_MODEL), lambda i: (i, 0)),
        out_shape=jax.ShapeDtypeStruct((t, D_MODEL), F32),
        scratch_shapes=[pltpu.VMEM((MOE_TOPK, tc, D_MODEL), F32), pltpu.SemaphoreType.DMA(())],
        compiler_params=_cparams("arbitrary"),
        name="combine",
    )(dest, x1, wts_t, lg, lb, yb)


def _tile(n, want):
    while n % want:
        want //= 2
    return want


def _mixer(x2d, b, l, w_in, rel_bias, conv_w, a_log, dt_bias, norm_g, pool_w, pool_scale,
           w_br_a, w_br_b, w_br_c, gate_b, w_out, ln_g, ln_b, wr, rb):
    t = x2d.shape[0]
    c0 = 3 * A_WIDTH
    c1 = c0 + 3 * B_WIDTH
    c2 = c1 + 2 * B_HEADS
    c3 = c2 + B_WIDTH
    c4 = c3 + C_WIDTH
    wbf = w_in.astype(BF)
    wbd = jnp.pad(wbf[:, c1:c2], ((0, 0), (0, LANES - 2 * B_HEADS)))
    qkva, qkvb, bd, go, oc = _proj(
        x2d, wbf[:, :c0], wbf[:, c0:c1], wbd, wbf[:, c2:c3], wbf[:, c3:c4],
        pool_w.astype(BF), pool_scale.reshape(1, C_WIDTH), seq=l, tm=_tile(l, 512))

    oa = _attn(qkva.reshape(b, l, 3 * A_WIDTH), _attn_bias_table(rel_bias)).reshape(t, A_WIDTH)

    lane_pad = lambda v: jnp.pad(v.reshape(1, B_HEADS), ((0, 0), (B_HEADS, LANES - 2 * B_HEADS)))
    ob = _gdn(qkvb.reshape(b, l, 3 * B_WIDTH), bd.reshape(b, l, LANES), go.reshape(b, l, B_WIDTH),
              conv_w, lane_pad(a_log), lane_pad(dt_bias), norm_g.reshape(1, B_HEAD_DIM)).reshape(t, B_WIDTH)

    wrh, wrl = _split_bf16(wr)
    return _merge(x2d, oa, ob, oc, wbf[:, c4:], gate_b, w_br_a.astype(BF), w_br_b.astype(BF),
                  w_br_c.astype(BF), w_out.astype(BF), ln_g.reshape(1, -1), ln_b.reshape(1, -1),
                  wrh, wrl, rb, tm=_tile(t, 512))


def _moe(x1, lgt, w1, w3, w2, ln_g, ln_b):
    t = x1.shape[0]
    n_blocks = -(-t * MOE_TOPK // EXPERT_BLOCK) + N_EXPERTS
    cap = n_blocks * EXPERT_BLOCK
    eid, wts, rank, cnt = _route(lgt, tn=_tile(t, 512))
    dest, bexp, nused = _dest(eid, rank, cnt, tn=_tile(t, 2048), n_blocks=n_blocks)
    xs = _dispatch(dest, x1, jnp.zeros((cap, D_MODEL), F32), td=_tile(t, 1024))
    yb = _expert(bexp.reshape(n_blocks), nused[0, :1], xs, w1.astype(BF), w3.astype(BF), w2.astype(BF))
    return _combine(dest, x1, wts.T, ln_g.reshape(1, -1), ln_b.reshape(1, -1), yb, tc=_tile(t, 512))


def kernel(x, w_in, attn_rel_bias, gdn_conv_w, gdn_a_log, gdn_dt_bias, gdn_norm_g, pool_w, pool_scale,
           w_branch_a, w_branch_b, w_branch_c, gate_b, w_out, ln1_g, ln1_b, router_group_w,
           router_group_b, router_expert_w, router_expert_b, moe_w1, moe_w3, moe_w2, ln2_g, ln2_b):
    b, l, d = x.shape
    x2d = x.reshape(b * l, d)
    for i in range(DEPTH):
        wr = jnp.concatenate([router_expert_w[i].T, router_group_w[i].T,
                              jnp.zeros((ROUTER_ROWS - N_EXPERTS - MOE_GROUPS, d), F32)], axis=0)
        rb = jnp.concatenate([router_expert_b[i], router_group_b[i],
                              jnp.zeros((ROUTER_ROWS - N_EXPERTS - MOE_GROUPS,), F32)]).reshape(ROUTER_ROWS, 1)
        x1, lgt = _mixer(x2d, b, l, w_in[i], attn_rel_bias[i], gdn_conv_w[i], gdn_a_log[i], gdn_dt_bias[i],
                         gdn_norm_g[i], pool_w[i], pool_scale[i], w_branch_a[i], w_branch_b[i],
                         w_branch_c[i], gate_b[i], w_out[i], ln1_g[i], ln1_b[i], wr, rb)
        x2d = _moe(x1, lgt, moe_w1[i], moe_w3[i], moe_w2[i], ln2_g[i], ln2_b[i])
    return x2d.reshape(b, l, d)
```

```python
import functools

import jax
import jax.numpy as jnp
from jax import lax
from jax.experimental import pallas as pl
from jax.experimental.pallas import tpu as pltpu

BF = jnp.bfloat16
F32 = jnp.float32
I32 = jnp.int32

D_MODEL = 1024
DEPTH = 2
CHUNK = 64

A_HEADS = 8
A_HEAD_DIM = 64
A_WIDTH = A_HEADS * A_HEAD_DIM
A_LEFT_CHUNKS = 8
A_REL_CLIP = 256

B_HEADS = 4
B_HEAD_DIM = 128
B_WIDTH = B_HEADS * B_HEAD_DIM
B_CONV = 4

C_WINDOWS = (2, 4, 8, 16)
C_GROUPS = 4
C_GROUP_DIM = 128
C_WIDTH = C_GROUPS * C_GROUP_DIM

N_BRANCH = 3
MOE_GROUPS = 4
MOE_PER_GROUP = 8
N_EXPERTS = MOE_GROUPS * MOE_PER_GROUP
MOE_TOPK = 2
MOE_FF = 512

DN_ALPHA = (2 * DEPTH) ** 0.25
LN_EPS = 1e-5
RMS_EPS = 1e-6

LANES = 128
SUBLANES = 8
VMEM_LIMIT_BYTES = 56 * 1024 * 1024

QBLOCK = 2 * CHUNK
KBAND = (A_LEFT_CHUNKS + 2) * CHUNK
NEG_BIG = -1e30

ROUTER_ROWS = 40
EXPERT_BLOCK = 512
DMA_ISSUE_UNROLL = 8


def _cparams(*sem):
    return pltpu.CompilerParams(dimension_semantics=sem, vmem_limit_bytes=VMEM_LIMIT_BYTES)


def _mm(a, b):
    return jnp.dot(a.astype(BF), b.astype(BF), preferred_element_type=F32)


def _mm_nt(a, b):
    return lax.dot_general(a.astype(BF), b.astype(BF), (((1,), (1,)), ((), ())),
                           preferred_element_type=F32)


def _mm_tn(a, b):
    return lax.dot_general(a.astype(BF), b.astype(BF), (((0,), (0,)), ((), ())),
                           preferred_element_type=F32)


def _split_bf16(a):
    hi = a.astype(BF)
    lo = (a - hi.astype(F32)).astype(BF)
    return hi, lo


def _mm3(a, b):
    ah, al = _split_bf16(a)
    bh, bl = _split_bf16(b)
    return (jnp.dot(ah, bh, preferred_element_type=F32)
            + (jnp.dot(ah, bl, preferred_element_type=F32)
               + jnp.dot(al, bh, preferred_element_type=F32)))


def _sigmoid(x):
    return 1.0 / (1.0 + jnp.exp(-x))


def _silu(x):
    return x * _sigmoid(x)


def _layer_norm(z, g, b):
    mu = jnp.mean(z, axis=-1, keepdims=True)
    zc = z - mu
    var = jnp.mean(zc * zc, axis=-1, keepdims=True)
    return zc * lax.rsqrt(var + LN_EPS) * g + b


POOL_TAIL = 16


def _proj_kernel(x_ref, wa_ref, wb_ref, wbd_ref, wgo_ref, wc_ref, pw_ref, ps_ref,
                 qkva_ref, qkvb_ref, bd_ref, go_ref, oc_ref, tail_ref, *, tiles_per_seq, tm):
    it = pl.program_id(0) % tiles_per_seq
    xb = x_ref[...].astype(BF)
    qkva_ref[...] = jnp.dot(xb, wa_ref[...], preferred_element_type=F32).astype(BF)
    qkvb_ref[...] = jnp.dot(xb, wb_ref[...], preferred_element_type=F32).astype(BF)
    bd_ref[...] = jnp.dot(xb, wbd_ref[...], preferred_element_type=F32)
    go_ref[...] = jnp.dot(xb, wgo_ref[...], preferred_element_type=F32).astype(BF)
    u = jnp.dot(xb, wc_ref[...], preferred_element_type=F32)

    @pl.when(it == 0)
    def _():
        tail_ref[...] = jnp.zeros_like(tail_ref)

    ext = jnp.concatenate([tail_ref[...], u], axis=0)
    tail_ref[...] = u[tm - POOL_TAIL:, :]
    pos = (it * tm + 1 + lax.broadcasted_iota(I32, (tm, 1), 0)).astype(F32)
    s = ext
    for gi, win in enumerate(C_WINDOWS):
        if gi:
            s = s[:, C_GROUP_DIM:]
        s = s + pltpu.roll(s, win // 2, 0)
        lo_c, hi_c = gi * C_GROUP_DIM, (gi + 1) * C_GROUP_DIM
        cnt = jnp.minimum(pos, float(win))
        pooled = s[POOL_TAIL:, :C_GROUP_DIM] / cnt - u[:, lo_c:hi_c]
        mixed = jnp.dot(pooled.astype(BF), pw_ref[gi], preferred_element_type=F32)
        oc_ref[:, lo_c:hi_c] = (mixed * ps_ref[:, lo_c:hi_c]).astype(BF)


def _proj(x2d, wa, wb, wbd, wgo, wc, pw, ps, *, seq, tm):
    t = x2d.shape[0]
    full = lambda a: pl.BlockSpec(a.shape, lambda i: (0,) * a.ndim)
    row = lambda n: pl.BlockSpec((tm, n), lambda i: (i, 0))
    return pl.pallas_call(
        functools.partial(_proj_kernel, tiles_per_seq=seq // tm, tm=tm),
        grid=(t // tm,),
        in_specs=[row(D_MODEL), full(wa), full(wb), full(wbd), full(wgo), full(wc), full(pw), full(ps)],
        out_specs=[row(3 * A_WIDTH), row(3 * B_WIDTH), row(LANES), row(B_WIDTH), row(C_WIDTH)],
        out_shape=[jax.ShapeDtypeStruct((t, 3 * A_WIDTH), BF),
                   jax.ShapeDtypeStruct((t, 3 * B_WIDTH), BF),
                   jax.ShapeDtypeStruct((t, LANES), F32),
                   jax.ShapeDtypeStruct((t, B_WIDTH), BF),
                   jax.ShapeDtypeStruct((t, C_WIDTH), BF)],
        scratch_shapes=[pltpu.VMEM((POOL_TAIL, C_WIDTH), F32)],
        compiler_params=_cparams("arbitrary"),
        name="proj",
    )(x2d, wa, wb, wbd, wgo, wc, pw, ps)


def _attn_bias_table(rel_bias):
    h = rel_bias.shape[0]
    pad = A_LEFT_CHUNKS * CHUNK
    width = KBAND + pad
    n = width + QBLOCK - 1
    far = jnp.repeat(rel_bias[:, -1:], QBLOCK - 1 + pad - A_REL_CLIP, axis=1)
    near = jnp.repeat(rel_bias[:, :1], n - far.shape[1] - rel_bias.shape[1], axis=1)
    w = jnp.concatenate([far, rel_bias[:, ::-1], near], axis=1).astype(F32)
    w = jnp.roll(w, -(QBLOCK - 1), axis=1)
    toep = jnp.tile(w, (1, QBLOCK))[:, :QBLOCK * (n - 1)].reshape(h, QBLOCK, n - 1)[:, :, :width]
    r = jnp.arange(QBLOCK)[:, None]
    m = jnp.arange(width)[None, :]
    kc, qc = m // CHUNK, r // CHUNK
    valid = (kc >= qc) & (kc <= qc + A_LEFT_CHUNKS)
    tab = jnp.where(valid[None], toep, NEG_BIG)
    return tab.reshape(h, QBLOCK, width // LANES, LANES).transpose(0, 2, 1, 3)


def _attn_kernel(q_ref, k_ref, v_ref, tab_ref, o_ref):
    j = pl.program_id(1)
    first_blk = A_LEFT_CHUNKS * CHUNK // QBLOCK
    kstart = pl.multiple_of(jnp.maximum(j - first_blk, 0) * QBLOCK, QBLOCK)
    offb = jnp.maximum(first_blk - j, 0)
    lane = lax.broadcasted_iota(I32, (QBLOCK, LANES), 1)
    lo = lane < A_HEAD_DIM
    nkb = KBAND // LANES
    pairs = range(A_HEADS // 2)
    cols = [slice(hp * LANES, (hp + 1) * LANES) for hp in pairs]
    scores = []
    for hp in pairs:
        qp = q_ref[0, :, cols[hp]] * jnp.asarray(A_HEAD_DIM ** -0.5, BF)
        zero = jnp.zeros_like(qp)
        q2 = jnp.concatenate([jnp.where(lo, qp, zero), jnp.where(lo, zero, qp)], axis=0)
        kp = k_ref[0, pl.ds(kstart, KBAND), cols[hp]]
        scores.append(lax.dot_general(q2, kp, (((1,), (1,)), ((), ())), preferred_element_type=F32))
    probs, denoms = [], []
    for hp in pairs:
        bias = jnp.concatenate(
            [jnp.concatenate([tab_ref[2 * hp + par, offb + i] for i in range(nkb)], axis=1) for par in range(2)],
            axis=0)
        s = scores[hp] + bias
        p = jnp.exp(s - jnp.max(s, axis=-1, keepdims=True))
        denoms.append(jnp.sum(p, axis=-1, keepdims=True))
        probs.append(p.astype(BF))
    for hp in pairs:
        vp = v_ref[0, pl.ds(kstart, KBAND), cols[hp]]
        o2 = jnp.dot(probs[hp], vp, preferred_element_type=F32) / denoms[hp]
        o_ref[0, :, cols[hp]] = jnp.where(lo, o2[:QBLOCK], o2[QBLOCK:]).astype(BF)


def _attn(qkva, tab):
    b, l, _ = qkva.shape
    return pl.pallas_call(
        _attn_kernel,
        grid=(b, l // QBLOCK),
        in_specs=[pl.BlockSpec((1, QBLOCK, A_WIDTH), lambda bi, j: (bi, j, 0)),
                  pl.BlockSpec((1, l, A_WIDTH), lambda bi, j: (bi, 0, 1)),
                  pl.BlockSpec((1, l, A_WIDTH), lambda bi, j: (bi, 0, 2)),
                  pl.BlockSpec(tab.shape, lambda bi, j: (0, 0, 0, 0))],
        out_specs=pl.BlockSpec((1, QBLOCK, A_WIDTH), lambda bi, j: (bi, j, 0)),
        out_shape=jax.ShapeDtypeStruct((b, l, A_WIDTH), BF),
        compiler_params=_cparams("arbitrary", "arbitrary"),
        name="attn",
    )(qkva, qkva, qkva, tab)


CONV_HIST = 16
GDN_PREP_UNROLL = 2


def _gdn_kernel(qkv_ref, bd_ref, go_ref, cw_ref, alog_ref, dtb_ref, ng_ref, o_ref,
                u_ref, wq_ref, qkkd_ref, egl_ref, state_ref, *, n_chunks):
    rt = lax.broadcasted_iota(I32, (CHUNK, CHUNK), 0)
    ct = lax.broadcasted_iota(I32, (CHUNK, CHUNK), 1)
    tril = rt >= ct
    strict = rt > ct
    trilf = tril.astype(F32)
    eye_f = (rt == ct).astype(F32)
    eye_d = (lax.broadcasted_iota(I32, (B_HEAD_DIM, B_HEAD_DIM), 0)
             == lax.broadcasted_iota(I32, (B_HEAD_DIM, B_HEAD_DIM), 1)).astype(BF)
    neg_rate = -jnp.exp(alog_ref[...])

    def heads(fn):
        return jnp.concatenate([fn(h, slice(h * B_HEAD_DIM, (h + 1) * B_HEAD_DIM)) for h in range(B_HEADS)],
                               axis=1)

    def prep(c):
        r0 = pl.multiple_of(c * CHUNK, CHUNK)
        cur = qkv_ref[0, pl.ds(r0, CHUNK), :].astype(F32)
        h0 = pl.multiple_of(jnp.maximum(r0 - CONV_HIST, 0), CONV_HIST)
        hist = qkv_ref[0, pl.ds(h0, CONV_HIST), :].astype(F32)
        hist = jnp.where(c > 0, hist, 0.0)
        ext = jnp.concatenate([hist, cur], axis=0)
        acc = cur * cw_ref[B_CONV - 1:B_CONV, :]
        for back in range(1, B_CONV):
            tap = B_CONV - 1 - back
            acc = acc + pltpu.roll(ext, back, 0)[CONV_HIST:, :] * cw_ref[tap:tap + 1, :]
        act = _silu(acc)
        q_all, k_all, v_all = act[:, :B_WIDTH], act[:, B_WIDTH:2 * B_WIDTH], act[:, 2 * B_WIDTH:]

        bd = bd_ref[0, pl.ds(r0, CHUNK), :]
        beta_all = _sigmoid(bd)
        z = bd + dtb_ref[...]
        softplus = jnp.maximum(z, 0.0) + jnp.log1p(jnp.exp(-jnp.abs(z)))
        g_all = neg_rate * softplus
        gc_all = jnp.dot(trilf, g_all, preferred_element_type=F32,
                         precision=lax.Precision.HIGHEST)
        egc_all = jnp.exp(gc_all)
        glast = gc_all[CHUNK - 1:CHUNK, :]
        ekd_all = jnp.exp(glast - gc_all)
        egl = jnp.exp(glast)
        col = lambda m, h: m[:, B_HEADS + h:B_HEADS + h + 1]

        def l2n(m):
            return m * lax.rsqrt(jnp.sum(m * m, axis=-1, keepdims=True) + RMS_EPS)

        qn = heads(lambda h, hs: l2n(q_all[:, hs]) * (B_HEAD_DIM ** -0.5))
        kn = heads(lambda h, hs: l2n(k_all[:, hs]))
        kbeta = heads(lambda h, hs: kn[:, hs] * beta_all[:, h:h + 1])
        vbeta = heads(lambda h, hs: v_all[:, hs] * beta_all[:, h:h + 1])
        kbe = heads(lambda h, hs: kbeta[:, hs] * col(egc_all, h))
        qe = heads(lambda h, hs: qn[:, hs] * col(egc_all, h))
        wq_ref[c, CHUNK:, :] = qe.astype(BF)
        egl_ref[c] = heads(lambda h, hs: jnp.broadcast_to(col(egl, h), (1, B_HEAD_DIM)))
        chains = []
        for h in range(B_HEADS):
            hs = slice(h * B_HEAD_DIM, (h + 1) * B_HEAD_DIM)
            gcc = jnp.broadcast_to(col(gc_all, h), (CHUNK, CHUNK))
            gcr = jnp.sum(gcc * eye_f, axis=0, keepdims=True)
            chains.append(dict(
                c=c, h=h, hs=hs,
                decay=jnp.exp(jnp.where(tril, gcc - gcr, -jnp.inf)),
                kq=jnp.concatenate([kbeta[:, hs], qn[:, hs]], axis=0).astype(BF),
                kn=kn[:, hs].astype(BF),
                rhs=jnp.concatenate([vbeta[:, hs], kbe[:, hs]], axis=1).astype(BF),
                kd=(kn[:, hs] * col(ekd_all, h)).astype(BF)))
        return chains

    def prep_body(i, carry):
        ch = [x for g in range(GDN_PREP_UNROLL) for x in prep(i * GDN_PREP_UNROLL + g)]
        nt = lambda a, b: lax.dot_general(a, b, (((1,), (1,)), ((), ())), preferred_element_type=F32)
        mm = lambda a, b: jnp.dot(a, b, preferred_element_type=F32)
        aq = [nt(x["kq"], x["kn"]) for x in ch]
        kdt = [nt(eye_d, x["kd"]) for x in ch]
        for x, a, kt in zip(ch, aq, kdt):
            qkkd_ref[x["c"], x["h"], :CHUNK, :] = (a[CHUNK:] * x["decay"]).astype(BF)
            qkkd_ref[x["c"], x["h"], CHUNK:, :] = kt.astype(BF)
        pw = [-jnp.where(strict, a[:CHUNK] * x["decay"], 0.0) for x, a in zip(ch, aq)]
        tmat = [eye_f + p for p in pw]
        pwb = [p.astype(BF) for p in pw]
        pw = [mm(p, p) for p in pwb]
        n_sq = CHUNK.bit_length() - 2
        for lvl in range(n_sq):
            pwb = [p.astype(BF) for p in pw]
            if lvl + 1 < n_sq:
                both = [mm(jnp.concatenate([t.astype(BF), p], axis=0), p) for t, p in zip(tmat, pwb)]
                tmat = [t + r[:CHUNK] for t, r in zip(tmat, both)]
                pw = [r[CHUNK:] for r in both]
            else:
                tmat = [t + mm(t.astype(BF), p) for t, p in zip(tmat, pwb)]
        uw = [mm(t.astype(BF), x["rhs"]) for t, x in zip(tmat, ch)]
        for x, r in zip(ch, uw):
            u_ref[x["c"], :, x["hs"]] = r[:, :B_HEAD_DIM]
            wq_ref[x["c"], :CHUNK, x["hs"]] = r[:, B_HEAD_DIM:].astype(BF)
        return carry

    lax.fori_loop(0, n_chunks // GDN_PREP_UNROLL, prep_body, 0)

    state_ref[...] = jnp.zeros_like(state_ref)

    def step(c, carry):
        r0 = pl.multiple_of(c * CHUNK, CHUNK)
        st = state_ref[...]
        stb = st.astype(BF)
        hss = [slice(h * B_HEAD_DIM, (h + 1) * B_HEAD_DIM) for h in range(B_HEADS)]
        r1 = [jnp.dot(wq_ref[c, :, hs], stb[:, hs], preferred_element_type=F32) for hs in hss]
        v_new = [(u_ref[c, :, hs] - r[:CHUNK]).astype(BF) for hs, r in zip(hss, r1)]
        r2 = [jnp.dot(qkkd_ref[c, h], v_new[h], preferred_element_type=F32) for h in range(B_HEADS)]
        o_all = jnp.concatenate([a[CHUNK:] + b[:CHUNK] for a, b in zip(r1, r2)], axis=1)
        state_ref[...] = st * egl_ref[c] + jnp.concatenate([b[CHUNK:] for b in r2], axis=1)
        gate = go_ref[0, pl.ds(r0, CHUNK), :].astype(F32)
        o_n = heads(lambda h, hs: o_all[:, hs]
                    * lax.rsqrt(jnp.mean(o_all[:, hs] * o_all[:, hs], axis=-1, keepdims=True) + RMS_EPS)
                    * ng_ref[...])
        o_ref[0, pl.ds(r0, CHUNK), :] = (o_n * _silu(gate)).astype(BF)
        return carry

    lax.fori_loop(0, n_chunks, step, 0)


def _gdn(qkvb, bd, go, cw, alog, dtb, ng):
    b, l, _ = qkvb.shape
    nc = l // CHUNK
    full = lambda a: pl.BlockSpec(a.shape, lambda bi: (0,) * a.ndim)
    seq = lambda n: pl.BlockSpec((1, l, n), lambda bi: (bi, 0, 0))
    return pl.pallas_call(
        functools.partial(_gdn_kernel, n_chunks=nc),
        grid=(b,),
        in_specs=[seq(3 * B_WIDTH), seq(LANES), seq(B_WIDTH), full(cw), full(alog), full(dtb), full(ng)],
        out_specs=seq(B_WIDTH),
        out_shape=jax.ShapeDtypeStruct((b, l, B_WIDTH), BF),
        scratch_shapes=[pltpu.VMEM((nc, CHUNK, B_WIDTH), F32),
                        pltpu.VMEM((nc, 2 * CHUNK, B_WIDTH), BF),
                        pltpu.VMEM((nc, B_HEADS, CHUNK + B_HEAD_DIM, CHUNK), BF),
                        pltpu.VMEM((nc, 1, B_WIDTH), F32),
                        pltpu.VMEM((B_HEAD_DIM, B_WIDTH), F32)],
        compiler_params=_cparams("arbitrary"),
        name="gdn",
    )(qkvb, bd, go, cw, alog, dtb, ng)


def _merge_kernel(x_ref, oa_ref, ob_ref, oc_ref, wg_ref, gb_ref, wa_ref, wb_ref, wc_ref, wo_ref,
                  lg_ref, lb_ref, wrh_ref, wrl_ref, rb_ref, x1_ref, lgt_ref):
    x = x_ref[...]
    xb = x.astype(BF)
    y = None
    for i, (o_ref, w_ref) in enumerate(((oa_ref, wa_ref), (ob_ref, wb_ref), (oc_ref, wc_ref))):
        gl = jnp.dot(xb, wg_ref[:, i * D_MODEL:(i + 1) * D_MODEL], preferred_element_type=F32)
        gate = _sigmoid(gl + gb_ref[i:i + 1, :])
        br = gate * jnp.dot(o_ref[...], w_ref[...], preferred_element_type=F32)
        y = br if y is None else y + br
    mix = jnp.dot(y.astype(BF), wo_ref[...], preferred_element_type=F32)
    x1 = _layer_norm(DN_ALPHA * x + mix, lg_ref[...], lb_ref[...])
    x1_ref[...] = x1
    xh, xl = _split_bf16(x1)
    nt = lambda a, b: lax.dot_general(a, b, (((1,), (1,)), ((), ())), preferred_element_type=F32)
    lgt_ref[...] = nt(wrh_ref[...], xh) + (nt(wrh_ref[...], xl) + nt(wrl_ref[...], xh)) + rb_ref[...]


def _merge(x2d, oa, ob, oc, wg, gb, wa, wb, wc, wo, lg, lb, wrh, wrl, rb, *, tm):
    t = x2d.shape[0]
    full = lambda a: pl.BlockSpec(a.shape, lambda i: (0,) * a.ndim)
    row = lambda n: pl.BlockSpec((tm, n), lambda i: (i, 0))
    return pl.pallas_call(
        _merge_kernel,
        grid=(t // tm,),
        in_specs=[row(D_MODEL), row(A_WIDTH), row(B_WIDTH), row(C_WIDTH), full(wg), full(gb), full(wa),
                  full(wb), full(wc), full(wo), full(lg), full(lb), full(wrh), full(wrl), full(rb)],
        out_specs=[row(D_MODEL), pl.BlockSpec((ROUTER_ROWS, tm), lambda i: (0, i))],
        out_shape=[jax.ShapeDtypeStruct((t, D_MODEL), F32),
                   jax.ShapeDtypeStruct((ROUTER_ROWS, t), F32)],
        compiler_params=_cparams("arbitrary"),
        name="merge",
    )(x2d, oa, ob, oc, wg, gb, wa, wb, wc, wo, lg, lb, wrh, wrl, rb)


def _route_kernel(lgt_ref, eid_ref, wts_ref, rank_ref, cnt_ref, run_ref, *, tn):
    @pl.when(pl.program_id(0) == 0)
    def _():
        run_ref[...] = jnp.zeros_like(run_ref)

    le = lgt_ref[0:N_EXPERTS, :]
    lg = lgt_ref[N_EXPERTS:N_EXPERTS + MOE_GROUPS, :]
    gi = lax.broadcasted_iota(I32, (MOE_GROUPS, tn), 0).astype(F32)
    ei = lax.broadcasted_iota(I32, (N_EXPERTS, tn), 0).astype(F32)
    eg = jnp.right_shift(lax.broadcasted_iota(I32, (N_EXPERTS, tn), 0),
                         MOE_PER_GROUP.bit_length() - 1).astype(F32)
    mg = jnp.max(lg, axis=0, keepdims=True)
    gsel = jnp.min(jnp.where(lg == mg, gi, float(MOE_GROUPS)), axis=0, keepdims=True)
    p_top = 1.0 / jnp.sum(jnp.exp(lg - mg), axis=0, keepdims=True)
    l1 = jnp.where(eg == gsel, le, -jnp.inf)
    m1 = jnp.max(l1, axis=0, keepdims=True)
    i1 = jnp.min(jnp.where(l1 == m1, ei, float(N_EXPERTS)), axis=0, keepdims=True)
    l2 = jnp.where(ei == i1, -jnp.inf, l1)
    m2 = jnp.max(l2, axis=0, keepdims=True)
    i2 = jnp.min(jnp.where(l2 == m2, ei, float(N_EXPERTS)), axis=0, keepdims=True)
    e2 = jnp.exp(m2 - m1)
    den = 1.0 + e2
    eid_ref[...] = jnp.concatenate([i1, i2], axis=0).astype(I32)
    wts_ref[...] = jnp.concatenate([p_top / den, p_top * (e2 / den)], axis=0)

    before = (lax.broadcasted_iota(I32, (tn, tn), 0) < lax.broadcasted_iota(I32, (tn, tn), 1)).astype(BF)
    run = run_ref[...]
    ranks = []
    for ik in (i1, i2):
        onehot = (ei == ik)
        ohf = onehot.astype(F32)
        prefix = jnp.dot(ohf.astype(BF), before, preferred_element_type=F32)
        ranks.append(jnp.sum(jnp.where(onehot, prefix + run, 0.0), axis=0, keepdims=True))
        run = run + jnp.sum(ohf, axis=1, keepdims=True)
    run_ref[...] = run
    rank_ref[...] = jnp.concatenate(ranks, axis=0).astype(I32)
    cnt_ref[...] = jnp.broadcast_to(run, cnt_ref.shape)


def _route(lgt, *, tn):
    t = lgt.shape[1]
    tok = lambda: pl.BlockSpec((MOE_TOPK, tn), lambda i: (0, i))
    return pl.pallas_call(
        functools.partial(_route_kernel, tn=tn),
        grid=(t // tn,),
        in_specs=[pl.BlockSpec((ROUTER_ROWS, tn), lambda i: (0, i))],
        out_specs=[tok(), tok(), tok(), pl.BlockSpec((N_EXPERTS, LANES), lambda i: (0, 0))],
        out_shape=[jax.ShapeDtypeStruct((MOE_TOPK, t), I32),
                   jax.ShapeDtypeStruct((MOE_TOPK, t), F32),
                   jax.ShapeDtypeStruct((MOE_TOPK, t), I32),
                   jax.ShapeDtypeStruct((N_EXPERTS, LANES), F32)],
        scratch_shapes=[pltpu.VMEM((N_EXPERTS, 1), F32)],
        compiler_params=_cparams("arbitrary"),
        name="route",
    )(lgt)


def _dest_kernel(eid_ref, rank_ref, cnt_ref, dest_ref, bexp_ref, nused_ref, *, n_blocks):
    cnt = cnt_ref[...].astype(I32)
    shift = EXPERT_BLOCK.bit_length() - 1
    padded_blocks = jnp.right_shift(cnt + (EXPERT_BLOCK - 1), shift)
    er = lax.broadcasted_iota(I32, (N_EXPERTS, LANES), 0)
    start = jnp.zeros((N_EXPERTS, LANES), I32)
    for e in range(N_EXPERTS - 1):
        start = start + jnp.where(er > e, padded_blocks[e:e + 1, :], 0)
    end = start + padded_blocks
    eid = eid_ref[...]
    d = rank_ref[...]
    for e in range(N_EXPERTS):
        d = d + jnp.where(eid == e, start[e:e + 1, 0:1] * EXPERT_BLOCK, 0)
    dest_ref[...] = d

    @pl.when(pl.program_id(0) == 0)
    def _():
        blk = lax.broadcasted_iota(I32, (1, n_blocks), 1)
        owner = jnp.zeros((1, n_blocks), I32)
        for e in range(N_EXPERTS):
            owner = owner + (blk >= end[e:e + 1, 0:1]).astype(I32)
        bexp_ref[...] = jnp.minimum(owner, N_EXPERTS - 1)
        nused_ref[...] = end[N_EXPERTS - 1:N_EXPERTS, :]


def _dest(eid, rank, cnt, *, tn, n_blocks):
    t = eid.shape[1]
    tok = lambda: pl.BlockSpec((MOE_TOPK, tn), lambda i: (0, i))
    return pl.pallas_call(
        functools.partial(_dest_kernel, n_blocks=n_blocks),
        grid=(t // tn,),
        in_specs=[tok(), tok(), pl.BlockSpec(cnt.shape, lambda i: (0, 0))],
        out_specs=[tok(), pl.BlockSpec((1, n_blocks), lambda i: (0, 0)),
                   pl.BlockSpec((1, LANES), lambda i: (0, 0))],
        out_shape=[jax.ShapeDtypeStruct((MOE_TOPK, t), I32),
                   jax.ShapeDtypeStruct((1, n_blocks), I32),
                   jax.ShapeDtypeStruct((1, LANES), I32)],
        compiler_params=_cparams("arbitrary"),
        name="dest",
    )(eid, rank, cnt)


def _dispatch_kernel(dest_ref, x_ref, xs_in_ref, xs_ref, sem, *, td):
    del xs_in_ref

    def issue(t, carry):
        for k in range(MOE_TOPK):
            pltpu.make_async_copy(x_ref.at[pl.ds(t, 1), :], xs_ref.at[pl.ds(dest_ref[k, t], 1), :], sem).start()
        return carry

    lax.fori_loop(0, td, issue, 0, unroll=DMA_ISSUE_UNROLL)
    all_rows = xs_ref.at[pl.ds(0, MOE_TOPK * td), :]
    pltpu.make_async_copy(all_rows, all_rows, sem).wait()


def _dispatch(dest, x1, xs_zero, *, td):
    t = x1.shape[0]
    return pl.pallas_call(
        functools.partial(_dispatch_kernel, td=td),
        grid=(t // td,),
        in_specs=[pl.BlockSpec((MOE_TOPK, td), lambda i: (0, i), memory_space=pltpu.SMEM),
                  pl.BlockSpec((td, D_MODEL), lambda i: (i, 0)),
                  pl.BlockSpec(memory_space=pl.ANY)],
        out_specs=pl.BlockSpec(memory_space=pl.ANY),
        out_shape=jax.ShapeDtypeStruct(xs_zero.shape, xs_zero.dtype),
        scratch_shapes=[pltpu.SemaphoreType.DMA(())],
        input_output_aliases={2: 0},
        compiler_params=_cparams("arbitrary"),
        name="dispatch",
    )(dest, x1, xs_zero)


def _expert_kernel(bexp_ref, nused_ref, xs_ref, w1_ref, w3_ref, w2_ref, y_ref):
    del bexp_ref
    i = pl.program_id(0)

    @pl.when(i < nused_ref[0])
    def _():
        xb = xs_ref[...].astype(BF)
        h1 = jnp.dot(xb, w1_ref[0], preferred_element_type=F32)
        h3 = jnp.dot(xb, w3_ref[0], preferred_element_type=F32)
        hid = (_silu(h1) * h3).astype(BF)
        y_ref[...] = jnp.dot(hid, w2_ref[0], preferred_element_type=F32)

    @pl.when(i >= nused_ref[0])
    def _():
        y_ref[...] = jnp.zeros_like(y_ref)


def _expert(bexp, nused, xs, w1, w3, w2):
    cap = xs.shape[0]
    nb = cap // EXPERT_BLOCK
    last = lambda i, nu: jnp.minimum(i, nu[0] - 1)
    grid_spec = pltpu.PrefetchScalarGridSpec(
        num_scalar_prefetch=2,
        grid=(nb,),
        in_specs=[pl.BlockSpec((EXPERT_BLOCK, D_MODEL), lambda i, be, nu: (last(i, nu), 0)),
                  pl.BlockSpec((1, D_MODEL, MOE_FF), lambda i, be, nu: (be[last(i, nu)], 0, 0)),
                  pl.BlockSpec((1, D_MODEL, MOE_FF), lambda i, be, nu: (be[last(i, nu)], 0, 0)),
                  pl.BlockSpec((1, MOE_FF, D_MODEL), lambda i, be, nu: (be[last(i, nu)], 0, 0))],
        out_specs=pl.BlockSpec((EXPERT_BLOCK, D_MODEL), lambda i, be, nu: (i, 0)),
    )
    return pl.pallas_call(
        _expert_kernel,
        grid_spec=grid_spec,
        out_shape=jax.ShapeDtypeStruct((cap, D_MODEL), F32),
        compiler_params=_cparams("arbitrary"),
        name="expert",
    )(bexp, nused, xs, w1, w3, w2)


def _combine_kernel(dest_ref, x_ref, w_ref, lg_ref, lb_ref, yb_ref, o_ref, buf_ref, sem, *, tc):
    def issue(t, carry):
        for k in range(MOE_TOPK):
            pltpu.make_async_copy(yb_ref.at[pl.ds(dest_ref[k, t], 1), :],
                                  buf_ref.at[k, pl.ds(t, 1), :], sem).start()
        return carry

    lax.fori_loop(0, tc, issue, 0, unroll=DMA_ISSUE_UNROLL)
    pltpu.make_async_copy(buf_ref, buf_ref, sem).wait()
    ffn = w_ref[:, 0:1] * buf_ref[0] + w_ref[:, 1:2] * buf_ref[1]
    o_ref[...] = _layer_norm(DN_ALPHA * x_ref[...] + ffn, lg_ref[...], lb_ref[...])


def _combine(dest, x1, wts_t, lg, lb, yb, *, tc):
    t = x1.shape[0]
    full = lambda a: pl.BlockSpec(a.shape, lambda i: (0,) * a.ndim)
    return pl.pallas_call(
        functools.partial(_combine_kernel, tc=tc),
        grid=(t // tc,),
        in_specs=[pl.BlockSpec((MOE_TOPK, tc), lambda i: (0, i), memory_space=pltpu.SMEM),
                  pl.BlockSpec((tc, D_MODEL), lambda i: (i, 0)),
                  pl.BlockSpec((tc, MOE_TOPK), lambda i: (i, 0)),
                  full(lg), full(lb),
                  pl.BlockSpec(memory_space=pl.ANY)],
        out_specs=pl.BlockSpec((tc, D_MODEL), lambda i: (i, 0)),
        out_shape=jax.ShapeDtypeStruct((t, D_MODEL), F32),
        scratch_shapes=[pltpu.VMEM((MOE_TOPK, tc, D_MODEL), F32), pltpu.SemaphoreType.DMA(())],
        compiler_params=_cparams("arbitrary"),
        name="combine",
    )(dest, x1, wts_t, lg, lb, yb)


def _tile(n, want):
    while n % want:
        want //= 2
    return want


def _mixer(x2d, b, l, w_in, rel_bias, conv_w, a_log, dt_bias, norm_g, pool_w, pool_scale,
           w_br_a, w_br_b, w_br_c, gate_b, w_out, ln_g, ln_b, wr, rb):
    t = x2d.shape[0]
    c0 = 3 * A_WIDTH
    c1 = c0 + 3 * B_WIDTH
    c2 = c1 + 2 * B_HEADS
    c3 = c2 + B_WIDTH
    c4 = c3 + C_WIDTH
    wbf = w_in.astype(BF)
    wbd = jnp.pad(wbf[:, c1:c2], ((0, 0), (0, LANES - 2 * B_HEADS)))
    qkva, qkvb, bd, go, oc = _proj(
        x2d, wbf[:, :c0], wbf[:, c0:c1], wbd, wbf[:, c2:c3], wbf[:, c3:c4],
        pool_w.astype(BF), pool_scale.reshape(1, C_WIDTH), seq=l, tm=_tile(l, 512))

    oa = _attn(qkva.reshape(b, l, 3 * A_WIDTH), _attn_bias_table(rel_bias)).reshape(t, A_WIDTH)

    lane_pad = lambda v: jnp.pad(v.reshape(1, B_HEADS), ((0, 0), (B_HEADS, LANES - 2 * B_HEADS)))
    ob = _gdn(qkvb.reshape(b, l, 3 * B_WIDTH), bd.reshape(b, l, LANES), go.reshape(b, l, B_WIDTH),
              conv_w, lane_pad(a_log), lane_pad(dt_bias), norm_g.reshape(1, B_HEAD_DIM)).reshape(t, B_WIDTH)

    wrh, wrl = _split_bf16(wr)
    return _merge(x2d, oa, ob, oc, wbf[:, c4:], gate_b, w_br_a.astype(BF), w_br_b.astype(BF),
                  w_br_c.astype(BF), w_out.astype(BF), ln_g.reshape(1, -1), ln_b.reshape(1, -1),
                  wrh, wrl, rb, tm=_tile(t, 512))


def _moe(x1, lgt, w1, w3, w2, ln_g, ln_b):
    t = x1.shape[0]
    n_blocks = -(-t * MOE_TOPK // EXPERT_BLOCK) + N_EXPERTS
    cap = n_blocks * EXPERT_BLOCK
    eid, wts, rank, cnt = _route(lgt, tn=_tile(t, 512))
    dest, bexp, nused = _dest(eid, rank, cnt, tn=_tile(t, 2048), n_blocks=n_blocks)
    xs = _dispatch(dest, x1, jnp.zeros((cap, D_MODEL), F32), td=_tile(t, 1024))
    yb = _expert(bexp.reshape(n_blocks), nused[0, :1], xs, w1.astype(BF), w3.astype(BF), w2.astype(BF))
    return _combine(dest, x1, wts.T, ln_g.reshape(1, -1), ln_b.reshape(1, -1), yb, tc=_tile(t, 512))


def kernel(x, w_in, attn_rel_bias, gdn_conv_w, gdn_a_log, gdn_dt_bias, gdn_norm_g, pool_w, pool_scale,
           w_branch_a, w_branch_b, w_branch_c, gate_b, w_out, ln1_g, ln1_b, router_group_w,
           router_group_b, router_expert_w, router_expert_b, moe_w1, moe_w3, moe_w2, ln2_g, ln2_b):
    b, l, d = x.shape
    x2d = x.reshape(b * l, d)
    for i in range(DEPTH):
        wr = jnp.concatenate([router_expert_w[i].T, router_group_w[i].T,
                              jnp.zeros((ROUTER_ROWS - N_EXPERTS - MOE_GROUPS, d), F32)], axis=0)
        rb = jnp.concatenate([router_expert_b[i], router_group_b[i],
                              jnp.zeros((ROUTER_ROWS - N_EXPERTS - MOE_GROUPS,), F32)]).reshape(ROUTER_ROWS, 1)
        x1, lgt = _mixer(x2d, b, l, w_in[i], attn_rel_bias[i], gdn_conv_w[i], gdn_a_log[i], gdn_dt_bias[i],
                         gdn_norm_g[i], pool_w[i], pool_scale[i], w_branch_a[i], w_branch_b[i],
                         w_branch_c[i], gate_b[i], w_out[i], ln1_g[i], ln1_b[i], wr, rb)
        x2d = _moe(x1, lgt, moe_w1[i], moe_w3[i], moe_w2[i], ln2_g[i], ln2_b[i])
    return x2d.reshape(b, l, d)
```

```python
import functools

import jax
import jax.numpy as jnp
from jax import lax
from jax.experimental import pallas as pl
from jax.experimental.pallas import tpu as pltpu

BF = jnp.bfloat16
F32 = jnp.float32
I32 = jnp.int32

D_MODEL = 1024
DEPTH = 2
CHUNK = 64

A_HEADS = 8
A_HEAD_DIM = 64
A_WIDTH = A_HEADS * A_HEAD_DIM
A_LEFT_CHUNKS = 8
A_REL_CLIP = 256

B_HEADS = 4
B_HEAD_DIM = 128
B_WIDTH = B_HEADS * B_HEAD_DIM
B_CONV = 4

C_WINDOWS = (2, 4, 8, 16)
C_GROUPS = 4
C_GROUP_DIM = 128
C_WIDTH = C_GROUPS * C_GROUP_DIM

N_BRANCH = 3
MOE_GROUPS = 4
MOE_PER_GROUP = 8
N_EXPERTS = MOE_GROUPS * MOE_PER_GROUP
MOE_TOPK = 2
MOE_FF = 512

DN_ALPHA = (2 * DEPTH) ** 0.25
LN_EPS = 1e-5
RMS_EPS = 1e-6

LANES = 128
SUBLANES = 8
VMEM_LIMIT_BYTES = 56 * 1024 * 1024

QBLOCK = 2 * CHUNK
KBAND = (A_LEFT_CHUNKS + 2) * CHUNK
NEG_BIG = -1e30

ROUTER_ROWS = 40
EXPERT_BLOCK = 512
DMA_ISSUE_UNROLL = 8


def _cparams(*sem):
    return pltpu.CompilerParams(dimension_semantics=sem, vmem_limit_bytes=VMEM_LIMIT_BYTES)


def _mm(a, b):
    return jnp.dot(a.astype(BF), b.astype(BF), preferred_element_type=F32)


def _mm_nt(a, b):
    return lax.dot_general(a.astype(BF), b.astype(BF), (((1,), (1,)), ((), ())),
                           preferred_element_type=F32)


def _mm_tn(a, b):
    return lax.dot_general(a.astype(BF), b.astype(BF), (((0,), (0,)), ((), ())),
                           preferred_element_type=F32)


def _split_bf16(a):
    hi = a.astype(BF)
    lo = (a - hi.astype(F32)).astype(BF)
    return hi, lo


def _mm3(a, b):
    ah, al = _split_bf16(a)
    bh, bl = _split_bf16(b)
    return (jnp.dot(ah, bh, preferred_element_type=F32)
            + (jnp.dot(ah, bl, preferred_element_type=F32)
               + jnp.dot(al, bh, preferred_element_type=F32)))


def _sigmoid(x):
    return 1.0 / (1.0 + jnp.exp(-x))


def _silu(x):
    return x * _sigmoid(x)


def _layer_norm(z, g, b):
    mu = jnp.mean(z, axis=-1, keepdims=True)
    zc = z - mu
    var = jnp.mean(zc * zc, axis=-1, keepdims=True)
    return zc * lax.rsqrt(var + LN_EPS) * g + b


POOL_TAIL = 16


def _proj_kernel(x_ref, wa_ref, wb_ref, wbd_ref, wgo_ref, wc_ref, pw_ref, ps_ref,
                 qkva_ref, qkvb_ref, bd_ref, go_ref, oc_ref, tail_ref, *, tiles_per_seq, tm):
    it = pl.program_id(0) % tiles_per_seq
    xb = x_ref[...].astype(BF)
    qkva_ref[...] = jnp.dot(xb, wa_ref[...], preferred_element_type=F32).astype(BF)
    qkvb_ref[...] = jnp.dot(xb, wb_ref[...], preferred_element_type=F32).astype(BF)
    bd_ref[...] = jnp.dot(xb, wbd_ref[...], preferred_element_type=F32)
    go_ref[...] = jnp.dot(xb, wgo_ref[...], preferred_element_type=F32).astype(BF)
    u = jnp.dot(xb, wc_ref[...], preferred_element_type=F32)

    @pl.when(it == 0)
    def _():
        tail_ref[...] = jnp.zeros_like(tail_ref)

    ext = jnp.concatenate([tail_ref[...], u], axis=0)
    tail_ref[...] = u[tm - POOL_TAIL:, :]
    pos = (it * tm + 1 + lax.broadcasted_iota(I32, (tm, 1), 0)).astype(F32)
    s = ext
    for gi, win in enumerate(C_WINDOWS):
        if gi:
            s = s[:, C_GROUP_DIM:]
        s = s + pltpu.roll(s, win // 2, 0)
        lo_c, hi_c = gi * C_GROUP_DIM, (gi + 1) * C_GROUP_DIM
        cnt = jnp.minimum(pos, float(win))
        pooled = s[POOL_TAIL:, :C_GROUP_DIM] / cnt - u[:, lo_c:hi_c]
        mixed = jnp.dot(pooled.astype(BF), pw_ref[gi], preferred_element_type=F32)
        oc_ref[:, lo_c:hi_c] = (mixed * ps_ref[:, lo_c:hi_c]).astype(BF)


def _proj(x2d, wa, wb, wbd, wgo, wc, pw, ps, *, seq, tm):
    t = x2d.shape[0]
    full = lambda a: pl.BlockSpec(a.shape, lambda i: (0,) * a.ndim)
    row = lambda n: pl.BlockSpec((tm, n), lambda i: (i, 0))
    return pl.pallas_call(
        functools.partial(_proj_kernel, tiles_per_seq=seq // tm, tm=tm),
        grid=(t // tm,),
        in_specs=[row(D_MODEL), full(wa), full(wb), full(wbd), full(wgo), full(wc), full(pw), full(ps)],
        out_specs=[row(3 * A_WIDTH), row(3 * B_WIDTH), row(LANES), row(B_WIDTH), row(C_WIDTH)],
        out_shape=[jax.ShapeDtypeStruct((t, 3 * A_WIDTH), BF),
                   jax.ShapeDtypeStruct((t, 3 * B_WIDTH), BF),
                   jax.ShapeDtypeStruct((t, LANES), F32),
                   jax.ShapeDtypeStruct((t, B_WIDTH), BF),
                   jax.ShapeDtypeStruct((t, C_WIDTH), BF)],
        scratch_shapes=[pltpu.VMEM((POOL_TAIL, C_WIDTH), F32)],
        compiler_params=_cparams("arbitrary"),
        name="proj",
    )(x2d, wa, wb, wbd, wgo, wc, pw, ps)


def _attn_bias_table(rel_bias):
    h = rel_bias.shape[0]
    pad = A_LEFT_CHUNKS * CHUNK
    width = KBAND + pad
    n = width + QBLOCK - 1
    far = jnp.repeat(rel_bias[:, -1:], QBLOCK - 1 + pad - A_REL_CLIP, axis=1)
    near = jnp.repeat(rel_bias[:, :1], n - far.shape[1] - rel_bias.shape[1], axis=1)
    w = jnp.concatenate([far, rel_bias[:, ::-1], near], axis=1).astype(F32)
    w = jnp.roll(w, -(QBLOCK - 1), axis=1)
    toep = jnp.tile(w, (1, QBLOCK))[:, :QBLOCK * (n - 1)].reshape(h, QBLOCK, n - 1)[:, :, :width]
    r = jnp.arange(QBLOCK)[:, None]
    m = jnp.arange(width)[None, :]
    kc, qc = m // CHUNK, r // CHUNK
    valid = (kc >= qc) & (kc <= qc + A_LEFT_CHUNKS)
    tab = jnp.where(valid[None], toep, NEG_BIG)
    return tab.reshape(h, QBLOCK, width // LANES, LANES).transpose(0, 2, 1, 3)


def _attn_kernel(q_ref, k_ref, v_ref, tab_ref, o_ref):
    j = pl.program_id(1)
    first_blk = A_LEFT_CHUNKS * CHUNK // QBLOCK
    kstart = pl.multiple_of(jnp.maximum(j - first_blk, 0) * QBLOCK, QBLOCK)
    offb = jnp.maximum(first_blk - j, 0)
    lane = lax.broadcasted_iota(I32, (QBLOCK, LANES), 1)
    lo = lane < A_HEAD_DIM
    nkb = KBAND // LANES
    pairs = range(A_HEADS // 2)
    cols = [slice(hp * LANES, (hp + 1) * LANES) for hp in pairs]
    scores = []
    for hp in pairs:
        qp = q_ref[0, :, cols[hp]] * jnp.asarray(A_HEAD_DIM ** -0.5, BF)
        zero = jnp.zeros_like(qp)
        q2 = jnp.concatenate([jnp.where(lo, qp, zero), jnp.where(lo, zero, qp)], axis=0)
        kp = k_ref[0, pl.ds(kstart, KBAND), cols[hp]]
        scores.append(lax.dot_general(q2, kp, (((1,), (1,)), ((), ())), preferred_element_type=F32))
    probs, denoms = [], []
    for hp in pairs:
        bias = jnp.concatenate(
            [jnp.concatenate([tab_ref[2 * hp + par, offb + i] for i in range(nkb)], axis=1) for par in range(2)],
            axis=0)
        s = scores[hp] + bias
        p = jnp.exp(s - jnp.max(s, axis=-1, keepdims=True))
        denoms.append(jnp.sum(p, axis=-1, keepdims=True))
        probs.append(p.astype(BF))
    for hp in pairs:
        vp = v_ref[0, pl.ds(kstart, KBAND), cols[hp]]
        o2 = jnp.dot(probs[hp], vp, preferred_element_type=F32) / denoms[hp]
        o_ref[0, :, cols[hp]] = jnp.where(lo, o2[:QBLOCK], o2[QBLOCK:]).astype(BF)


def _attn(qkva, tab):
    b, l, _ = qkva.shape
    return pl.pallas_call(
        _attn_kernel,
        grid=(b, l // QBLOCK),
        in_specs=[pl.BlockSpec((1, QBLOCK, A_WIDTH), lambda bi, j: (bi, j, 0)),
                  pl.BlockSpec((1, l, A_WIDTH), lambda bi, j: (bi, 0, 1)),
                  pl.BlockSpec((1, l, A_WIDTH), lambda bi, j: (bi, 0, 2)),
                  pl.BlockSpec(tab.shape, lambda bi, j: (0, 0, 0, 0))],
        out_specs=pl.BlockSpec((1, QBLOCK, A_WIDTH), lambda bi, j: (bi, j, 0)),
        out_shape=jax.ShapeDtypeStruct((b, l, A_WIDTH), BF),
        compiler_params=_cparams("arbitrary", "arbitrary"),
        name="attn",
    )(qkva, qkva, qkva, tab)


CONV_HIST = 16
GDN_PREP_UNROLL = 2


def _gdn_kernel(qkv_ref, bd_ref, go_ref, cw_ref, alog_ref, dtb_ref, ng_ref, o_ref,
                u_ref, wq_ref, qkkd_ref, egl_ref, state_ref, *, n_chunks):
    rt = lax.broadcasted_iota(I32, (CHUNK, CHUNK), 0)
    ct = lax.broadcasted_iota(I32, (CHUNK, CHUNK), 1)
    tril = rt >= ct
    strict = rt > ct
    trilf = tril.astype(F32)
    eye_f = (rt == ct).astype(F32)
    eye_d = (lax.broadcasted_iota(I32, (B_HEAD_DIM, B_HEAD_DIM), 0)
             == lax.broadcasted_iota(I32, (B_HEAD_DIM, B_HEAD_DIM), 1)).astype(BF)
    neg_rate = -jnp.exp(alog_ref[...])
    sr = lax.broadcasted_iota(I32, (B_CONV * CHUNK, CONV_HIST + CHUNK), 0)
    sc = lax.broadcasted_iota(I32, (B_CONV * CHUNK, CONV_HIST + CHUNK), 1)
    shift_sel = (sc == CONV_HIST + jnp.bitwise_and(sr, CHUNK - 1)
                 - jnp.right_shift(sr, CHUNK.bit_length() - 1)).astype(BF)

    def heads(fn):
        return jnp.concatenate([fn(h, slice(h * B_HEAD_DIM, (h + 1) * B_HEAD_DIM)) for h in range(B_HEADS)],
                               axis=1)

    def prep(c):
        r0 = pl.multiple_of(c * CHUNK, CHUNK)
        cur = qkv_ref[0, pl.ds(r0, CHUNK), :]
        h0 = pl.multiple_of(jnp.maximum(r0 - CONV_HIST, 0), CONV_HIST)
        hist = qkv_ref[0, pl.ds(h0, CONV_HIST), :]
        hist = jnp.where(c > 0, hist, jnp.zeros_like(hist))
        taps = jnp.dot(shift_sel, jnp.concatenate([hist, cur], axis=0), preferred_element_type=F32)
        acc = taps[:CHUNK] * cw_ref[B_CONV - 1:B_CONV, :]
        for back in range(1, B_CONV):
            tap = B_CONV - 1 - back
            acc = acc + taps[back * CHUNK:(back + 1) * CHUNK] * cw_ref[tap:tap + 1, :]
        act = _silu(acc)
        q_all, k_all, v_all = act[:, :B_WIDTH], act[:, B_WIDTH:2 * B_WIDTH], act[:, 2 * B_WIDTH:]

        bd = bd_ref[0, pl.ds(r0, CHUNK), :]
        beta_all = _sigmoid(bd)
        z = bd + dtb_ref[...]
        softplus = jnp.maximum(z, 0.0) + jnp.log1p(jnp.exp(-jnp.abs(z)))
        g_all = neg_rate * softplus
        gc_all = jnp.dot(trilf, g_all, preferred_element_type=F32,
                         precision=lax.Precision.HIGHEST)
        egc_all = jnp.exp(gc_all)
        glast = gc_all[CHUNK - 1:CHUNK, :]
        ekd_all = jnp.exp(glast - gc_all)
        egl = jnp.exp(glast)
        col = lambda m, h: m[:, B_HEADS + h:B_HEADS + h + 1]

        def l2n(m):
            return m * lax.rsqrt(jnp.sum(m * m, axis=-1, keepdims=True) + RMS_EPS)

        qn = heads(lambda h, hs: l2n(q_all[:, hs]) * (B_HEAD_DIM ** -0.5))
        kn = heads(lambda h, hs: l2n(k_all[:, hs]))
        kbeta = heads(lambda h, hs: kn[:, hs] * beta_all[:, h:h + 1])
        vbeta = heads(lambda h, hs: v_all[:, hs] * beta_all[:, h:h + 1])
        kbe = heads(lambda h, hs: kbeta[:, hs] * col(egc_all, h))
        qe = heads(lambda h, hs: qn[:, hs] * col(egc_all, h))
        chains = []
        for h in range(B_HEADS):
            hs = slice(h * B_HEAD_DIM, (h + 1) * B_HEAD_DIM)
            gcc = jnp.broadcast_to(col(gc_all, h), (CHUNK, CHUNK))
            gcr = jnp.sum(gcc * eye_f, axis=0, keepdims=True)
            chains.append(dict(
                c=c, h=h, hs=hs,
                decay=jnp.exp(jnp.where(tril, gcc - gcr, -jnp.inf)),
                kq=jnp.concatenate([kbeta[:, hs], qn[:, hs]], axis=0).astype(BF),
                kn=kn[:, hs].astype(BF),
                rhs=jnp.concatenate([vbeta[:, hs], kbe[:, hs]], axis=1).astype(BF),
                kd=(kn[:, hs] * col(ekd_all, h)).astype(BF)))
        egl_row = heads(lambda h, hs: jnp.broadcast_to(col(egl, h), (1, B_HEAD_DIM)))
        return chains, qe.astype(BF), egl_row

    nt = lambda a, b: lax.dot_general(a, b, (((1,), (1,)), ((), ())), preferred_element_type=F32)
    mm = lambda a, b: jnp.dot(a, b, preferred_element_type=F32)

    def prep_levels(i):
        fronts = [prep(i * GDN_PREP_UNROLL + g) for g in range(GDN_PREP_UNROLL)]
        ch = [x for f in fronts for x in f[0]]
        aq = [nt(x["kq"], x["kn"]) for x in ch]
        kdt = [nt(eye_d, x["kd"]) for x in ch]
        yield
        qk = [(a[CHUNK:] * x["decay"]).astype(BF) for x, a in zip(ch, aq)]
        pw = [-jnp.where(strict, a[:CHUNK] * x["decay"], 0.0) for x, a in zip(ch, aq)]
        tmat = [eye_f + p for p in pw]
        pwb = [p.astype(BF) for p in pw]
        pw = [mm(p, p) for p in pwb]
        yield
        n_sq = CHUNK.bit_length() - 2
        for lvl in range(n_sq):
            pwb = [p.astype(BF) for p in pw]
            if lvl + 1 < n_sq:
                both = [mm(jnp.concatenate([t.astype(BF), p], axis=0), p) for t, p in zip(tmat, pwb)]
                yield
                tmat = [t + r[:CHUNK] for t, r in zip(tmat, both)]
                pw = [r[CHUNK:] for r in both]
            else:
                last = [mm(t.astype(BF), p) for t, p in zip(tmat, pwb)]
                yield
                tmat = [t + r for t, r in zip(tmat, last)]
        uw = [mm(t.astype(BF), x["rhs"]) for t, x in zip(tmat, ch)]
        yield
        for x, r, qkx, kt in zip(ch, uw, qk, kdt):
            u_ref[x["c"], :, x["hs"]] = r[:, :B_HEAD_DIM]
            wq_ref[x["c"], :CHUNK, x["hs"]] = r[:, B_HEAD_DIM:].astype(BF)
            qkkd_ref[x["c"], x["h"], :CHUNK, :] = qkx
            qkkd_ref[x["c"], x["h"], CHUNK:, :] = kt.astype(BF)
        for g, (_, qe_b, egl_row) in enumerate(fronts):
            wq_ref[i * GDN_PREP_UNROLL + g, CHUNK:, :] = qe_b
            egl_ref[i * GDN_PREP_UNROLL + g] = egl_row

    def step_levels(c):
        r0 = pl.multiple_of(c * CHUNK, CHUNK)
        st = state_ref[...]
        stb = st.astype(BF)
        hss = [slice(h * B_HEAD_DIM, (h + 1) * B_HEAD_DIM) for h in range(B_HEADS)]
        r1 = [mm(wq_ref[c, :, hs], stb[:, hs]) for hs in hss]
        yield
        v_new = [(u_ref[c, :, hs] - r[:CHUNK]).astype(BF) for hs, r in zip(hss, r1)]
        r2 = [mm(qkkd_ref[c, h], v_new[h]) for h in range(B_HEADS)]
        yield
        o_all = jnp.concatenate([a[CHUNK:] + b[:CHUNK] for a, b in zip(r1, r2)], axis=1)
        state_ref[...] = st * egl_ref[c] + jnp.concatenate([b[CHUNK:] for b in r2], axis=1)
        gate = go_ref[0, pl.ds(r0, CHUNK), :].astype(F32)
        o_n = heads(lambda h, hs: o_all[:, hs]
                    * lax.rsqrt(jnp.mean(o_all[:, hs] * o_all[:, hs], axis=-1, keepdims=True) + RMS_EPS)
                    * ng_ref[...])
        o_ref[0, pl.ds(r0, CHUNK), :] = (o_n * _silu(gate)).astype(BF)

    def steps_of(i):
        for g in range(GDN_PREP_UNROLL):
            yield from step_levels(i * GDN_PREP_UNROLL + g)

    def run_interleaved(*gens):
        live = list(gens)
        while live:
            for g in list(live):
                if next(g, StopIteration) is StopIteration:
                    live.remove(g)

    n_groups = n_chunks // GDN_PREP_UNROLL
    state_ref[...] = jnp.zeros_like(state_ref)
    run_interleaved(prep_levels(0))

    def body(i, carry):
        run_interleaved(steps_of(i - 1), prep_levels(i))
        return carry

    lax.fori_loop(1, n_groups, body, 0)
    run_interleaved(steps_of(n_groups - 1))


def _gdn(qkvb, bd, go, cw, alog, dtb, ng):
    b, l, _ = qkvb.shape
    nc = l // CHUNK
    full = lambda a: pl.BlockSpec(a.shape, lambda bi: (0,) * a.ndim)
    seq = lambda n: pl.BlockSpec((1, l, n), lambda bi: (bi, 0, 0))
    return pl.pallas_call(
        functools.partial(_gdn_kernel, n_chunks=nc),
        grid=(b,),
        in_specs=[seq(3 * B_WIDTH), seq(LANES), seq(B_WIDTH), full(cw), full(alog), full(dtb), full(ng)],
        out_specs=seq(B_WIDTH),
        out_shape=jax.ShapeDtypeStruct((b, l, B_WIDTH), BF),
        scratch_shapes=[pltpu.VMEM((nc, CHUNK, B_WIDTH), F32),
                        pltpu.VMEM((nc, 2 * CHUNK, B_WIDTH), BF),
                        pltpu.VMEM((nc, B_HEADS, CHUNK + B_HEAD_DIM, CHUNK), BF),
                        pltpu.VMEM((nc, 1, B_WIDTH), F32),
                        pltpu.VMEM((B_HEAD_DIM, B_WIDTH), F32)],
        compiler_params=_cparams("arbitrary"),
        name="gdn",
    )(qkvb, bd, go, cw, alog, dtb, ng)


def _merge_kernel(x_ref, oa_ref, ob_ref, oc_ref, wg_ref, gb_ref, wa_ref, wb_ref, wc_ref, wo_ref,
                  lg_ref, lb_ref, wrh_ref, wrl_ref, rb_ref, x1_ref, lgt_ref):
    x = x_ref[...]
    xb = x.astype(BF)
    y = None
    for i, (o_ref, w_ref) in enumerate(((oa_ref, wa_ref), (ob_ref, wb_ref), (oc_ref, wc_ref))):
        gl = jnp.dot(xb, wg_ref[:, i * D_MODEL:(i + 1) * D_MODEL], preferred_element_type=F32)
        gate = _sigmoid(gl + gb_ref[i:i + 1, :])
        br = gate * jnp.dot(o_ref[...], w_ref[...], preferred_element_type=F32)
        y = br if y is None else y + br
    mix = jnp.dot(y.astype(BF), wo_ref[...], preferred_element_type=F32)
    x1 = _layer_norm(DN_ALPHA * x + mix, lg_ref[...], lb_ref[...])
    x1_ref[...] = x1
    xh, xl = _split_bf16(x1)
    nt = lambda a, b: lax.dot_general(a, b, (((1,), (1,)), ((), ())), preferred_element_type=F32)
    lgt_ref[...] = nt(wrh_ref[...], xh) + (nt(wrh_ref[...], xl) + nt(wrl_ref[...], xh)) + rb_ref[...]


def _merge(x2d, oa, ob, oc, wg, gb, wa, wb, wc, wo, lg, lb, wrh, wrl, rb, *, tm):
    t = x2d.shape[0]
    full = lambda a: pl.BlockSpec(a.shape, lambda i: (0,) * a.ndim)
    row = lambda n: pl.BlockSpec((tm, n), lambda i: (i, 0))
    return pl.pallas_call(
        _merge_kernel,
        grid=(t // tm,),
        in_specs=[row(D_MODEL), row(A_WIDTH), row(B_WIDTH), row(C_WIDTH), full(wg), full(gb), full(wa),
                  full(wb), full(wc), full(wo), full(lg), full(lb), full(wrh), full(wrl), full(rb)],
        out_specs=[row(D_MODEL), pl.BlockSpec((ROUTER_ROWS, tm), lambda i: (0, i))],
        out_shape=[jax.ShapeDtypeStruct((t, D_MODEL), F32),
                   jax.ShapeDtypeStruct((ROUTER_ROWS, t), F32)],
        compiler_params=_cparams("arbitrary"),
        name="merge",
    )(x2d, oa, ob, oc, wg, gb, wa, wb, wc, wo, lg, lb, wrh, wrl, rb)


def _route_kernel(lgt_ref, eid_ref, wts_ref, rank_ref, cnt_ref, run_ref, *, tn):
    @pl.when(pl.program_id(0) == 0)
    def _():
        run_ref[...] = jnp.zeros_like(run_ref)

    le = lgt_ref[0:N_EXPERTS, :]
    lg = lgt_ref[N_EXPERTS:N_EXPERTS + MOE_GROUPS, :]
    gi = lax.broadcasted_iota(I32, (MOE_GROUPS, tn), 0).astype(F32)
    ei = lax.broadcasted_iota(I32, (N_EXPERTS, tn), 0).astype(F32)
    eg = jnp.right_shift(lax.broadcasted_iota(I32, (N_EXPERTS, tn), 0),
                         MOE_PER_GROUP.bit_length() - 1).astype(F32)
    mg = jnp.max(lg, axis=0, keepdims=True)
    gsel = jnp.min(jnp.where(lg == mg, gi, float(MOE_GROUPS)), axis=0, keepdims=True)
    p_top = 1.0 / jnp.sum(jnp.exp(lg - mg), axis=0, keepdims=True)
    l1 = jnp.where(eg == gsel, le, -jnp.inf)
    m1 = jnp.max(l1, axis=0, keepdims=True)
    i1 = jnp.min(jnp.where(l1 == m1, ei, float(N_EXPERTS)), axis=0, keepdims=True)
    l2 = jnp.where(ei == i1, -jnp.inf, l1)
    m2 = jnp.max(l2, axis=0, keepdims=True)
    i2 = jnp.min(jnp.where(l2 == m2, ei, float(N_EXPERTS)), axis=0, keepdims=True)
    e2 = jnp.exp(m2 - m1)
    den = 1.0 + e2
    eid_ref[...] = jnp.concatenate([i1, i2], axis=0).astype(I32)
    wts_ref[...] = jnp.concatenate([p_top / den, p_top * (e2 / den)], axis=0)

    before = (lax.broadcasted_iota(I32, (tn, tn), 0) < lax.broadcasted_iota(I32, (tn, tn), 1)).astype(BF)
    run = run_ref[...]
    ranks = []
    for ik in (i1, i2):
        onehot = (ei == ik)
        ohf = onehot.astype(F32)
        prefix = jnp.dot(ohf.astype(BF), before, preferred_element_type=F32)
        ranks.append(jnp.sum(jnp.where(onehot, prefix + run, 0.0), axis=0, keepdims=True))
        run = run + jnp.sum(ohf, axis=1, keepdims=True)
    run_ref[...] = run
    rank_ref[...] = jnp.concatenate(ranks, axis=0).astype(I32)
    cnt_ref[...] = jnp.broadcast_to(run, cnt_ref.shape)


def _route(lgt, *, tn):
    t = lgt.shape[1]
    tok = lambda: pl.BlockSpec((MOE_TOPK, tn), lambda i: (0, i))
    return pl.pallas_call(
        functools.partial(_route_kernel, tn=tn),
        grid=(t // tn,),
        in_specs=[pl.BlockSpec((ROUTER_ROWS, tn), lambda i: (0, i))],
        out_specs=[tok(), tok(), tok(), pl.BlockSpec((N_EXPERTS, LANES), lambda i: (0, 0))],
        out_shape=[jax.ShapeDtypeStruct((MOE_TOPK, t), I32),
                   jax.ShapeDtypeStruct((MOE_TOPK, t), F32),
                   jax.ShapeDtypeStruct((MOE_TOPK, t), I32),
                   jax.ShapeDtypeStruct((N_EXPERTS, LANES), F32)],
        scratch_shapes=[pltpu.VMEM((N_EXPERTS, 1), F32)],
        compiler_params=_cparams("arbitrary"),
        name="route",
    )(lgt)


def _dest_kernel(eid_ref, rank_ref, cnt_ref, dest_ref, bexp_ref, nused_ref, *, n_blocks):
    cnt = cnt_ref[...].astype(I32)
    shift = EXPERT_BLOCK.bit_length() - 1
    padded_blocks = jnp.right_shift(cnt + (EXPERT_BLOCK - 1), shift)
    er = lax.broadcasted_iota(I32, (N_EXPERTS, LANES), 0)
    start = jnp.zeros((N_EXPERTS, LANES), I32)
    for e in range(N_EXPERTS - 1):
        start = start + jnp.where(er > e, padded_blocks[e:e + 1, :], 0)
    end = start + padded_blocks
    eid = eid_ref[...]
    d = rank_ref[...]
    for e in range(N_EXPERTS):
        d = d + jnp.where(eid == e, start[e:e + 1, 0:1] * EXPERT_BLOCK, 0)
    dest_ref[...] = d

    @pl.when(pl.program_id(0) == 0)
    def _():
        blk = lax.broadcasted_iota(I32, (1, n_blocks), 1)
        owner = jnp.zeros((1, n_blocks), I32)
        for e in range(N_EXPERTS):
            owner = owner + (blk >= end[e:e + 1, 0:1]).astype(I32)
        bexp_ref[...] = jnp.minimum(owner, N_EXPERTS - 1)
        nused_ref[...] = end[N_EXPERTS - 1:N_EXPERTS, :]


def _dest(eid, rank, cnt, *, tn, n_blocks):
    t = eid.shape[1]
    tok = lambda: pl.BlockSpec((MOE_TOPK, tn), lambda i: (0, i))
    return pl.pallas_call(
        functools.partial(_dest_kernel, n_blocks=n_blocks),
        grid=(t // tn,),
        in_specs=[tok(), tok(), pl.BlockSpec(cnt.shape, lambda i: (0, 0))],
        out_specs=[tok(), pl.BlockSpec((1, n_blocks), lambda i: (0, 0)),
                   pl.BlockSpec((1, LANES), lambda i: (0, 0))],
        out_shape=[jax.ShapeDtypeStruct((MOE_TOPK, t), I32),
                   jax.ShapeDtypeStruct((1, n_blocks), I32),
                   jax.ShapeDtypeStruct((1, LANES), I32)],
        compiler_params=_cparams("arbitrary"),
        name="dest",
    )(eid, rank, cnt)


def _dispatch_kernel(dest_ref, x_ref, xs_in_ref, xs_ref, sem, *, td):
    del xs_in_ref

    def issue(t, carry):
        for k in range(MOE_TOPK):
            pltpu.make_async_copy(x_ref.at[pl.ds(t, 1), :], xs_ref.at[pl.ds(dest_ref[k, t], 1), :], sem).start()
        return carry

    lax.fori_loop(0, td, issue, 0, unroll=DMA_ISSUE_UNROLL)
    all_rows = xs_ref.at[pl.ds(0, MOE_TOPK * td), :]
    pltpu.make_async_copy(all_rows, all_rows, sem).wait()


def _dispatch(dest, x1, xs_zero, *, td):
    t = x1.shape[0]
    return pl.pallas_call(
        functools.partial(_dispatch_kernel, td=td),
        grid=(t // td,),
        in_specs=[pl.BlockSpec((MOE_TOPK, td), lambda i: (0, i), memory_space=pltpu.SMEM),
                  pl.BlockSpec((td, D_MODEL), lambda i: (i, 0)),
                  pl.BlockSpec(memory_space=pl.ANY)],
        out_specs=pl.BlockSpec(memory_space=pl.ANY),
        out_shape=jax.ShapeDtypeStruct(xs_zero.shape, xs_zero.dtype),
        scratch_shapes=[pltpu.SemaphoreType.DMA(())],
        input_output_aliases={2: 0},
        compiler_params=_cparams("arbitrary"),
        name="dispatch",
    )(dest, x1, xs_zero)


def _expert_kernel(bexp_ref, nused_ref, xs_ref, w1_ref, w3_ref, w2_ref, y_ref):
    del bexp_ref
    i = pl.program_id(0)

    @pl.when(i < nused_ref[0])
    def _():
        xb = xs_ref[...].astype(BF)
        h1 = jnp.dot(xb, w1_ref[0], preferred_element_type=F32)
        h3 = jnp.dot(xb, w3_ref[0], preferred_element_type=F32)
        hid = (_silu(h1) * h3).astype(BF)
        y_ref[...] = jnp.dot(hid, w2_ref[0], preferred_element_type=F32)

    @pl.when(i >= nused_ref[0])
    def _():
        y_ref[...] = jnp.zeros_like(y_ref)


def _expert(bexp, nused, xs, w1, w3, w2):
    cap = xs.shape[0]
    nb = cap // EXPERT_BLOCK
    last = lambda i, nu: jnp.minimum(i, nu[0] - 1)
    grid_spec = pltpu.PrefetchScalarGridSpec(
        num_scalar_prefetch=2,
        grid=(nb,),
        in_specs=[pl.BlockSpec((EXPERT_BLOCK, D_MODEL), lambda i, be, nu: (last(i, nu), 0)),
                  pl.BlockSpec((1, D_MODEL, MOE_FF), lambda i, be, nu: (be[last(i, nu)], 0, 0)),
                  pl.BlockSpec((1, D_MODEL, MOE_FF), lambda i, be, nu: (be[last(i, nu)], 0, 0)),
                  pl.BlockSpec((1, MOE_FF, D_MODEL), lambda i, be, nu: (be[last(i, nu)], 0, 0))],
        out_specs=pl.BlockSpec((EXPERT_BLOCK, D_MODEL), lambda i, be, nu: (i, 0)),
    )
    return pl.pallas_call(
        _expert_kernel,
        grid_spec=grid_spec,
        out_shape=jax.ShapeDtypeStruct((cap, D_MODEL), F32),
        compiler_params=_cparams("arbitrary"),
        name="expert",
    )(bexp, nused, xs, w1, w3, w2)


def _combine_kernel(dest_ref, x_ref, w_ref, lg_ref, lb_ref, yb_ref, o_ref, buf_ref, sem, *, tc):
    def issue(t, carry):
        for k in range(MOE_TOPK):
            pltpu.make_async_copy(yb_ref.at[pl.ds(dest_ref[k, t], 1), :],
                                  buf_ref.at[k, pl.ds(t, 1), :], sem).start()
        return carry

    lax.fori_loop(0, tc, issue, 0, unroll=DMA_ISSUE_UNROLL)
    pltpu.make_async_copy(buf_ref, buf_ref, sem).wait()
    ffn = w_ref[:, 0:1] * buf_ref[0] + w_ref[:, 1:2] * buf_ref[1]
    o_ref[...] = _layer_norm(DN_ALPHA * x_ref[...] + ffn, lg_ref[...], lb_ref[...])


def _combine(dest, x1, wts_t, lg, lb, yb, *, tc):
    t = x1.shape[0]
    full = lambda a: pl.BlockSpec(a.shape, lambda i: (0,) * a.ndim)
    return pl.pallas_call(
        functools.partial(_combine_kernel, tc=tc),
        grid=(t // tc,),
        in_specs=[pl.BlockSpec((MOE_TOPK, tc), lambda i: (0, i), memory_space=pltpu.SMEM),
                  pl.BlockSpec((tc, D_MODEL), lambda i: (i, 0)),
                  pl.BlockSpec((tc, MOE_TOPK), lambda i: (i, 0)),
                  full(lg), full(lb),
                  pl.BlockSpec(memory_space=pl.ANY)],
        out_specs=pl.BlockSpec((tc, D_MODEL), lambda i: (i, 0)),
        out_shape=jax.ShapeDtypeStruct((t, D_MODEL), F32),
        scratch_shapes=[pltpu.VMEM((MOE_TOPK, tc, D_MODEL), F32), pltpu.SemaphoreType.DMA(())],
        compiler_params=_cparams("arbitrary"),
        name="combine",
    )(dest, x1, wts_t, lg, lb, yb)


def _tile(n, want):
    while n % want:
        want //= 2
    return want


def _mixer(x2d, b, l, w_in, rel_bias, conv_w, a_log, dt_bias, norm_g, pool_w, pool_scale,
           w_br_a, w_br_b, w_br_c, gate_b, w_out, ln_g, ln_b, wr, rb):
    t = x2d.shape[0]
    c0 = 3 * A_WIDTH
    c1 = c0 + 3 * B_WIDTH
    c2 = c1 + 2 * B_HEADS
    c3 = c2 + B_WIDTH
    c4 = c3 + C_WIDTH
    wbf = w_in.astype(BF)
    wbd = jnp.pad(wbf[:, c1:c2], ((0, 0), (0, LANES - 2 * B_HEADS)))
    qkva, qkvb, bd, go, oc = _proj(
        x2d, wbf[:, :c0], wbf[:, c0:c1], wbd, wbf[:, c2:c3], wbf[:, c3:c4],
        pool_w.astype(BF), pool_scale.reshape(1, C_WIDTH), seq=l, tm=_tile(l, 1024))

    oa = _attn(qkva.reshape(b, l, 3 * A_WIDTH), _attn_bias_table(rel_bias)).reshape(t, A_WIDTH)

    lane_pad = lambda v: jnp.pad(v.reshape(1, B_HEADS), ((0, 0), (B_HEADS, LANES - 2 * B_HEADS)))
    ob = _gdn(qkvb.reshape(b, l, 3 * B_WIDTH), bd.reshape(b, l, LANES), go.reshape(b, l, B_WIDTH),
              conv_w, lane_pad(a_log), lane_pad(dt_bias), norm_g.reshape(1, B_HEAD_DIM)).reshape(t, B_WIDTH)

    wrh, wrl = _split_bf16(wr)
    return _merge(x2d, oa, ob, oc, wbf[:, c4:], gate_b, w_br_a.astype(BF), w_br_b.astype(BF),
                  w_br_c.astype(BF), w_out.astype(BF), ln_g.reshape(1, -1), ln_b.reshape(1, -1),
                  wrh, wrl, rb, tm=_tile(t, 1024))


def _moe(x1, lgt, w1, w3, w2, ln_g, ln_b):
    t = x1.shape[0]
    n_blocks = -(-t * MOE_TOPK // EXPERT_BLOCK) + N_EXPERTS
    cap = n_blocks * EXPERT_BLOCK
    eid, wts, rank, cnt = _route(lgt, tn=_tile(t, 512))
    dest, bexp, nused = _dest(eid, rank, cnt, tn=_tile(t, 2048), n_blocks=n_blocks)
    xs = _dispatch(dest, x1, jnp.zeros((cap, D_MODEL), F32), td=_tile(t, 1024))
    yb = _expert(bexp.reshape(n_blocks), nused[0, :1], xs, w1.astype(BF), w3.astype(BF), w2.astype(BF))
    return _combine(dest, x1, wts.T, ln_g.reshape(1, -1), ln_b.reshape(1, -1), yb, tc=_tile(t, 512))


def kernel(x, w_in, attn_rel_bias, gdn_conv_w, gdn_a_log, gdn_dt_bias, gdn_norm_g, pool_w, pool_scale,
           w_branch_a, w_branch_b, w_branch_c, gate_b, w_out, ln1_g, ln1_b, router_group_w,
           router_group_b, router_expert_w, router_expert_b, moe_w1, moe_w3, moe_w2, ln2_g, ln2_b):
    b, l, d = x.shape
    x2d = x.reshape(b * l, d)
    for i in range(DEPTH):
        wr = jnp.concatenate([router_expert_w[i].T, router_group_w[i].T,
                              jnp.zeros((ROUTER_ROWS - N_EXPERTS - MOE_GROUPS, d), F32)], axis=0)
        rb = jnp.concatenate([router_expert_b[i], router_group_b[i],
                              jnp.zeros((ROUTER_ROWS - N_EXPERTS - MOE_GROUPS,), F32)]).reshape(ROUTER_ROWS, 1)
        x1, lgt = _mixer(x2d, b, l, w_in[i], attn_rel_bias[i], gdn_conv_w[i], gdn_a_log[i], gdn_dt_bias[i],
                         gdn_norm_g[i], pool_w[i], pool_scale[i], w_branch_a[i], w_branch_b[i],
                         w_branch_c[i], gate_b[i], w_out[i], ln1_g[i], ln1_b[i], wr, rb)
        x2d = _moe(x1, lgt, moe_w1[i], moe_w3[i], moe_w2[i], ln2_g[i], ln2_b[i])
    return x2d.reshape(b, l, d)
```

```python
import functools

import jax
import jax.numpy as jnp
from jax import lax
from jax.experimental import pallas as pl
from jax.experimental.pallas import tpu as pltpu

BF = jnp.bfloat16
F32 = jnp.float32
I32 = jnp.int32

D_MODEL = 1024
DEPTH = 2
CHUNK = 64

A_HEADS = 8
A_HEAD_DIM = 64
A_WIDTH = A_HEADS * A_HEAD_DIM
A_LEFT_CHUNKS = 8
A_REL_CLIP = 256

B_HEADS = 4
B_HEAD_DIM = 128
B_WIDTH = B_HEADS * B_HEAD_DIM
B_CONV = 4

C_WINDOWS = (2, 4, 8, 16)
C_GROUPS = 4
C_GROUP_DIM = 128
C_WIDTH = C_GROUPS * C_GROUP_DIM

N_BRANCH = 3
MOE_GROUPS = 4
MOE_PER_GROUP = 8
N_EXPERTS = MOE_GROUPS * MOE_PER_GROUP
MOE_TOPK = 2
MOE_FF = 512

DN_ALPHA = (2 * DEPTH) ** 0.25
LN_EPS = 1e-5
RMS_EPS = 1e-6

LANES = 128
SUBLANES = 8
VMEM_LIMIT_BYTES = 56 * 1024 * 1024

QBLOCK = 2 * CHUNK
KBAND = (A_LEFT_CHUNKS + 2) * CHUNK
NEG_BIG = -1e30

ROUTER_ROWS = 40
EXPERT_BLOCK = 512
DMA_ISSUE_UNROLL = 8
MOE_TILE = 512
RUN_ALIGN = 8


def _cparams(*sem):
    return pltpu.CompilerParams(dimension_semantics=sem, vmem_limit_bytes=VMEM_LIMIT_BYTES)


def _mm(a, b):
    return jnp.dot(a.astype(BF), b.astype(BF), preferred_element_type=F32)


def _mm_nt(a, b):
    return lax.dot_general(a.astype(BF), b.astype(BF), (((1,), (1,)), ((), ())),
                           preferred_element_type=F32)


def _mm_tn(a, b):
    return lax.dot_general(a.astype(BF), b.astype(BF), (((0,), (0,)), ((), ())),
                           preferred_element_type=F32)


def _split_bf16(a):
    hi = a.astype(BF)
    lo = (a - hi.astype(F32)).astype(BF)
    return hi, lo


def _mm3(a, b):
    ah, al = _split_bf16(a)
    bh, bl = _split_bf16(b)
    return (jnp.dot(ah, bh, preferred_element_type=F32)
            + (jnp.dot(ah, bl, preferred_element_type=F32)
               + jnp.dot(al, bh, preferred_element_type=F32)))


def _sigmoid(x):
    return 1.0 / (1.0 + jnp.exp(-x))


def _silu(x):
    return x * _sigmoid(x)


def _layer_norm(z, g, b):
    mu = jnp.mean(z, axis=-1, keepdims=True)
    zc = z - mu
    var = jnp.mean(zc * zc, axis=-1, keepdims=True)
    return zc * lax.rsqrt(var + LN_EPS) * g + b


POOL_TAIL = 16


def _proj_kernel(x_ref, wa_ref, wb_ref, wbd_ref, wgo_ref, wc_ref, pw_ref, ps_ref,
                 qkva_ref, qkvb_ref, bd_ref, go_ref, oc_ref, tail_ref, *, tiles_per_seq, tm):
    it = pl.program_id(0) % tiles_per_seq
    xb = x_ref[...].astype(BF)
    qkva_ref[...] = jnp.dot(xb, wa_ref[...], preferred_element_type=F32).astype(BF)
    qkvb_ref[...] = jnp.dot(xb, wb_ref[...], preferred_element_type=F32).astype(BF)
    bd_ref[...] = jnp.dot(xb, wbd_ref[...], preferred_element_type=F32)
    go_ref[...] = jnp.dot(xb, wgo_ref[...], preferred_element_type=F32).astype(BF)
    u = jnp.dot(xb, wc_ref[...], preferred_element_type=F32)

    @pl.when(it == 0)
    def _():
        tail_ref[...] = jnp.zeros_like(tail_ref)

    ext = jnp.concatenate([tail_ref[...], u], axis=0)
    tail_ref[...] = u[tm - POOL_TAIL:, :]
    pos = (it * tm + 1 + lax.broadcasted_iota(I32, (tm, 1), 0)).astype(F32)
    s = ext
    for gi, win in enumerate(C_WINDOWS):
        if gi:
            s = s[:, C_GROUP_DIM:]
        s = s + pltpu.roll(s, win // 2, 0)
        lo_c, hi_c = gi * C_GROUP_DIM, (gi + 1) * C_GROUP_DIM
        cnt = jnp.minimum(pos, float(win))
        pooled = s[POOL_TAIL:, :C_GROUP_DIM] / cnt - u[:, lo_c:hi_c]
        mixed = jnp.dot(pooled.astype(BF), pw_ref[gi], preferred_element_type=F32)
        oc_ref[:, lo_c:hi_c] = (mixed * ps_ref[:, lo_c:hi_c]).astype(BF)


def _proj(x2d, wa, wb, wbd, wgo, wc, pw, ps, *, seq, tm):
    t = x2d.shape[0]
    full = lambda a: pl.BlockSpec(a.shape, lambda i: (0,) * a.ndim)
    row = lambda n: pl.BlockSpec((tm, n), lambda i: (i, 0))
    return pl.pallas_call(
        functools.partial(_proj_kernel, tiles_per_seq=seq // tm, tm=tm),
        grid=(t // tm,),
        in_specs=[row(D_MODEL), full(wa), full(wb), full(wbd), full(wgo), full(wc), full(pw), full(ps)],
        out_specs=[row(3 * A_WIDTH), row(3 * B_WIDTH), row(LANES), row(B_WIDTH), row(C_WIDTH)],
        out_shape=[jax.ShapeDtypeStruct((t, 3 * A_WIDTH), BF),
                   jax.ShapeDtypeStruct((t, 3 * B_WIDTH), BF),
                   jax.ShapeDtypeStruct((t, LANES), F32),
                   jax.ShapeDtypeStruct((t, B_WIDTH), BF),
                   jax.ShapeDtypeStruct((t, C_WIDTH), BF)],
        scratch_shapes=[pltpu.VMEM((POOL_TAIL, C_WIDTH), F32)],
        compiler_params=_cparams("arbitrary"),
        name="proj",
    )(x2d, wa, wb, wbd, wgo, wc, pw, ps)


def _attn_bias_table(rel_bias):
    h = rel_bias.shape[0]
    pad = A_LEFT_CHUNKS * CHUNK
    width = KBAND + pad
    n = width + QBLOCK - 1
    far = jnp.repeat(rel_bias[:, -1:], QBLOCK - 1 + pad - A_REL_CLIP, axis=1)
    near = jnp.repeat(rel_bias[:, :1], n - far.shape[1] - rel_bias.shape[1], axis=1)
    w = jnp.concatenate([far, rel_bias[:, ::-1], near], axis=1).astype(F32)
    w = jnp.roll(w, -(QBLOCK - 1), axis=1)
    toep = jnp.tile(w, (1, QBLOCK))[:, :QBLOCK * (n - 1)].reshape(h, QBLOCK, n - 1)[:, :, :width]
    r = jnp.arange(QBLOCK)[:, None]
    m = jnp.arange(width)[None, :]
    kc, qc = m // CHUNK, r // CHUNK
    valid = (kc >= qc) & (kc <= qc + A_LEFT_CHUNKS)
    tab = jnp.where(valid[None], toep, NEG_BIG)
    return tab.reshape(h, QBLOCK, width // LANES, LANES).transpose(0, 2, 1, 3)


def _attn_kernel(q_ref, k_ref, v_ref, tab_ref, o_ref):
    j = pl.program_id(1)
    first_blk = A_LEFT_CHUNKS * CHUNK // QBLOCK
    kstart = pl.multiple_of(jnp.maximum(j - first_blk, 0) * QBLOCK, QBLOCK)
    offb = jnp.maximum(first_blk - j, 0)
    lane = lax.broadcasted_iota(I32, (QBLOCK, LANES), 1)
    lo = lane < A_HEAD_DIM
    nkb = KBAND // LANES
    pairs = range(A_HEADS // 2)
    cols = [slice(hp * LANES, (hp + 1) * LANES) for hp in pairs]
    scores = []
    for hp in pairs:
        qp = q_ref[0, :, cols[hp]] * jnp.asarray(A_HEAD_DIM ** -0.5, BF)
        zero = jnp.zeros_like(qp)
        q2 = jnp.concatenate([jnp.where(lo, qp, zero), jnp.where(lo, zero, qp)], axis=0)
        kp = k_ref[0, pl.ds(kstart, KBAND), cols[hp]]
        scores.append(lax.dot_general(q2, kp, (((1,), (1,)), ((), ())), preferred_element_type=F32))
    probs, denoms = [], []
    for hp in pairs:
        bias = jnp.concatenate(
            [jnp.concatenate([tab_ref[2 * hp + par, offb + i] for i in range(nkb)], axis=1) for par in range(2)],
            axis=0)
        s = scores[hp] + bias
        p = jnp.exp(s - jnp.max(s, axis=-1, keepdims=True))
        denoms.append(jnp.sum(p, axis=-1, keepdims=True))
        probs.append(p.astype(BF))
    for hp in pairs:
        vp = v_ref[0, pl.ds(kstart, KBAND), cols[hp]]
        o2 = jnp.dot(probs[hp], vp, preferred_element_type=F32) / denoms[hp]
        o_ref[0, :, cols[hp]] = jnp.where(lo, o2[:QBLOCK], o2[QBLOCK:]).astype(BF)


def _attn(qkva, tab):
    b, l, _ = qkva.shape
    return pl.pallas_call(
        _attn_kernel,
        grid=(b, l // QBLOCK),
        in_specs=[pl.BlockSpec((1, QBLOCK, A_WIDTH), lambda bi, j: (bi, j, 0)),
                  pl.BlockSpec((1, l, A_WIDTH), lambda bi, j: (bi, 0, 1)),
                  pl.BlockSpec((1, l, A_WIDTH), lambda bi, j: (bi, 0, 2)),
                  pl.BlockSpec(tab.shape, lambda bi, j: (0, 0, 0, 0))],
        out_specs=pl.BlockSpec((1, QBLOCK, A_WIDTH), lambda bi, j: (bi, j, 0)),
        out_shape=jax.ShapeDtypeStruct((b, l, A_WIDTH), BF),
        compiler_params=_cparams("arbitrary", "arbitrary"),
        name="attn",
    )(qkva, qkva, qkva, tab)


CONV_HIST = 16
GDN_PREP_UNROLL = 2


def _gdn_kernel(qkv_ref, bd_ref, go_ref, cw_ref, alog_ref, dtb_ref, ng_ref, o_ref,
                u_ref, wq_ref, qkkd_ref, egl_ref, state_ref, *, n_chunks):
    rt = lax.broadcasted_iota(I32, (CHUNK, CHUNK), 0)
    ct = lax.broadcasted_iota(I32, (CHUNK, CHUNK), 1)
    tril = rt >= ct
    strict = rt > ct
    trilf = tril.astype(F32)
    eye_f = (rt == ct).astype(F32)
    eye_d = (lax.broadcasted_iota(I32, (B_HEAD_DIM, B_HEAD_DIM), 0)
             == lax.broadcasted_iota(I32, (B_HEAD_DIM, B_HEAD_DIM), 1)).astype(BF)
    neg_rate = -jnp.exp(alog_ref[...])
    sr = lax.broadcasted_iota(I32, (B_CONV * CHUNK, CONV_HIST + CHUNK), 0)
    sc = lax.broadcasted_iota(I32, (B_CONV * CHUNK, CONV_HIST + CHUNK), 1)
    shift_sel = (sc == CONV_HIST + jnp.bitwise_and(sr, CHUNK - 1)
                 - jnp.right_shift(sr, CHUNK.bit_length() - 1)).astype(BF)

    def heads(fn):
        return jnp.concatenate([fn(h, slice(h * B_HEAD_DIM, (h + 1) * B_HEAD_DIM)) for h in range(B_HEADS)],
                               axis=1)

    def prep(c):
        r0 = pl.multiple_of(c * CHUNK, CHUNK)
        cur = qkv_ref[0, pl.ds(r0, CHUNK), :]
        h0 = pl.multiple_of(jnp.maximum(r0 - CONV_HIST, 0), CONV_HIST)
        hist = qkv_ref[0, pl.ds(h0, CONV_HIST), :]
        hist = jnp.where(c > 0, hist, jnp.zeros_like(hist))
        taps = jnp.dot(shift_sel, jnp.concatenate([hist, cur], axis=0), preferred_element_type=F32)
        acc = taps[:CHUNK] * cw_ref[B_CONV - 1:B_CONV, :]
        for back in range(1, B_CONV):
            tap = B_CONV - 1 - back
            acc = acc + taps[back * CHUNK:(back + 1) * CHUNK] * cw_ref[tap:tap + 1, :]
        act = _silu(acc)
        q_all, k_all, v_all = act[:, :B_WIDTH], act[:, B_WIDTH:2 * B_WIDTH], act[:, 2 * B_WIDTH:]

        bd = bd_ref[0, pl.ds(r0, CHUNK), :]
        beta_all = _sigmoid(bd)
        z = bd + dtb_ref[...]
        softplus = jnp.maximum(z, 0.0) + jnp.log1p(jnp.exp(-jnp.abs(z)))
        g_all = neg_rate * softplus
        gc_all = jnp.dot(trilf, g_all, preferred_element_type=F32,
                         precision=lax.Precision.HIGHEST)
        egc_all = jnp.exp(gc_all)
        glast = gc_all[CHUNK - 1:CHUNK, :]
        ekd_all = jnp.exp(glast - gc_all)
        egl = jnp.exp(glast)
        col = lambda m, h: m[:, B_HEADS + h:B_HEADS + h + 1]

        def l2n(m):
            return m * lax.rsqrt(jnp.sum(m * m, axis=-1, keepdims=True) + RMS_EPS)

        qn = heads(lambda h, hs: l2n(q_all[:, hs]) * (B_HEAD_DIM ** -0.5))
        kn = heads(lambda h, hs: l2n(k_all[:, hs]))
        kbeta = heads(lambda h, hs: kn[:, hs] * beta_all[:, h:h + 1])
        vbeta = heads(lambda h, hs: v_all[:, hs] * beta_all[:, h:h + 1])
        kbe = heads(lambda h, hs: kbeta[:, hs] * col(egc_all, h))
        qe = heads(lambda h, hs: qn[:, hs] * col(egc_all, h))
        chains = []
        for h in range(B_HEADS):
            hs = slice(h * B_HEAD_DIM, (h + 1) * B_HEAD_DIM)
            gcc = jnp.broadcast_to(col(gc_all, h), (CHUNK, CHUNK))
            gcr = jnp.sum(gcc * eye_f, axis=0, keepdims=True)
            chains.append(dict(
                c=c, h=h, hs=hs,
                decay=jnp.exp(jnp.where(tril, gcc - gcr, -jnp.inf)),
                kq=jnp.concatenate([kbeta[:, hs], qn[:, hs]], axis=0).astype(BF),
                kn=kn[:, hs].astype(BF),
                rhs=jnp.concatenate([vbeta[:, hs], kbe[:, hs]], axis=1).astype(BF),
                kd=(kn[:, hs] * col(ekd_all, h)).astype(BF)))
        egl_row = heads(lambda h, hs: jnp.broadcast_to(col(egl, h), (1, B_HEAD_DIM)))
        return chains, qe.astype(BF), egl_row

    nt = lambda a, b: lax.dot_general(a, b, (((1,), (1,)), ((), ())), preferred_element_type=F32)
    mm = lambda a, b: jnp.dot(a, b, preferred_element_type=F32)

    def prep_levels(i):
        fronts = [prep(i * GDN_PREP_UNROLL + g) for g in range(GDN_PREP_UNROLL)]
        ch = [x for f in fronts for x in f[0]]
        aq = [nt(x["kq"], x["kn"]) for x in ch]
        kdt = [nt(eye_d, x["kd"]) for x in ch]
        yield
        qk = [(a[CHUNK:] * x["decay"]).astype(BF) for x, a in zip(ch, aq)]
        pw = [-jnp.where(strict, a[:CHUNK] * x["decay"], 0.0) for x, a in zip(ch, aq)]
        tmat = [eye_f + p for p in pw]
        pwb = [p.astype(BF) for p in pw]
        pw = [mm(p, p) for p in pwb]
        yield
        n_sq = CHUNK.bit_length() - 2
        for lvl in range(n_sq):
            pwb = [p.astype(BF) for p in pw]
            if lvl + 1 < n_sq:
                both = [mm(jnp.concatenate([t.astype(BF), p], axis=0), p) for t, p in zip(tmat, pwb)]
                yield
                tmat = [t + r[:CHUNK] for t, r in zip(tmat, both)]
                pw = [r[CHUNK:] for r in both]
            else:
                last = [mm(t.astype(BF), p) for t, p in zip(tmat, pwb)]
                yield
                tmat = [t + r for t, r in zip(tmat, last)]
        uw = [mm(t.astype(BF), x["rhs"]) for t, x in zip(tmat, ch)]
        yield
        for x, r, qkx, kt in zip(ch, uw, qk, kdt):
            u_ref[x["c"], :, x["hs"]] = r[:, :B_HEAD_DIM]
            wq_ref[x["c"], :CHUNK, x["hs"]] = r[:, B_HEAD_DIM:].astype(BF)
            qkkd_ref[x["c"], x["h"], :CHUNK, :] = qkx
            qkkd_ref[x["c"], x["h"], CHUNK:, :] = kt.astype(BF)
        for g, (_, qe_b, egl_row) in enumerate(fronts):
            wq_ref[i * GDN_PREP_UNROLL + g, CHUNK:, :] = qe_b
            egl_ref[i * GDN_PREP_UNROLL + g] = egl_row

    def step_levels(c):
        r0 = pl.multiple_of(c * CHUNK, CHUNK)
        st = state_ref[...]
        stb = st.astype(BF)
        hss = [slice(h * B_HEAD_DIM, (h + 1) * B_HEAD_DIM) for h in range(B_HEADS)]
        r1 = [mm(wq_ref[c, :, hs], stb[:, hs]) for hs in hss]
        yield
        v_new = [(u_ref[c, :, hs] - r[:CHUNK]).astype(BF) for hs, r in zip(hss, r1)]
        r2 = [mm(qkkd_ref[c, h], v_new[h]) for h in range(B_HEADS)]
        yield
        o_all = jnp.concatenate([a[CHUNK:] + b[:CHUNK] for a, b in zip(r1, r2)], axis=1)
        state_ref[...] = st * egl_ref[c] + jnp.concatenate([b[CHUNK:] for b in r2], axis=1)
        gate = go_ref[0, pl.ds(r0, CHUNK), :].astype(F32)
        o_n = heads(lambda h, hs: o_all[:, hs]
                    * lax.rsqrt(jnp.mean(o_all[:, hs] * o_all[:, hs], axis=-1, keepdims=True) + RMS_EPS)
                    * ng_ref[...])
        o_ref[0, pl.ds(r0, CHUNK), :] = (o_n * _silu(gate)).astype(BF)

    def steps_of(i):
        for g in range(GDN_PREP_UNROLL):
            yield from step_levels(i * GDN_PREP_UNROLL + g)

    def run_interleaved(*gens):
        live = list(gens)
        while live:
            for g in list(live):
                if next(g, StopIteration) is StopIteration:
                    live.remove(g)

    n_groups = n_chunks // GDN_PREP_UNROLL
    state_ref[...] = jnp.zeros_like(state_ref)
    run_interleaved(prep_levels(0))

    def body(i, carry):
        run_interleaved(steps_of(i - 1), prep_levels(i))
        return carry

    lax.fori_loop(1, n_groups, body, 0)
    run_interleaved(steps_of(n_groups - 1))


def _gdn(qkvb, bd, go, cw, alog, dtb, ng):
    b, l, _ = qkvb.shape
    nc = l // CHUNK
    full = lambda a: pl.BlockSpec(a.shape, lambda bi: (0,) * a.ndim)
    seq = lambda n: pl.BlockSpec((1, l, n), lambda bi: (bi, 0, 0))
    return pl.pallas_call(
        functools.partial(_gdn_kernel, n_chunks=nc),
        grid=(b,),
        in_specs=[seq(3 * B_WIDTH), seq(LANES), seq(B_WIDTH), full(cw), full(alog), full(dtb), full(ng)],
        out_specs=seq(B_WIDTH),
        out_shape=jax.ShapeDtypeStruct((b, l, B_WIDTH), BF),
        scratch_shapes=[pltpu.VMEM((nc, CHUNK, B_WIDTH), F32),
                        pltpu.VMEM((nc, 2 * CHUNK, B_WIDTH), BF),
                        pltpu.VMEM((nc, B_HEADS, CHUNK + B_HEAD_DIM, CHUNK), BF),
                        pltpu.VMEM((nc, 1, B_WIDTH), F32),
                        pltpu.VMEM((B_HEAD_DIM, B_WIDTH), F32)],
        compiler_params=_cparams("arbitrary"),
        name="gdn",
    )(qkvb, bd, go, cw, alog, dtb, ng)


def _merge_kernel(x_ref, oa_ref, ob_ref, oc_ref, wg_ref, gb_ref, wa_ref, wb_ref, wc_ref, wo_ref,
                  lg_ref, lb_ref, wrh_ref, wrl_ref, rb_ref, x1_ref, lgt_ref):
    x = x_ref[...]
    xb = x.astype(BF)
    y = None
    for i, (o_ref, w_ref) in enumerate(((oa_ref, wa_ref), (ob_ref, wb_ref), (oc_ref, wc_ref))):
        gl = jnp.dot(xb, wg_ref[:, i * D_MODEL:(i + 1) * D_MODEL], preferred_element_type=F32)
        gate = _sigmoid(gl + gb_ref[i:i + 1, :])
        br = gate * jnp.dot(o_ref[...], w_ref[...], preferred_element_type=F32)
        y = br if y is None else y + br
    mix = jnp.dot(y.astype(BF), wo_ref[...], preferred_element_type=F32)
    x1 = _layer_norm(DN_ALPHA * x + mix, lg_ref[...], lb_ref[...])
    x1_ref[...] = x1
    xh, xl = _split_bf16(x1)
    nt = lambda a, b: lax.dot_general(a, b, (((1,), (1,)), ((), ())), preferred_element_type=F32)
    lgt_ref[...] = nt(wrh_ref[...], xh) + (nt(wrh_ref[...], xl) + nt(wrl_ref[...], xh)) + rb_ref[...]


def _merge(x2d, oa, ob, oc, wg, gb, wa, wb, wc, wo, lg, lb, wrh, wrl, rb, *, tm):
    t = x2d.shape[0]
    full = lambda a: pl.BlockSpec(a.shape, lambda i: (0,) * a.ndim)
    row = lambda n: pl.BlockSpec((tm, n), lambda i: (i, 0))
    return pl.pallas_call(
        _merge_kernel,
        grid=(t // tm,),
        in_specs=[row(D_MODEL), row(A_WIDTH), row(B_WIDTH), row(C_WIDTH), full(wg), full(gb), full(wa),
                  full(wb), full(wc), full(wo), full(lg), full(lb), full(wrh), full(wrl), full(rb)],
        out_specs=[row(D_MODEL), pl.BlockSpec((ROUTER_ROWS, tm), lambda i: (0, i))],
        out_shape=[jax.ShapeDtypeStruct((t, D_MODEL), F32),
                   jax.ShapeDtypeStruct((ROUTER_ROWS, t), F32)],
        compiler_params=_cparams("arbitrary"),
        name="merge",
    )(x2d, oa, ob, oc, wg, gb, wa, wb, wc, wo, lg, lb, wrh, wrl, rb)


def _route_kernel(lgt_ref, eid_ref, wts_ref, rank_ref, lpos_ref, tab_ref, cnt_ref, run_ref, *, tn):
    @pl.when(pl.program_id(0) == 0)
    def _():
        run_ref[...] = jnp.zeros_like(run_ref)

    le = lgt_ref[0:N_EXPERTS, :]
    lg = lgt_ref[N_EXPERTS:N_EXPERTS + MOE_GROUPS, :]
    gi = lax.broadcasted_iota(I32, (MOE_GROUPS, tn), 0).astype(F32)
    ei = lax.broadcasted_iota(I32, (N_EXPERTS, tn), 0).astype(F32)
    eg = jnp.right_shift(lax.broadcasted_iota(I32, (N_EXPERTS, tn), 0),
                         MOE_PER_GROUP.bit_length() - 1).astype(F32)
    mg = jnp.max(lg, axis=0, keepdims=True)
    gsel = jnp.min(jnp.where(lg == mg, gi, float(MOE_GROUPS)), axis=0, keepdims=True)
    p_top = 1.0 / jnp.sum(jnp.exp(lg - mg), axis=0, keepdims=True)
    l1 = jnp.where(eg == gsel, le, -jnp.inf)
    m1 = jnp.max(l1, axis=0, keepdims=True)
    i1 = jnp.min(jnp.where(l1 == m1, ei, float(N_EXPERTS)), axis=0, keepdims=True)
    l2 = jnp.where(ei == i1, -jnp.inf, l1)
    m2 = jnp.max(l2, axis=0, keepdims=True)
    i2 = jnp.min(jnp.where(l2 == m2, ei, float(N_EXPERTS)), axis=0, keepdims=True)
    e2 = jnp.exp(m2 - m1)
    den = 1.0 + e2
    eid_ref[...] = jnp.concatenate([i1, i2], axis=0).astype(I32)
    wts_ref[...] = jnp.concatenate([p_top / den, p_top * (e2 / den)], axis=0)

    before = (lax.broadcasted_iota(I32, (tn, tn), 0) < lax.broadcasted_iota(I32, (tn, tn), 1)).astype(BF)
    run = run_ref[...]
    oh = [ei == ik for ik in (i1, i2)]
    prefix = [jnp.dot(o.astype(BF), before, preferred_element_type=F32) for o in oh]
    cnt_k = [jnp.sum(o.astype(F32), axis=1, keepdims=True) for o in oh]
    cnt = cnt_k[0] + cnt_k[1]
    cnt_al = jnp.floor((cnt + (RUN_ALIGN - 1)) * (1.0 / RUN_ALIGN)) * RUN_ALIGN
    er = lax.broadcasted_iota(I32, (N_EXPERTS, N_EXPERTS), 0)
    ec = lax.broadcasted_iota(I32, (N_EXPERTS, N_EXPERTS), 1)
    lstart = jnp.dot((ec < er).astype(BF), jnp.broadcast_to(cnt_al, (N_EXPERTS, LANES)).astype(BF),
                     preferred_element_type=F32)[:, 0:1]
    first = [0.0, cnt_k[0]]
    pick = lambda k, base: jnp.sum(jnp.where(oh[k], prefix[k] + first[k] + base, 0.0), axis=0, keepdims=True)
    rank_ref[...] = jnp.concatenate([pick(0, run), pick(1, run)], axis=0).astype(I32)
    lpos_ref[...] = jnp.concatenate([pick(0, lstart), pick(1, lstart)], axis=0).astype(I32)
    eye_l = (lax.broadcasted_iota(I32, (N_EXPERTS, LANES), 0) == lax.broadcasted_iota(I32, (N_EXPERTS, LANES), 1))
    as_row = lambda colv: jnp.sum(jnp.where(eye_l, colv, 0.0), axis=0, keepdims=True)
    tab_ref[0] = jnp.concatenate([as_row(run), as_row(cnt), as_row(lstart),
                                  jnp.zeros((SUBLANES - 3, LANES), F32)], axis=0).astype(I32)
    run = run + cnt_al
    run_ref[...] = run
    cnt_ref[...] = jnp.broadcast_to(run, cnt_ref.shape)


def _route(lgt, *, tn):
    t = lgt.shape[1]
    tok = lambda: pl.BlockSpec((MOE_TOPK, tn), lambda i: (0, i))
    return pl.pallas_call(
        functools.partial(_route_kernel, tn=tn),
        grid=(t // tn,),
        in_specs=[pl.BlockSpec((ROUTER_ROWS, tn), lambda i: (0, i))],
        out_specs=[tok(), tok(), tok(), tok(), pl.BlockSpec((1, SUBLANES, LANES), lambda i: (i, 0, 0)),
                   pl.BlockSpec((N_EXPERTS, LANES), lambda i: (0, 0))],
        out_shape=[jax.ShapeDtypeStruct((MOE_TOPK, t), I32),
                   jax.ShapeDtypeStruct((MOE_TOPK, t), F32),
                   jax.ShapeDtypeStruct((MOE_TOPK, t), I32),
                   jax.ShapeDtypeStruct((MOE_TOPK, t), I32),
                   jax.ShapeDtypeStruct((t // tn, SUBLANES, LANES), I32),
                   jax.ShapeDtypeStruct((N_EXPERTS, LANES), F32)],
        scratch_shapes=[pltpu.VMEM((N_EXPERTS, 1), F32)],
        compiler_params=_cparams("arbitrary"),
        name="route",
    )(lgt)


def _dest_kernel(eid_ref, rank_ref, cnt_ref, tab_ref, dest_ref, gtab_ref, bexp_ref, nused_ref, *, n_blocks):
    cnt = cnt_ref[...].astype(I32)
    shift = EXPERT_BLOCK.bit_length() - 1
    padded_blocks = jnp.right_shift(cnt + (EXPERT_BLOCK - 1), shift)
    er = lax.broadcasted_iota(I32, (N_EXPERTS, LANES), 0)
    start = jnp.zeros((N_EXPERTS, LANES), I32)
    for e in range(N_EXPERTS - 1):
        start = start + jnp.where(er > e, padded_blocks[e:e + 1, :], 0)
    end = start + padded_blocks
    eid = eid_ref[...]
    d = rank_ref[...]
    for e in range(N_EXPERTS):
        d = d + jnp.where(eid == e, start[e:e + 1, 0:1] * EXPERT_BLOCK, 0)
    dest_ref[...] = d
    el = lax.broadcasted_iota(I32, (N_EXPERTS, LANES), 1)
    start_row = jnp.sum(jnp.where(er == el, start * EXPERT_BLOCK, 0), axis=0, keepdims=True)
    tab = tab_ref[...]
    first_row = lax.broadcasted_iota(I32, tab.shape, 1) == 0
    gtab_ref[...] = tab + jnp.where(first_row, start_row[None], 0)

    @pl.when(pl.program_id(0) == 0)
    def _():
        blk = lax.broadcasted_iota(I32, (1, n_blocks), 1)
        owner = jnp.zeros((1, n_blocks), I32)
        for e in range(N_EXPERTS):
            owner = owner + (blk >= end[e:e + 1, 0:1]).astype(I32)
        bexp_ref[...] = jnp.minimum(owner, N_EXPERTS - 1)
        nused_ref[...] = end[N_EXPERTS - 1:N_EXPERTS, :]


def _dest(eid, rank, cnt, tab, *, tn, n_blocks):
    t = eid.shape[1]
    tiles = tab.shape[0] // (t // tn)
    tok = lambda: pl.BlockSpec((MOE_TOPK, tn), lambda i: (0, i))
    tabs = lambda: pl.BlockSpec((tiles, SUBLANES, LANES), lambda i: (i, 0, 0))
    return pl.pallas_call(
        functools.partial(_dest_kernel, n_blocks=n_blocks),
        grid=(t // tn,),
        in_specs=[tok(), tok(), pl.BlockSpec(cnt.shape, lambda i: (0, 0)), tabs()],
        out_specs=[tok(), tabs(), pl.BlockSpec((1, n_blocks), lambda i: (0, 0)),
                   pl.BlockSpec((1, LANES), lambda i: (0, 0))],
        out_shape=[jax.ShapeDtypeStruct((MOE_TOPK, t), I32),
                   jax.ShapeDtypeStruct(tab.shape, I32),
                   jax.ShapeDtypeStruct((1, n_blocks), I32),
                   jax.ShapeDtypeStruct((1, LANES), I32)],
        compiler_params=_cparams("arbitrary"),
        name="dest",
    )(eid, rank, cnt, tab)


def _dispatch_kernel(tab_ref, lpos_ref, x_ref, xs_in_ref, xs_ref, stage_ref, sem, *, td):
    del xs_in_ref
    rows = stage_ref.shape[0]
    r = lax.broadcasted_iota(I32, (rows, td), 0)
    sel = jnp.logical_or(r == lpos_ref[0:1, :], r == lpos_ref[1:2, :]).astype(BF)
    stage_ref[...] = jnp.dot(sel, x_ref[...].astype(BF), preferred_element_type=F32)

    def piece(e, j):
        src = pl.multiple_of(tab_ref[0, 2, e] + j * RUN_ALIGN, RUN_ALIGN)
        dst = pl.multiple_of(tab_ref[0, 0, e] + j * RUN_ALIGN, RUN_ALIGN)
        return pltpu.make_async_copy(stage_ref.at[pl.ds(src, RUN_ALIGN), :], xs_ref.at[pl.ds(dst, RUN_ALIGN), :], sem)

    def n_pieces(e):
        return (tab_ref[0, 1, e] + (RUN_ALIGN - 1)) // RUN_ALIGN

    def per_expert(fn):
        def body(e, carry):
            lax.fori_loop(0, n_pieces(e), lambda j, c: (fn(e, j), c)[1], 0)
            return carry
        lax.fori_loop(0, N_EXPERTS, body, 0)

    per_expert(lambda e, j: piece(e, j).start())
    per_expert(lambda e, j: piece(e, j).wait())


def _dispatch(gtab, lpos, x1, xs_zero, *, td):
    t = x1.shape[0]
    stage_rows = MOE_TOPK * td + N_EXPERTS * RUN_ALIGN
    return pl.pallas_call(
        functools.partial(_dispatch_kernel, td=td),
        grid=(t // td,),
        in_specs=[pl.BlockSpec((1, SUBLANES, LANES), lambda i: (i, 0, 0), memory_space=pltpu.SMEM),
                  pl.BlockSpec((MOE_TOPK, td), lambda i: (0, i)),
                  pl.BlockSpec((td, D_MODEL), lambda i: (i, 0)),
                  pl.BlockSpec(memory_space=pl.ANY)],
        out_specs=pl.BlockSpec(memory_space=pl.ANY),
        out_shape=jax.ShapeDtypeStruct(xs_zero.shape, xs_zero.dtype),
        scratch_shapes=[pltpu.VMEM((stage_rows, D_MODEL), F32), pltpu.SemaphoreType.DMA(())],
        input_output_aliases={3: 0},
        compiler_params=_cparams("arbitrary"),
        name="dispatch",
    )(gtab, lpos, x1, xs_zero)


def _expert_kernel(bexp_ref, nused_ref, xs_ref, w1_ref, w3_ref, w2_ref, y_ref):
    del bexp_ref
    i = pl.program_id(0)

    @pl.when(i < nused_ref[0])
    def _():
        xb = xs_ref[...].astype(BF)
        h1 = jnp.dot(xb, w1_ref[0], preferred_element_type=F32)
        h3 = jnp.dot(xb, w3_ref[0], preferred_element_type=F32)
        hid = (_silu(h1) * h3).astype(BF)
        y_ref[...] = jnp.dot(hid, w2_ref[0], preferred_element_type=F32)

    @pl.when(i >= nused_ref[0])
    def _():
        y_ref[...] = jnp.zeros_like(y_ref)


def _expert(bexp, nused, xs, w1, w3, w2):
    cap = xs.shape[0]
    nb = cap // EXPERT_BLOCK
    last = lambda i, nu: jnp.minimum(i, nu[0] - 1)
    grid_spec = pltpu.PrefetchScalarGridSpec(
        num_scalar_prefetch=2,
        grid=(nb,),
        in_specs=[pl.BlockSpec((EXPERT_BLOCK, D_MODEL), lambda i, be, nu: (last(i, nu), 0)),
                  pl.BlockSpec((1, D_MODEL, MOE_FF), lambda i, be, nu: (be[last(i, nu)], 0, 0)),
                  pl.BlockSpec((1, D_MODEL, MOE_FF), lambda i, be, nu: (be[last(i, nu)], 0, 0)),
                  pl.BlockSpec((1, MOE_FF, D_MODEL), lambda i, be, nu: (be[last(i, nu)], 0, 0))],
        out_specs=pl.BlockSpec((EXPERT_BLOCK, D_MODEL), lambda i, be, nu: (i, 0)),
    )
    return pl.pallas_call(
        _expert_kernel,
        grid_spec=grid_spec,
        out_shape=jax.ShapeDtypeStruct((cap, D_MODEL), F32),
        compiler_params=_cparams("arbitrary"),
        name="expert",
    )(bexp, nused, xs, w1, w3, w2)


def _combine_kernel(dest_ref, x_ref, w_ref, lg_ref, lb_ref, yb_ref, o_ref, buf_ref, sem, *, tc):
    def issue(t, carry):
        for k in range(MOE_TOPK):
            pltpu.make_async_copy(yb_ref.at[pl.ds(dest_ref[k, t], 1), :],
                                  buf_ref.at[k, pl.ds(t, 1), :], sem).start()
        return carry

    lax.fori_loop(0, tc, issue, 0, unroll=DMA_ISSUE_UNROLL)
    pltpu.make_async_copy(buf_ref, buf_ref, sem).wait()
    ffn = w_ref[:, 0:1] * buf_ref[0] + w_ref[:, 1:2] * buf_ref[1]
    o_ref[...] = _layer_norm(DN_ALPHA * x_ref[...] + ffn, lg_ref[...], lb_ref[...])


def _combine(dest, x1, wts_t, lg, lb, yb, *, tc):
    t = x1.shape[0]
    full = lambda a: pl.BlockSpec(a.shape, lambda i: (0,) * a.ndim)
    return pl.pallas_call(
        functools.partial(_combine_kernel, tc=tc),
        grid=(t // tc,),
        in_specs=[pl.BlockSpec((MOE_TOPK, tc), lambda i: (0, i), memory_space=pltpu.SMEM),
                  pl.BlockSpec((tc, D_MODEL), lambda i: (i, 0)),
                  pl.BlockSpec((tc, MOE_TOPK), lambda i: (i, 0)),
                  full(lg), full(lb),
                  pl.BlockSpec(memory_space=pl.ANY)],
        out_specs=pl.BlockSpec((tc, D_MODEL), lambda i: (i, 0)),
        out_shape=jax.ShapeDtypeStruct((t, D_MODEL), F32),
        scratch_shapes=[pltpu.VMEM((MOE_TOPK, tc, D_MODEL), F32), pltpu.SemaphoreType.DMA(())],
        compiler_params=_cparams("arbitrary"),
        name="combine",
    )(dest, x1, wts_t, lg, lb, yb)


def _tile(n, want):
    while n % want:
        want //= 2
    return want


def _mixer(x2d, b, l, w_in, rel_bias, conv_w, a_log, dt_bias, norm_g, pool_w, pool_scale,
           w_br_a, w_br_b, w_br_c, gate_b, w_out, ln_g, ln_b, wr, rb):
    t = x2d.shape[0]
    c0 = 3 * A_WIDTH
    c1 = c0 + 3 * B_WIDTH
    c2 = c1 + 2 * B_HEADS
    c3 = c2 + B_WIDTH
    c4 = c3 + C_WIDTH
    wbf = w_in.astype(BF)
    wbd = jnp.pad(wbf[:, c1:c2], ((0, 0), (0, LANES - 2 * B_HEADS)))
    qkva, qkvb, bd, go, oc = _proj(
        x2d, wbf[:, :c0], wbf[:, c0:c1], wbd, wbf[:, c2:c3], wbf[:, c3:c4],
        pool_w.astype(BF), pool_scale.reshape(1, C_WIDTH), seq=l, tm=_tile(l, 1024))

    oa = _attn(qkva.reshape(b, l, 3 * A_WIDTH), _attn_bias_table(rel_bias)).reshape(t, A_WIDTH)

    lane_pad = lambda v: jnp.pad(v.reshape(1, B_HEADS), ((0, 0), (B_HEADS, LANES - 2 * B_HEADS)))
    ob = _gdn(qkvb.reshape(b, l, 3 * B_WIDTH), bd.reshape(b, l, LANES), go.reshape(b, l, B_WIDTH),
              conv_w, lane_pad(a_log), lane_pad(dt_bias), norm_g.reshape(1, B_HEAD_DIM)).reshape(t, B_WIDTH)

    wrh, wrl = _split_bf16(wr)
    return _merge(x2d, oa, ob, oc, wbf[:, c4:], gate_b, w_br_a.astype(BF), w_br_b.astype(BF),
                  w_br_c.astype(BF), w_out.astype(BF), ln_g.reshape(1, -1), ln_b.reshape(1, -1),
                  wrh, wrl, rb, tm=_tile(t, 1024))


def _moe(x1, lgt, w1, w3, w2, ln_g, ln_b):
    t = x1.shape[0]
    tile = _tile(t, MOE_TILE)
    max_rows = t * MOE_TOPK + (t // tile) * N_EXPERTS * (RUN_ALIGN - 1)
    n_blocks = -(-max_rows // EXPERT_BLOCK) + N_EXPERTS
    cap = n_blocks * EXPERT_BLOCK
    eid, wts, rank, lpos, tab, cnt = _route(lgt, tn=tile)
    dest, gtab, bexp, nused = _dest(eid, rank, cnt, tab, tn=_tile(t, 4 * tile), n_blocks=n_blocks)
    xs = _dispatch(gtab, lpos, x1, jnp.zeros((cap, D_MODEL), F32), td=tile)
    yb = _expert(bexp.reshape(n_blocks), nused[0, :1], xs, w1.astype(BF), w3.astype(BF), w2.astype(BF))
    return _combine(dest, x1, wts.T, ln_g.reshape(1, -1), ln_b.reshape(1, -1), yb, tc=_tile(t, 512))


def kernel(x, w_in, attn_rel_bias, gdn_conv_w, gdn_a_log, gdn_dt_bias, gdn_norm_g, pool_w, pool_scale,
           w_branch_a, w_branch_b, w_branch_c, gate_b, w_out, ln1_g, ln1_b, router_group_w,
           router_group_b, router_expert_w, router_expert_b, moe_w1, moe_w3, moe_w2, ln2_g, ln2_b):
    b, l, d = x.shape
    x2d = x.reshape(b * l, d)
    for i in range(DEPTH):
        wr = jnp.concatenate([router_expert_w[i].T, router_group_w[i].T,
                              jnp.zeros((ROUTER_ROWS - N_EXPERTS - MOE_GROUPS, d), F32)], axis=0)
        rb = jnp.concatenate([router_expert_b[i], router_group_b[i],
                              jnp.zeros((ROUTER_ROWS - N_EXPERTS - MOE_GROUPS,), F32)]).reshape(ROUTER_ROWS, 1)
        x1, lgt = _mixer(x2d, b, l, w_in[i], attn_rel_bias[i], gdn_conv_w[i], gdn_a_log[i], gdn_dt_bias[i],
                         gdn_norm_g[i], pool_w[i], pool_scale[i], w_branch_a[i], w_branch_b[i],
                         w_branch_c[i], gate_b[i], w_out[i], ln1_g[i], ln1_b[i], wr, rb)
        x2d = _moe(x1, lgt, moe_w1[i], moe_w3[i], moe_w2[i], ln2_g[i], ln2_b[i])
    return x2d.reshape(b, l, d)
```

```python
import functools

import jax
import jax.numpy as jnp
from jax import lax
from jax.experimental import pallas as pl
from jax.experimental.pallas import tpu as pltpu

BF = jnp.bfloat16
F32 = jnp.float32
I32 = jnp.int32

D_MODEL = 1024
DEPTH = 2
CHUNK = 64

A_HEADS = 8
A_HEAD_DIM = 64
A_WIDTH = A_HEADS * A_HEAD_DIM
A_LEFT_CHUNKS = 8
A_REL_CLIP = 256

B_HEADS = 4
B_HEAD_DIM = 128
B_WIDTH = B_HEADS * B_HEAD_DIM
B_CONV = 4

C_WINDOWS = (2, 4, 8, 16)
C_GROUPS = 4
C_GROUP_DIM = 128
C_WIDTH = C_GROUPS * C_GROUP_DIM

N_BRANCH = 3
MOE_GROUPS = 4
MOE_PER_GROUP = 8
N_EXPERTS = MOE_GROUPS * MOE_PER_GROUP
MOE_TOPK = 2
MOE_FF = 512

DN_ALPHA = (2 * DEPTH) ** 0.25
LN_EPS = 1e-5
RMS_EPS = 1e-6

LANES = 128
SUBLANES = 8
VMEM_LIMIT_BYTES = 56 * 1024 * 1024

QBLOCK = 2 * CHUNK
KBAND = (A_LEFT_CHUNKS + 2) * CHUNK
NEG_BIG = -1e30

ROUTER_ROWS = 40
EXPERT_BLOCK = 512
MOE_TILE = 512
RUN_ALIGN = 8


def _cparams(*sem):
    return pltpu.CompilerParams(dimension_semantics=sem, vmem_limit_bytes=VMEM_LIMIT_BYTES)


def _mm(a, b):
    return jnp.dot(a.astype(BF), b.astype(BF), preferred_element_type=F32)


def _mm_nt(a, b):
    return lax.dot_general(a.astype(BF), b.astype(BF), (((1,), (1,)), ((), ())),
                           preferred_element_type=F32)


def _mm_tn(a, b):
    return lax.dot_general(a.astype(BF), b.astype(BF), (((0,), (0,)), ((), ())),
                           preferred_element_type=F32)


def _split_bf16(a):
    hi = a.astype(BF)
    lo = (a - hi.astype(F32)).astype(BF)
    return hi, lo


def _mm3(a, b):
    ah, al = _split_bf16(a)
    bh, bl = _split_bf16(b)
    return (jnp.dot(ah, bh, preferred_element_type=F32)
            + (jnp.dot(ah, bl, preferred_element_type=F32)
               + jnp.dot(al, bh, preferred_element_type=F32)))


def _sigmoid(x):
    return 1.0 / (1.0 + jnp.exp(-x))


def _silu(x):
    return x * _sigmoid(x)


def _layer_norm(z, g, b):
    mu = jnp.mean(z, axis=-1, keepdims=True)
    zc = z - mu
    var = jnp.mean(zc * zc, axis=-1, keepdims=True)
    return zc * lax.rsqrt(var + LN_EPS) * g + b


POOL_TAIL = 16


def _proj_kernel(x_ref, wa_ref, wb_ref, wbd_ref, wgo_ref, wc_ref, pw_ref, ps_ref,
                 qkva_ref, qkvb_ref, bd_ref, go_ref, oc_ref, tail_ref, *, tiles_per_seq, tm):
    it = pl.program_id(0) % tiles_per_seq
    xb = x_ref[...].astype(BF)
    qkva_ref[...] = jnp.dot(xb, wa_ref[...], preferred_element_type=F32).astype(BF)
    qkvb_ref[...] = jnp.dot(xb, wb_ref[...], preferred_element_type=F32).astype(BF)
    bd_ref[...] = jnp.dot(xb, wbd_ref[...], preferred_element_type=F32)
    go_ref[...] = jnp.dot(xb, wgo_ref[...], preferred_element_type=F32).astype(BF)
    u = jnp.dot(xb, wc_ref[...], preferred_element_type=F32)

    @pl.when(it == 0)
    def _():
        tail_ref[...] = jnp.zeros_like(tail_ref)

    ext = jnp.concatenate([tail_ref[...], u], axis=0)
    tail_ref[...] = u[tm - POOL_TAIL:, :]
    pos = (it * tm + 1 + lax.broadcasted_iota(I32, (tm, 1), 0)).astype(F32)
    s = ext
    for gi, win in enumerate(C_WINDOWS):
        if gi:
            s = s[:, C_GROUP_DIM:]
        s = s + pltpu.roll(s, win // 2, 0)
        lo_c, hi_c = gi * C_GROUP_DIM, (gi + 1) * C_GROUP_DIM
        cnt = jnp.minimum(pos, float(win))
        pooled = s[POOL_TAIL:, :C_GROUP_DIM] / cnt - u[:, lo_c:hi_c]
        mixed = jnp.dot(pooled.astype(BF), pw_ref[gi], preferred_element_type=F32)
        oc_ref[:, lo_c:hi_c] = (mixed * ps_ref[:, lo_c:hi_c]).astype(BF)


def _proj(x2d, wa, wb, wbd, wgo, wc, pw, ps, *, seq, tm):
    t = x2d.shape[0]
    full = lambda a: pl.BlockSpec(a.shape, lambda i: (0,) * a.ndim)
    row = lambda n: pl.BlockSpec((tm, n), lambda i: (i, 0))
    return pl.pallas_call(
        functools.partial(_proj_kernel, tiles_per_seq=seq // tm, tm=tm),
        grid=(t // tm,),
        in_specs=[row(D_MODEL), full(wa), full(wb), full(wbd), full(wgo), full(wc), full(pw), full(ps)],
        out_specs=[row(3 * A_WIDTH), row(3 * B_WIDTH), row(LANES), row(B_WIDTH), row(C_WIDTH)],
        out_shape=[jax.ShapeDtypeStruct((t, 3 * A_WIDTH), BF),
                   jax.ShapeDtypeStruct((t, 3 * B_WIDTH), BF),
                   jax.ShapeDtypeStruct((t, LANES), F32),
                   jax.ShapeDtypeStruct((t, B_WIDTH), BF),
                   jax.ShapeDtypeStruct((t, C_WIDTH), BF)],
        scratch_shapes=[pltpu.VMEM((POOL_TAIL, C_WIDTH), F32)],
        compiler_params=_cparams("arbitrary"),
        name="proj",
    )(x2d, wa, wb, wbd, wgo, wc, pw, ps)


def _attn_bias_table(rel_bias):
    h = rel_bias.shape[0]
    pad = A_LEFT_CHUNKS * CHUNK
    width = KBAND + pad
    n = width + QBLOCK - 1
    far = jnp.repeat(rel_bias[:, -1:], QBLOCK - 1 + pad - A_REL_CLIP, axis=1)
    near = jnp.repeat(rel_bias[:, :1], n - far.shape[1] - rel_bias.shape[1], axis=1)
    w = jnp.concatenate([far, rel_bias[:, ::-1], near], axis=1).astype(F32)
    w = jnp.roll(w, -(QBLOCK - 1), axis=1)
    toep = jnp.tile(w, (1, QBLOCK))[:, :QBLOCK * (n - 1)].reshape(h, QBLOCK, n - 1)[:, :, :width]
    r = jnp.arange(QBLOCK)[:, None]
    m = jnp.arange(width)[None, :]
    kc, qc = m // CHUNK, r // CHUNK
    valid = (kc >= qc) & (kc <= qc + A_LEFT_CHUNKS)
    tab = jnp.where(valid[None], toep, NEG_BIG)
    return tab.reshape(h, QBLOCK, width // LANES, LANES).transpose(0, 2, 1, 3)


def _attn_kernel(q_ref, k_ref, v_ref, tab_ref, o_ref):
    j = pl.program_id(1)
    first_blk = A_LEFT_CHUNKS * CHUNK // QBLOCK
    kstart = pl.multiple_of(jnp.maximum(j - first_blk, 0) * QBLOCK, QBLOCK)
    offb = jnp.maximum(first_blk - j, 0)
    lane = lax.broadcasted_iota(I32, (QBLOCK, LANES), 1)
    lo = lane < A_HEAD_DIM
    nkb = KBAND // LANES
    pairs = range(A_HEADS // 2)
    cols = [slice(hp * LANES, (hp + 1) * LANES) for hp in pairs]
    scores = []
    for hp in pairs:
        qp = q_ref[0, :, cols[hp]] * jnp.asarray(A_HEAD_DIM ** -0.5, BF)
        zero = jnp.zeros_like(qp)
        q2 = jnp.concatenate([jnp.where(lo, qp, zero), jnp.where(lo, zero, qp)], axis=0)
        kp = k_ref[0, pl.ds(kstart, KBAND), cols[hp]]
        scores.append(lax.dot_general(q2, kp, (((1,), (1,)), ((), ())), preferred_element_type=F32))
    probs, denoms = [], []
    for hp in pairs:
        bias = jnp.concatenate(
            [jnp.concatenate([tab_ref[2 * hp + par, offb + i] for i in range(nkb)], axis=1) for par in range(2)],
            axis=0)
        s = scores[hp] + bias
        p = jnp.exp(s - jnp.max(s, axis=-1, keepdims=True))
        denoms.append(jnp.sum(p, axis=-1, keepdims=True))
        probs.append(p.astype(BF))
    for hp in pairs:
        vp = v_ref[0, pl.ds(kstart, KBAND), cols[hp]]
        o2 = jnp.dot(probs[hp], vp, preferred_element_type=F32) / denoms[hp]
        o_ref[0, :, cols[hp]] = jnp.where(lo, o2[:QBLOCK], o2[QBLOCK:]).astype(BF)


def _attn(qkva, tab):
    b, l, _ = qkva.shape
    return pl.pallas_call(
        _attn_kernel,
        grid=(b, l // QBLOCK),
        in_specs=[pl.BlockSpec((1, QBLOCK, A_WIDTH), lambda bi, j: (bi, j, 0)),
                  pl.BlockSpec((1, l, A_WIDTH), lambda bi, j: (bi, 0, 1)),
                  pl.BlockSpec((1, l, A_WIDTH), lambda bi, j: (bi, 0, 2)),
                  pl.BlockSpec(tab.shape, lambda bi, j: (0, 0, 0, 0))],
        out_specs=pl.BlockSpec((1, QBLOCK, A_WIDTH), lambda bi, j: (bi, j, 0)),
        out_shape=jax.ShapeDtypeStruct((b, l, A_WIDTH), BF),
        compiler_params=_cparams("arbitrary", "arbitrary"),
        name="attn",
    )(qkva, qkva, qkva, tab)


CONV_HIST = 16
GDN_PREP_UNROLL = 2


def _gdn_kernel(qkv_ref, bd_ref, go_ref, cw_ref, alog_ref, dtb_ref, ng_ref, o_ref,
                u_ref, wq_ref, qkkd_ref, egl_ref, state_ref, *, n_chunks):
    rt = lax.broadcasted_iota(I32, (CHUNK, CHUNK), 0)
    ct = lax.broadcasted_iota(I32, (CHUNK, CHUNK), 1)
    tril = rt >= ct
    strict = rt > ct
    trilf = tril.astype(F32)
    eye_f = (rt == ct).astype(F32)
    eye_d = (lax.broadcasted_iota(I32, (B_HEAD_DIM, B_HEAD_DIM), 0)
             == lax.broadcasted_iota(I32, (B_HEAD_DIM, B_HEAD_DIM), 1)).astype(BF)
    neg_rate = -jnp.exp(alog_ref[...])
    sr = lax.broadcasted_iota(I32, (B_CONV * CHUNK, CONV_HIST + CHUNK), 0)
    sc = lax.broadcasted_iota(I32, (B_CONV * CHUNK, CONV_HIST + CHUNK), 1)
    shift_sel = (sc == CONV_HIST + jnp.bitwise_and(sr, CHUNK - 1)
                 - jnp.right_shift(sr, CHUNK.bit_length() - 1)).astype(BF)

    def heads(fn):
        return jnp.concatenate([fn(h, slice(h * B_HEAD_DIM, (h + 1) * B_HEAD_DIM)) for h in range(B_HEADS)],
                               axis=1)

    def prep(c):
        r0 = pl.multiple_of(c * CHUNK, CHUNK)
        cur = qkv_ref[0, pl.ds(r0, CHUNK), :]
        h0 = pl.multiple_of(jnp.maximum(r0 - CONV_HIST, 0), CONV_HIST)
        hist = qkv_ref[0, pl.ds(h0, CONV_HIST), :]
        hist = jnp.where(c > 0, hist, jnp.zeros_like(hist))
        taps = jnp.dot(shift_sel, jnp.concatenate([hist, cur], axis=0), preferred_element_type=F32)
        acc = taps[:CHUNK] * cw_ref[B_CONV - 1:B_CONV, :]
        for back in range(1, B_CONV):
            tap = B_CONV - 1 - back
            acc = acc + taps[back * CHUNK:(back + 1) * CHUNK] * cw_ref[tap:tap + 1, :]
        act = _silu(acc)
        q_all, k_all, v_all = act[:, :B_WIDTH], act[:, B_WIDTH:2 * B_WIDTH], act[:, 2 * B_WIDTH:]

        bd = bd_ref[0, pl.ds(r0, CHUNK), :]
        beta_all = _sigmoid(bd)
        z = bd + dtb_ref[...]
        softplus = jnp.maximum(z, 0.0) + jnp.log1p(jnp.exp(-jnp.abs(z)))
        g_all = neg_rate * softplus
        gc_all = jnp.dot(trilf, g_all, preferred_element_type=F32,
                         precision=lax.Precision.HIGHEST)
        egc_all = jnp.exp(gc_all)
        glast = gc_all[CHUNK - 1:CHUNK, :]
        ekd_all = jnp.exp(glast - gc_all)
        egl = jnp.exp(glast)
        col = lambda m, h: m[:, B_HEADS + h:B_HEADS + h + 1]

        def l2n(m):
            return m * lax.rsqrt(jnp.sum(m * m, axis=-1, keepdims=True) + RMS_EPS)

        qn = heads(lambda h, hs: l2n(q_all[:, hs]) * (B_HEAD_DIM ** -0.5))
        kn = heads(lambda h, hs: l2n(k_all[:, hs]))
        kbeta = heads(lambda h, hs: kn[:, hs] * beta_all[:, h:h + 1])
        vbeta = heads(lambda h, hs: v_all[:, hs] * beta_all[:, h:h + 1])
        kbe = heads(lambda h, hs: kbeta[:, hs] * col(egc_all, h))
        qe = heads(lambda h, hs: qn[:, hs] * col(egc_all, h))
        chains = []
        for h in range(B_HEADS):
            hs = slice(h * B_HEAD_DIM, (h + 1) * B_HEAD_DIM)
            gcc = jnp.broadcast_to(col(gc_all, h), (CHUNK, CHUNK))
            gcr = jnp.sum(gcc * eye_f, axis=0, keepdims=True)
            chains.append(dict(
                c=c, h=h, hs=hs,
                decay=jnp.exp(jnp.where(tril, gcc - gcr, -jnp.inf)),
                kq=jnp.concatenate([kbeta[:, hs], qn[:, hs]], axis=0).astype(BF),
                kn=kn[:, hs].astype(BF),
                rhs=jnp.concatenate([vbeta[:, hs], kbe[:, hs]], axis=1).astype(BF),
                kd=(kn[:, hs] * col(ekd_all, h)).astype(BF)))
        egl_row = heads(lambda h, hs: jnp.broadcast_to(col(egl, h), (1, B_HEAD_DIM)))
        return chains, qe.astype(BF), egl_row

    nt = lambda a, b: lax.dot_general(a, b, (((1,), (1,)), ((), ())), preferred_element_type=F32)
    mm = lambda a, b: jnp.dot(a, b, preferred_element_type=F32)

    def prep_levels(i):
        fronts = [prep(i * GDN_PREP_UNROLL + g) for g in range(GDN_PREP_UNROLL)]
        ch = [x for f in fronts for x in f[0]]
        aq = [nt(x["kq"], x["kn"]) for x in ch]
        kdt = [nt(eye_d, x["kd"]) for x in ch]
        yield
        qk = [(a[CHUNK:] * x["decay"]).astype(BF) for x, a in zip(ch, aq)]
        pw = [-jnp.where(strict, a[:CHUNK] * x["decay"], 0.0) for x, a in zip(ch, aq)]
        tmat = [eye_f + p for p in pw]
        pwb = [p.astype(BF) for p in pw]
        pw = [mm(p, p) for p in pwb]
        yield
        n_sq = CHUNK.bit_length() - 2
        for lvl in range(n_sq):
            pwb = [p.astype(BF) for p in pw]
            if lvl + 1 < n_sq:
                both = [mm(jnp.concatenate([t.astype(BF), p], axis=0), p) for t, p in zip(tmat, pwb)]
                yield
                tmat = [t + r[:CHUNK] for t, r in zip(tmat, both)]
                pw = [r[CHUNK:] for r in both]
            else:
                last = [mm(t.astype(BF), p) for t, p in zip(tmat, pwb)]
                yield
                tmat = [t + r for t, r in zip(tmat, last)]
        uw = [mm(t.astype(BF), x["rhs"]) for t, x in zip(tmat, ch)]
        yield
        for x, r, qkx, kt in zip(ch, uw, qk, kdt):
            u_ref[x["c"], :, x["hs"]] = r[:, :B_HEAD_DIM]
            wq_ref[x["c"], :CHUNK, x["hs"]] = r[:, B_HEAD_DIM:].astype(BF)
            qkkd_ref[x["c"], x["h"], :CHUNK, :] = qkx
            qkkd_ref[x["c"], x["h"], CHUNK:, :] = kt.astype(BF)
        for g, (_, qe_b, egl_row) in enumerate(fronts):
            wq_ref[i * GDN_PREP_UNROLL + g, CHUNK:, :] = qe_b
            egl_ref[i * GDN_PREP_UNROLL + g] = egl_row

    def step_levels(c):
        r0 = pl.multiple_of(c * CHUNK, CHUNK)
        st = state_ref[...]
        stb = st.astype(BF)
        hss = [slice(h * B_HEAD_DIM, (h + 1) * B_HEAD_DIM) for h in range(B_HEADS)]
        r1 = [mm(wq_ref[c, :, hs], stb[:, hs]) for hs in hss]
        yield
        v_new = [(u_ref[c, :, hs] - r[:CHUNK]).astype(BF) for hs, r in zip(hss, r1)]
        r2 = [mm(qkkd_ref[c, h], v_new[h]) for h in range(B_HEADS)]
        yield
        o_all = jnp.concatenate([a[CHUNK:] + b[:CHUNK] for a, b in zip(r1, r2)], axis=1)
        state_ref[...] = st * egl_ref[c] + jnp.concatenate([b[CHUNK:] for b in r2], axis=1)
        gate = go_ref[0, pl.ds(r0, CHUNK), :].astype(F32)
        o_n = heads(lambda h, hs: o_all[:, hs]
                    * lax.rsqrt(jnp.mean(o_all[:, hs] * o_all[:, hs], axis=-1, keepdims=True) + RMS_EPS)
                    * ng_ref[...])
        o_ref[0, pl.ds(r0, CHUNK), :] = (o_n * _silu(gate)).astype(BF)

    def steps_of(i):
        for g in range(GDN_PREP_UNROLL):
            yield from step_levels(i * GDN_PREP_UNROLL + g)

    def run_interleaved(*gens):
        live = list(gens)
        while live:
            for g in list(live):
                if next(g, StopIteration) is StopIteration:
                    live.remove(g)

    n_groups = n_chunks // GDN_PREP_UNROLL
    state_ref[...] = jnp.zeros_like(state_ref)
    run_interleaved(prep_levels(0))

    def body(i, carry):
        run_interleaved(steps_of(i - 1), prep_levels(i))
        return carry

    lax.fori_loop(1, n_groups, body, 0)
    run_interleaved(steps_of(n_groups - 1))


def _gdn(qkvb, bd, go, cw, alog, dtb, ng):
    b, l, _ = qkvb.shape
    nc = l // CHUNK
    full = lambda a: pl.BlockSpec(a.shape, lambda bi: (0,) * a.ndim)
    seq = lambda n: pl.BlockSpec((1, l, n), lambda bi: (bi, 0, 0))
    return pl.pallas_call(
        functools.partial(_gdn_kernel, n_chunks=nc),
        grid=(b,),
        in_specs=[seq(3 * B_WIDTH), seq(LANES), seq(B_WIDTH), full(cw), full(alog), full(dtb), full(ng)],
        out_specs=seq(B_WIDTH),
        out_shape=jax.ShapeDtypeStruct((b, l, B_WIDTH), BF),
        scratch_shapes=[pltpu.VMEM((nc, CHUNK, B_WIDTH), F32),
                        pltpu.VMEM((nc, 2 * CHUNK, B_WIDTH), BF),
                        pltpu.VMEM((nc, B_HEADS, CHUNK + B_HEAD_DIM, CHUNK), BF),
                        pltpu.VMEM((nc, 1, B_WIDTH), F32),
                        pltpu.VMEM((B_HEAD_DIM, B_WIDTH), F32)],
        compiler_params=_cparams("arbitrary"),
        name="gdn",
    )(qkvb, bd, go, cw, alog, dtb, ng)


def _merge_kernel(x_ref, oa_ref, ob_ref, oc_ref, wg_ref, gb_ref, wa_ref, wb_ref, wc_ref, wo_ref,
                  lg_ref, lb_ref, wrh_ref, wrl_ref, rb_ref, x1_ref, lgt_ref):
    x = x_ref[...]
    xb = x.astype(BF)
    y = None
    for i, (o_ref, w_ref) in enumerate(((oa_ref, wa_ref), (ob_ref, wb_ref), (oc_ref, wc_ref))):
        gl = jnp.dot(xb, wg_ref[:, i * D_MODEL:(i + 1) * D_MODEL], preferred_element_type=F32)
        gate = _sigmoid(gl + gb_ref[i:i + 1, :])
        br = gate * jnp.dot(o_ref[...], w_ref[...], preferred_element_type=F32)
        y = br if y is None else y + br
    mix = jnp.dot(y.astype(BF), wo_ref[...], preferred_element_type=F32)
    x1 = _layer_norm(DN_ALPHA * x + mix, lg_ref[...], lb_ref[...])
    x1_ref[...] = x1
    xh, xl = _split_bf16(x1)
    nt = lambda a, b: lax.dot_general(a, b, (((1,), (1,)), ((), ())), preferred_element_type=F32)
    lgt_ref[...] = nt(wrh_ref[...], xh) + (nt(wrh_ref[...], xl) + nt(wrl_ref[...], xh)) + rb_ref[...]


def _merge(x2d, oa, ob, oc, wg, gb, wa, wb, wc, wo, lg, lb, wrh, wrl, rb, *, tm):
    t = x2d.shape[0]
    full = lambda a: pl.BlockSpec(a.shape, lambda i: (0,) * a.ndim)
    row = lambda n: pl.BlockSpec((tm, n), lambda i: (i, 0))
    return pl.pallas_call(
        _merge_kernel,
        grid=(t // tm,),
        in_specs=[row(D_MODEL), row(A_WIDTH), row(B_WIDTH), row(C_WIDTH), full(wg), full(gb), full(wa),
                  full(wb), full(wc), full(wo), full(lg), full(lb), full(wrh), full(wrl), full(rb)],
        out_specs=[row(D_MODEL), pl.BlockSpec((ROUTER_ROWS, tm), lambda i: (0, i))],
        out_shape=[jax.ShapeDtypeStruct((t, D_MODEL), F32),
                   jax.ShapeDtypeStruct((ROUTER_ROWS, t), F32)],
        compiler_params=_cparams("arbitrary"),
        name="merge",
    )(x2d, oa, ob, oc, wg, gb, wa, wb, wc, wo, lg, lb, wrh, wrl, rb)


def _route_kernel(lgt_ref, wts_ref, lpos_ref, tab_ref, cnt_ref, run_ref, *, tn):
    @pl.when(pl.program_id(0) == 0)
    def _():
        run_ref[...] = jnp.zeros_like(run_ref)

    le = lgt_ref[0:N_EXPERTS, :]
    lg = lgt_ref[N_EXPERTS:N_EXPERTS + MOE_GROUPS, :]
    gi = lax.broadcasted_iota(I32, (MOE_GROUPS, tn), 0).astype(F32)
    ei = lax.broadcasted_iota(I32, (N_EXPERTS, tn), 0).astype(F32)
    eg = jnp.right_shift(lax.broadcasted_iota(I32, (N_EXPERTS, tn), 0),
                         MOE_PER_GROUP.bit_length() - 1).astype(F32)
    mg = jnp.max(lg, axis=0, keepdims=True)
    gsel = jnp.min(jnp.where(lg == mg, gi, float(MOE_GROUPS)), axis=0, keepdims=True)
    p_top = 1.0 / jnp.sum(jnp.exp(lg - mg), axis=0, keepdims=True)
    l1 = jnp.where(eg == gsel, le, -jnp.inf)
    m1 = jnp.max(l1, axis=0, keepdims=True)
    i1 = jnp.min(jnp.where(l1 == m1, ei, float(N_EXPERTS)), axis=0, keepdims=True)
    l2 = jnp.where(ei == i1, -jnp.inf, l1)
    m2 = jnp.max(l2, axis=0, keepdims=True)
    i2 = jnp.min(jnp.where(l2 == m2, ei, float(N_EXPERTS)), axis=0, keepdims=True)
    e2 = jnp.exp(m2 - m1)
    den = 1.0 + e2
    wts_ref[...] = jnp.concatenate([p_top / den, p_top * (e2 / den)], axis=0)

    before = (lax.broadcasted_iota(I32, (tn, tn), 0) < lax.broadcasted_iota(I32, (tn, tn), 1)).astype(BF)
    run = run_ref[...]
    oh = [ei == ik for ik in (i1, i2)]
    prefix = [jnp.dot(o.astype(BF), before, preferred_element_type=F32) for o in oh]
    cnt_k = [jnp.sum(o.astype(F32), axis=1, keepdims=True) for o in oh]
    cnt = cnt_k[0] + cnt_k[1]
    cnt_al = jnp.floor((cnt + (RUN_ALIGN - 1)) * (1.0 / RUN_ALIGN)) * RUN_ALIGN
    er = lax.broadcasted_iota(I32, (N_EXPERTS, N_EXPERTS), 0)
    ec = lax.broadcasted_iota(I32, (N_EXPERTS, N_EXPERTS), 1)
    lstart = jnp.dot((ec < er).astype(BF), jnp.broadcast_to(cnt_al, (N_EXPERTS, LANES)).astype(BF),
                     preferred_element_type=F32)[:, 0:1]
    first = [0.0, cnt_k[0]]
    pick = lambda k: jnp.sum(jnp.where(oh[k], prefix[k] + first[k] + lstart, 0.0), axis=0, keepdims=True)
    lpos_ref[...] = jnp.concatenate([pick(0), pick(1)], axis=0).astype(I32)
    eye_l = (lax.broadcasted_iota(I32, (N_EXPERTS, LANES), 0) == lax.broadcasted_iota(I32, (N_EXPERTS, LANES), 1))
    as_row = lambda colv: jnp.sum(jnp.where(eye_l, colv, 0.0), axis=0, keepdims=True)
    tab_ref[0] = jnp.concatenate([as_row(run), as_row(cnt), as_row(lstart),
                                  jnp.zeros((SUBLANES - 3, LANES), F32)], axis=0).astype(I32)
    run = run + cnt_al
    run_ref[...] = run
    cnt_ref[...] = jnp.broadcast_to(run, cnt_ref.shape)


def _route(lgt, *, tn):
    t = lgt.shape[1]
    tok = lambda: pl.BlockSpec((MOE_TOPK, tn), lambda i: (0, i))
    return pl.pallas_call(
        functools.partial(_route_kernel, tn=tn),
        grid=(t // tn,),
        in_specs=[pl.BlockSpec((ROUTER_ROWS, tn), lambda i: (0, i))],
        out_specs=[tok(), tok(), pl.BlockSpec((1, SUBLANES, LANES), lambda i: (i, 0, 0)),
                   pl.BlockSpec((N_EXPERTS, LANES), lambda i: (0, 0))],
        out_shape=[jax.ShapeDtypeStruct((MOE_TOPK, t), F32),
                   jax.ShapeDtypeStruct((MOE_TOPK, t), I32),
                   jax.ShapeDtypeStruct((t // tn, SUBLANES, LANES), I32),
                   jax.ShapeDtypeStruct((N_EXPERTS, LANES), F32)],
        scratch_shapes=[pltpu.VMEM((N_EXPERTS, 1), F32)],
        compiler_params=_cparams("arbitrary"),
        name="route",
    )(lgt)


def _layout_kernel(cnt_ref, tab_ref, gtab_ref, ttab_ref, bexp_ref, nused_ref, *, n_blocks):
    cnt = cnt_ref[...].astype(I32)
    shift = EXPERT_BLOCK.bit_length() - 1
    padded_blocks = jnp.right_shift(cnt + (EXPERT_BLOCK - 1), shift)
    er = lax.broadcasted_iota(I32, (N_EXPERTS, LANES), 0)
    el = lax.broadcasted_iota(I32, (N_EXPERTS, LANES), 1)
    start = jnp.zeros((N_EXPERTS, LANES), I32)
    for e in range(N_EXPERTS - 1):
        start = start + jnp.where(er > e, padded_blocks[e:e + 1, :], 0)
    end = start + padded_blocks
    as_row = lambda m: jnp.sum(jnp.where(er == el, m, 0), axis=0, keepdims=True)
    tab = tab_ref[...]
    first_row = lax.broadcasted_iota(I32, tab.shape, 1) == 0
    gtab_ref[...] = tab + jnp.where(first_row, as_row(start * EXPERT_BLOCK)[None], 0)
    tail_rows = padded_blocks * EXPERT_BLOCK - cnt
    ttab_ref[...] = jnp.concatenate(
        [as_row(start * EXPERT_BLOCK + cnt), as_row(jnp.right_shift(tail_rows, RUN_ALIGN.bit_length() - 1)),
         jnp.zeros((SUBLANES - 2, LANES), I32)], axis=0)[None]
    blk = lax.broadcasted_iota(I32, (1, n_blocks), 1)
    owner = jnp.zeros((1, n_blocks), I32)
    for e in range(N_EXPERTS):
        owner = owner + (blk >= end[e:e + 1, 0:1]).astype(I32)
    bexp_ref[...] = jnp.minimum(owner, N_EXPERTS - 1)
    nused_ref[...] = end[N_EXPERTS - 1:N_EXPERTS, :]


def _layout(cnt, tab, *, n_blocks):
    full = lambda shape: pl.BlockSpec(shape, lambda i: (0,) * len(shape))
    return pl.pallas_call(
        functools.partial(_layout_kernel, n_blocks=n_blocks),
        grid=(1,),
        in_specs=[full(cnt.shape), full(tab.shape)],
        out_specs=[full(tab.shape), full((1, SUBLANES, LANES)), full((1, n_blocks)), full((1, LANES))],
        out_shape=[jax.ShapeDtypeStruct(tab.shape, I32),
                   jax.ShapeDtypeStruct((1, SUBLANES, LANES), I32),
                   jax.ShapeDtypeStruct((1, n_blocks), I32),
                   jax.ShapeDtypeStruct((1, LANES), I32)],
        compiler_params=_cparams("arbitrary"),
        name="layout",
    )(cnt, tab)


def _run_pieces(tab_ref, fn):
    def per_expert(e, carry):
        n_pieces = (tab_ref[0, 1, e] + (RUN_ALIGN - 1)) // RUN_ALIGN
        lax.fori_loop(0, n_pieces, lambda j, c: (fn(e, j), c)[1], 0)
        return carry
    lax.fori_loop(0, N_EXPERTS, per_expert, 0)


def _piece_rows(tab_ref, e, j):
    return (pl.multiple_of(tab_ref[0, 0, e] + j * RUN_ALIGN, RUN_ALIGN),
            pl.multiple_of(tab_ref[0, 2, e] + j * RUN_ALIGN, RUN_ALIGN))


def _dispatch_kernel(tab_ref, ptab_ref, ttab_ref, lpos_ref, x_ref, xs_ref, stage_ref, zero_ref, sems, tsem,
                     *, td):
    i = pl.program_id(0)
    last = pl.num_programs(0) - 1
    slot = lax.rem(i, 2)
    rows = stage_ref.shape[1]
    r = lax.broadcasted_iota(I32, (rows, td), 0)
    sel = jnp.logical_or(r == lpos_ref[0:1, :], r == lpos_ref[1:2, :]).astype(BF)
    stage_ref[slot] = jnp.dot(sel, x_ref[...].astype(BF), preferred_element_type=F32)

    def run_copy(tab, sl, e, j):
        dst, src = _piece_rows(tab, e, j)
        return pltpu.make_async_copy(stage_ref.at[sl, pl.ds(src, RUN_ALIGN), :],
                                     xs_ref.at[pl.ds(dst, RUN_ALIGN), :], sems.at[sl])

    def tail_copies(fn):
        def per_expert(e, carry):
            def one(j, c):
                dst = pl.multiple_of(ttab_ref[0, 0, e] + j * RUN_ALIGN, RUN_ALIGN)
                fn(pltpu.make_async_copy(zero_ref, xs_ref.at[pl.ds(dst, RUN_ALIGN), :], tsem))
                return c
            lax.fori_loop(0, ttab_ref[0, 1, e], one, 0)
            return carry
        lax.fori_loop(0, N_EXPERTS, per_expert, 0)

    @pl.when(i == 0)
    def _():
        zero_ref[...] = jnp.zeros_like(zero_ref)
        tail_copies(lambda cp: cp.start())

    @pl.when(i > 0)
    def _():
        _run_pieces(ptab_ref, lambda e, j: run_copy(ptab_ref, 1 - slot, e, j).wait())

    _run_pieces(tab_ref, lambda e, j: run_copy(tab_ref, slot, e, j).start())

    @pl.when(i == last)
    def _():
        _run_pieces(tab_ref, lambda e, j: run_copy(tab_ref, slot, e, j).wait())
        tail_copies(lambda cp: cp.wait())


def _stage_rows(td):
    return MOE_TOPK * td + N_EXPERTS * RUN_ALIGN


def _dispatch(gtab, ttab, lpos, x1, *, td, cap):
    t = x1.shape[0]
    smem_tab = lambda imap: pl.BlockSpec((1, SUBLANES, LANES), imap, memory_space=pltpu.SMEM)
    return pl.pallas_call(
        functools.partial(_dispatch_kernel, td=td),
        grid=(t // td,),
        in_specs=[smem_tab(lambda i: (i, 0, 0)), smem_tab(lambda i: (jnp.maximum(i - 1, 0), 0, 0)),
                  smem_tab(lambda i: (0, 0, 0)),
                  pl.BlockSpec((MOE_TOPK, td), lambda i: (0, i)),
                  pl.BlockSpec((td, D_MODEL), lambda i: (i, 0))],
        out_specs=pl.BlockSpec(memory_space=pl.ANY),
        out_shape=jax.ShapeDtypeStruct((cap, D_MODEL), F32),
        scratch_shapes=[pltpu.VMEM((2, _stage_rows(td), D_MODEL), F32), pltpu.VMEM((RUN_ALIGN, D_MODEL), F32),
                        pltpu.SemaphoreType.DMA((2,)), pltpu.SemaphoreType.DMA(())],
        compiler_params=_cparams("arbitrary"),
        name="dispatch",
    )(gtab, gtab, ttab, lpos, x1)


def _expert_kernel(bexp_ref, nused_ref, xs_ref, w1_ref, w3_ref, w2_ref, y_ref):
    del bexp_ref
    i = pl.program_id(0)

    @pl.when(i < nused_ref[0])
    def _():
        xb = xs_ref[...].astype(BF)
        h1 = jnp.dot(xb, w1_ref[0], preferred_element_type=F32)
        h3 = jnp.dot(xb, w3_ref[0], preferred_element_type=F32)
        hid = (_silu(h1) * h3).astype(BF)
        y_ref[...] = jnp.dot(hid, w2_ref[0], preferred_element_type=F32)

    @pl.when(i >= nused_ref[0])
    def _():
        y_ref[...] = jnp.zeros_like(y_ref)


def _expert(bexp, nused, xs, w1, w3, w2):
    cap = xs.shape[0]
    nb = cap // EXPERT_BLOCK
    last = lambda i, nu: jnp.minimum(i, nu[0] - 1)
    grid_spec = pltpu.PrefetchScalarGridSpec(
        num_scalar_prefetch=2,
        grid=(nb,),
        in_specs=[pl.BlockSpec((EXPERT_BLOCK, D_MODEL), lambda i, be, nu: (last(i, nu), 0)),
                  pl.BlockSpec((1, D_MODEL, MOE_FF), lambda i, be, nu: (be[last(i, nu)], 0, 0)),
                  pl.BlockSpec((1, D_MODEL, MOE_FF), lambda i, be, nu: (be[last(i, nu)], 0, 0)),
                  pl.BlockSpec((1, MOE_FF, D_MODEL), lambda i, be, nu: (be[last(i, nu)], 0, 0))],
        out_specs=pl.BlockSpec((EXPERT_BLOCK, D_MODEL), lambda i, be, nu: (i, 0)),
    )
    return pl.pallas_call(
        _expert_kernel,
        grid_spec=grid_spec,
        out_shape=jax.ShapeDtypeStruct((cap, D_MODEL), F32),
        compiler_params=_cparams("arbitrary"),
        name="expert",
    )(bexp, nused, xs, w1, w3, w2)


def _combine_kernel(tab_ref, ntab_ref, lpos_ref, x_ref, w_ref, lg_ref, lb_ref, yb_ref, o_ref, stage_ref, sems,
                    *, tc):
    i = pl.program_id(0)
    last = pl.num_programs(0) - 1
    slot = lax.rem(i, 2)

    def run_copy(tab, sl, e, j):
        src, dst = _piece_rows(tab, e, j)
        return pltpu.make_async_copy(yb_ref.at[pl.ds(src, RUN_ALIGN), :],
                                     stage_ref.at[sl, pl.ds(dst, RUN_ALIGN), :], sems.at[sl])

    @pl.when(i == 0)
    def _():
        stage_ref[...] = jnp.zeros_like(stage_ref)
        _run_pieces(tab_ref, lambda e, j: run_copy(tab_ref, slot, e, j).start())

    @pl.when(i < last)
    def _():
        _run_pieces(ntab_ref, lambda e, j: run_copy(ntab_ref, 1 - slot, e, j).start())

    _run_pieces(tab_ref, lambda e, j: run_copy(tab_ref, slot, e, j).wait())
    staged = stage_ref[slot].astype(BF)
    c = lax.broadcasted_iota(I32, (tc, staged.shape[0]), 1)
    y = [jnp.dot((c == lpos_ref[:, k:k + 1]).astype(BF), staged, preferred_element_type=F32)
         for k in range(MOE_TOPK)]
    ffn = w_ref[:, 0:1] * y[0] + w_ref[:, 1:2] * y[1]
    o_ref[...] = _layer_norm(DN_ALPHA * x_ref[...] + ffn, lg_ref[...], lb_ref[...])


def _combine(gtab, lpos_t, x1, wts_t, lg, lb, yb, *, tc):
    t = x1.shape[0]
    n_tiles = t // tc
    full = lambda a: pl.BlockSpec(a.shape, lambda i: (0,) * a.ndim)
    smem_tab = lambda imap: pl.BlockSpec((1, SUBLANES, LANES), imap, memory_space=pltpu.SMEM)
    return pl.pallas_call(
        functools.partial(_combine_kernel, tc=tc),
        grid=(n_tiles,),
        in_specs=[smem_tab(lambda i: (i, 0, 0)), smem_tab(lambda i: (jnp.minimum(i + 1, n_tiles - 1), 0, 0)),
                  pl.BlockSpec((tc, MOE_TOPK), lambda i: (i, 0)),
                  pl.BlockSpec((tc, D_MODEL), lambda i: (i, 0)),
                  pl.BlockSpec((tc, MOE_TOPK), lambda i: (i, 0)),
                  full(lg), full(lb),
                  pl.BlockSpec(memory_space=pl.ANY)],
        out_specs=pl.BlockSpec((tc, D_MODEL), lambda i: (i, 0)),
        out_shape=jax.ShapeDtypeStruct((t, D_MODEL), F32),
        scratch_shapes=[pltpu.VMEM((2, _stage_rows(tc), D_MODEL), F32), pltpu.SemaphoreType.DMA((2,))],
        compiler_params=_cparams("arbitrary"),
        name="combine",
    )(gtab, gtab, lpos_t, x1, wts_t, lg, lb, yb)


def _tile(n, want):
    while n % want:
        want //= 2
    return want


def _mixer(x2d, b, l, w_in, rel_bias, conv_w, a_log, dt_bias, norm_g, pool_w, pool_scale,
           w_br_a, w_br_b, w_br_c, gate_b, w_out, ln_g, ln_b, wr, rb):
    t = x2d.shape[0]
    c0 = 3 * A_WIDTH
    c1 = c0 + 3 * B_WIDTH
    c2 = c1 + 2 * B_HEADS
    c3 = c2 + B_WIDTH
    c4 = c3 + C_WIDTH
    wbf = w_in.astype(BF)
    wbd = jnp.pad(wbf[:, c1:c2], ((0, 0), (0, LANES - 2 * B_HEADS)))
    qkva, qkvb, bd, go, oc = _proj(
        x2d, wbf[:, :c0], wbf[:, c0:c1], wbd, wbf[:, c2:c3], wbf[:, c3:c4],
        pool_w.astype(BF), pool_scale.reshape(1, C_WIDTH), seq=l, tm=_tile(l, 1024))

    oa = _attn(qkva.reshape(b, l, 3 * A_WIDTH), _attn_bias_table(rel_bias)).reshape(t, A_WIDTH)

    lane_pad = lambda v: jnp.pad(v.reshape(1, B_HEADS), ((0, 0), (B_HEADS, LANES - 2 * B_HEADS)))
    ob = _gdn(qkvb.reshape(b, l, 3 * B_WIDTH), bd.reshape(b, l, LANES), go.reshape(b, l, B_WIDTH),
              conv_w, lane_pad(a_log), lane_pad(dt_bias), norm_g.reshape(1, B_HEAD_DIM)).reshape(t, B_WIDTH)

    wrh, wrl = _split_bf16(wr)
    return _merge(x2d, oa, ob, oc, wbf[:, c4:], gate_b, w_br_a.astype(BF), w_br_b.astype(BF),
                  w_br_c.astype(BF), w_out.astype(BF), ln_g.reshape(1, -1), ln_b.reshape(1, -1),
                  wrh, wrl, rb, tm=_tile(t, 1024))


def _moe(x1, lgt, w1, w3, w2, ln_g, ln_b):
    t = x1.shape[0]
    tile = _tile(t, MOE_TILE)
    max_rows = t * MOE_TOPK + (t // tile) * N_EXPERTS * (RUN_ALIGN - 1)
    n_blocks = -(-max_rows // EXPERT_BLOCK) + N_EXPERTS
    cap = n_blocks * EXPERT_BLOCK
    wts, lpos, tab, cnt = _route(lgt, tn=tile)
    gtab, ttab, bexp, nused = _layout(cnt, tab, n_blocks=n_blocks)
    xs = _dispatch(gtab, ttab, lpos, x1, td=tile, cap=cap)
    yb = _expert(bexp.reshape(n_blocks), nused[0, :1], xs, w1.astype(BF), w3.astype(BF), w2.astype(BF))
    return _combine(gtab, lpos.T, x1, wts.T, ln_g.reshape(1, -1), ln_b.reshape(1, -1), yb, tc=tile)


def kernel(x, w_in, attn_rel_bias, gdn_conv_w, gdn_a_log, gdn_dt_bias, gdn_norm_g, pool_w, pool_scale,
           w_branch_a, w_branch_b, w_branch_c, gate_b, w_out, ln1_g, ln1_b, router_group_w,
           router_group_b, router_expert_w, router_expert_b, moe_w1, moe_w3, moe_w2, ln2_g, ln2_b):
    b, l, d = x.shape
    x2d = x.reshape(b * l, d)
    for i in range(DEPTH):
        wr = jnp.concatenate([router_expert_w[i].T, router_group_w[i].T,
                              jnp.zeros((ROUTER_ROWS - N_EXPERTS - MOE_GROUPS, d), F32)], axis=0)
        rb = jnp.concatenate([router_expert_b[i], router_group_b[i],
                              jnp.zeros((ROUTER_ROWS - N_EXPERTS - MOE_GROUPS,), F32)]).reshape(ROUTER_ROWS, 1)
        x1, lgt = _mixer(x2d, b, l, w_in[i], attn_rel_bias[i], gdn_conv_w[i], gdn_a_log[i], gdn_dt_bias[i],
                         gdn_norm_g[i], pool_w[i], pool_scale[i], w_branch_a[i], w_branch_b[i],
                         w_branch_c[i], gate_b[i], w_out[i], ln1_g[i], ln1_b[i], wr, rb)
        x2d = _moe(x1, lgt, moe_w1[i], moe_w3[i], moe_w2[i], ln2_g[i], ln2_b[i])
    return x2d.reshape(b, l, d)
```

```python
import functools

import jax
import jax.numpy as jnp
from jax import lax
from jax.experimental import pallas as pl
from jax.experimental.pallas import tpu as pltpu

BF = jnp.bfloat16
F32 = jnp.float32
I32 = jnp.int32

D_MODEL = 1024
DEPTH = 2
CHUNK = 64

A_HEADS = 8
A_HEAD_DIM = 64
A_WIDTH = A_HEADS * A_HEAD_DIM
A_LEFT_CHUNKS = 8
A_REL_CLIP = 256

B_HEADS = 4
B_HEAD_DIM = 128
B_WIDTH = B_HEADS * B_HEAD_DIM
B_CONV = 4

C_WINDOWS = (2, 4, 8, 16)
C_GROUPS = 4
C_GROUP_DIM = 128
C_WIDTH = C_GROUPS * C_GROUP_DIM

N_BRANCH = 3
MOE_GROUPS = 4
MOE_PER_GROUP = 8
N_EXPERTS = MOE_GROUPS * MOE_PER_GROUP
MOE_TOPK = 2
MOE_FF = 512

DN_ALPHA = (2 * DEPTH) ** 0.25
LN_EPS = 1e-5
RMS_EPS = 1e-6

LANES = 128
SUBLANES = 8
VMEM_LIMIT_BYTES = 56 * 1024 * 1024

QBLOCK = 2 * CHUNK
KBAND = (A_LEFT_CHUNKS + 2) * CHUNK
NEG_BIG = -1e30

ROUTER_ROWS = 40
EXPERT_BLOCK = 512
MOE_TILE = 512
RUN_ALIGN = 8


def _cparams(*sem):
    return pltpu.CompilerParams(dimension_semantics=sem, vmem_limit_bytes=VMEM_LIMIT_BYTES)


def _mm(a, b):
    return jnp.dot(a.astype(BF), b.astype(BF), preferred_element_type=F32)


def _mm_nt(a, b):
    return lax.dot_general(a.astype(BF), b.astype(BF), (((1,), (1,)), ((), ())),
                           preferred_element_type=F32)


def _mm_tn(a, b):
    return lax.dot_general(a.astype(BF), b.astype(BF), (((0,), (0,)), ((), ())),
                           preferred_element_type=F32)


def _split_bf16(a):
    hi = a.astype(BF)
    lo = (a - hi.astype(F32)).astype(BF)
    return hi, lo


def _mm3(a, b):
    ah, al = _split_bf16(a)
    bh, bl = _split_bf16(b)
    return (jnp.dot(ah, bh, preferred_element_type=F32)
            + (jnp.dot(ah, bl, preferred_element_type=F32)
               + jnp.dot(al, bh, preferred_element_type=F32)))


def _sigmoid(x):
    return 1.0 / (1.0 + jnp.exp(-x))


def _silu(x):
    return x * _sigmoid(x)


def _layer_norm(z, g, b):
    mu = jnp.mean(z, axis=-1, keepdims=True)
    zc = z - mu
    var = jnp.mean(zc * zc, axis=-1, keepdims=True)
    return zc * lax.rsqrt(var + LN_EPS) * g + b


POOL_TAIL = 16


def _proj_kernel(x_ref, wa_ref, wb_ref, wbd_ref, wgo_ref, wc_ref, pw_ref, ps_ref,
                 qkva_ref, qkvb_ref, bd_ref, go_ref, oc_ref, tail_ref, *, tiles_per_seq, tm):
    it = pl.program_id(0) % tiles_per_seq
    xb = x_ref[...].astype(BF)
    qkva_ref[...] = jnp.dot(xb, wa_ref[...], preferred_element_type=F32).astype(BF)
    qkvb_ref[...] = jnp.dot(xb, wb_ref[...], preferred_element_type=F32).astype(BF)
    bd_ref[...] = jnp.dot(xb, wbd_ref[...], preferred_element_type=F32)
    go_ref[...] = jnp.dot(xb, wgo_ref[...], preferred_element_type=F32).astype(BF)
    u = jnp.dot(xb, wc_ref[...], preferred_element_type=F32)

    @pl.when(it == 0)
    def _():
        tail_ref[...] = jnp.zeros_like(tail_ref)

    ext = jnp.concatenate([tail_ref[...], u], axis=0)
    tail_ref[...] = u[tm - POOL_TAIL:, :]
    pos = (it * tm + 1 + lax.broadcasted_iota(I32, (tm, 1), 0)).astype(F32)
    s = ext
    for gi, win in enumerate(C_WINDOWS):
        if gi:
            s = s[:, C_GROUP_DIM:]
        s = s + pltpu.roll(s, win // 2, 0)
        lo_c, hi_c = gi * C_GROUP_DIM, (gi + 1) * C_GROUP_DIM
        cnt = jnp.minimum(pos, float(win))
        pooled = s[POOL_TAIL:, :C_GROUP_DIM] / cnt - u[:, lo_c:hi_c]
        mixed = jnp.dot(pooled.astype(BF), pw_ref[gi], preferred_element_type=F32)
        oc_ref[:, lo_c:hi_c] = (mixed * ps_ref[:, lo_c:hi_c]).astype(BF)


def _proj(x2d, wa, wb, wbd, wgo, wc, pw, ps, *, seq, tm):
    t = x2d.shape[0]
    full = lambda a: pl.BlockSpec(a.shape, lambda i: (0,) * a.ndim)
    row = lambda n: pl.BlockSpec((tm, n), lambda i: (i, 0))
    return pl.pallas_call(
        functools.partial(_proj_kernel, tiles_per_seq=seq // tm, tm=tm),
        grid=(t // tm,),
        in_specs=[row(D_MODEL), full(wa), full(wb), full(wbd), full(wgo), full(wc), full(pw), full(ps)],
        out_specs=[row(3 * A_WIDTH), row(3 * B_WIDTH), row(LANES), row(B_WIDTH), row(C_WIDTH)],
        out_shape=[jax.ShapeDtypeStruct((t, 3 * A_WIDTH), BF),
                   jax.ShapeDtypeStruct((t, 3 * B_WIDTH), BF),
                   jax.ShapeDtypeStruct((t, LANES), F32),
                   jax.ShapeDtypeStruct((t, B_WIDTH), BF),
                   jax.ShapeDtypeStruct((t, C_WIDTH), BF)],
        scratch_shapes=[pltpu.VMEM((POOL_TAIL, C_WIDTH), F32)],
        compiler_params=_cparams("arbitrary"),
        name="proj",
    )(x2d, wa, wb, wbd, wgo, wc, pw, ps)


def _attn_bias_table(rel_bias):
    h = rel_bias.shape[0]
    pad = A_LEFT_CHUNKS * CHUNK
    width = KBAND + pad
    n = width + QBLOCK - 1
    far = jnp.repeat(rel_bias[:, -1:], QBLOCK - 1 + pad - A_REL_CLIP, axis=1)
    near = jnp.repeat(rel_bias[:, :1], n - far.shape[1] - rel_bias.shape[1], axis=1)
    w = jnp.concatenate([far, rel_bias[:, ::-1], near], axis=1).astype(F32)
    w = jnp.roll(w, -(QBLOCK - 1), axis=1)
    toep = jnp.tile(w, (1, QBLOCK))[:, :QBLOCK * (n - 1)].reshape(h, QBLOCK, n - 1)[:, :, :width]
    r = jnp.arange(QBLOCK)[:, None]
    m = jnp.arange(width)[None, :]
    kc, qc = m // CHUNK, r // CHUNK
    valid = (kc >= qc) & (kc <= qc + A_LEFT_CHUNKS)
    tab = jnp.where(valid[None], toep, NEG_BIG)
    return tab.reshape(h, QBLOCK, width // LANES, LANES).transpose(0, 2, 1, 3)


def _attn_kernel(q_ref, k_ref, v_ref, tab_ref, o_ref):
    j = pl.program_id(1)
    first_blk = A_LEFT_CHUNKS * CHUNK // QBLOCK
    kstart = pl.multiple_of(jnp.maximum(j - first_blk, 0) * QBLOCK, QBLOCK)
    offb = jnp.maximum(first_blk - j, 0)
    lane = lax.broadcasted_iota(I32, (QBLOCK, LANES), 1)
    lo = lane < A_HEAD_DIM
    nkb = KBAND // LANES
    pairs = range(A_HEADS // 2)
    cols = [slice(hp * LANES, (hp + 1) * LANES) for hp in pairs]
    scores = []
    for hp in pairs:
        qp = q_ref[0, :, cols[hp]] * jnp.asarray(A_HEAD_DIM ** -0.5, BF)
        zero = jnp.zeros_like(qp)
        q2 = jnp.concatenate([jnp.where(lo, qp, zero), jnp.where(lo, zero, qp)], axis=0)
        kp = k_ref[0, pl.ds(kstart, KBAND), cols[hp]]
        scores.append(lax.dot_general(q2, kp, (((1,), (1,)), ((), ())), preferred_element_type=F32))
    probs, denoms = [], []
    for hp in pairs:
        bias = jnp.concatenate(
            [jnp.concatenate([tab_ref[2 * hp + par, offb + i] for i in range(nkb)], axis=1) for par in range(2)],
            axis=0)
        s = scores[hp] + bias
        p = jnp.exp(s - jnp.max(s, axis=-1, keepdims=True))
        denoms.append(jnp.sum(p, axis=-1, keepdims=True))
        probs.append(p.astype(BF))
    for hp in pairs:
        vp = v_ref[0, pl.ds(kstart, KBAND), cols[hp]]
        o2 = jnp.dot(probs[hp], vp, preferred_element_type=F32) / denoms[hp]
        o_ref[0, :, cols[hp]] = jnp.where(lo, o2[:QBLOCK], o2[QBLOCK:]).astype(BF)


def _attn(qkva, tab):
    b, l, _ = qkva.shape
    return pl.pallas_call(
        _attn_kernel,
        grid=(b, l // QBLOCK),
        in_specs=[pl.BlockSpec((1, QBLOCK, A_WIDTH), lambda bi, j: (bi, j, 0)),
                  pl.BlockSpec((1, l, A_WIDTH), lambda bi, j: (bi, 0, 1)),
                  pl.BlockSpec((1, l, A_WIDTH), lambda bi, j: (bi, 0, 2)),
                  pl.BlockSpec(tab.shape, lambda bi, j: (0, 0, 0, 0))],
        out_specs=pl.BlockSpec((1, QBLOCK, A_WIDTH), lambda bi, j: (bi, j, 0)),
        out_shape=jax.ShapeDtypeStruct((b, l, A_WIDTH), BF),
        compiler_params=_cparams("arbitrary", "arbitrary"),
        name="attn",
    )(qkva, qkva, qkva, tab)


CONV_HIST = 16
GDN_PREP_UNROLL = 2


def _gdn_kernel(qkv_ref, bd_ref, go_ref, cw_ref, alog_ref, dtb_ref, ng_ref, o_ref,
                u_ref, wq_ref, qkkd_ref, egl_ref, state_ref, *, n_chunks):
    rt = lax.broadcasted_iota(I32, (CHUNK, CHUNK), 0)
    ct = lax.broadcasted_iota(I32, (CHUNK, CHUNK), 1)
    tril = rt >= ct
    strict = rt > ct
    trilf = tril.astype(F32)
    eye_f = (rt == ct).astype(F32)
    eye_d = (lax.broadcasted_iota(I32, (B_HEAD_DIM, B_HEAD_DIM), 0)
             == lax.broadcasted_iota(I32, (B_HEAD_DIM, B_HEAD_DIM), 1)).astype(BF)
    neg_rate = -jnp.exp(alog_ref[...])
    sr = lax.broadcasted_iota(I32, (B_CONV * CHUNK, CONV_HIST + CHUNK), 0)
    sc = lax.broadcasted_iota(I32, (B_CONV * CHUNK, CONV_HIST + CHUNK), 1)
    shift_sel = (sc == CONV_HIST + jnp.bitwise_and(sr, CHUNK - 1)
                 - jnp.right_shift(sr, CHUNK.bit_length() - 1)).astype(BF)

    def heads(fn):
        return jnp.concatenate([fn(h, slice(h * B_HEAD_DIM, (h + 1) * B_HEAD_DIM)) for h in range(B_HEADS)],
                               axis=1)

    def prep(c):
        r0 = pl.multiple_of(c * CHUNK, CHUNK)
        cur = qkv_ref[0, pl.ds(r0, CHUNK), :]
        h0 = pl.multiple_of(jnp.maximum(r0 - CONV_HIST, 0), CONV_HIST)
        hist = qkv_ref[0, pl.ds(h0, CONV_HIST), :]
        hist = jnp.where(c > 0, hist, jnp.zeros_like(hist))
        taps = jnp.dot(shift_sel, jnp.concatenate([hist, cur], axis=0), preferred_element_type=F32)
        acc = taps[:CHUNK] * cw_ref[B_CONV - 1:B_CONV, :]
        for back in range(1, B_CONV):
            tap = B_CONV - 1 - back
            acc = acc + taps[back * CHUNK:(back + 1) * CHUNK] * cw_ref[tap:tap + 1, :]
        act = _silu(acc)
        q_all, k_all, v_all = act[:, :B_WIDTH], act[:, B_WIDTH:2 * B_WIDTH], act[:, 2 * B_WIDTH:]

        bd = bd_ref[0, pl.ds(r0, CHUNK), :]
        beta_all = _sigmoid(bd)
        z = bd + dtb_ref[...]
        softplus = jnp.maximum(z, 0.0) + jnp.log1p(jnp.exp(-jnp.abs(z)))
        g_all = neg_rate * softplus
        gc_all = jnp.dot(trilf, g_all, preferred_element_type=F32,
                         precision=lax.Precision.HIGHEST)
        egc_all = jnp.exp(gc_all)
        glast = gc_all[CHUNK - 1:CHUNK, :]
        ekd_all = jnp.exp(glast - gc_all)
        egl = jnp.exp(glast)
        col = lambda m, h: m[:, B_HEADS + h:B_HEADS + h + 1]

        def l2n(m):
            return m * lax.rsqrt(jnp.sum(m * m, axis=-1, keepdims=True) + RMS_EPS)

        qn = heads(lambda h, hs: l2n(q_all[:, hs]) * (B_HEAD_DIM ** -0.5))
        kn = heads(lambda h, hs: l2n(k_all[:, hs]))
        kbeta = heads(lambda h, hs: kn[:, hs] * beta_all[:, h:h + 1])
        vbeta = heads(lambda h, hs: v_all[:, hs] * beta_all[:, h:h + 1])
        kbe = heads(lambda h, hs: kbeta[:, hs] * col(egc_all, h))
        qe = heads(lambda h, hs: qn[:, hs] * col(egc_all, h))
        chains = []
        for h in range(B_HEADS):
            hs = slice(h * B_HEAD_DIM, (h + 1) * B_HEAD_DIM)
            gcc = jnp.broadcast_to(col(gc_all, h), (CHUNK, CHUNK))
            gcr = jnp.sum(gcc * eye_f, axis=0, keepdims=True)
            chains.append(dict(
                c=c, h=h, hs=hs,
                decay=jnp.exp(jnp.where(tril, gcc - gcr, -jnp.inf)),
                kq=jnp.concatenate([kbeta[:, hs], qn[:, hs]], axis=0).astype(BF),
                kn=kn[:, hs].astype(BF),
                rhs=jnp.concatenate([vbeta[:, hs], kbe[:, hs]], axis=1).astype(BF),
                kd=(kn[:, hs] * col(ekd_all, h)).astype(BF)))
        egl_row = heads(lambda h, hs: jnp.broadcast_to(col(egl, h), (1, B_HEAD_DIM)))
        return chains, qe.astype(BF), egl_row

    nt = lambda a, b: lax.dot_general(a, b, (((1,), (1,)), ((), ())), preferred_element_type=F32)
    mm = lambda a, b: jnp.dot(a, b, preferred_element_type=F32)

    def prep_levels(i):
        fronts = [prep(i * GDN_PREP_UNROLL + g) for g in range(GDN_PREP_UNROLL)]
        ch = [x for f in fronts for x in f[0]]
        aq = [nt(x["kq"], x["kn"]) for x in ch]
        kdt = [nt(eye_d, x["kd"]) for x in ch]
        yield
        qk = [(a[CHUNK:] * x["decay"]).astype(BF) for x, a in zip(ch, aq)]
        pw = [-jnp.where(strict, a[:CHUNK] * x["decay"], 0.0) for x, a in zip(ch, aq)]
        tmat = [eye_f + p for p in pw]
        pwb = [p.astype(BF) for p in pw]
        pw = [mm(p, p) for p in pwb]
        yield
        n_sq = CHUNK.bit_length() - 2
        for lvl in range(n_sq):
            pwb = [p.astype(BF) for p in pw]
            if lvl + 1 < n_sq:
                both = [mm(jnp.concatenate([t.astype(BF), p], axis=0), p) for t, p in zip(tmat, pwb)]
                yield
                tmat = [t + r[:CHUNK] for t, r in zip(tmat, both)]
                pw = [r[CHUNK:] for r in both]
            else:
                last = [mm(t.astype(BF), p) for t, p in zip(tmat, pwb)]
                yield
                tmat = [t + r for t, r in zip(tmat, last)]
        uw = [mm(t.astype(BF), x["rhs"]) for t, x in zip(tmat, ch)]
        yield
        for x, r, qkx, kt in zip(ch, uw, qk, kdt):
            u_ref[x["c"], :, x["hs"]] = r[:, :B_HEAD_DIM]
            wq_ref[x["c"], :CHUNK, x["hs"]] = r[:, B_HEAD_DIM:].astype(BF)
            qkkd_ref[x["c"], x["h"], :CHUNK, :] = qkx
            qkkd_ref[x["c"], x["h"], CHUNK:, :] = kt.astype(BF)
        for g, (_, qe_b, egl_row) in enumerate(fronts):
            wq_ref[i * GDN_PREP_UNROLL + g, CHUNK:, :] = qe_b
            egl_ref[i * GDN_PREP_UNROLL + g] = egl_row

    def step_levels(c):
        r0 = pl.multiple_of(c * CHUNK, CHUNK)
        st = state_ref[...]
        stb = st.astype(BF)
        hss = [slice(h * B_HEAD_DIM, (h + 1) * B_HEAD_DIM) for h in range(B_HEADS)]
        r1 = [mm(wq_ref[c, :, hs], stb[:, hs]) for hs in hss]
        yield
        v_new = [(u_ref[c, :, hs] - r[:CHUNK]).astype(BF) for hs, r in zip(hss, r1)]
        r2 = [mm(qkkd_ref[c, h], v_new[h]) for h in range(B_HEADS)]
        yield
        o_all = jnp.concatenate([a[CHUNK:] + b[:CHUNK] for a, b in zip(r1, r2)], axis=1)
        state_ref[...] = st * egl_ref[c] + jnp.concatenate([b[CHUNK:] for b in r2], axis=1)
        gate = go_ref[0, pl.ds(r0, CHUNK), :].astype(F32)
        o_n = heads(lambda h, hs: o_all[:, hs]
                    * lax.rsqrt(jnp.mean(o_all[:, hs] * o_all[:, hs], axis=-1, keepdims=True) + RMS_EPS)
                    * ng_ref[...])
        o_ref[0, pl.ds(r0, CHUNK), :] = (o_n * _silu(gate)).astype(BF)

    def steps_of(i):
        for g in range(GDN_PREP_UNROLL):
            yield from step_levels(i * GDN_PREP_UNROLL + g)

    def run_interleaved(*gens):
        live = list(gens)
        while live:
            for g in list(live):
                if next(g, StopIteration) is StopIteration:
                    live.remove(g)

    n_groups = n_chunks // GDN_PREP_UNROLL
    state_ref[...] = jnp.zeros_like(state_ref)
    run_interleaved(prep_levels(0))

    def body(i, carry):
        run_interleaved(steps_of(i - 1), prep_levels(i))
        return carry

    lax.fori_loop(1, n_groups, body, 0)
    run_interleaved(steps_of(n_groups - 1))


def _gdn(qkvb, bd, go, cw, alog, dtb, ng):
    b, l, _ = qkvb.shape
    nc = l // CHUNK
    full = lambda a: pl.BlockSpec(a.shape, lambda bi: (0,) * a.ndim)
    seq = lambda n: pl.BlockSpec((1, l, n), lambda bi: (bi, 0, 0))
    return pl.pallas_call(
        functools.partial(_gdn_kernel, n_chunks=nc),
        grid=(b,),
        in_specs=[seq(3 * B_WIDTH), seq(LANES), seq(B_WIDTH), full(cw), full(alog), full(dtb), full(ng)],
        out_specs=seq(B_WIDTH),
        out_shape=jax.ShapeDtypeStruct((b, l, B_WIDTH), BF),
        scratch_shapes=[pltpu.VMEM((nc, CHUNK, B_WIDTH), F32),
                        pltpu.VMEM((nc, 2 * CHUNK, B_WIDTH), BF),
                        pltpu.VMEM((nc, B_HEADS, CHUNK + B_HEAD_DIM, CHUNK), BF),
                        pltpu.VMEM((nc, 1, B_WIDTH), F32),
                        pltpu.VMEM((B_HEAD_DIM, B_WIDTH), F32)],
        compiler_params=_cparams("arbitrary"),
        name="gdn",
    )(qkvb, bd, go, cw, alog, dtb, ng)


def _merge_kernel(x_ref, oa_ref, ob_ref, oc_ref, wg_ref, gb_ref, wa_ref, wb_ref, wc_ref, wo_ref,
                  lg_ref, lb_ref, wrh_ref, wrl_ref, rb_ref, x1_ref, lgt_ref):
    x = x_ref[...]
    xb = x.astype(BF)
    y = None
    for i, (o_ref, w_ref) in enumerate(((oa_ref, wa_ref), (ob_ref, wb_ref), (oc_ref, wc_ref))):
        gl = jnp.dot(xb, wg_ref[:, i * D_MODEL:(i + 1) * D_MODEL], preferred_element_type=F32)
        gate = _sigmoid(gl + gb_ref[i:i + 1, :])
        br = gate * jnp.dot(o_ref[...], w_ref[...], preferred_element_type=F32)
        y = br if y is None else y + br
    mix = jnp.dot(y.astype(BF), wo_ref[...], preferred_element_type=F32)
    x1 = _layer_norm(DN_ALPHA * x + mix, lg_ref[...], lb_ref[...])
    x1_ref[...] = x1
    xh, xl = _split_bf16(x1)
    nt = lambda a, b: lax.dot_general(a, b, (((1,), (1,)), ((), ())), preferred_element_type=F32)
    lgt_ref[...] = nt(wrh_ref[...], xh) + (nt(wrh_ref[...], xl) + nt(wrl_ref[...], xh)) + rb_ref[...]


def _merge(x2d, oa, ob, oc, wg, gb, wa, wb, wc, wo, lg, lb, wrh, wrl, rb, *, tm):
    t = x2d.shape[0]
    full = lambda a: pl.BlockSpec(a.shape, lambda i: (0,) * a.ndim)
    row = lambda n: pl.BlockSpec((tm, n), lambda i: (i, 0))
    return pl.pallas_call(
        _merge_kernel,
        grid=(t // tm,),
        in_specs=[row(D_MODEL), row(A_WIDTH), row(B_WIDTH), row(C_WIDTH), full(wg), full(gb), full(wa),
                  full(wb), full(wc), full(wo), full(lg), full(lb), full(wrh), full(wrl), full(rb)],
        out_specs=[row(D_MODEL), pl.BlockSpec((ROUTER_ROWS, tm), lambda i: (0, i))],
        out_shape=[jax.ShapeDtypeStruct((t, D_MODEL), F32),
                   jax.ShapeDtypeStruct((ROUTER_ROWS, t), F32)],
        compiler_params=_cparams("arbitrary"),
        name="merge",
    )(x2d, oa, ob, oc, wg, gb, wa, wb, wc, wo, lg, lb, wrh, wrl, rb)


def _route_kernel(lgt_ref, wts_ref, lpos_ref, tab_ref, cnt_ref, run_ref, *, tn):
    @pl.when(pl.program_id(0) == 0)
    def _():
        run_ref[...] = jnp.zeros_like(run_ref)

    le = lgt_ref[0:N_EXPERTS, :]
    lg = lgt_ref[N_EXPERTS:N_EXPERTS + MOE_GROUPS, :]
    gi = lax.broadcasted_iota(I32, (MOE_GROUPS, tn), 0).astype(F32)
    ei = lax.broadcasted_iota(I32, (N_EXPERTS, tn), 0).astype(F32)
    eg = jnp.right_shift(lax.broadcasted_iota(I32, (N_EXPERTS, tn), 0),
                         MOE_PER_GROUP.bit_length() - 1).astype(F32)
    mg = jnp.max(lg, axis=0, keepdims=True)
    gsel = jnp.min(jnp.where(lg == mg, gi, float(MOE_GROUPS)), axis=0, keepdims=True)
    p_top = 1.0 / jnp.sum(jnp.exp(lg - mg), axis=0, keepdims=True)
    l1 = jnp.where(eg == gsel, le, -jnp.inf)
    m1 = jnp.max(l1, axis=0, keepdims=True)
    i1 = jnp.min(jnp.where(l1 == m1, ei, float(N_EXPERTS)), axis=0, keepdims=True)
    l2 = jnp.where(ei == i1, -jnp.inf, l1)
    m2 = jnp.max(l2, axis=0, keepdims=True)
    i2 = jnp.min(jnp.where(l2 == m2, ei, float(N_EXPERTS)), axis=0, keepdims=True)
    e2 = jnp.exp(m2 - m1)
    den = 1.0 + e2
    wts_ref[...] = jnp.concatenate([p_top / den, p_top * (e2 / den)], axis=0)

    before = (lax.broadcasted_iota(I32, (tn, tn), 0) < lax.broadcasted_iota(I32, (tn, tn), 1)).astype(BF)
    run = run_ref[...]
    oh = [ei == ik for ik in (i1, i2)]
    prefix = [jnp.dot(o.astype(BF), before, preferred_element_type=F32) for o in oh]
    cnt_k = [jnp.sum(o.astype(F32), axis=1, keepdims=True) for o in oh]
    cnt = cnt_k[0] + cnt_k[1]
    cnt_al = jnp.floor((cnt + (RUN_ALIGN - 1)) * (1.0 / RUN_ALIGN)) * RUN_ALIGN
    er = lax.broadcasted_iota(I32, (N_EXPERTS, N_EXPERTS), 0)
    ec = lax.broadcasted_iota(I32, (N_EXPERTS, N_EXPERTS), 1)
    lstart = jnp.dot((ec < er).astype(BF), jnp.broadcast_to(cnt_al, (N_EXPERTS, LANES)).astype(BF),
                     preferred_element_type=F32)[:, 0:1]
    first = [0.0, cnt_k[0]]
    pick = lambda k: jnp.sum(jnp.where(oh[k], prefix[k] + first[k] + lstart, 0.0), axis=0, keepdims=True)
    lpos_ref[...] = jnp.concatenate([pick(0), pick(1)], axis=0).astype(I32)
    eye_l = (lax.broadcasted_iota(I32, (N_EXPERTS, LANES), 0) == lax.broadcasted_iota(I32, (N_EXPERTS, LANES), 1))
    as_row = lambda colv: jnp.sum(jnp.where(eye_l, colv, 0.0), axis=0, keepdims=True)
    pieces = jnp.broadcast_to(jnp.sum(cnt_al, axis=0, keepdims=True) * (1.0 / RUN_ALIGN), (1, LANES))
    tab_ref[0] = jnp.concatenate([as_row(run), as_row(cnt), as_row(lstart), pieces,
                                  jnp.zeros((SUBLANES - 4, LANES), F32)], axis=0).astype(I32)
    run = run + cnt_al
    run_ref[...] = run
    cnt_ref[...] = jnp.broadcast_to(run, cnt_ref.shape)


def _route(lgt, *, tn):
    t = lgt.shape[1]
    tok = lambda: pl.BlockSpec((MOE_TOPK, tn), lambda i: (0, i))
    return pl.pallas_call(
        functools.partial(_route_kernel, tn=tn),
        grid=(t // tn,),
        in_specs=[pl.BlockSpec((ROUTER_ROWS, tn), lambda i: (0, i))],
        out_specs=[tok(), tok(), pl.BlockSpec((1, SUBLANES, LANES), lambda i: (i, 0, 0)),
                   pl.BlockSpec((N_EXPERTS, LANES), lambda i: (0, 0))],
        out_shape=[jax.ShapeDtypeStruct((MOE_TOPK, t), F32),
                   jax.ShapeDtypeStruct((MOE_TOPK, t), I32),
                   jax.ShapeDtypeStruct((t // tn, SUBLANES, LANES), I32),
                   jax.ShapeDtypeStruct((N_EXPERTS, LANES), F32)],
        scratch_shapes=[pltpu.VMEM((N_EXPERTS, 1), F32)],
        compiler_params=_cparams("arbitrary"),
        name="route",
    )(lgt)


def _layout_kernel(cnt_ref, tab_ref, gtab_ref, ttab_ref, bexp_ref, nused_ref, *, n_blocks):
    cnt = cnt_ref[...].astype(I32)
    shift = EXPERT_BLOCK.bit_length() - 1
    padded_blocks = jnp.right_shift(cnt + (EXPERT_BLOCK - 1), shift)
    er = lax.broadcasted_iota(I32, (N_EXPERTS, LANES), 0)
    el = lax.broadcasted_iota(I32, (N_EXPERTS, LANES), 1)
    start = jnp.zeros((N_EXPERTS, LANES), I32)
    for e in range(N_EXPERTS - 1):
        start = start + jnp.where(er > e, padded_blocks[e:e + 1, :], 0)
    end = start + padded_blocks
    as_row = lambda m: jnp.sum(jnp.where(er == el, m, 0), axis=0, keepdims=True)
    tab = tab_ref[...]
    first_row = lax.broadcasted_iota(I32, tab.shape, 1) == 0
    gtab_ref[...] = tab + jnp.where(first_row, as_row(start * EXPERT_BLOCK)[None], 0)
    tail_rows = padded_blocks * EXPERT_BLOCK - cnt
    ttab_ref[...] = jnp.concatenate(
        [as_row(start * EXPERT_BLOCK + cnt), as_row(jnp.right_shift(tail_rows, RUN_ALIGN.bit_length() - 1)),
         jnp.zeros((SUBLANES - 2, LANES), I32)], axis=0)[None]
    blk = lax.broadcasted_iota(I32, (1, n_blocks), 1)
    owner = jnp.zeros((1, n_blocks), I32)
    for e in range(N_EXPERTS):
        owner = owner + (blk >= end[e:e + 1, 0:1]).astype(I32)
    bexp_ref[...] = jnp.minimum(owner, N_EXPERTS - 1)
    nused_ref[...] = end[N_EXPERTS - 1:N_EXPERTS, :]


def _layout(cnt, tab, *, n_blocks):
    full = lambda shape: pl.BlockSpec(shape, lambda i: (0,) * len(shape))
    return pl.pallas_call(
        functools.partial(_layout_kernel, n_blocks=n_blocks),
        grid=(1,),
        in_specs=[full(cnt.shape), full(tab.shape)],
        out_specs=[full(tab.shape), full((1, SUBLANES, LANES)), full((1, n_blocks)), full((1, LANES))],
        out_shape=[jax.ShapeDtypeStruct(tab.shape, I32),
                   jax.ShapeDtypeStruct((1, SUBLANES, LANES), I32),
                   jax.ShapeDtypeStruct((1, n_blocks), I32),
                   jax.ShapeDtypeStruct((1, LANES), I32)],
        compiler_params=_cparams("arbitrary"),
        name="layout",
    )(cnt, tab)


def _run_pieces(tab_ref, fn):
    def per_expert(e, carry):
        sorted0, staging0 = tab_ref[0, 0, e], tab_ref[0, 2, e]
        n_pieces = (tab_ref[0, 1, e] + (RUN_ALIGN - 1)) // RUN_ALIGN

        def one(j, c):
            fn(pl.multiple_of(sorted0 + j * RUN_ALIGN, RUN_ALIGN), pl.multiple_of(staging0 + j * RUN_ALIGN, RUN_ALIGN))
            return c

        lax.fori_loop(0, n_pieces, one, 0)
        return carry
    lax.fori_loop(0, N_EXPERTS, per_expert, 0)


WAIT_GROUPS = (64, 8, 1)


def _wait_pieces(tab_ref, make_copy):
    total = tab_ref[0, 3, 0]
    counts = (total // WAIT_GROUPS[0], (total // WAIT_GROUPS[1]) % (WAIT_GROUPS[0] // WAIT_GROUPS[1]),
              total % WAIT_GROUPS[1])
    for group, count in zip(WAIT_GROUPS, counts):
        def one(j, c, group=group):
            make_copy(group * RUN_ALIGN).wait()
            return c
        lax.fori_loop(0, count, one, 0)


def _dispatch_kernel(tab_ref, ptab_ref, ttab_ref, lpos_ref, x_ref, xs_ref, stage_ref, zero_ref, sems, tsem,
                     *, td):
    i = pl.program_id(0)
    last = pl.num_programs(0) - 1
    slot = lax.rem(i, 2)
    rows = stage_ref.shape[1]
    r = lax.broadcasted_iota(I32, (rows, td), 0)
    sel = jnp.logical_or(r == lpos_ref[0:1, :], r == lpos_ref[1:2, :]).astype(BF)
    stage_ref[slot] = jnp.dot(sel, x_ref[...].astype(BF), preferred_element_type=F32)

    def run_copy(sl, dst, src, rows=RUN_ALIGN):
        return pltpu.make_async_copy(stage_ref.at[sl, pl.ds(src, rows), :],
                                     xs_ref.at[pl.ds(dst, rows), :], sems.at[sl])

    def tail_copies(fn):
        def per_expert(e, carry):
            def one(j, c):
                dst = pl.multiple_of(ttab_ref[0, 0, e] + j * RUN_ALIGN, RUN_ALIGN)
                fn(pltpu.make_async_copy(zero_ref, xs_ref.at[pl.ds(dst, RUN_ALIGN), :], tsem))
                return c
            lax.fori_loop(0, ttab_ref[0, 1, e], one, 0)
            return carry
        lax.fori_loop(0, N_EXPERTS, per_expert, 0)

    @pl.when(i == 0)
    def _():
        zero_ref[...] = jnp.zeros_like(zero_ref)
        tail_copies(lambda cp: cp.start())

    @pl.when(i > 0)
    def _():
        _wait_pieces(ptab_ref, lambda rows: run_copy(1 - slot, 0, 0, rows))

    _run_pieces(tab_ref, lambda dst, src: run_copy(slot, dst, src).start())

    @pl.when(i == last)
    def _():
        _wait_pieces(tab_ref, lambda rows: run_copy(slot, 0, 0, rows))
        tail_copies(lambda cp: cp.wait())


def _stage_rows(td):
    return MOE_TOPK * td + N_EXPERTS * RUN_ALIGN


def _dispatch(gtab, ttab, lpos, x1, *, td, cap):
    t = x1.shape[0]
    smem_tab = lambda imap: pl.BlockSpec((1, SUBLANES, LANES), imap, memory_space=pltpu.SMEM)
    return pl.pallas_call(
        functools.partial(_dispatch_kernel, td=td),
        grid=(t // td,),
        in_specs=[smem_tab(lambda i: (i, 0, 0)), smem_tab(lambda i: (jnp.maximum(i - 1, 0), 0, 0)),
                  smem_tab(lambda i: (0, 0, 0)),
                  pl.BlockSpec((MOE_TOPK, td), lambda i: (0, i)),
                  pl.BlockSpec((td, D_MODEL), lambda i: (i, 0))],
        out_specs=pl.BlockSpec(memory_space=pl.ANY),
        out_shape=jax.ShapeDtypeStruct((cap, D_MODEL), F32),
        scratch_shapes=[pltpu.VMEM((2, _stage_rows(td), D_MODEL), F32), pltpu.VMEM((RUN_ALIGN, D_MODEL), F32),
                        pltpu.SemaphoreType.DMA((2,)), pltpu.SemaphoreType.DMA(())],
        compiler_params=_cparams("arbitrary"),
        name="dispatch",
    )(gtab, gtab, ttab, lpos, x1)


def _expert_kernel(bexp_ref, nused_ref, xs_ref, w1_ref, w3_ref, w2_ref, y_ref):
    del bexp_ref
    i = pl.program_id(0)

    @pl.when(i < nused_ref[0])
    def _():
        xb = xs_ref[...].astype(BF)
        h1 = jnp.dot(xb, w1_ref[0], preferred_element_type=F32)
        h3 = jnp.dot(xb, w3_ref[0], preferred_element_type=F32)
        hid = (_silu(h1) * h3).astype(BF)
        y_ref[...] = jnp.dot(hid, w2_ref[0], preferred_element_type=F32)

    @pl.when(i >= nused_ref[0])
    def _():
        y_ref[...] = jnp.zeros_like(y_ref)


def _expert(bexp, nused, xs, w1, w3, w2):
    cap = xs.shape[0]
    nb = cap // EXPERT_BLOCK
    last = lambda i, nu: jnp.minimum(i, nu[0] - 1)
    grid_spec = pltpu.PrefetchScalarGridSpec(
        num_scalar_prefetch=2,
        grid=(nb,),
        in_specs=[pl.BlockSpec((EXPERT_BLOCK, D_MODEL), lambda i, be, nu: (last(i, nu), 0)),
                  pl.BlockSpec((1, D_MODEL, MOE_FF), lambda i, be, nu: (be[last(i, nu)], 0, 0)),
                  pl.BlockSpec((1, D_MODEL, MOE_FF), lambda i, be, nu: (be[last(i, nu)], 0, 0)),
                  pl.BlockSpec((1, MOE_FF, D_MODEL), lambda i, be, nu: (be[last(i, nu)], 0, 0))],
        out_specs=pl.BlockSpec((EXPERT_BLOCK, D_MODEL), lambda i, be, nu: (i, 0)),
    )
    return pl.pallas_call(
        _expert_kernel,
        grid_spec=grid_spec,
        out_shape=jax.ShapeDtypeStruct((cap, D_MODEL), F32),
        compiler_params=_cparams("arbitrary"),
        name="expert",
    )(bexp, nused, xs, w1, w3, w2)


def _combine_kernel(tab_ref, ntab_ref, lpos_ref, x_ref, w_ref, lg_ref, lb_ref, yb_ref, o_ref, stage_ref, sems,
                    *, tc):
    i = pl.program_id(0)
    last = pl.num_programs(0) - 1
    slot = lax.rem(i, 2)

    def run_copy(sl, src, dst, rows=RUN_ALIGN):
        return pltpu.make_async_copy(yb_ref.at[pl.ds(src, rows), :],
                                     stage_ref.at[sl, pl.ds(dst, rows), :], sems.at[sl])

    @pl.when(i == 0)
    def _():
        stage_ref[...] = jnp.zeros_like(stage_ref)
        _run_pieces(tab_ref, lambda src, dst: run_copy(slot, src, dst).start())

    @pl.when(i < last)
    def _():
        _run_pieces(ntab_ref, lambda src, dst: run_copy(1 - slot, src, dst).start())

    _wait_pieces(tab_ref, lambda rows: run_copy(slot, 0, 0, rows))
    staged = stage_ref[slot].astype(BF)
    c = lax.broadcasted_iota(I32, (tc, staged.shape[0]), 1)
    y = [jnp.dot((c == lpos_ref[:, k:k + 1]).astype(BF), staged, preferred_element_type=F32)
         for k in range(MOE_TOPK)]
    ffn = w_ref[:, 0:1] * y[0] + w_ref[:, 1:2] * y[1]
    o_ref[...] = _layer_norm(DN_ALPHA * x_ref[...] + ffn, lg_ref[...], lb_ref[...])


def _combine(gtab, lpos_t, x1, wts_t, lg, lb, yb, *, tc):
    t = x1.shape[0]
    n_tiles = t // tc
    full = lambda a: pl.BlockSpec(a.shape, lambda i: (0,) * a.ndim)
    smem_tab = lambda imap: pl.BlockSpec((1, SUBLANES, LANES), imap, memory_space=pltpu.SMEM)
    return pl.pallas_call(
        functools.partial(_combine_kernel, tc=tc),
        grid=(n_tiles,),
        in_specs=[smem_tab(lambda i: (i, 0, 0)), smem_tab(lambda i: (jnp.minimum(i + 1, n_tiles - 1), 0, 0)),
                  pl.BlockSpec((tc, MOE_TOPK), lambda i: (i, 0)),
                  pl.BlockSpec((tc, D_MODEL), lambda i: (i, 0)),
                  pl.BlockSpec((tc, MOE_TOPK), lambda i: (i, 0)),
                  full(lg), full(lb),
                  pl.BlockSpec(memory_space=pl.ANY)],
        out_specs=pl.BlockSpec((tc, D_MODEL), lambda i: (i, 0)),
        out_shape=jax.ShapeDtypeStruct((t, D_MODEL), F32),
        scratch_shapes=[pltpu.VMEM((2, _stage_rows(tc), D_MODEL), F32), pltpu.SemaphoreType.DMA((2,))],
        compiler_params=_cparams("arbitrary"),
        name="combine",
    )(gtab, gtab, lpos_t, x1, wts_t, lg, lb, yb)


def _tile(n, want):
    while n % want:
        want //= 2
    return want


def _mixer(x2d, b, l, w_in, rel_bias, conv_w, a_log, dt_bias, norm_g, pool_w, pool_scale,
           w_br_a, w_br_b, w_br_c, gate_b, w_out, ln_g, ln_b, wr, rb):
    t = x2d.shape[0]
    c0 = 3 * A_WIDTH
    c1 = c0 + 3 * B_WIDTH
    c2 = c1 + 2 * B_HEADS
    c3 = c2 + B_WIDTH
    c4 = c3 + C_WIDTH
    wbf = w_in.astype(BF)
    wbd = jnp.pad(wbf[:, c1:c2], ((0, 0), (0, LANES - 2 * B_HEADS)))
    qkva, qkvb, bd, go, oc = _proj(
        x2d, wbf[:, :c0], wbf[:, c0:c1], wbd, wbf[:, c2:c3], wbf[:, c3:c4],
        pool_w.astype(BF), pool_scale.reshape(1, C_WIDTH), seq=l, tm=_tile(l, 1024))

    oa = _attn(qkva.reshape(b, l, 3 * A_WIDTH), _attn_bias_table(rel_bias)).reshape(t, A_WIDTH)

    lane_pad = lambda v: jnp.pad(v.reshape(1, B_HEADS), ((0, 0), (B_HEADS, LANES - 2 * B_HEADS)))
    ob = _gdn(qkvb.reshape(b, l, 3 * B_WIDTH), bd.reshape(b, l, LANES), go.reshape(b, l, B_WIDTH),
              conv_w, lane_pad(a_log), lane_pad(dt_bias), norm_g.reshape(1, B_HEAD_DIM)).reshape(t, B_WIDTH)

    wrh, wrl = _split_bf16(wr)
    return _merge(x2d, oa, ob, oc, wbf[:, c4:], gate_b, w_br_a.astype(BF), w_br_b.astype(BF),
                  w_br_c.astype(BF), w_out.astype(BF), ln_g.reshape(1, -1), ln_b.reshape(1, -1),
                  wrh, wrl, rb, tm=_tile(t, 1024))


def _moe(x1, lgt, w1, w3, w2, ln_g, ln_b):
    t = x1.shape[0]
    tile = _tile(t, MOE_TILE)
    max_rows = t * MOE_TOPK + (t // tile) * N_EXPERTS * (RUN_ALIGN - 1)
    n_blocks = -(-max_rows // EXPERT_BLOCK) + N_EXPERTS
    cap = n_blocks * EXPERT_BLOCK
    wts, lpos, tab, cnt = _route(lgt, tn=tile)
    gtab, ttab, bexp, nused = _layout(cnt, tab, n_blocks=n_blocks)
    xs = _dispatch(gtab, ttab, lpos, x1, td=tile, cap=cap)
    yb = _expert(bexp.reshape(n_blocks), nused[0, :1], xs, w1.astype(BF), w3.astype(BF), w2.astype(BF))
    return _combine(gtab, lpos.T, x1, wts.T, ln_g.reshape(1, -1), ln_b.reshape(1, -1), yb, tc=tile)


def kernel(x, w_in, attn_rel_bias, gdn_conv_w, gdn_a_log, gdn_dt_bias, gdn_norm_g, pool_w, pool_scale,
           w_branch_a, w_branch_b, w_branch_c, gate_b, w_out, ln1_g, ln1_b, router_group_w,
           router_group_b, router_expert_w, router_expert_b, moe_w1, moe_w3, moe_w2, ln2_g, ln2_b):
    b, l, d = x.shape
    x2d = x.reshape(b * l, d)
    for i in range(DEPTH):
        wr = jnp.concatenate([router_expert_w[i].T, router_group_w[i].T,
                              jnp.zeros((ROUTER_ROWS - N_EXPERTS - MOE_GROUPS, d), F32)], axis=0)
        rb = jnp.concatenate([router_expert_b[i], router_group_b[i],
                              jnp.zeros((ROUTER_ROWS - N_EXPERTS - MOE_GROUPS,), F32)]).reshape(ROUTER_ROWS, 1)
        x1, lgt = _mixer(x2d, b, l, w_in[i], attn_rel_bias[i], gdn_conv_w[i], gdn_a_log[i], gdn_dt_bias[i],
                         gdn_norm_g[i], pool_w[i], pool_scale[i], w_branch_a[i], w_branch_b[i],
                         w_branch_c[i], gate_b[i], w_out[i], ln1_g[i], ln1_b[i], wr, rb)
        x2d = _moe(x1, lgt, moe_w1[i], moe_w3[i], moe_w2[i], ln2_g[i], ln2_b[i])
    return x2d.reshape(b, l, d)
```

```python
import functools

import jax
import jax.numpy as jnp
from jax import lax
from jax.experimental import pallas as pl
from jax.experimental.pallas import tpu as pltpu

BF = jnp.bfloat16
F32 = jnp.float32
I32 = jnp.int32

D_MODEL = 1024
DEPTH = 2
CHUNK = 64

A_HEADS = 8
A_HEAD_DIM = 64
A_WIDTH = A_HEADS * A_HEAD_DIM
A_LEFT_CHUNKS = 8
A_REL_CLIP = 256

B_HEADS = 4
B_HEAD_DIM = 128
B_WIDTH = B_HEADS * B_HEAD_DIM
B_CONV = 4

C_WINDOWS = (2, 4, 8, 16)
C_GROUPS = 4
C_GROUP_DIM = 128
C_WIDTH = C_GROUPS * C_GROUP_DIM

N_BRANCH = 3
MOE_GROUPS = 4
MOE_PER_GROUP = 8
N_EXPERTS = MOE_GROUPS * MOE_PER_GROUP
MOE_TOPK = 2
MOE_FF = 512

DN_ALPHA = (2 * DEPTH) ** 0.25
LN_EPS = 1e-5
RMS_EPS = 1e-6

LANES = 128
SUBLANES = 8
VMEM_LIMIT_BYTES = 56 * 1024 * 1024

QBLOCK = 2 * CHUNK
KBAND = (A_LEFT_CHUNKS + 2) * CHUNK
NEG_BIG = -1e30

ROUTER_ROWS = 40
EXPERT_BLOCK = 512
MOE_TILE = 512
RUN_ALIGN = 8


def _cparams(*sem):
    return pltpu.CompilerParams(dimension_semantics=sem, vmem_limit_bytes=VMEM_LIMIT_BYTES)


def _mm(a, b):
    return jnp.dot(a.astype(BF), b.astype(BF), preferred_element_type=F32)


def _mm_nt(a, b):
    return lax.dot_general(a.astype(BF), b.astype(BF), (((1,), (1,)), ((), ())),
                           preferred_element_type=F32)


def _mm_tn(a, b):
    return lax.dot_general(a.astype(BF), b.astype(BF), (((0,), (0,)), ((), ())),
                           preferred_element_type=F32)


def _split_bf16(a):
    hi = a.astype(BF)
    lo = (a - hi.astype(F32)).astype(BF)
    return hi, lo


def _mm3(a, b):
    ah, al = _split_bf16(a)
    bh, bl = _split_bf16(b)
    return (jnp.dot(ah, bh, preferred_element_type=F32)
            + (jnp.dot(ah, bl, preferred_element_type=F32)
               + jnp.dot(al, bh, preferred_element_type=F32)))


def _sigmoid(x):
    return 1.0 / (1.0 + jnp.exp(-x))


def _silu(x):
    return x * _sigmoid(x)


def _layer_norm(z, g, b):
    mu = jnp.mean(z, axis=-1, keepdims=True)
    zc = z - mu
    var = jnp.mean(zc * zc, axis=-1, keepdims=True)
    return zc * lax.rsqrt(var + LN_EPS) * g + b


POOL_TAIL = 16


def _proj_kernel(x_ref, wa_ref, wb_ref, wbd_ref, wgo_ref, wc_ref, pw_ref, ps_ref,
                 qkva_ref, qkvb_ref, bd_ref, go_ref, oc_ref, tail_ref, *, tiles_per_seq, tm):
    it = pl.program_id(0) % tiles_per_seq
    xb = x_ref[...].astype(BF)
    qkva_ref[...] = jnp.dot(xb, wa_ref[...], preferred_element_type=F32).astype(BF)
    qkvb_ref[...] = jnp.dot(xb, wb_ref[...], preferred_element_type=F32).astype(BF)
    bd_ref[...] = jnp.dot(xb, wbd_ref[...], preferred_element_type=F32)
    go_ref[...] = jnp.dot(xb, wgo_ref[...], preferred_element_type=F32).astype(BF)
    u = jnp.dot(xb, wc_ref[...], preferred_element_type=F32)

    @pl.when(it == 0)
    def _():
        tail_ref[...] = jnp.zeros_like(tail_ref)

    ext = jnp.concatenate([tail_ref[...], u], axis=0)
    tail_ref[...] = u[tm - POOL_TAIL:, :]
    pos = (it * tm + 1 + lax.broadcasted_iota(I32, (tm, 1), 0)).astype(F32)
    s = ext
    for gi, win in enumerate(C_WINDOWS):
        if gi:
            s = s[:, C_GROUP_DIM:]
        s = s + pltpu.roll(s, win // 2, 0)
        lo_c, hi_c = gi * C_GROUP_DIM, (gi + 1) * C_GROUP_DIM
        cnt = jnp.minimum(pos, float(win))
        pooled = s[POOL_TAIL:, :C_GROUP_DIM] / cnt - u[:, lo_c:hi_c]
        mixed = jnp.dot(pooled.astype(BF), pw_ref[gi], preferred_element_type=F32)
        oc_ref[:, lo_c:hi_c] = (mixed * ps_ref[:, lo_c:hi_c]).astype(BF)


def _proj(x2d, wa, wb, wbd, wgo, wc, pw, ps, *, seq, tm):
    t = x2d.shape[0]
    full = lambda a: pl.BlockSpec(a.shape, lambda i: (0,) * a.ndim)
    row = lambda n: pl.BlockSpec((tm, n), lambda i: (i, 0))
    return pl.pallas_call(
        functools.partial(_proj_kernel, tiles_per_seq=seq // tm, tm=tm),
        grid=(t // tm,),
        in_specs=[row(D_MODEL), full(wa), full(wb), full(wbd), full(wgo), full(wc), full(pw), full(ps)],
        out_specs=[row(3 * A_WIDTH), row(3 * B_WIDTH), row(LANES), row(B_WIDTH), row(C_WIDTH)],
        out_shape=[jax.ShapeDtypeStruct((t, 3 * A_WIDTH), BF),
                   jax.ShapeDtypeStruct((t, 3 * B_WIDTH), BF),
                   jax.ShapeDtypeStruct((t, LANES), F32),
                   jax.ShapeDtypeStruct((t, B_WIDTH), BF),
                   jax.ShapeDtypeStruct((t, C_WIDTH), BF)],
        scratch_shapes=[pltpu.VMEM((POOL_TAIL, C_WIDTH), F32)],
        compiler_params=_cparams("arbitrary"),
        name="proj",
    )(x2d, wa, wb, wbd, wgo, wc, pw, ps)


def _attn_bias_table(rel_bias):
    h = rel_bias.shape[0]
    pad = A_LEFT_CHUNKS * CHUNK
    width = KBAND + pad
    n = width + QBLOCK - 1
    far = jnp.repeat(rel_bias[:, -1:], QBLOCK - 1 + pad - A_REL_CLIP, axis=1)
    near = jnp.repeat(rel_bias[:, :1], n - far.shape[1] - rel_bias.shape[1], axis=1)
    w = jnp.concatenate([far, rel_bias[:, ::-1], near], axis=1).astype(F32)
    w = jnp.roll(w, -(QBLOCK - 1), axis=1)
    toep = jnp.tile(w, (1, QBLOCK))[:, :QBLOCK * (n - 1)].reshape(h, QBLOCK, n - 1)[:, :, :width]
    r = jnp.arange(QBLOCK)[:, None]
    m = jnp.arange(width)[None, :]
    kc, qc = m // CHUNK, r // CHUNK
    valid = (kc >= qc) & (kc <= qc + A_LEFT_CHUNKS)
    tab = jnp.where(valid[None], toep, NEG_BIG)
    return tab.reshape(h, QBLOCK, width // LANES, LANES).transpose(0, 2, 1, 3)


def _attn_kernel(q_ref, k_ref, v_ref, tab_ref, o_ref):
    j = pl.program_id(1)
    first_blk = A_LEFT_CHUNKS * CHUNK // QBLOCK
    kstart = pl.multiple_of(jnp.maximum(j - first_blk, 0) * QBLOCK, QBLOCK)
    offb = jnp.maximum(first_blk - j, 0)
    lane = lax.broadcasted_iota(I32, (QBLOCK, LANES), 1)
    lo = lane < A_HEAD_DIM
    nkb = KBAND // LANES
    pairs = range(A_HEADS // 2)
    cols = [slice(hp * LANES, (hp + 1) * LANES) for hp in pairs]
    scores = []
    for hp in pairs:
        qp = q_ref[0, :, cols[hp]] * jnp.asarray(A_HEAD_DIM ** -0.5, BF)
        zero = jnp.zeros_like(qp)
        q2 = jnp.concatenate([jnp.where(lo, qp, zero), jnp.where(lo, zero, qp)], axis=0)
        kp = k_ref[0, pl.ds(kstart, KBAND), cols[hp]]
        scores.append(lax.dot_general(q2, kp, (((1,), (1,)), ((), ())), preferred_element_type=F32))
    probs, denoms = [], []
    for hp in pairs:
        bias = jnp.concatenate(
            [jnp.concatenate([tab_ref[2 * hp + par, offb + i] for i in range(nkb)], axis=1) for par in range(2)],
            axis=0)
        s = scores[hp] + bias
        p = jnp.exp(s - jnp.max(s, axis=-1, keepdims=True))
        denoms.append(jnp.sum(p, axis=-1, keepdims=True))
        probs.append(p.astype(BF))
    for hp in pairs:
        vp = v_ref[0, pl.ds(kstart, KBAND), cols[hp]]
        o2 = jnp.dot(probs[hp], vp, preferred_element_type=F32) / denoms[hp]
        o_ref[0, :, cols[hp]] = jnp.where(lo, o2[:QBLOCK], o2[QBLOCK:]).astype(BF)


def _attn(qkva, tab):
    b, l, _ = qkva.shape
    return pl.pallas_call(
        _attn_kernel,
        grid=(b, l // QBLOCK),
        in_specs=[pl.BlockSpec((1, QBLOCK, A_WIDTH), lambda bi, j: (bi, j, 0)),
                  pl.BlockSpec((1, l, A_WIDTH), lambda bi, j: (bi, 0, 1)),
                  pl.BlockSpec((1, l, A_WIDTH), lambda bi, j: (bi, 0, 2)),
                  pl.BlockSpec(tab.shape, lambda bi, j: (0, 0, 0, 0))],
        out_specs=pl.BlockSpec((1, QBLOCK, A_WIDTH), lambda bi, j: (bi, j, 0)),
        out_shape=jax.ShapeDtypeStruct((b, l, A_WIDTH), BF),
        compiler_params=_cparams("arbitrary", "arbitrary"),
        name="attn",
    )(qkva, qkva, qkva, tab)


CONV_HIST = 16
GDN_PREP_UNROLL = 4


def _gdn_kernel(qkv_ref, bd_ref, go_ref, cw_ref, alog_ref, dtb_ref, ng_ref, o_ref,
                u_ref, wq_ref, qkkd_ref, egl_ref, state_ref, *, n_chunks):
    rt = lax.broadcasted_iota(I32, (CHUNK, CHUNK), 0)
    ct = lax.broadcasted_iota(I32, (CHUNK, CHUNK), 1)
    tril = rt >= ct
    strict = rt > ct
    row_c = lax.broadcasted_iota(I32, (CHUNK, LANES), 0)
    eye_f = (rt == ct).astype(F32)
    eye_d = (lax.broadcasted_iota(I32, (B_HEAD_DIM, B_HEAD_DIM), 0)
             == lax.broadcasted_iota(I32, (B_HEAD_DIM, B_HEAD_DIM), 1)).astype(BF)
    neg_rate = -jnp.exp(alog_ref[...])
    sr = lax.broadcasted_iota(I32, (B_CONV * CHUNK, CONV_HIST + CHUNK), 0)
    sc = lax.broadcasted_iota(I32, (B_CONV * CHUNK, CONV_HIST + CHUNK), 1)
    shift_sel = (sc == CONV_HIST + jnp.bitwise_and(sr, CHUNK - 1)
                 - jnp.right_shift(sr, CHUNK.bit_length() - 1)).astype(BF)

    def heads(fn):
        return jnp.concatenate([fn(h, slice(h * B_HEAD_DIM, (h + 1) * B_HEAD_DIM)) for h in range(B_HEADS)],
                               axis=1)

    def prep(c):
        r0 = pl.multiple_of(c * CHUNK, CHUNK)
        cur = qkv_ref[0, pl.ds(r0, CHUNK), :]
        h0 = pl.multiple_of(jnp.maximum(r0 - CONV_HIST, 0), CONV_HIST)
        hist = qkv_ref[0, pl.ds(h0, CONV_HIST), :]
        hist = jnp.where(c > 0, hist, jnp.zeros_like(hist))
        taps = jnp.dot(shift_sel, jnp.concatenate([hist, cur], axis=0), preferred_element_type=F32)
        acc = taps[:CHUNK] * cw_ref[B_CONV - 1:B_CONV, :]
        for back in range(1, B_CONV):
            tap = B_CONV - 1 - back
            acc = acc + taps[back * CHUNK:(back + 1) * CHUNK] * cw_ref[tap:tap + 1, :]
        act = _silu(acc)
        q_all, k_all, v_all = act[:, :B_WIDTH], act[:, B_WIDTH:2 * B_WIDTH], act[:, 2 * B_WIDTH:]

        bd = bd_ref[0, pl.ds(r0, CHUNK), :]
        beta_all = _sigmoid(bd)
        z = bd + dtb_ref[...]
        softplus = jnp.maximum(z, 0.0) + jnp.log1p(jnp.exp(-jnp.abs(z)))
        g_all = neg_rate * softplus
        gc_all = g_all
        for sh in (1, 2, 4, 8, 16, 32):
            gc_all = gc_all + jnp.where(row_c >= sh, pltpu.roll(gc_all, sh, 0), 0.0)
        egc_all = jnp.exp(gc_all)
        glast = gc_all[CHUNK - 1:CHUNK, :]
        ekd_all = jnp.exp(glast - gc_all)
        egl = jnp.exp(glast)
        col = lambda m, h: m[:, B_HEADS + h:B_HEADS + h + 1]

        def l2n(m):
            return m * lax.rsqrt(jnp.sum(m * m, axis=-1, keepdims=True) + RMS_EPS)

        qn = heads(lambda h, hs: l2n(q_all[:, hs]) * (B_HEAD_DIM ** -0.5))
        kn = heads(lambda h, hs: l2n(k_all[:, hs]))
        kbeta = heads(lambda h, hs: kn[:, hs] * beta_all[:, h:h + 1])
        vbeta = heads(lambda h, hs: v_all[:, hs] * beta_all[:, h:h + 1])
        kbe = heads(lambda h, hs: kbeta[:, hs] * col(egc_all, h))
        qe = heads(lambda h, hs: qn[:, hs] * col(egc_all, h))
        chains = []
        for h in range(B_HEADS):
            hs = slice(h * B_HEAD_DIM, (h + 1) * B_HEAD_DIM)
            gcc = jnp.broadcast_to(col(gc_all, h), (CHUNK, CHUNK))
            gcr = jnp.sum(gcc * eye_f, axis=0, keepdims=True)
            chains.append(dict(
                c=c, h=h, hs=hs,
                decay=jnp.exp(jnp.where(tril, gcc - gcr, -jnp.inf)),
                kq=jnp.concatenate([kbeta[:, hs], qn[:, hs]], axis=0).astype(BF),
                kn=kn[:, hs].astype(BF),
                rhs=jnp.concatenate([vbeta[:, hs], kbe[:, hs]], axis=1).astype(BF),
                kd=kn[:, hs] * col(ekd_all, h)))
        egl_row = heads(lambda h, hs: jnp.broadcast_to(col(egl, h), (1, B_HEAD_DIM)))
        return chains, qe.astype(BF), egl_row

    nt = lambda a, b: lax.dot_general(a, b, (((1,), (1,)), ((), ())), preferred_element_type=F32)
    mm = lambda a, b: jnp.dot(a, b, preferred_element_type=F32)

    def prep_levels(i):
        fronts = [prep(i * GDN_PREP_UNROLL + g) for g in range(GDN_PREP_UNROLL)]
        ch = [x for f in fronts for x in f[0]]
        aq = [nt(x["kq"], x["kn"]) for x in ch]
        kdt = [x["kd"].T for x in ch]
        yield
        qk = [(a[CHUNK:] * x["decay"]).astype(BF) for x, a in zip(ch, aq)]
        pw = [-jnp.where(strict, a[:CHUNK] * x["decay"], 0.0) for x, a in zip(ch, aq)]
        tmat = [eye_f + p for p in pw]
        pwb = [p.astype(BF) for p in pw]
        pw = [mm(p, p) for p in pwb]
        yield
        n_sq = CHUNK.bit_length() - 2
        for lvl in range(n_sq):
            pwb = [p.astype(BF) for p in pw]
            if lvl + 1 < n_sq:
                both = [mm(jnp.concatenate([t.astype(BF), p], axis=0), p) for t, p in zip(tmat, pwb)]
                yield
                tmat = [t + r[:CHUNK] for t, r in zip(tmat, both)]
                pw = [r[CHUNK:] for r in both]
            else:
                last = [mm(t.astype(BF), p) for t, p in zip(tmat, pwb)]
                yield
                tmat = [t + r for t, r in zip(tmat, last)]
        uw = [mm(t.astype(BF), x["rhs"]) for t, x in zip(tmat, ch)]
        yield
        for x, r, qkx, kt in zip(ch, uw, qk, kdt):
            u_ref[x["c"], :, x["hs"]] = r[:, :B_HEAD_DIM]
            wq_ref[x["c"], :CHUNK, x["hs"]] = r[:, B_HEAD_DIM:].astype(BF)
            qkkd_ref[x["c"], x["h"], :CHUNK, :] = qkx
            qkkd_ref[x["c"], x["h"], CHUNK:, :] = kt.astype(BF)
        for g, (_, qe_b, egl_row) in enumerate(fronts):
            wq_ref[i * GDN_PREP_UNROLL + g, CHUNK:, :] = qe_b
            egl_ref[i * GDN_PREP_UNROLL + g] = egl_row

    def step_levels(c):
        r0 = pl.multiple_of(c * CHUNK, CHUNK)
        st = state_ref[...]
        stb = st.astype(BF)
        hss = [slice(h * B_HEAD_DIM, (h + 1) * B_HEAD_DIM) for h in range(B_HEADS)]
        r1 = [mm(wq_ref[c, :, hs], stb[:, hs]) for hs in hss]
        yield
        v_new = [(u_ref[c, :, hs] - r[:CHUNK]).astype(BF) for hs, r in zip(hss, r1)]
        r2 = [mm(qkkd_ref[c, h], v_new[h]) for h in range(B_HEADS)]
        yield
        o_all = jnp.concatenate([a[CHUNK:] + b[:CHUNK] for a, b in zip(r1, r2)], axis=1)
        state_ref[...] = st * egl_ref[c] + jnp.concatenate([b[CHUNK:] for b in r2], axis=1)
        gate = go_ref[0, pl.ds(r0, CHUNK), :].astype(F32)
        o_n = heads(lambda h, hs: o_all[:, hs]
                    * lax.rsqrt(jnp.mean(o_all[:, hs] * o_all[:, hs], axis=-1, keepdims=True) + RMS_EPS)
                    * ng_ref[...])
        o_ref[0, pl.ds(r0, CHUNK), :] = (o_n * _silu(gate)).astype(BF)

    def steps_of(i):
        for g in range(GDN_PREP_UNROLL):
            yield from step_levels(i * GDN_PREP_UNROLL + g)

    def run_interleaved(*gens):
        live = list(gens)
        while live:
            for g in list(live):
                if next(g, StopIteration) is StopIteration:
                    live.remove(g)

    n_groups = n_chunks // GDN_PREP_UNROLL
    state_ref[...] = jnp.zeros_like(state_ref)
    run_interleaved(prep_levels(0))

    def body(i, carry):
        run_interleaved(steps_of(i - 1), prep_levels(i))
        return carry

    lax.fori_loop(1, n_groups, body, 0)
    run_interleaved(steps_of(n_groups - 1))


def _gdn(qkvb, bd, go, cw, alog, dtb, ng):
    b, l, _ = qkvb.shape
    nc = l // CHUNK
    full = lambda a: pl.BlockSpec(a.shape, lambda bi: (0,) * a.ndim)
    seq = lambda n: pl.BlockSpec((1, l, n), lambda bi: (bi, 0, 0))
    return pl.pallas_call(
        functools.partial(_gdn_kernel, n_chunks=nc),
        grid=(b,),
        in_specs=[seq(3 * B_WIDTH), seq(LANES), seq(B_WIDTH), full(cw), full(alog), full(dtb), full(ng)],
        out_specs=seq(B_WIDTH),
        out_shape=jax.ShapeDtypeStruct((b, l, B_WIDTH), BF),
        scratch_shapes=[pltpu.VMEM((nc, CHUNK, B_WIDTH), F32),
                        pltpu.VMEM((nc, 2 * CHUNK, B_WIDTH), BF),
                        pltpu.VMEM((nc, B_HEADS, CHUNK + B_HEAD_DIM, CHUNK), BF),
                        pltpu.VMEM((nc, 1, B_WIDTH), F32),
                        pltpu.VMEM((B_HEAD_DIM, B_WIDTH), F32)],
        compiler_params=_cparams("arbitrary"),
        name="gdn",
    )(qkvb, bd, go, cw, alog, dtb, ng)


def _merge_kernel(x_ref, oa_ref, ob_ref, oc_ref, wg_ref, gb_ref, wa_ref, wb_ref, wc_ref, wo_ref,
                  lg_ref, lb_ref, wrh_ref, wrl_ref, rb_ref, x1_ref, lgt_ref):
    x = x_ref[...]
    xb = x.astype(BF)
    y = None
    for i, (o_ref, w_ref) in enumerate(((oa_ref, wa_ref), (ob_ref, wb_ref), (oc_ref, wc_ref))):
        gl = jnp.dot(xb, wg_ref[:, i * D_MODEL:(i + 1) * D_MODEL], preferred_element_type=F32)
        gate = _sigmoid(gl + gb_ref[i:i + 1, :])
        br = gate * jnp.dot(o_ref[...], w_ref[...], preferred_element_type=F32)
        y = br if y is None else y + br
    mix = jnp.dot(y.astype(BF), wo_ref[...], preferred_element_type=F32)
    x1 = _layer_norm(DN_ALPHA * x + mix, lg_ref[...], lb_ref[...])
    x1_ref[...] = x1
    xh, xl = _split_bf16(x1)
    nt = lambda a, b: lax.dot_general(a, b, (((1,), (1,)), ((), ())), preferred_element_type=F32)
    lgt_ref[...] = nt(wrh_ref[...], xh) + (nt(wrh_ref[...], xl) + nt(wrl_ref[...], xh)) + rb_ref[...]


def _merge(x2d, oa, ob, oc, wg, gb, wa, wb, wc, wo, lg, lb, wrh, wrl, rb, *, tm):
    t = x2d.shape[0]
    full = lambda a: pl.BlockSpec(a.shape, lambda i: (0,) * a.ndim)
    row = lambda n: pl.BlockSpec((tm, n), lambda i: (i, 0))
    return pl.pallas_call(
        _merge_kernel,
        grid=(t // tm,),
        in_specs=[row(D_MODEL), row(A_WIDTH), row(B_WIDTH), row(C_WIDTH), full(wg), full(gb), full(wa),
                  full(wb), full(wc), full(wo), full(lg), full(lb), full(wrh), full(wrl), full(rb)],
        out_specs=[row(D_MODEL), pl.BlockSpec((ROUTER_ROWS, tm), lambda i: (0, i))],
        out_shape=[jax.ShapeDtypeStruct((t, D_MODEL), F32),
                   jax.ShapeDtypeStruct((ROUTER_ROWS, t), F32)],
        compiler_params=_cparams("arbitrary"),
        name="merge",
    )(x2d, oa, ob, oc, wg, gb, wa, wb, wc, wo, lg, lb, wrh, wrl, rb)


def _route_kernel(lgt_ref, wts_ref, lpos_ref, tab_ref, cnt_ref, run_ref, *, tn):
    @pl.when(pl.program_id(0) == 0)
    def _():
        run_ref[...] = jnp.zeros_like(run_ref)

    le = lgt_ref[0:N_EXPERTS, :]
    lg = lgt_ref[N_EXPERTS:N_EXPERTS + MOE_GROUPS, :]
    gi = lax.broadcasted_iota(I32, (MOE_GROUPS, tn), 0).astype(F32)
    ei = lax.broadcasted_iota(I32, (N_EXPERTS, tn), 0).astype(F32)
    eg = jnp.right_shift(lax.broadcasted_iota(I32, (N_EXPERTS, tn), 0),
                         MOE_PER_GROUP.bit_length() - 1).astype(F32)
    mg = jnp.max(lg, axis=0, keepdims=True)
    gsel = jnp.min(jnp.where(lg == mg, gi, float(MOE_GROUPS)), axis=0, keepdims=True)
    p_top = 1.0 / jnp.sum(jnp.exp(lg - mg), axis=0, keepdims=True)
    l1 = jnp.where(eg == gsel, le, -jnp.inf)
    m1 = jnp.max(l1, axis=0, keepdims=True)
    i1 = jnp.min(jnp.where(l1 == m1, ei, float(N_EXPERTS)), axis=0, keepdims=True)
    l2 = jnp.where(ei == i1, -jnp.inf, l1)
    m2 = jnp.max(l2, axis=0, keepdims=True)
    i2 = jnp.min(jnp.where(l2 == m2, ei, float(N_EXPERTS)), axis=0, keepdims=True)
    e2 = jnp.exp(m2 - m1)
    den = 1.0 + e2
    wts_ref[...] = jnp.concatenate([p_top / den, p_top * (e2 / den)], axis=0)

    before = (lax.broadcasted_iota(I32, (tn, tn), 0) < lax.broadcasted_iota(I32, (tn, tn), 1)).astype(BF)
    run = run_ref[...]
    oh = [ei == ik for ik in (i1, i2)]
    prefix = [jnp.dot(o.astype(BF), before, preferred_element_type=F32) for o in oh]
    cnt_k = [jnp.sum(o.astype(F32), axis=1, keepdims=True) for o in oh]
    cnt = cnt_k[0] + cnt_k[1]
    cnt_al = jnp.floor((cnt + (RUN_ALIGN - 1)) * (1.0 / RUN_ALIGN)) * RUN_ALIGN
    er = lax.broadcasted_iota(I32, (N_EXPERTS, N_EXPERTS), 0)
    ec = lax.broadcasted_iota(I32, (N_EXPERTS, N_EXPERTS), 1)
    lstart = jnp.dot((ec < er).astype(BF), jnp.broadcast_to(cnt_al, (N_EXPERTS, LANES)).astype(BF),
                     preferred_element_type=F32)[:, 0:1]
    first = [0.0, cnt_k[0]]
    pick = lambda k: jnp.sum(jnp.where(oh[k], prefix[k] + first[k] + lstart, 0.0), axis=0, keepdims=True)
    lpos_ref[...] = jnp.concatenate([pick(0), pick(1)], axis=0).astype(I32)
    eye_l = (lax.broadcasted_iota(I32, (N_EXPERTS, LANES), 0) == lax.broadcasted_iota(I32, (N_EXPERTS, LANES), 1))
    as_row = lambda colv: jnp.sum(jnp.where(eye_l, colv, 0.0), axis=0, keepdims=True)
    pieces = jnp.broadcast_to(jnp.sum(cnt_al, axis=0, keepdims=True) * (1.0 / RUN_ALIGN), (1, LANES))
    tab_ref[0] = jnp.concatenate([as_row(run), as_row(cnt), as_row(lstart), pieces,
                                  jnp.zeros((SUBLANES - 4, LANES), F32)], axis=0).astype(I32)
    run = run + cnt_al
    run_ref[...] = run
    cnt_ref[...] = jnp.broadcast_to(run, cnt_ref.shape)


def _route(lgt, *, tn):
    t = lgt.shape[1]
    tok = lambda: pl.BlockSpec((MOE_TOPK, tn), lambda i: (0, i))
    return pl.pallas_call(
        functools.partial(_route_kernel, tn=tn),
        grid=(t // tn,),
        in_specs=[pl.BlockSpec((ROUTER_ROWS, tn), lambda i: (0, i))],
        out_specs=[tok(), tok(), pl.BlockSpec((1, SUBLANES, LANES), lambda i: (i, 0, 0)),
                   pl.BlockSpec((N_EXPERTS, LANES), lambda i: (0, 0))],
        out_shape=[jax.ShapeDtypeStruct((MOE_TOPK, t), F32),
                   jax.ShapeDtypeStruct((MOE_TOPK, t), I32),
                   jax.ShapeDtypeStruct((t // tn, SUBLANES, LANES), I32),
                   jax.ShapeDtypeStruct((N_EXPERTS, LANES), F32)],
        scratch_shapes=[pltpu.VMEM((N_EXPERTS, 1), F32)],
        compiler_params=_cparams("arbitrary"),
        name="route",
    )(lgt)


def _layout_kernel(cnt_ref, tab_ref, gtab_ref, ttab_ref, bexp_ref, nused_ref, *, n_blocks):
    cnt = cnt_ref[...].astype(I32)
    shift = EXPERT_BLOCK.bit_length() - 1
    padded_blocks = jnp.right_shift(cnt + (EXPERT_BLOCK - 1), shift)
    er = lax.broadcasted_iota(I32, (N_EXPERTS, LANES), 0)
    el = lax.broadcasted_iota(I32, (N_EXPERTS, LANES), 1)
    start = jnp.zeros((N_EXPERTS, LANES), I32)
    for e in range(N_EXPERTS - 1):
        start = start + jnp.where(er > e, padded_blocks[e:e + 1, :], 0)
    end = start + padded_blocks
    as_row = lambda m: jnp.sum(jnp.where(er == el, m, 0), axis=0, keepdims=True)
    tab = tab_ref[...]
    first_row = lax.broadcasted_iota(I32, tab.shape, 1) == 0
    gtab_ref[...] = tab + jnp.where(first_row, as_row(start * EXPERT_BLOCK)[None], 0)
    tail_rows = padded_blocks * EXPERT_BLOCK - cnt
    ttab_ref[...] = jnp.concatenate(
        [as_row(start * EXPERT_BLOCK + cnt), as_row(jnp.right_shift(tail_rows, RUN_ALIGN.bit_length() - 1)),
         jnp.zeros((SUBLANES - 2, LANES), I32)], axis=0)[None]
    blk = lax.broadcasted_iota(I32, (1, n_blocks), 1)
    owner = jnp.zeros((1, n_blocks), I32)
    for e in range(N_EXPERTS):
        owner = owner + (blk >= end[e:e + 1, 0:1]).astype(I32)
    bexp_ref[...] = jnp.minimum(owner, N_EXPERTS - 1)
    nused_ref[...] = end[N_EXPERTS - 1:N_EXPERTS, :]


def _layout(cnt, tab, *, n_blocks):
    full = lambda shape: pl.BlockSpec(shape, lambda i: (0,) * len(shape))
    return pl.pallas_call(
        functools.partial(_layout_kernel, n_blocks=n_blocks),
        grid=(1,),
        in_specs=[full(cnt.shape), full(tab.shape)],
        out_specs=[full(tab.shape), full((1, SUBLANES, LANES)), full((1, n_blocks)), full((1, LANES))],
        out_shape=[jax.ShapeDtypeStruct(tab.shape, I32),
                   jax.ShapeDtypeStruct((1, SUBLANES, LANES), I32),
                   jax.ShapeDtypeStruct((1, n_blocks), I32),
                   jax.ShapeDtypeStruct((1, LANES), I32)],
        compiler_params=_cparams("arbitrary"),
        name="layout",
    )(cnt, tab)


def _run_pieces(tab_ref, fn):
    def per_expert(e, carry):
        sorted0, staging0 = tab_ref[0, 0, e], tab_ref[0, 2, e]
        n_pieces = (tab_ref[0, 1, e] + (RUN_ALIGN - 1)) // RUN_ALIGN

        def one(j, c):
            fn(pl.multiple_of(sorted0 + j * RUN_ALIGN, RUN_ALIGN), pl.multiple_of(staging0 + j * RUN_ALIGN, RUN_ALIGN))
            return c

        lax.fori_loop(0, n_pieces, one, 0)
        return carry
    lax.fori_loop(0, N_EXPERTS, per_expert, 0)


WAIT_GROUPS = (64, 8, 1)


def _wait_pieces(tab_ref, make_copy):
    total = tab_ref[0, 3, 0]
    counts = (total // WAIT_GROUPS[0], (total // WAIT_GROUPS[1]) % (WAIT_GROUPS[0] // WAIT_GROUPS[1]),
              total % WAIT_GROUPS[1])
    for group, count in zip(WAIT_GROUPS, counts):
        def one(j, c, group=group):
            make_copy(group * RUN_ALIGN).wait()
            return c
        lax.fori_loop(0, count, one, 0)


def _dispatch_kernel(tab_ref, ptab_ref, ttab_ref, lpos_ref, x_ref, xs_ref, stage_ref, zero_ref, sems, tsem,
                     *, td):
    i = pl.program_id(0)
    last = pl.num_programs(0) - 1
    slot = lax.rem(i, 2)
    rows = stage_ref.shape[1]
    r = lax.broadcasted_iota(I32, (rows, td), 0)
    sel = jnp.logical_or(r == lpos_ref[0:1, :], r == lpos_ref[1:2, :]).astype(BF)
    stage_ref[slot] = jnp.dot(sel, x_ref[...].astype(BF), preferred_element_type=F32)

    def run_copy(sl, dst, src, rows=RUN_ALIGN):
        return pltpu.make_async_copy(stage_ref.at[sl, pl.ds(src, rows), :],
                                     xs_ref.at[pl.ds(dst, rows), :], sems.at[sl])

    def tail_copies(fn):
        def per_expert(e, carry):
            def one(j, c):
                dst = pl.multiple_of(ttab_ref[0, 0, e] + j * RUN_ALIGN, RUN_ALIGN)
                fn(pltpu.make_async_copy(zero_ref, xs_ref.at[pl.ds(dst, RUN_ALIGN), :], tsem))
                return c
            lax.fori_loop(0, ttab_ref[0, 1, e], one, 0)
            return carry
        lax.fori_loop(0, N_EXPERTS, per_expert, 0)

    @pl.when(i == 0)
    def _():
        zero_ref[...] = jnp.zeros_like(zero_ref)
        tail_copies(lambda cp: cp.start())

    @pl.when(i > 0)
    def _():
        _wait_pieces(ptab_ref, lambda rows: run_copy(1 - slot, 0, 0, rows))

    _run_pieces(tab_ref, lambda dst, src: run_copy(slot, dst, src).start())

    @pl.when(i == last)
    def _():
        _wait_pieces(tab_ref, lambda rows: run_copy(slot, 0, 0, rows))
        tail_copies(lambda cp: cp.wait())


def _stage_rows(td):
    return MOE_TOPK * td + N_EXPERTS * RUN_ALIGN


def _dispatch(gtab, ttab, lpos, x1, *, td, cap):
    t = x1.shape[0]
    smem_tab = lambda imap: pl.BlockSpec((1, SUBLANES, LANES), imap, memory_space=pltpu.SMEM)
    return pl.pallas_call(
        functools.partial(_dispatch_kernel, td=td),
        grid=(t // td,),
        in_specs=[smem_tab(lambda i: (i, 0, 0)), smem_tab(lambda i: (jnp.maximum(i - 1, 0), 0, 0)),
                  smem_tab(lambda i: (0, 0, 0)),
                  pl.BlockSpec((MOE_TOPK, td), lambda i: (0, i)),
                  pl.BlockSpec((td, D_MODEL), lambda i: (i, 0))],
        out_specs=pl.BlockSpec(memory_space=pl.ANY),
        out_shape=jax.ShapeDtypeStruct((cap, D_MODEL), F32),
        scratch_shapes=[pltpu.VMEM((2, _stage_rows(td), D_MODEL), F32), pltpu.VMEM((RUN_ALIGN, D_MODEL), F32),
                        pltpu.SemaphoreType.DMA((2,)), pltpu.SemaphoreType.DMA(())],
        compiler_params=_cparams("arbitrary"),
        name="dispatch",
    )(gtab, gtab, ttab, lpos, x1)


def _expert_kernel(bexp_ref, nused_ref, xs_ref, w1_ref, w3_ref, w2_ref, y_ref):
    del bexp_ref
    i = pl.program_id(0)

    @pl.when(i < nused_ref[0])
    def _():
        xb = xs_ref[...].astype(BF)
        h1 = jnp.dot(xb, w1_ref[0], preferred_element_type=F32)
        h3 = jnp.dot(xb, w3_ref[0], preferred_element_type=F32)
        hid = (_silu(h1) * h3).astype(BF)
        y_ref[...] = jnp.dot(hid, w2_ref[0], preferred_element_type=F32)

    @pl.when(i >= nused_ref[0])
    def _():
        y_ref[...] = jnp.zeros_like(y_ref)


def _expert(bexp, nused, xs, w1, w3, w2):
    cap = xs.shape[0]
    nb = cap // EXPERT_BLOCK
    last = lambda i, nu: jnp.minimum(i, nu[0] - 1)
    grid_spec = pltpu.PrefetchScalarGridSpec(
        num_scalar_prefetch=2,
        grid=(nb,),
        in_specs=[pl.BlockSpec((EXPERT_BLOCK, D_MODEL), lambda i, be, nu: (last(i, nu), 0)),
                  pl.BlockSpec((1, D_MODEL, MOE_FF), lambda i, be, nu: (be[last(i, nu)], 0, 0)),
                  pl.BlockSpec((1, D_MODEL, MOE_FF), lambda i, be, nu: (be[last(i, nu)], 0, 0)),
                  pl.BlockSpec((1, MOE_FF, D_MODEL), lambda i, be, nu: (be[last(i, nu)], 0, 0))],
        out_specs=pl.BlockSpec((EXPERT_BLOCK, D_MODEL), lambda i, be, nu: (i, 0)),
    )
    return pl.pallas_call(
        _expert_kernel,
        grid_spec=grid_spec,
        out_shape=jax.ShapeDtypeStruct((cap, D_MODEL), F32),
        compiler_params=_cparams("arbitrary"),
        name="expert",
    )(bexp, nused, xs, w1, w3, w2)


def _combine_kernel(tab_ref, ntab_ref, lpos_ref, x_ref, w_ref, lg_ref, lb_ref, yb_ref, o_ref, stage_ref, sems,
                    *, tc):
    i = pl.program_id(0)
    last = pl.num_programs(0) - 1
    slot = lax.rem(i, 2)

    def run_copy(sl, src, dst, rows=RUN_ALIGN):
        return pltpu.make_async_copy(yb_ref.at[pl.ds(src, rows), :],
                                     stage_ref.at[sl, pl.ds(dst, rows), :], sems.at[sl])

    @pl.when(i == 0)
    def _():
        stage_ref[...] = jnp.zeros_like(stage_ref)
        _run_pieces(tab_ref, lambda src, dst: run_copy(slot, src, dst).start())

    @pl.when(i < last)
    def _():
        _run_pieces(ntab_ref, lambda src, dst: run_copy(1 - slot, src, dst).start())

    _wait_pieces(tab_ref, lambda rows: run_copy(slot, 0, 0, rows))
    staged = stage_ref[slot].astype(BF)
    c = lax.broadcasted_iota(I32, (tc, staged.shape[0]), 1)
    y = [jnp.dot((c == lpos_ref[:, k:k + 1]).astype(BF), staged, preferred_element_type=F32)
         for k in range(MOE_TOPK)]
    ffn = w_ref[:, 0:1] * y[0] + w_ref[:, 1:2] * y[1]
    o_ref[...] = _layer_norm(DN_ALPHA * x_ref[...] + ffn, lg_ref[...], lb_ref[...])


def _combine(gtab, lpos_t, x1, wts_t, lg, lb, yb, *, tc):
    t = x1.shape[0]
    n_tiles = t // tc
    full = lambda a: pl.BlockSpec(a.shape, lambda i: (0,) * a.ndim)
    smem_tab = lambda imap: pl.BlockSpec((1, SUBLANES, LANES), imap, memory_space=pltpu.SMEM)
    return pl.pallas_call(
        functools.partial(_combine_kernel, tc=tc),
        grid=(n_tiles,),
        in_specs=[smem_tab(lambda i: (i, 0, 0)), smem_tab(lambda i: (jnp.minimum(i + 1, n_tiles - 1), 0, 0)),
                  pl.BlockSpec((tc, MOE_TOPK), lambda i: (i, 0)),
                  pl.BlockSpec((tc, D_MODEL), lambda i: (i, 0)),
                  pl.BlockSpec((tc, MOE_TOPK), lambda i: (i, 0)),
                  full(lg), full(lb),
                  pl.BlockSpec(memory_space=pl.ANY)],
        out_specs=pl.BlockSpec((tc, D_MODEL), lambda i: (i, 0)),
        out_shape=jax.ShapeDtypeStruct((t, D_MODEL), F32),
        scratch_shapes=[pltpu.VMEM((2, _stage_rows(tc), D_MODEL), F32), pltpu.SemaphoreType.DMA((2,))],
        compiler_params=_cparams("arbitrary"),
        name="combine",
    )(gtab, gtab, lpos_t, x1, wts_t, lg, lb, yb)


def _tile(n, want):
    while n % want:
        want //= 2
    return want


def _mixer(x2d, b, l, w_in, rel_bias, conv_w, a_log, dt_bias, norm_g, pool_w, pool_scale,
           w_br_a, w_br_b, w_br_c, gate_b, w_out, ln_g, ln_b, wr, rb):
    t = x2d.shape[0]
    c0 = 3 * A_WIDTH
    c1 = c0 + 3 * B_WIDTH
    c2 = c1 + 2 * B_HEADS
    c3 = c2 + B_WIDTH
    c4 = c3 + C_WIDTH
    wbf = w_in.astype(BF)
    wbd = jnp.pad(wbf[:, c1:c2], ((0, 0), (0, LANES - 2 * B_HEADS)))
    qkva, qkvb, bd, go, oc = _proj(
        x2d, wbf[:, :c0], wbf[:, c0:c1], wbd, wbf[:, c2:c3], wbf[:, c3:c4],
        pool_w.astype(BF), pool_scale.reshape(1, C_WIDTH), seq=l, tm=_tile(l, 1024))

    oa = _attn(qkva.reshape(b, l, 3 * A_WIDTH), _attn_bias_table(rel_bias)).reshape(t, A_WIDTH)

    lane_pad = lambda v: jnp.pad(v.reshape(1, B_HEADS), ((0, 0), (B_HEADS, LANES - 2 * B_HEADS)))
    ob = _gdn(qkvb.reshape(b, l, 3 * B_WIDTH), bd.reshape(b, l, LANES), go.reshape(b, l, B_WIDTH),
              conv_w, lane_pad(a_log), lane_pad(dt_bias), norm_g.reshape(1, B_HEAD_DIM)).reshape(t, B_WIDTH)

    wrh, wrl = _split_bf16(wr)
    return _merge(x2d, oa, ob, oc, wbf[:, c4:], gate_b, w_br_a.astype(BF), w_br_b.astype(BF),
                  w_br_c.astype(BF), w_out.astype(BF), ln_g.reshape(1, -1), ln_b.reshape(1, -1),
                  wrh, wrl, rb, tm=_tile(t, 1024))


def _moe(x1, lgt, w1, w3, w2, ln_g, ln_b):
    t = x1.shape[0]
    tile = _tile(t, MOE_TILE)
    max_rows = t * MOE_TOPK + (t // tile) * N_EXPERTS * (RUN_ALIGN - 1)
    n_blocks = -(-max_rows // EXPERT_BLOCK) + N_EXPERTS
    cap = n_blocks * EXPERT_BLOCK
    wts, lpos, tab, cnt = _route(lgt, tn=tile)
    gtab, ttab, bexp, nused = _layout(cnt, tab, n_blocks=n_blocks)
    xs = _dispatch(gtab, ttab, lpos, x1, td=tile, cap=cap)
    yb = _expert(bexp.reshape(n_blocks), nused[0, :1], xs, w1.astype(BF), w3.astype(BF), w2.astype(BF))
    return _combine(gtab, lpos.T, x1, wts.T, ln_g.reshape(1, -1), ln_b.reshape(1, -1), yb, tc=tile)


def kernel(x, w_in, attn_rel_bias, gdn_conv_w, gdn_a_log, gdn_dt_bias, gdn_norm_g, pool_w, pool_scale,
           w_branch_a, w_branch_b, w_branch_c, gate_b, w_out, ln1_g, ln1_b, router_group_w,
           router_group_b, router_expert_w, router_expert_b, moe_w1, moe_w3, moe_w2, ln2_g, ln2_b):
    b, l, d = x.shape
    x2d = x.reshape(b * l, d)
    for i in range(DEPTH):
        wr = jnp.concatenate([router_expert_w[i].T, router_group_w[i].T,
                              jnp.zeros((ROUTER_ROWS - N_EXPERTS - MOE_GROUPS, d), F32)], axis=0)
        rb = jnp.concatenate([router_expert_b[i], router_group_b[i],
                              jnp.zeros((ROUTER_ROWS - N_EXPERTS - MOE_GROUPS,), F32)]).reshape(ROUTER_ROWS, 1)
        x1, lgt = _mixer(x2d, b, l, w_in[i], attn_rel_bias[i], gdn_conv_w[i], gdn_a_log[i], gdn_dt_bias[i],
                         gdn_norm_g[i], pool_w[i], pool_scale[i], w_branch_a[i], w_branch_b[i],
                         w_branch_c[i], gate_b[i], w_out[i], ln1_g[i], ln1_b[i], wr, rb)
        x2d = _moe(x1, lgt, moe_w1[i], moe_w3[i], moe_w2[i], ln2_g[i], ln2_b[i])
    return x2d.reshape(b, l, d)
```

```python
import functools

import jax
import jax.numpy as jnp
from jax import lax
from jax.experimental import pallas as pl
from jax.experimental.pallas import tpu as pltpu

BF = jnp.bfloat16
F32 = jnp.float32
I32 = jnp.int32

D_MODEL = 1024
DEPTH = 2
CHUNK = 64

A_HEADS = 8
A_HEAD_DIM = 64
A_WIDTH = A_HEADS * A_HEAD_DIM
A_LEFT_CHUNKS = 8
A_REL_CLIP = 256

B_HEADS = 4
B_HEAD_DIM = 128
B_WIDTH = B_HEADS * B_HEAD_DIM
B_CONV = 4

C_WINDOWS = (2, 4, 8, 16)
C_GROUPS = 4
C_GROUP_DIM = 128
C_WIDTH = C_GROUPS * C_GROUP_DIM

N_BRANCH = 3
MOE_GROUPS = 4
MOE_PER_GROUP = 8
N_EXPERTS = MOE_GROUPS * MOE_PER_GROUP
MOE_TOPK = 2
MOE_FF = 512

DN_ALPHA = (2 * DEPTH) ** 0.25
LN_EPS = 1e-5
RMS_EPS = 1e-6

LANES = 128
SUBLANES = 8
VMEM_LIMIT_BYTES = 56 * 1024 * 1024

QBLOCK = 2 * CHUNK
KBAND = (A_LEFT_CHUNKS + 2) * CHUNK
NEG_BIG = -1e30

ROUTER_ROWS = 40
EXPERT_BLOCK = 512
MOE_TILE = 512
RUN_ALIGN = 8


def _cparams(*sem):
    return pltpu.CompilerParams(dimension_semantics=sem, vmem_limit_bytes=VMEM_LIMIT_BYTES)


def _mm(a, b):
    return jnp.dot(a.astype(BF), b.astype(BF), preferred_element_type=F32)


def _mm_nt(a, b):
    return lax.dot_general(a.astype(BF), b.astype(BF), (((1,), (1,)), ((), ())),
                           preferred_element_type=F32)


def _mm_tn(a, b):
    return lax.dot_general(a.astype(BF), b.astype(BF), (((0,), (0,)), ((), ())),
                           preferred_element_type=F32)


def _split_bf16(a):
    hi = a.astype(BF)
    lo = (a - hi.astype(F32)).astype(BF)
    return hi, lo


def _mm3(a, b):
    ah, al = _split_bf16(a)
    bh, bl = _split_bf16(b)
    return (jnp.dot(ah, bh, preferred_element_type=F32)
            + (jnp.dot(ah, bl, preferred_element_type=F32)
               + jnp.dot(al, bh, preferred_element_type=F32)))


def _sigmoid(x):
    return 1.0 / (1.0 + jnp.exp(-x))


def _silu(x):
    return x * _sigmoid(x)


def _layer_norm(z, g, b):
    mu = jnp.mean(z, axis=-1, keepdims=True)
    zc = z - mu
    var = jnp.mean(zc * zc, axis=-1, keepdims=True)
    return zc * lax.rsqrt(var + LN_EPS) * g + b


POOL_TAIL = 16


def _proj_kernel(x_ref, wa_ref, wb_ref, wbd_ref, wgo_ref, wc_ref, pw_ref, ps_ref,
                 qkva_ref, qkvb_ref, bd_ref, go_ref, oc_ref, tail_ref, *, tiles_per_seq, tm):
    it = pl.program_id(0) % tiles_per_seq
    xb = x_ref[...].astype(BF)
    qkva_ref[...] = jnp.dot(xb, wa_ref[...], preferred_element_type=F32).astype(BF)
    qkvb_ref[...] = jnp.dot(xb, wb_ref[...], preferred_element_type=F32).astype(BF)
    bd_ref[...] = jnp.dot(xb, wbd_ref[...], preferred_element_type=F32)
    go_ref[...] = jnp.dot(xb, wgo_ref[...], preferred_element_type=F32).astype(BF)
    u = jnp.dot(xb, wc_ref[...], preferred_element_type=F32)

    @pl.when(it == 0)
    def _():
        tail_ref[...] = jnp.zeros_like(tail_ref)

    ext = jnp.concatenate([tail_ref[...], u], axis=0)
    tail_ref[...] = u[tm - POOL_TAIL:, :]
    pos = (it * tm + 1 + lax.broadcasted_iota(I32, (tm, 1), 0)).astype(F32)
    s = ext
    for gi, win in enumerate(C_WINDOWS):
        if gi:
            s = s[:, C_GROUP_DIM:]
        s = s + pltpu.roll(s, win // 2, 0)
        lo_c, hi_c = gi * C_GROUP_DIM, (gi + 1) * C_GROUP_DIM
        cnt = jnp.minimum(pos, float(win))
        pooled = s[POOL_TAIL:, :C_GROUP_DIM] / cnt - u[:, lo_c:hi_c]
        mixed = jnp.dot(pooled.astype(BF), pw_ref[gi], preferred_element_type=F32)
        oc_ref[:, lo_c:hi_c] = (mixed * ps_ref[:, lo_c:hi_c]).astype(BF)


def _proj(x2d, wa, wb, wbd, wgo, wc, pw, ps, *, seq, tm):
    t = x2d.shape[0]
    full = lambda a: pl.BlockSpec(a.shape, lambda i: (0,) * a.ndim)
    row = lambda n: pl.BlockSpec((tm, n), lambda i: (i, 0))
    return pl.pallas_call(
        functools.partial(_proj_kernel, tiles_per_seq=seq // tm, tm=tm),
        grid=(t // tm,),
        in_specs=[row(D_MODEL), full(wa), full(wb), full(wbd), full(wgo), full(wc), full(pw), full(ps)],
        out_specs=[row(3 * A_WIDTH), row(3 * B_WIDTH), row(LANES), row(B_WIDTH), row(C_WIDTH)],
        out_shape=[jax.ShapeDtypeStruct((t, 3 * A_WIDTH), BF),
                   jax.ShapeDtypeStruct((t, 3 * B_WIDTH), BF),
                   jax.ShapeDtypeStruct((t, LANES), F32),
                   jax.ShapeDtypeStruct((t, B_WIDTH), BF),
                   jax.ShapeDtypeStruct((t, C_WIDTH), BF)],
        scratch_shapes=[pltpu.VMEM((POOL_TAIL, C_WIDTH), F32)],
        compiler_params=_cparams("arbitrary"),
        name="proj",
    )(x2d, wa, wb, wbd, wgo, wc, pw, ps)


def _attn_bias_table(rel_bias):
    h = rel_bias.shape[0]
    pad = A_LEFT_CHUNKS * CHUNK
    width = KBAND + pad
    n = width + QBLOCK - 1
    far = jnp.repeat(rel_bias[:, -1:], QBLOCK - 1 + pad - A_REL_CLIP, axis=1)
    near = jnp.repeat(rel_bias[:, :1], n - far.shape[1] - rel_bias.shape[1], axis=1)
    w = jnp.concatenate([far, rel_bias[:, ::-1], near], axis=1).astype(F32)
    w = jnp.roll(w, -(QBLOCK - 1), axis=1)
    toep = jnp.tile(w, (1, QBLOCK))[:, :QBLOCK * (n - 1)].reshape(h, QBLOCK, n - 1)[:, :, :width]
    r = jnp.arange(QBLOCK)[:, None]
    m = jnp.arange(width)[None, :]
    kc, qc = m // CHUNK, r // CHUNK
    valid = (kc >= qc) & (kc <= qc + A_LEFT_CHUNKS)
    tab = jnp.where(valid[None], toep, NEG_BIG)
    return tab.reshape(h, QBLOCK, width // LANES, LANES).transpose(0, 2, 1, 3)


ATTN_QBLOCKS_PER_STEP = 4


def _attn_kernel(q_ref, k_ref, v_ref, tab_ref, o_ref):
    first_blk = A_LEFT_CHUNKS * CHUNK // QBLOCK
    lane = lax.broadcasted_iota(I32, (QBLOCK, LANES), 1)
    lo = lane < A_HEAD_DIM
    nkb = KBAND // LANES
    items = []
    for sub in range(ATTN_QBLOCKS_PER_STEP):
        j = pl.program_id(1) * ATTN_QBLOCKS_PER_STEP + sub
        kstart = pl.multiple_of(jnp.maximum(j - first_blk, 0) * QBLOCK, QBLOCK)
        offb = jnp.maximum(first_blk - j, 0)
        for hp in range(A_HEADS // 2):
            items.append((slice(sub * QBLOCK, (sub + 1) * QBLOCK), kstart, offb, hp,
                          slice(hp * LANES, (hp + 1) * LANES)))
    scores = []
    for rows, kstart, offb, hp, cols in items:
        qp = q_ref[0, rows, cols] * jnp.asarray(A_HEAD_DIM ** -0.5, BF)
        zero = jnp.zeros_like(qp)
        q2 = jnp.concatenate([jnp.where(lo, qp, zero), jnp.where(lo, zero, qp)], axis=0)
        kp = k_ref[0, pl.ds(kstart, KBAND), cols]
        scores.append(lax.dot_general(q2, kp, (((1,), (1,)), ((), ())), preferred_element_type=F32))
    probs, denoms = [], []
    for (rows, kstart, offb, hp, cols), sc in zip(items, scores):
        bias = jnp.concatenate(
            [jnp.concatenate([tab_ref[2 * hp + par, offb + i] for i in range(nkb)], axis=1) for par in range(2)],
            axis=0)
        s = sc + bias
        p = jnp.exp(s - jnp.max(s, axis=-1, keepdims=True))
        denoms.append(jnp.sum(p, axis=-1, keepdims=True))
        probs.append(p.astype(BF))
    for (rows, kstart, offb, hp, cols), p, den in zip(items, probs, denoms):
        vp = v_ref[0, pl.ds(kstart, KBAND), cols]
        o2 = jnp.dot(p, vp, preferred_element_type=F32) / den
        o_ref[0, rows, cols] = jnp.where(lo, o2[:QBLOCK], o2[QBLOCK:]).astype(BF)


def _attn(qkva, tab):
    b, l, _ = qkva.shape
    step_rows = ATTN_QBLOCKS_PER_STEP * QBLOCK
    return pl.pallas_call(
        _attn_kernel,
        grid=(b, l // step_rows),
        in_specs=[pl.BlockSpec((1, step_rows, A_WIDTH), lambda bi, j: (bi, j, 0)),
                  pl.BlockSpec((1, l, A_WIDTH), lambda bi, j: (bi, 0, 1)),
                  pl.BlockSpec((1, l, A_WIDTH), lambda bi, j: (bi, 0, 2)),
                  pl.BlockSpec(tab.shape, lambda bi, j: (0, 0, 0, 0))],
        out_specs=pl.BlockSpec((1, step_rows, A_WIDTH), lambda bi, j: (bi, j, 0)),
        out_shape=jax.ShapeDtypeStruct((b, l, A_WIDTH), BF),
        compiler_params=_cparams("arbitrary", "arbitrary"),
        name="attn",
    )(qkva, qkva, qkva, tab)


CONV_HIST = 16
GDN_PREP_UNROLL = 4


def _gdn_kernel(qkv_ref, bd_ref, go_ref, cw_ref, alog_ref, dtb_ref, ng_ref, o_ref,
                u_ref, wq_ref, qkkd_ref, egl_ref, state_ref, *, n_chunks):
    rt = lax.broadcasted_iota(I32, (CHUNK, CHUNK), 0)
    ct = lax.broadcasted_iota(I32, (CHUNK, CHUNK), 1)
    tril = rt >= ct
    strict = rt > ct
    row_c = lax.broadcasted_iota(I32, (CHUNK, LANES), 0)
    eye_f = (rt == ct).astype(F32)
    eye_d = (lax.broadcasted_iota(I32, (B_HEAD_DIM, B_HEAD_DIM), 0)
             == lax.broadcasted_iota(I32, (B_HEAD_DIM, B_HEAD_DIM), 1)).astype(BF)
    neg_rate = -jnp.exp(alog_ref[...])
    sr = lax.broadcasted_iota(I32, (B_CONV * CHUNK, CONV_HIST + CHUNK), 0)
    sc = lax.broadcasted_iota(I32, (B_CONV * CHUNK, CONV_HIST + CHUNK), 1)
    shift_sel = (sc == CONV_HIST + jnp.bitwise_and(sr, CHUNK - 1)
                 - jnp.right_shift(sr, CHUNK.bit_length() - 1)).astype(BF)

    def heads(fn):
        return jnp.concatenate([fn(h, slice(h * B_HEAD_DIM, (h + 1) * B_HEAD_DIM)) for h in range(B_HEADS)],
                               axis=1)

    def prep(c):
        r0 = pl.multiple_of(c * CHUNK, CHUNK)
        cur = qkv_ref[0, pl.ds(r0, CHUNK), :]
        h0 = pl.multiple_of(jnp.maximum(r0 - CONV_HIST, 0), CONV_HIST)
        hist = qkv_ref[0, pl.ds(h0, CONV_HIST), :]
        hist = jnp.where(c > 0, hist, jnp.zeros_like(hist))
        taps = jnp.dot(shift_sel, jnp.concatenate([hist, cur], axis=0), preferred_element_type=F32)
        acc = taps[:CHUNK] * cw_ref[B_CONV - 1:B_CONV, :]
        for back in range(1, B_CONV):
            tap = B_CONV - 1 - back
            acc = acc + taps[back * CHUNK:(back + 1) * CHUNK] * cw_ref[tap:tap + 1, :]
        act = _silu(acc)
        q_all, k_all, v_all = act[:, :B_WIDTH], act[:, B_WIDTH:2 * B_WIDTH], act[:, 2 * B_WIDTH:]

        bd = bd_ref[0, pl.ds(r0, CHUNK), :]
        beta_all = _sigmoid(bd)
        z = bd + dtb_ref[...]
        softplus = jnp.maximum(z, 0.0) + jnp.log1p(jnp.exp(-jnp.abs(z)))
        g_all = neg_rate * softplus
        gc_all = g_all
        for sh in (1, 2, 4, 8, 16, 32):
            gc_all = gc_all + jnp.where(row_c >= sh, pltpu.roll(gc_all, sh, 0), 0.0)
        egc_all = jnp.exp(gc_all)
        glast = gc_all[CHUNK - 1:CHUNK, :]
        ekd_all = jnp.exp(glast - gc_all)
        egl = jnp.exp(glast)
        col = lambda m, h: m[:, B_HEADS + h:B_HEADS + h + 1]

        def l2n(m):
            return m * lax.rsqrt(jnp.sum(m * m, axis=-1, keepdims=True) + RMS_EPS)

        qn = heads(lambda h, hs: l2n(q_all[:, hs]) * (B_HEAD_DIM ** -0.5))
        kn = heads(lambda h, hs: l2n(k_all[:, hs]))
        kbeta = heads(lambda h, hs: kn[:, hs] * beta_all[:, h:h + 1])
        vbeta = heads(lambda h, hs: v_all[:, hs] * beta_all[:, h:h + 1])
        kbe = heads(lambda h, hs: kbeta[:, hs] * col(egc_all, h))
        qe = heads(lambda h, hs: qn[:, hs] * col(egc_all, h))
        chains = []
        for h in range(B_HEADS):
            hs = slice(h * B_HEAD_DIM, (h + 1) * B_HEAD_DIM)
            gcc = jnp.broadcast_to(col(gc_all, h), (CHUNK, CHUNK))
            gcr = jnp.sum(gcc * eye_f, axis=0, keepdims=True)
            chains.append(dict(
                c=c, h=h, hs=hs,
                decay=jnp.exp(jnp.where(tril, gcc - gcr, -jnp.inf)),
                kq=jnp.concatenate([kbeta[:, hs], qn[:, hs]], axis=0).astype(BF),
                kn=kn[:, hs].astype(BF),
                rhs=jnp.concatenate([vbeta[:, hs], kbe[:, hs]], axis=1).astype(BF),
                kd=kn[:, hs] * col(ekd_all, h)))
        egl_row = heads(lambda h, hs: jnp.broadcast_to(col(egl, h), (1, B_HEAD_DIM)))
        return chains, qe.astype(BF), egl_row

    nt = lambda a, b: lax.dot_general(a, b, (((1,), (1,)), ((), ())), preferred_element_type=F32)
    mm = lambda a, b: jnp.dot(a, b, preferred_element_type=F32)

    def prep_levels(i):
        fronts = [prep(i * GDN_PREP_UNROLL + g) for g in range(GDN_PREP_UNROLL)]
        ch = [x for f in fronts for x in f[0]]
        aq = [nt(x["kq"], x["kn"]) for x in ch]
        kdt = [x["kd"].T for x in ch]
        yield
        qk = [(a[CHUNK:] * x["decay"]).astype(BF) for x, a in zip(ch, aq)]
        pw = [-jnp.where(strict, a[:CHUNK] * x["decay"], 0.0) for x, a in zip(ch, aq)]
        tmat = [eye_f + p for p in pw]
        pwb = [p.astype(BF) for p in pw]
        pw = [mm(p, p) for p in pwb]
        yield
        n_sq = CHUNK.bit_length() - 2
        for lvl in range(n_sq):
            pwb = [p.astype(BF) for p in pw]
            if lvl + 1 < n_sq:
                both = [mm(jnp.concatenate([t.astype(BF), p], axis=0), p) for t, p in zip(tmat, pwb)]
                yield
                tmat = [t + r[:CHUNK] for t, r in zip(tmat, both)]
                pw = [r[CHUNK:] for r in both]
            else:
                last = [mm(t.astype(BF), p) for t, p in zip(tmat, pwb)]
                yield
                tmat = [t + r for t, r in zip(tmat, last)]
        uw = [mm(t.astype(BF), x["rhs"]) for t, x in zip(tmat, ch)]
        yield
        for x, r, qkx, kt in zip(ch, uw, qk, kdt):
            u_ref[x["c"], :, x["hs"]] = r[:, :B_HEAD_DIM]
            wq_ref[x["c"], :CHUNK, x["hs"]] = r[:, B_HEAD_DIM:].astype(BF)
            qkkd_ref[x["c"], x["h"], :CHUNK, :] = qkx
            qkkd_ref[x["c"], x["h"], CHUNK:, :] = kt.astype(BF)
        for g, (_, qe_b, egl_row) in enumerate(fronts):
            wq_ref[i * GDN_PREP_UNROLL + g, CHUNK:, :] = qe_b
            egl_ref[i * GDN_PREP_UNROLL + g] = egl_row

    def step_levels(c):
        r0 = pl.multiple_of(c * CHUNK, CHUNK)
        st = state_ref[...]
        stb = st.astype(BF)
        hss = [slice(h * B_HEAD_DIM, (h + 1) * B_HEAD_DIM) for h in range(B_HEADS)]
        r1 = [mm(wq_ref[c, :, hs], stb[:, hs]) for hs in hss]
        yield
        v_new = [(u_ref[c, :, hs] - r[:CHUNK]).astype(BF) for hs, r in zip(hss, r1)]
        r2 = [mm(qkkd_ref[c, h], v_new[h]) for h in range(B_HEADS)]
        yield
        o_all = jnp.concatenate([a[CHUNK:] + b[:CHUNK] for a, b in zip(r1, r2)], axis=1)
        state_ref[...] = st * egl_ref[c] + jnp.concatenate([b[CHUNK:] for b in r2], axis=1)
        gate = go_ref[0, pl.ds(r0, CHUNK), :].astype(F32)
        o_n = heads(lambda h, hs: o_all[:, hs]
                    * lax.rsqrt(jnp.mean(o_all[:, hs] * o_all[:, hs], axis=-1, keepdims=True) + RMS_EPS)
                    * ng_ref[...])
        o_ref[0, pl.ds(r0, CHUNK), :] = (o_n * _silu(gate)).astype(BF)

    def steps_of(i):
        for g in range(GDN_PREP_UNROLL):
            yield from step_levels(i * GDN_PREP_UNROLL + g)

    def run_interleaved(*gens):
        live = list(gens)
        while live:
            for g in list(live):
                if next(g, StopIteration) is StopIteration:
                    live.remove(g)

    n_groups = n_chunks // GDN_PREP_UNROLL
    state_ref[...] = jnp.zeros_like(state_ref)
    run_interleaved(prep_levels(0))

    def body(i, carry):
        run_interleaved(steps_of(i - 1), prep_levels(i))
        return carry

    lax.fori_loop(1, n_groups, body, 0)
    run_interleaved(steps_of(n_groups - 1))


def _gdn(qkvb, bd, go, cw, alog, dtb, ng):
    b, l, _ = qkvb.shape
    nc = l // CHUNK
    full = lambda a: pl.BlockSpec(a.shape, lambda bi: (0,) * a.ndim)
    seq = lambda n: pl.BlockSpec((1, l, n), lambda bi: (bi, 0, 0))
    return pl.pallas_call(
        functools.partial(_gdn_kernel, n_chunks=nc),
        grid=(b,),
        in_specs=[seq(3 * B_WIDTH), seq(LANES), seq(B_WIDTH), full(cw), full(alog), full(dtb), full(ng)],
        out_specs=seq(B_WIDTH),
        out_shape=jax.ShapeDtypeStruct((b, l, B_WIDTH), BF),
        scratch_shapes=[pltpu.VMEM((nc, CHUNK, B_WIDTH), F32),
                        pltpu.VMEM((nc, 2 * CHUNK, B_WIDTH), BF),
                        pltpu.VMEM((nc, B_HEADS, CHUNK + B_HEAD_DIM, CHUNK), BF),
                        pltpu.VMEM((nc, 1, B_WIDTH), F32),
                        pltpu.VMEM((B_HEAD_DIM, B_WIDTH), F32)],
        compiler_params=_cparams("arbitrary"),
        name="gdn",
    )(qkvb, bd, go, cw, alog, dtb, ng)


def _merge_kernel(x_ref, oa_ref, ob_ref, oc_ref, wg_ref, gb_ref, wa_ref, wb_ref, wc_ref, wo_ref,
                  lg_ref, lb_ref, wrh_ref, wrl_ref, rb_ref, x1_ref, lgt_ref):
    x = x_ref[...]
    xb = x.astype(BF)
    y = None
    for i, (o_ref, w_ref) in enumerate(((oa_ref, wa_ref), (ob_ref, wb_ref), (oc_ref, wc_ref))):
        gl = jnp.dot(xb, wg_ref[:, i * D_MODEL:(i + 1) * D_MODEL], preferred_element_type=F32)
        gate = _sigmoid(gl + gb_ref[i:i + 1, :])
        br = gate * jnp.dot(o_ref[...], w_ref[...], preferred_element_type=F32)
        y = br if y is None else y + br
    mix = jnp.dot(y.astype(BF), wo_ref[...], preferred_element_type=F32)
    x1 = _layer_norm(DN_ALPHA * x + mix, lg_ref[...], lb_ref[...])
    x1_ref[...] = x1
    xh, xl = _split_bf16(x1)
    nt = lambda a, b: lax.dot_general(a, b, (((1,), (1,)), ((), ())), preferred_element_type=F32)
    lgt_ref[...] = nt(wrh_ref[...], xh) + (nt(wrh_ref[...], xl) + nt(wrl_ref[...], xh)) + rb_ref[...]


def _merge(x2d, oa, ob, oc, wg, gb, wa, wb, wc, wo, lg, lb, wrh, wrl, rb, *, tm):
    t = x2d.shape[0]
    full = lambda a: pl.BlockSpec(a.shape, lambda i: (0,) * a.ndim)
    row = lambda n: pl.BlockSpec((tm, n), lambda i: (i, 0))
    return pl.pallas_call(
        _merge_kernel,
        grid=(t // tm,),
        in_specs=[row(D_MODEL), row(A_WIDTH), row(B_WIDTH), row(C_WIDTH), full(wg), full(gb), full(wa),
                  full(wb), full(wc), full(wo), full(lg), full(lb), full(wrh), full(wrl), full(rb)],
        out_specs=[row(D_MODEL), pl.BlockSpec((ROUTER_ROWS, tm), lambda i: (0, i))],
        out_shape=[jax.ShapeDtypeStruct((t, D_MODEL), F32),
                   jax.ShapeDtypeStruct((ROUTER_ROWS, t), F32)],
        compiler_params=_cparams("arbitrary"),
        name="merge",
    )(x2d, oa, ob, oc, wg, gb, wa, wb, wc, wo, lg, lb, wrh, wrl, rb)


def _route_kernel(lgt_ref, wts_ref, lpos_ref, tab_ref, cnt_ref, run_ref, *, tn):
    @pl.when(pl.program_id(0) == 0)
    def _():
        run_ref[...] = jnp.zeros_like(run_ref)

    le = lgt_ref[0:N_EXPERTS, :]
    lg = lgt_ref[N_EXPERTS:N_EXPERTS + MOE_GROUPS, :]
    gi = lax.broadcasted_iota(I32, (MOE_GROUPS, tn), 0).astype(F32)
    ei = lax.broadcasted_iota(I32, (N_EXPERTS, tn), 0).astype(F32)
    eg = jnp.right_shift(lax.broadcasted_iota(I32, (N_EXPERTS, tn), 0),
                         MOE_PER_GROUP.bit_length() - 1).astype(F32)
    mg = jnp.max(lg, axis=0, keepdims=True)
    gsel = jnp.min(jnp.where(lg == mg, gi, float(MOE_GROUPS)), axis=0, keepdims=True)
    p_top = 1.0 / jnp.sum(jnp.exp(lg - mg), axis=0, keepdims=True)
    l1 = jnp.where(eg == gsel, le, -jnp.inf)
    m1 = jnp.max(l1, axis=0, keepdims=True)
    i1 = jnp.min(jnp.where(l1 == m1, ei, float(N_EXPERTS)), axis=0, keepdims=True)
    l2 = jnp.where(ei == i1, -jnp.inf, l1)
    m2 = jnp.max(l2, axis=0, keepdims=True)
    i2 = jnp.min(jnp.where(l2 == m2, ei, float(N_EXPERTS)), axis=0, keepdims=True)
    e2 = jnp.exp(m2 - m1)
    den = 1.0 + e2
    wts_ref[...] = jnp.concatenate([p_top / den, p_top * (e2 / den)], axis=0)

    before = (lax.broadcasted_iota(I32, (tn, tn), 0) < lax.broadcasted_iota(I32, (tn, tn), 1)).astype(BF)
    run = run_ref[...]
    oh = [ei == ik for ik in (i1, i2)]
    prefix = [jnp.dot(o.astype(BF), before, preferred_element_type=F32) for o in oh]
    cnt_k = [jnp.sum(o.astype(F32), axis=1, keepdims=True) for o in oh]
    cnt = cnt_k[0] + cnt_k[1]
    cnt_al = jnp.floor((cnt + (RUN_ALIGN - 1)) * (1.0 / RUN_ALIGN)) * RUN_ALIGN
    er = lax.broadcasted_iota(I32, (N_EXPERTS, N_EXPERTS), 0)
    ec = lax.broadcasted_iota(I32, (N_EXPERTS, N_EXPERTS), 1)
    lstart = jnp.dot((ec < er).astype(BF), jnp.broadcast_to(cnt_al, (N_EXPERTS, LANES)).astype(BF),
                     preferred_element_type=F32)[:, 0:1]
    first = [0.0, cnt_k[0]]
    pick = lambda k: jnp.sum(jnp.where(oh[k], prefix[k] + first[k] + lstart, 0.0), axis=0, keepdims=True)
    lpos_ref[...] = jnp.concatenate([pick(0), pick(1)], axis=0).astype(I32)
    eye_l = (lax.broadcasted_iota(I32, (N_EXPERTS, LANES), 0) == lax.broadcasted_iota(I32, (N_EXPERTS, LANES), 1))
    as_row = lambda colv: jnp.sum(jnp.where(eye_l, colv, 0.0), axis=0, keepdims=True)
    pieces = jnp.broadcast_to(jnp.sum(cnt_al, axis=0, keepdims=True) * (1.0 / RUN_ALIGN), (1, LANES))
    tab_ref[0] = jnp.concatenate([as_row(run), as_row(cnt), as_row(lstart), pieces,
                                  jnp.zeros((SUBLANES - 4, LANES), F32)], axis=0).astype(I32)
    run = run + cnt_al
    run_ref[...] = run
    cnt_ref[...] = jnp.broadcast_to(run, cnt_ref.shape)


def _route(lgt, *, tn):
    t = lgt.shape[1]
    tok = lambda: pl.BlockSpec((MOE_TOPK, tn), lambda i: (0, i))
    return pl.pallas_call(
        functools.partial(_route_kernel, tn=tn),
        grid=(t // tn,),
        in_specs=[pl.BlockSpec((ROUTER_ROWS, tn), lambda i: (0, i))],
        out_specs=[tok(), tok(), pl.BlockSpec((1, SUBLANES, LANES), lambda i: (i, 0, 0)),
                   pl.BlockSpec((N_EXPERTS, LANES), lambda i: (0, 0))],
        out_shape=[jax.ShapeDtypeStruct((MOE_TOPK, t), F32),
                   jax.ShapeDtypeStruct((MOE_TOPK, t), I32),
                   jax.ShapeDtypeStruct((t // tn, SUBLANES, LANES), I32),
                   jax.ShapeDtypeStruct((N_EXPERTS, LANES), F32)],
        scratch_shapes=[pltpu.VMEM((N_EXPERTS, 1), F32)],
        compiler_params=_cparams("arbitrary"),
        name="route",
    )(lgt)


def _layout_kernel(cnt_ref, tab_ref, gtab_ref, ttab_ref, bexp_ref, nused_ref, *, n_blocks):
    cnt = cnt_ref[...].astype(I32)
    shift = EXPERT_BLOCK.bit_length() - 1
    padded_blocks = jnp.right_shift(cnt + (EXPERT_BLOCK - 1), shift)
    er = lax.broadcasted_iota(I32, (N_EXPERTS, LANES), 0)
    el = lax.broadcasted_iota(I32, (N_EXPERTS, LANES), 1)
    start = jnp.zeros((N_EXPERTS, LANES), I32)
    for e in range(N_EXPERTS - 1):
        start = start + jnp.where(er > e, padded_blocks[e:e + 1, :], 0)
    end = start + padded_blocks
    as_row = lambda m: jnp.sum(jnp.where(er == el, m, 0), axis=0, keepdims=True)
    tab = tab_ref[...]
    first_row = lax.broadcasted_iota(I32, tab.shape, 1) == 0
    gtab_ref[...] = tab + jnp.where(first_row, as_row(start * EXPERT_BLOCK)[None], 0)
    tail_rows = padded_blocks * EXPERT_BLOCK - cnt
    ttab_ref[...] = jnp.concatenate(
        [as_row(start * EXPERT_BLOCK + cnt), as_row(jnp.right_shift(tail_rows, RUN_ALIGN.bit_length() - 1)),
         jnp.zeros((SUBLANES - 2, LANES), I32)], axis=0)[None]
    blk = lax.broadcasted_iota(I32, (1, n_blocks), 1)
    owner = jnp.zeros((1, n_blocks), I32)
    for e in range(N_EXPERTS):
        owner = owner + (blk >= end[e:e + 1, 0:1]).astype(I32)
    bexp_ref[...] = jnp.minimum(owner, N_EXPERTS - 1)
    nused_ref[...] = end[N_EXPERTS - 1:N_EXPERTS, :]


def _layout(cnt, tab, *, n_blocks):
    full = lambda shape: pl.BlockSpec(shape, lambda i: (0,) * len(shape))
    return pl.pallas_call(
        functools.partial(_layout_kernel, n_blocks=n_blocks),
        grid=(1,),
        in_specs=[full(cnt.shape), full(tab.shape)],
        out_specs=[full(tab.shape), full((1, SUBLANES, LANES)), full((1, n_blocks)), full((1, LANES))],
        out_shape=[jax.ShapeDtypeStruct(tab.shape, I32),
                   jax.ShapeDtypeStruct((1, SUBLANES, LANES), I32),
                   jax.ShapeDtypeStruct((1, n_blocks), I32),
                   jax.ShapeDtypeStruct((1, LANES), I32)],
        compiler_params=_cparams("arbitrary"),
        name="layout",
    )(cnt, tab)


def _run_pieces(tab_ref, fn):
    def per_expert(e, carry):
        sorted0, staging0 = tab_ref[0, 0, e], tab_ref[0, 2, e]
        n_pieces = (tab_ref[0, 1, e] + (RUN_ALIGN - 1)) // RUN_ALIGN

        def one(j, c):
            fn(pl.multiple_of(sorted0 + j * RUN_ALIGN, RUN_ALIGN), pl.multiple_of(staging0 + j * RUN_ALIGN, RUN_ALIGN))
            return c

        lax.fori_loop(0, n_pieces, one, 0)
        return carry
    lax.fori_loop(0, N_EXPERTS, per_expert, 0)


WAIT_GROUPS = (64, 8, 1)


def _wait_pieces(tab_ref, make_copy):
    total = tab_ref[0, 3, 0]
    counts = (total // WAIT_GROUPS[0], (total // WAIT_GROUPS[1]) % (WAIT_GROUPS[0] // WAIT_GROUPS[1]),
              total % WAIT_GROUPS[1])
    for group, count in zip(WAIT_GROUPS, counts):
        def one(j, c, group=group):
            make_copy(group * RUN_ALIGN).wait()
            return c
        lax.fori_loop(0, count, one, 0)


def _dispatch_kernel(tab_ref, ptab_ref, ttab_ref, lpos_ref, x_ref, xs_ref, stage_ref, zero_ref, sems, tsem,
                     *, td):
    i = pl.program_id(0)
    last = pl.num_programs(0) - 1
    slot = lax.rem(i, 2)
    rows = stage_ref.shape[1]
    r = lax.broadcasted_iota(I32, (rows, td), 0)
    sel = jnp.logical_or(r == lpos_ref[0:1, :], r == lpos_ref[1:2, :]).astype(BF)
    stage_ref[slot] = jnp.dot(sel, x_ref[...].astype(BF), preferred_element_type=F32)

    def run_copy(sl, dst, src, rows=RUN_ALIGN):
        return pltpu.make_async_copy(stage_ref.at[sl, pl.ds(src, rows), :],
                                     xs_ref.at[pl.ds(dst, rows), :], sems.at[sl])

    def tail_copies(fn):
        def per_expert(e, carry):
            def one(j, c):
                dst = pl.multiple_of(ttab_ref[0, 0, e] + j * RUN_ALIGN, RUN_ALIGN)
                fn(pltpu.make_async_copy(zero_ref, xs_ref.at[pl.ds(dst, RUN_ALIGN), :], tsem))
                return c
            lax.fori_loop(0, ttab_ref[0, 1, e], one, 0)
            return carry
        lax.fori_loop(0, N_EXPERTS, per_expert, 0)

    @pl.when(i == 0)
    def _():
        zero_ref[...] = jnp.zeros_like(zero_ref)
        tail_copies(lambda cp: cp.start())

    @pl.when(i > 0)
    def _():
        _wait_pieces(ptab_ref, lambda rows: run_copy(1 - slot, 0, 0, rows))

    _run_pieces(tab_ref, lambda dst, src: run_copy(slot, dst, src).start())

    @pl.when(i == last)
    def _():
        _wait_pieces(tab_ref, lambda rows: run_copy(slot, 0, 0, rows))
        tail_copies(lambda cp: cp.wait())


def _stage_rows(td):
    return MOE_TOPK * td + N_EXPERTS * RUN_ALIGN


def _dispatch(gtab, ttab, lpos, x1, *, td, cap):
    t = x1.shape[0]
    smem_tab = lambda imap: pl.BlockSpec((1, SUBLANES, LANES), imap, memory_space=pltpu.SMEM)
    return pl.pallas_call(
        functools.partial(_dispatch_kernel, td=td),
        grid=(t // td,),
        in_specs=[smem_tab(lambda i: (i, 0, 0)), smem_tab(lambda i: (jnp.maximum(i - 1, 0), 0, 0)),
                  smem_tab(lambda i: (0, 0, 0)),
                  pl.BlockSpec((MOE_TOPK, td), lambda i: (0, i)),
                  pl.BlockSpec((td, D_MODEL), lambda i: (i, 0))],
        out_specs=pl.BlockSpec(memory_space=pl.ANY),
        out_shape=jax.ShapeDtypeStruct((cap, D_MODEL), F32),
        scratch_shapes=[pltpu.VMEM((2, _stage_rows(td), D_MODEL), F32), pltpu.VMEM((RUN_ALIGN, D_MODEL), F32),
                        pltpu.SemaphoreType.DMA((2,)), pltpu.SemaphoreType.DMA(())],
        compiler_params=_cparams("arbitrary"),
        name="dispatch",
    )(gtab, gtab, ttab, lpos, x1)


def _expert_kernel(bexp_ref, nused_ref, xs_ref, w1_ref, w3_ref, w2_ref, y_ref):
    del bexp_ref
    i = pl.program_id(0)

    @pl.when(i < nused_ref[0])
    def _():
        xb = xs_ref[...].astype(BF)
        h1 = jnp.dot(xb, w1_ref[0], preferred_element_type=F32)
        h3 = jnp.dot(xb, w3_ref[0], preferred_element_type=F32)
        hid = (_silu(h1) * h3).astype(BF)
        y_ref[...] = jnp.dot(hid, w2_ref[0], preferred_element_type=F32)

    @pl.when(i >= nused_ref[0])
    def _():
        y_ref[...] = jnp.zeros_like(y_ref)


def _expert(bexp, nused, xs, w1, w3, w2):
    cap = xs.shape[0]
    nb = cap // EXPERT_BLOCK
    last = lambda i, nu: jnp.minimum(i, nu[0] - 1)
    grid_spec = pltpu.PrefetchScalarGridSpec(
        num_scalar_prefetch=2,
        grid=(nb,),
        in_specs=[pl.BlockSpec((EXPERT_BLOCK, D_MODEL), lambda i, be, nu: (last(i, nu), 0)),
                  pl.BlockSpec((1, D_MODEL, MOE_FF), lambda i, be, nu: (be[last(i, nu)], 0, 0)),
                  pl.BlockSpec((1, D_MODEL, MOE_FF), lambda i, be, nu: (be[last(i, nu)], 0, 0)),
                  pl.BlockSpec((1, MOE_FF, D_MODEL), lambda i, be, nu: (be[last(i, nu)], 0, 0))],
        out_specs=pl.BlockSpec((EXPERT_BLOCK, D_MODEL), lambda i, be, nu: (i, 0)),
    )
    return pl.pallas_call(
        _expert_kernel,
        grid_spec=grid_spec,
        out_shape=jax.ShapeDtypeStruct((cap, D_MODEL), F32),
        compiler_params=_cparams("arbitrary"),
        name="expert",
    )(bexp, nused, xs, w1, w3, w2)


def _combine_kernel(tab_ref, ntab_ref, lpos_ref, x_ref, w_ref, lg_ref, lb_ref, yb_ref, o_ref, stage_ref, sems,
                    *, tc):
    i = pl.program_id(0)
    last = pl.num_programs(0) - 1
    slot = lax.rem(i, 2)

    def run_copy(sl, src, dst, rows=RUN_ALIGN):
        return pltpu.make_async_copy(yb_ref.at[pl.ds(src, rows), :],
                                     stage_ref.at[sl, pl.ds(dst, rows), :], sems.at[sl])

    @pl.when(i == 0)
    def _():
        stage_ref[...] = jnp.zeros_like(stage_ref)
        _run_pieces(tab_ref, lambda src, dst: run_copy(slot, src, dst).start())

    @pl.when(i < last)
    def _():
        _run_pieces(ntab_ref, lambda src, dst: run_copy(1 - slot, src, dst).start())

    _wait_pieces(tab_ref, lambda rows: run_copy(slot, 0, 0, rows))
    staged = stage_ref[slot].astype(BF)
    c = lax.broadcasted_iota(I32, (tc, staged.shape[0]), 1)
    y = [jnp.dot((c == lpos_ref[:, k:k + 1]).astype(BF), staged, preferred_element_type=F32)
         for k in range(MOE_TOPK)]
    ffn = w_ref[:, 0:1] * y[0] + w_ref[:, 1:2] * y[1]
    o_ref[...] = _layer_norm(DN_ALPHA * x_ref[...] + ffn, lg_ref[...], lb_ref[...])


def _combine(gtab, lpos_t, x1, wts_t, lg, lb, yb, *, tc):
    t = x1.shape[0]
    n_tiles = t // tc
    full = lambda a: pl.BlockSpec(a.shape, lambda i: (0,) * a.ndim)
    smem_tab = lambda imap: pl.BlockSpec((1, SUBLANES, LANES), imap, memory_space=pltpu.SMEM)
    return pl.pallas_call(
        functools.partial(_combine_kernel, tc=tc),
        grid=(n_tiles,),
        in_specs=[smem_tab(lambda i: (i, 0, 0)), smem_tab(lambda i: (jnp.minimum(i + 1, n_tiles - 1), 0, 0)),
                  pl.BlockSpec((tc, MOE_TOPK), lambda i: (i, 0)),
                  pl.BlockSpec((tc, D_MODEL), lambda i: (i, 0)),
                  pl.BlockSpec((tc, MOE_TOPK), lambda i: (i, 0)),
                  full(lg), full(lb),
                  pl.BlockSpec(memory_space=pl.ANY)],
        out_specs=pl.BlockSpec((tc, D_MODEL), lambda i: (i, 0)),
        out_shape=jax.ShapeDtypeStruct((t, D_MODEL), F32),
        scratch_shapes=[pltpu.VMEM((2, _stage_rows(tc), D_MODEL), F32), pltpu.SemaphoreType.DMA((2,))],
        compiler_params=_cparams("arbitrary"),
        name="combine",
    )(gtab, gtab, lpos_t, x1, wts_t, lg, lb, yb)


def _tile(n, want):
    while n % want:
        want //= 2
    return want


def _mixer(x2d, b, l, w_in, rel_bias, conv_w, a_log, dt_bias, norm_g, pool_w, pool_scale,
           w_br_a, w_br_b, w_br_c, gate_b, w_out, ln_g, ln_b, wr, rb):
    t = x2d.shape[0]
    c0 = 3 * A_WIDTH
    c1 = c0 + 3 * B_WIDTH
    c2 = c1 + 2 * B_HEADS
    c3 = c2 + B_WIDTH
    c4 = c3 + C_WIDTH
    wbf = w_in.astype(BF)
    wbd = jnp.pad(wbf[:, c1:c2], ((0, 0), (0, LANES - 2 * B_HEADS)))
    qkva, qkvb, bd, go, oc = _proj(
        x2d, wbf[:, :c0], wbf[:, c0:c1], wbd, wbf[:, c2:c3], wbf[:, c3:c4],
        pool_w.astype(BF), pool_scale.reshape(1, C_WIDTH), seq=l, tm=_tile(l, 1024))

    oa = _attn(qkva.reshape(b, l, 3 * A_WIDTH), _attn_bias_table(rel_bias)).reshape(t, A_WIDTH)

    lane_pad = lambda v: jnp.pad(v.reshape(1, B_HEADS), ((0, 0), (B_HEADS, LANES - 2 * B_HEADS)))
    ob = _gdn(qkvb.reshape(b, l, 3 * B_WIDTH), bd.reshape(b, l, LANES), go.reshape(b, l, B_WIDTH),
              conv_w, lane_pad(a_log), lane_pad(dt_bias), norm_g.reshape(1, B_HEAD_DIM)).reshape(t, B_WIDTH)

    wrh, wrl = _split_bf16(wr)
    return _merge(x2d, oa, ob, oc, wbf[:, c4:], gate_b, w_br_a.astype(BF), w_br_b.astype(BF),
                  w_br_c.astype(BF), w_out.astype(BF), ln_g.reshape(1, -1), ln_b.reshape(1, -1),
                  wrh, wrl, rb, tm=_tile(t, 1024))


def _moe(x1, lgt, w1, w3, w2, ln_g, ln_b):
    t = x1.shape[0]
    tile = _tile(t, MOE_TILE)
    max_rows = t * MOE_TOPK + (t // tile) * N_EXPERTS * (RUN_ALIGN - 1)
    n_blocks = -(-max_rows // EXPERT_BLOCK) + N_EXPERTS
    cap = n_blocks * EXPERT_BLOCK
    wts, lpos, tab, cnt = _route(lgt, tn=tile)
    gtab, ttab, bexp, nused = _layout(cnt, tab, n_blocks=n_blocks)
    xs = _dispatch(gtab, ttab, lpos, x1, td=tile, cap=cap)
    yb = _expert(bexp.reshape(n_blocks), nused[0, :1], xs, w1.astype(BF), w3.astype(BF), w2.astype(BF))
    return _combine(gtab, lpos.T, x1, wts.T, ln_g.reshape(1, -1), ln_b.reshape(1, -1), yb, tc=tile)


def kernel(x, w_in, attn_rel_bias, gdn_conv_w, gdn_a_log, gdn_dt_bias, gdn_norm_g, pool_w, pool_scale,
           w_branch_a, w_branch_b, w_branch_c, gate_b, w_out, ln1_g, ln1_b, router_group_w,
           router_group_b, router_expert_w, router_expert_b, moe_w1, moe_w3, moe_w2, ln2_g, ln2_b):
    b, l, d = x.shape
    x2d = x.reshape(b * l, d)
    for i in range(DEPTH):
        wr = jnp.concatenate([router_expert_w[i].T, router_group_w[i].T,
                              jnp.zeros((ROUTER_ROWS - N_EXPERTS - MOE_GROUPS, d), F32)], axis=0)
        rb = jnp.concatenate([router_expert_b[i], router_group_b[i],
                              jnp.zeros((ROUTER_ROWS - N_EXPERTS - MOE_GROUPS,), F32)]).reshape(ROUTER_ROWS, 1)
        x1, lgt = _mixer(x2d, b, l, w_in[i], attn_rel_bias[i], gdn_conv_w[i], gdn_a_log[i], gdn_dt_bias[i],
                         gdn_norm_g[i], pool_w[i], pool_scale[i], w_branch_a[i], w_branch_b[i],
                         w_branch_c[i], gate_b[i], w_out[i], ln1_g[i], ln1_b[i], wr, rb)
        x2d = _moe(x1, lgt, moe_w1[i], moe_w3[i], moe_w2[i], ln2_g[i], ln2_b[i])
    return x2d.reshape(b, l, d)
```

```python
import functools

import jax
import jax.numpy as jnp
from jax import lax
from jax.experimental import pallas as pl
from jax.experimental.pallas import tpu as pltpu

BF = jnp.bfloat16
F32 = jnp.float32
I32 = jnp.int32

D_MODEL = 1024
DEPTH = 2
CHUNK = 64

A_HEADS = 8
A_HEAD_DIM = 64
A_WIDTH = A_HEADS * A_HEAD_DIM
A_LEFT_CHUNKS = 8
A_REL_CLIP = 256

B_HEADS = 4
B_HEAD_DIM = 128
B_WIDTH = B_HEADS * B_HEAD_DIM
B_CONV = 4

C_WINDOWS = (2, 4, 8, 16)
C_GROUPS = 4
C_GROUP_DIM = 128
C_WIDTH = C_GROUPS * C_GROUP_DIM

N_BRANCH = 3
MOE_GROUPS = 4
MOE_PER_GROUP = 8
N_EXPERTS = MOE_GROUPS * MOE_PER_GROUP
MOE_TOPK = 2
MOE_FF = 512

DN_ALPHA = (2 * DEPTH) ** 0.25
LN_EPS = 1e-5
RMS_EPS = 1e-6

LANES = 128
SUBLANES = 8
VMEM_LIMIT_BYTES = 56 * 1024 * 1024

QBLOCK = 2 * CHUNK
KBAND = (A_LEFT_CHUNKS + 2) * CHUNK
NEG_BIG = -1e30

ROUTER_ROWS = 40
EXPERT_BLOCK = 512
MOE_TILE = 512
RUN_ALIGN = 8


def _cparams(*sem):
    return pltpu.CompilerParams(dimension_semantics=sem, vmem_limit_bytes=VMEM_LIMIT_BYTES)


def _mm(a, b):
    return jnp.dot(a.astype(BF), b.astype(BF), preferred_element_type=F32)


def _mm_nt(a, b):
    return lax.dot_general(a.astype(BF), b.astype(BF), (((1,), (1,)), ((), ())),
                           preferred_element_type=F32)


def _mm_tn(a, b):
    return lax.dot_general(a.astype(BF), b.astype(BF), (((0,), (0,)), ((), ())),
                           preferred_element_type=F32)


def _split_bf16(a):
    hi = a.astype(BF)
    lo = (a - hi.astype(F32)).astype(BF)
    return hi, lo


def _mm3(a, b):
    ah, al = _split_bf16(a)
    bh, bl = _split_bf16(b)
    return (jnp.dot(ah, bh, preferred_element_type=F32)
            + (jnp.dot(ah, bl, preferred_element_type=F32)
               + jnp.dot(al, bh, preferred_element_type=F32)))


def _sigmoid(x):
    return 1.0 / (1.0 + jnp.exp(-x))


def _silu(x):
    return x * _sigmoid(x)


def _layer_norm(z, g, b):
    mu = jnp.mean(z, axis=-1, keepdims=True)
    zc = z - mu
    var = jnp.mean(zc * zc, axis=-1, keepdims=True)
    return zc * lax.rsqrt(var + LN_EPS) * g + b


POOL_TAIL = 16


def _proj_kernel(x_ref, wa_ref, wb_ref, wbd_ref, wgo_ref, wc_ref, pw_ref, ps_ref,
                 qkva_ref, qkvb_ref, bd_ref, go_ref, oc_ref, tail_ref, *, tiles_per_seq, tm):
    it = pl.program_id(0) % tiles_per_seq
    xb = x_ref[...].astype(BF)
    qkva_ref[...] = jnp.dot(xb, wa_ref[...], preferred_element_type=F32).astype(BF)
    qkvb_ref[...] = jnp.dot(xb, wb_ref[...], preferred_element_type=F32).astype(BF)
    bd_ref[...] = jnp.dot(xb, wbd_ref[...], preferred_element_type=F32)
    go_ref[...] = jnp.dot(xb, wgo_ref[...], preferred_element_type=F32).astype(BF)
    u = jnp.dot(xb, wc_ref[...], preferred_element_type=F32)

    @pl.when(it == 0)
    def _():
        tail_ref[...] = jnp.zeros_like(tail_ref)

    ext = jnp.concatenate([tail_ref[...], u], axis=0)
    tail_ref[...] = u[tm - POOL_TAIL:, :]
    pos = (it * tm + 1 + lax.broadcasted_iota(I32, (tm, 1), 0)).astype(F32)
    s = ext
    for gi, win in enumerate(C_WINDOWS):
        if gi:
            s = s[:, C_GROUP_DIM:]
        s = s + pltpu.roll(s, win // 2, 0)
        lo_c, hi_c = gi * C_GROUP_DIM, (gi + 1) * C_GROUP_DIM
        cnt = jnp.minimum(pos, float(win))
        pooled = s[POOL_TAIL:, :C_GROUP_DIM] / cnt - u[:, lo_c:hi_c]
        mixed = jnp.dot(pooled.astype(BF), pw_ref[gi], preferred_element_type=F32)
        oc_ref[:, lo_c:hi_c] = (mixed * ps_ref[:, lo_c:hi_c]).astype(BF)


def _proj(x2d, wa, wb, wbd, wgo, wc, pw, ps, *, seq, tm):
    t = x2d.shape[0]
    full = lambda a: pl.BlockSpec(a.shape, lambda i: (0,) * a.ndim)
    row = lambda n: pl.BlockSpec((tm, n), lambda i: (i, 0))
    return pl.pallas_call(
        functools.partial(_proj_kernel, tiles_per_seq=seq // tm, tm=tm),
        grid=(t // tm,),
        in_specs=[row(D_MODEL), full(wa), full(wb), full(wbd), full(wgo), full(wc), full(pw), full(ps)],
        out_specs=[row(3 * A_WIDTH), row(3 * B_WIDTH), row(LANES), row(B_WIDTH), row(C_WIDTH)],
        out_shape=[jax.ShapeDtypeStruct((t, 3 * A_WIDTH), BF),
                   jax.ShapeDtypeStruct((t, 3 * B_WIDTH), BF),
                   jax.ShapeDtypeStruct((t, LANES), F32),
                   jax.ShapeDtypeStruct((t, B_WIDTH), BF),
                   jax.ShapeDtypeStruct((t, C_WIDTH), BF)],
        scratch_shapes=[pltpu.VMEM((POOL_TAIL, C_WIDTH), F32)],
        compiler_params=_cparams("arbitrary"),
        name="proj",
    )(x2d, wa, wb, wbd, wgo, wc, pw, ps)


def _attn_bias_table(rel_bias):
    h = rel_bias.shape[0]
    pad = A_LEFT_CHUNKS * CHUNK
    width = KBAND + pad
    n = width + QBLOCK - 1
    far = jnp.repeat(rel_bias[:, -1:], QBLOCK - 1 + pad - A_REL_CLIP, axis=1)
    near = jnp.repeat(rel_bias[:, :1], n - far.shape[1] - rel_bias.shape[1], axis=1)
    w = jnp.concatenate([far, rel_bias[:, ::-1], near], axis=1).astype(F32)
    w = jnp.roll(w, -(QBLOCK - 1), axis=1)
    toep = jnp.tile(w, (1, QBLOCK))[:, :QBLOCK * (n - 1)].reshape(h, QBLOCK, n - 1)[:, :, :width]
    r = jnp.arange(QBLOCK)[:, None]
    m = jnp.arange(width)[None, :]
    kc, qc = m // CHUNK, r // CHUNK
    valid = (kc >= qc) & (kc <= qc + A_LEFT_CHUNKS)
    tab = jnp.where(valid[None], toep, NEG_BIG)
    return tab.reshape(h, QBLOCK, width // LANES, LANES).transpose(0, 2, 1, 3)


ATTN_QBLOCKS_PER_STEP = 4


def _attn_kernel(q_ref, k_ref, v_ref, tab_ref, o_ref):
    first_blk = A_LEFT_CHUNKS * CHUNK // QBLOCK
    lane = lax.broadcasted_iota(I32, (QBLOCK, LANES), 1)
    lo = lane < A_HEAD_DIM
    nkb = KBAND // LANES
    items = []
    for sub in range(ATTN_QBLOCKS_PER_STEP):
        j = pl.program_id(1) * ATTN_QBLOCKS_PER_STEP + sub
        kstart = pl.multiple_of(jnp.maximum(j - first_blk, 0) * QBLOCK, QBLOCK)
        offb = jnp.maximum(first_blk - j, 0)
        for hp in range(A_HEADS // 2):
            items.append((slice(sub * QBLOCK, (sub + 1) * QBLOCK), kstart, offb, hp,
                          slice(hp * LANES, (hp + 1) * LANES)))
    scores = []
    for rows, kstart, offb, hp, cols in items:
        qp = q_ref[0, rows, cols] * jnp.asarray(A_HEAD_DIM ** -0.5, BF)
        zero = jnp.zeros_like(qp)
        q2 = jnp.concatenate([jnp.where(lo, qp, zero), jnp.where(lo, zero, qp)], axis=0)
        kp = k_ref[0, pl.ds(kstart, KBAND), cols]
        scores.append(lax.dot_general(q2, kp, (((1,), (1,)), ((), ())), preferred_element_type=F32))
    probs, denoms = [], []
    for (rows, kstart, offb, hp, cols), sc in zip(items, scores):
        bias = jnp.concatenate(
            [jnp.concatenate([tab_ref[2 * hp + par, offb + i] for i in range(nkb)], axis=1) for par in range(2)],
            axis=0)
        s = sc + bias
        p = jnp.exp(s - jnp.max(s, axis=-1, keepdims=True))
        denoms.append(jnp.sum(p, axis=-1, keepdims=True))
        probs.append(p.astype(BF))
    for (rows, kstart, offb, hp, cols), p, den in zip(items, probs, denoms):
        vp = v_ref[0, pl.ds(kstart, KBAND), cols]
        o2 = jnp.dot(p, vp, preferred_element_type=F32) / den
        o_ref[0, rows, cols] = jnp.where(lo, o2[:QBLOCK], o2[QBLOCK:]).astype(BF)


def _attn(qkva, tab):
    b, l, _ = qkva.shape
    step_rows = ATTN_QBLOCKS_PER_STEP * QBLOCK
    return pl.pallas_call(
        _attn_kernel,
        grid=(b, l // step_rows),
        in_specs=[pl.BlockSpec((1, step_rows, A_WIDTH), lambda bi, j: (bi, j, 0)),
                  pl.BlockSpec((1, l, A_WIDTH), lambda bi, j: (bi, 0, 1)),
                  pl.BlockSpec((1, l, A_WIDTH), lambda bi, j: (bi, 0, 2)),
                  pl.BlockSpec(tab.shape, lambda bi, j: (0, 0, 0, 0))],
        out_specs=pl.BlockSpec((1, step_rows, A_WIDTH), lambda bi, j: (bi, j, 0)),
        out_shape=jax.ShapeDtypeStruct((b, l, A_WIDTH), BF),
        compiler_params=_cparams("arbitrary", "arbitrary"),
        name="attn",
    )(qkva, qkva, qkva, tab)


CONV_HIST = 16
GDN_PREP_UNROLL = 4


def _gdn_kernel(qkv_ref, bd_ref, go_ref, cw_ref, alog_ref, dtb_ref, ng_ref, o_ref,
                u_ref, wq_ref, qkkd_ref, egl_ref, state_ref, *, n_chunks):
    rt = lax.broadcasted_iota(I32, (CHUNK, CHUNK), 0)
    ct = lax.broadcasted_iota(I32, (CHUNK, CHUNK), 1)
    tril = rt >= ct
    strict = rt > ct
    row_c = lax.broadcasted_iota(I32, (CHUNK, LANES), 0)
    eye_f = (rt == ct).astype(F32)
    eye_d = (lax.broadcasted_iota(I32, (B_HEAD_DIM, B_HEAD_DIM), 0)
             == lax.broadcasted_iota(I32, (B_HEAD_DIM, B_HEAD_DIM), 1)).astype(BF)
    neg_rate = -jnp.exp(alog_ref[...])
    sr = lax.broadcasted_iota(I32, (B_CONV * CHUNK, CONV_HIST + CHUNK), 0)
    sc = lax.broadcasted_iota(I32, (B_CONV * CHUNK, CONV_HIST + CHUNK), 1)
    shift_sel = (sc == CONV_HIST + jnp.bitwise_and(sr, CHUNK - 1)
                 - jnp.right_shift(sr, CHUNK.bit_length() - 1)).astype(BF)

    def heads(fn):
        return jnp.concatenate([fn(h, slice(h * B_HEAD_DIM, (h + 1) * B_HEAD_DIM)) for h in range(B_HEADS)],
                               axis=1)

    def prep(c):
        r0 = pl.multiple_of(c * CHUNK, CHUNK)
        cur = qkv_ref[0, pl.ds(r0, CHUNK), :]
        h0 = pl.multiple_of(jnp.maximum(r0 - CONV_HIST, 0), CONV_HIST)
        hist = qkv_ref[0, pl.ds(h0, CONV_HIST), :]
        hist = jnp.where(c > 0, hist, jnp.zeros_like(hist))
        taps = jnp.dot(shift_sel, jnp.concatenate([hist, cur], axis=0), preferred_element_type=F32)
        acc = taps[:CHUNK] * cw_ref[B_CONV - 1:B_CONV, :]
        for back in range(1, B_CONV):
            tap = B_CONV - 1 - back
            acc = acc + taps[back * CHUNK:(back + 1) * CHUNK] * cw_ref[tap:tap + 1, :]
        act = _silu(acc)
        q_all, k_all, v_all = act[:, :B_WIDTH], act[:, B_WIDTH:2 * B_WIDTH], act[:, 2 * B_WIDTH:]

        bd = bd_ref[0, pl.ds(r0, CHUNK), :]
        beta_all = _sigmoid(bd)
        z = bd + dtb_ref[...]
        softplus = jnp.maximum(z, 0.0) + jnp.log1p(jnp.exp(-jnp.abs(z)))
        g_all = neg_rate * softplus
        gc_all = g_all
        for sh in (1, 2, 4, 8, 16, 32):
            gc_all = gc_all + jnp.where(row_c >= sh, pltpu.roll(gc_all, sh, 0), 0.0)
        egc_all = jnp.exp(gc_all)
        glast = gc_all[CHUNK - 1:CHUNK, :]
        ekd_all = jnp.exp(glast - gc_all)
        egl = jnp.exp(glast)
        col = lambda m, h: m[:, B_HEADS + h:B_HEADS + h + 1]

        def l2n(m):
            return m * lax.rsqrt(jnp.sum(m * m, axis=-1, keepdims=True) + RMS_EPS)

        qn = heads(lambda h, hs: l2n(q_all[:, hs]) * (B_HEAD_DIM ** -0.5))
        kn = heads(lambda h, hs: l2n(k_all[:, hs]))
        kbeta = heads(lambda h, hs: kn[:, hs] * beta_all[:, h:h + 1])
        vbeta = heads(lambda h, hs: v_all[:, hs] * beta_all[:, h:h + 1])
        kbe = heads(lambda h, hs: kbeta[:, hs] * col(egc_all, h))
        qe = heads(lambda h, hs: qn[:, hs] * col(egc_all, h))
        chains = []
        for h in range(B_HEADS):
            hs = slice(h * B_HEAD_DIM, (h + 1) * B_HEAD_DIM)
            gcc = jnp.broadcast_to(col(gc_all, h), (CHUNK, CHUNK))
            gcr = jnp.sum(gcc * eye_f, axis=0, keepdims=True)
            chains.append(dict(
                c=c, h=h, hs=hs,
                decay=jnp.exp(jnp.where(tril, gcc - gcr, -jnp.inf)),
                kq=jnp.concatenate([kbeta[:, hs], qn[:, hs]], axis=0).astype(BF),
                kn=kn[:, hs].astype(BF),
                rhs=jnp.concatenate([vbeta[:, hs], kbe[:, hs]], axis=1).astype(BF),
                kd=kn[:, hs] * col(ekd_all, h)))
        egl_row = heads(lambda h, hs: jnp.broadcast_to(col(egl, h), (1, B_HEAD_DIM)))
        return chains, qe.astype(BF), egl_row

    nt = lambda a, b: lax.dot_general(a, b, (((1,), (1,)), ((), ())), preferred_element_type=F32)
    mm = lambda a, b: jnp.dot(a, b, preferred_element_type=F32)

    def prep_levels(i):
        fronts = [prep(i * GDN_PREP_UNROLL + g) for g in range(GDN_PREP_UNROLL)]
        ch = [x for f in fronts for x in f[0]]
        aq = [nt(x["kq"], x["kn"]) for x in ch]
        kdt = [x["kd"].T for x in ch]
        yield
        qk = [(a[CHUNK:] * x["decay"]).astype(BF) for x, a in zip(ch, aq)]
        pw = [-jnp.where(strict, a[:CHUNK] * x["decay"], 0.0) for x, a in zip(ch, aq)]
        tmat = [eye_f + p for p in pw]
        pwb = [p.astype(BF) for p in pw]
        pw = [mm(p, p) for p in pwb]
        yield
        n_sq = CHUNK.bit_length() - 2
        for lvl in range(n_sq):
            pwb = [p.astype(BF) for p in pw]
            if lvl + 1 < n_sq:
                both = [mm(jnp.concatenate([t.astype(BF), p], axis=0), p) for t, p in zip(tmat, pwb)]
                yield
                tmat = [t + r[:CHUNK] for t, r in zip(tmat, both)]
                pw = [r[CHUNK:] for r in both]
            else:
                last = [mm(t.astype(BF), p) for t, p in zip(tmat, pwb)]
                yield
                tmat = [t + r for t, r in zip(tmat, last)]
        uw = [mm(t.astype(BF), x["rhs"]) for t, x in zip(tmat, ch)]
        yield
        for x, r, qkx, kt in zip(ch, uw, qk, kdt):
            u_ref[x["c"], :, x["hs"]] = r[:, :B_HEAD_DIM]
            wq_ref[x["c"], :CHUNK, x["hs"]] = r[:, B_HEAD_DIM:].astype(BF)
            qkkd_ref[x["c"], x["h"], :CHUNK, :] = qkx
            qkkd_ref[x["c"], x["h"], CHUNK:, :] = kt.astype(BF)
        for g, (_, qe_b, egl_row) in enumerate(fronts):
            wq_ref[i * GDN_PREP_UNROLL + g, CHUNK:, :] = qe_b
            egl_ref[i * GDN_PREP_UNROLL + g] = egl_row

    def step_levels(c):
        r0 = pl.multiple_of(c * CHUNK, CHUNK)
        st = state_ref[...]
        stb = st.astype(BF)
        hss = [slice(h * B_HEAD_DIM, (h + 1) * B_HEAD_DIM) for h in range(B_HEADS)]
        r1 = [mm(wq_ref[c, :, hs], stb[:, hs]) for hs in hss]
        yield
        v_new = [(u_ref[c, :, hs] - r[:CHUNK]).astype(BF) for hs, r in zip(hss, r1)]
        r2 = [mm(qkkd_ref[c, h], v_new[h]) for h in range(B_HEADS)]
        yield
        o_all = jnp.concatenate([a[CHUNK:] + b[:CHUNK] for a, b in zip(r1, r2)], axis=1)
        state_ref[...] = st * egl_ref[c] + jnp.concatenate([b[CHUNK:] for b in r2], axis=1)
        gate = go_ref[0, pl.ds(r0, CHUNK), :].astype(F32)
        o_n = heads(lambda h, hs: o_all[:, hs]
                    * lax.rsqrt(jnp.mean(o_all[:, hs] * o_all[:, hs], axis=-1, keepdims=True) + RMS_EPS)
                    * ng_ref[...])
        o_ref[0, pl.ds(r0, CHUNK), :] = (o_n * _silu(gate)).astype(BF)

    def steps_of(i):
        for g in range(GDN_PREP_UNROLL):
            yield from step_levels(i * GDN_PREP_UNROLL + g)

    def run_interleaved(*gens):
        live = list(gens)
        while live:
            for g in list(live):
                if next(g, StopIteration) is StopIteration:
                    live.remove(g)

    n_groups = n_chunks // GDN_PREP_UNROLL
    state_ref[...] = jnp.zeros_like(state_ref)
    run_interleaved(prep_levels(0))

    def body(i, carry):
        run_interleaved(steps_of(i - 1), prep_levels(i))
        return carry

    lax.fori_loop(1, n_groups, body, 0)
    run_interleaved(steps_of(n_groups - 1))


def _gdn(qkvb, bd, go, cw, alog, dtb, ng):
    b, l, _ = qkvb.shape
    nc = l // CHUNK
    full = lambda a: pl.BlockSpec(a.shape, lambda bi: (0,) * a.ndim)
    seq = lambda n: pl.BlockSpec((1, l, n), lambda bi: (bi, 0, 0))
    return pl.pallas_call(
        functools.partial(_gdn_kernel, n_chunks=nc),
        grid=(b,),
        in_specs=[seq(3 * B_WIDTH), seq(LANES), seq(B_WIDTH), full(cw), full(alog), full(dtb), full(ng)],
        out_specs=seq(B_WIDTH),
        out_shape=jax.ShapeDtypeStruct((b, l, B_WIDTH), BF),
        scratch_shapes=[pltpu.VMEM((nc, CHUNK, B_WIDTH), F32),
                        pltpu.VMEM((nc, 2 * CHUNK, B_WIDTH), BF),
                        pltpu.VMEM((nc, B_HEADS, CHUNK + B_HEAD_DIM, CHUNK), BF),
                        pltpu.VMEM((nc, 1, B_WIDTH), F32),
                        pltpu.VMEM((B_HEAD_DIM, B_WIDTH), F32)],
        compiler_params=_cparams("arbitrary"),
        name="gdn",
    )(qkvb, bd, go, cw, alog, dtb, ng)


def _merge_kernel(x_ref, oa_ref, ob_ref, oc_ref, wg_ref, gb_ref, wa_ref, wb_ref, wc_ref, wo_ref,
                  lg_ref, lb_ref, wrh_ref, wrl_ref, rb_ref, x1_ref, lgt_ref):
    x = x_ref[...]
    xb = x.astype(BF)
    y = None
    for i, (o_ref, w_ref) in enumerate(((oa_ref, wa_ref), (ob_ref, wb_ref), (oc_ref, wc_ref))):
        gl = jnp.dot(xb, wg_ref[:, i * D_MODEL:(i + 1) * D_MODEL], preferred_element_type=F32)
        gate = _sigmoid(gl + gb_ref[i:i + 1, :])
        br = gate * jnp.dot(o_ref[...], w_ref[...], preferred_element_type=F32)
        y = br if y is None else y + br
    mix = jnp.dot(y.astype(BF), wo_ref[...], preferred_element_type=F32)
    x1 = _layer_norm(DN_ALPHA * x + mix, lg_ref[...], lb_ref[...])
    x1_ref[...] = x1
    xh, xl = _split_bf16(x1)
    nt = lambda a, b: lax.dot_general(a, b, (((1,), (1,)), ((), ())), preferred_element_type=F32)
    lgt_ref[...] = nt(wrh_ref[...], xh) + (nt(wrh_ref[...], xl) + nt(wrl_ref[...], xh)) + rb_ref[...]


def _merge(x2d, oa, ob, oc, wg, gb, wa, wb, wc, wo, lg, lb, wrh, wrl, rb, *, tm):
    t = x2d.shape[0]
    full = lambda a: pl.BlockSpec(a.shape, lambda i: (0,) * a.ndim)
    row = lambda n: pl.BlockSpec((tm, n), lambda i: (i, 0))
    return pl.pallas_call(
        _merge_kernel,
        grid=(t // tm,),
        in_specs=[row(D_MODEL), row(A_WIDTH), row(B_WIDTH), row(C_WIDTH), full(wg), full(gb), full(wa),
                  full(wb), full(wc), full(wo), full(lg), full(lb), full(wrh), full(wrl), full(rb)],
        out_specs=[row(D_MODEL), pl.BlockSpec((ROUTER_ROWS, tm), lambda i: (0, i))],
        out_shape=[jax.ShapeDtypeStruct((t, D_MODEL), F32),
                   jax.ShapeDtypeStruct((ROUTER_ROWS, t), F32)],
        compiler_params=_cparams("arbitrary"),
        name="merge",
    )(x2d, oa, ob, oc, wg, gb, wa, wb, wc, wo, lg, lb, wrh, wrl, rb)


def _route_kernel(lgt_ref, wts_ref, lpos_ref, tab_ref, cnt_ref, run_ref, *, tn):
    @pl.when(pl.program_id(0) == 0)
    def _():
        run_ref[...] = jnp.zeros_like(run_ref)

    le = lgt_ref[0:N_EXPERTS, :]
    lg = lgt_ref[N_EXPERTS:N_EXPERTS + MOE_GROUPS, :]
    gi = lax.broadcasted_iota(I32, (MOE_GROUPS, tn), 0).astype(F32)
    ei = lax.broadcasted_iota(I32, (N_EXPERTS, tn), 0).astype(F32)
    eg = jnp.right_shift(lax.broadcasted_iota(I32, (N_EXPERTS, tn), 0),
                         MOE_PER_GROUP.bit_length() - 1).astype(F32)
    mg = jnp.max(lg, axis=0, keepdims=True)
    gsel = jnp.min(jnp.where(lg == mg, gi, float(MOE_GROUPS)), axis=0, keepdims=True)
    p_top = 1.0 / jnp.sum(jnp.exp(lg - mg), axis=0, keepdims=True)
    l1 = jnp.where(eg == gsel, le, -jnp.inf)
    m1 = jnp.max(l1, axis=0, keepdims=True)
    i1 = jnp.min(jnp.where(l1 == m1, ei, float(N_EXPERTS)), axis=0, keepdims=True)
    l2 = jnp.where(ei == i1, -jnp.inf, l1)
    m2 = jnp.max(l2, axis=0, keepdims=True)
    i2 = jnp.min(jnp.where(l2 == m2, ei, float(N_EXPERTS)), axis=0, keepdims=True)
    e2 = jnp.exp(m2 - m1)
    den = 1.0 + e2
    wts_ref[...] = jnp.concatenate([p_top / den, p_top * (e2 / den)], axis=0)

    before = (lax.broadcasted_iota(I32, (tn, tn), 0) < lax.broadcasted_iota(I32, (tn, tn), 1)).astype(BF)
    run = run_ref[...]
    oh = [ei == ik for ik in (i1, i2)]
    prefix = [jnp.dot(o.astype(BF), before, preferred_element_type=F32) for o in oh]
    cnt_k = [jnp.sum(o.astype(F32), axis=1, keepdims=True) for o in oh]
    cnt = cnt_k[0] + cnt_k[1]
    cnt_al = jnp.floor((cnt + (RUN_ALIGN - 1)) * (1.0 / RUN_ALIGN)) * RUN_ALIGN
    er = lax.broadcasted_iota(I32, (N_EXPERTS, N_EXPERTS), 0)
    ec = lax.broadcasted_iota(I32, (N_EXPERTS, N_EXPERTS), 1)
    lstart = jnp.dot((ec < er).astype(BF), jnp.broadcast_to(cnt_al, (N_EXPERTS, LANES)).astype(BF),
                     preferred_element_type=F32)[:, 0:1]
    first = [0.0, cnt_k[0]]
    pick = lambda k: jnp.sum(jnp.where(oh[k], prefix[k] + first[k] + lstart, 0.0), axis=0, keepdims=True)
    lpos_ref[...] = jnp.concatenate([pick(0), pick(1)], axis=0).astype(I32)
    eye_l = (lax.broadcasted_iota(I32, (N_EXPERTS, LANES), 0) == lax.broadcasted_iota(I32, (N_EXPERTS, LANES), 1))
    as_row = lambda colv: jnp.sum(jnp.where(eye_l, colv, 0.0), axis=0, keepdims=True)
    pieces = jnp.broadcast_to(jnp.sum(cnt_al, axis=0, keepdims=True) * (1.0 / RUN_ALIGN), (1, LANES))
    tab_ref[0] = jnp.concatenate([as_row(run), as_row(cnt), as_row(lstart), pieces,
                                  jnp.zeros((SUBLANES - 4, LANES), F32)], axis=0).astype(I32)
    run = run + cnt_al
    run_ref[...] = run
    cnt_ref[...] = jnp.broadcast_to(run, cnt_ref.shape)


def _route(lgt, *, tn):
    t = lgt.shape[1]
    tok = lambda: pl.BlockSpec((MOE_TOPK, tn), lambda i: (0, i))
    return pl.pallas_call(
        functools.partial(_route_kernel, tn=tn),
        grid=(t // tn,),
        in_specs=[pl.BlockSpec((ROUTER_ROWS, tn), lambda i: (0, i))],
        out_specs=[tok(), tok(), pl.BlockSpec((1, SUBLANES, LANES), lambda i: (i, 0, 0)),
                   pl.BlockSpec((N_EXPERTS, LANES), lambda i: (0, 0))],
        out_shape=[jax.ShapeDtypeStruct((MOE_TOPK, t), F32),
                   jax.ShapeDtypeStruct((MOE_TOPK, t), I32),
                   jax.ShapeDtypeStruct((t // tn, SUBLANES, LANES), I32),
                   jax.ShapeDtypeStruct((N_EXPERTS, LANES), F32)],
        scratch_shapes=[pltpu.VMEM((N_EXPERTS, 1), F32)],
        compiler_params=_cparams("arbitrary"),
        name="route",
    )(lgt)


def _layout_kernel(cnt_ref, tab_ref, gtab_ref, ttab_ref, bexp_ref, nused_ref, *, n_blocks):
    cnt = cnt_ref[...].astype(I32)
    shift = EXPERT_BLOCK.bit_length() - 1
    padded_blocks = jnp.right_shift(cnt + (EXPERT_BLOCK - 1), shift)
    er = lax.broadcasted_iota(I32, (N_EXPERTS, LANES), 0)
    el = lax.broadcasted_iota(I32, (N_EXPERTS, LANES), 1)
    start = jnp.zeros((N_EXPERTS, LANES), I32)
    for e in range(N_EXPERTS - 1):
        start = start + jnp.where(er > e, padded_blocks[e:e + 1, :], 0)
    end = start + padded_blocks
    as_row = lambda m: jnp.sum(jnp.where(er == el, m, 0), axis=0, keepdims=True)
    tab = tab_ref[...]
    first_row = lax.broadcasted_iota(I32, tab.shape, 1) == 0
    gtab_ref[...] = tab + jnp.where(first_row, as_row(start * EXPERT_BLOCK)[None], 0)
    tail_rows = padded_blocks * EXPERT_BLOCK - cnt
    ttab_ref[...] = jnp.concatenate(
        [as_row(start * EXPERT_BLOCK + cnt), as_row(jnp.right_shift(tail_rows, RUN_ALIGN.bit_length() - 1)),
         jnp.zeros((SUBLANES - 2, LANES), I32)], axis=0)[None]
    blk = lax.broadcasted_iota(I32, (1, n_blocks), 1)
    owner = jnp.zeros((1, n_blocks), I32)
    for e in range(N_EXPERTS):
        owner = owner + (blk >= end[e:e + 1, 0:1]).astype(I32)
    bexp_ref[...] = jnp.minimum(owner, N_EXPERTS - 1)
    nused_ref[...] = end[N_EXPERTS - 1:N_EXPERTS, :]


def _layout(cnt, tab, *, n_blocks):
    full = lambda shape: pl.BlockSpec(shape, lambda i: (0,) * len(shape))
    return pl.pallas_call(
        functools.partial(_layout_kernel, n_blocks=n_blocks),
        grid=(1,),
        in_specs=[full(cnt.shape), full(tab.shape)],
        out_specs=[full(tab.shape), full((1, SUBLANES, LANES)), full((1, n_blocks)), full((1, LANES))],
        out_shape=[jax.ShapeDtypeStruct(tab.shape, I32),
                   jax.ShapeDtypeStruct((1, SUBLANES, LANES), I32),
                   jax.ShapeDtypeStruct((1, n_blocks), I32),
                   jax.ShapeDtypeStruct((1, LANES), I32)],
        compiler_params=_cparams("arbitrary"),
        name="layout",
    )(cnt, tab)


def _run_pieces(tab_ref, fn):
    for e in range(N_EXPERTS):
        sorted0, staging0 = tab_ref[0, 0, e], tab_ref[0, 2, e]
        n_pieces = (tab_ref[0, 1, e] + (RUN_ALIGN - 1)) // RUN_ALIGN

        def one(j, c, sorted0=sorted0, staging0=staging0):
            fn(pl.multiple_of(sorted0 + j * RUN_ALIGN, RUN_ALIGN), pl.multiple_of(staging0 + j * RUN_ALIGN, RUN_ALIGN))
            return c

        lax.fori_loop(0, n_pieces, one, 0)


WAIT_GROUPS = (64, 8, 1)


def _wait_pieces(tab_ref, make_copy):
    total = tab_ref[0, 3, 0]
    counts = (total // WAIT_GROUPS[0], (total // WAIT_GROUPS[1]) % (WAIT_GROUPS[0] // WAIT_GROUPS[1]),
              total % WAIT_GROUPS[1])
    for group, count in zip(WAIT_GROUPS, counts):
        def one(j, c, group=group):
            make_copy(group * RUN_ALIGN).wait()
            return c
        lax.fori_loop(0, count, one, 0)


def _dispatch_kernel(tab_ref, ptab_ref, ttab_ref, lpos_ref, x_ref, xs_ref, stage_ref, zero_ref, sems, tsem,
                     *, td):
    i = pl.program_id(0)
    last = pl.num_programs(0) - 1
    slot = lax.rem(i, 2)
    rows = stage_ref.shape[1]
    r = lax.broadcasted_iota(I32, (rows, td), 0)
    sel = jnp.logical_or(r == lpos_ref[0:1, :], r == lpos_ref[1:2, :]).astype(BF)
    stage_ref[slot] = jnp.dot(sel, x_ref[...].astype(BF), preferred_element_type=F32)

    def run_copy(sl, dst, src, rows=RUN_ALIGN):
        return pltpu.make_async_copy(stage_ref.at[sl, pl.ds(src, rows), :],
                                     xs_ref.at[pl.ds(dst, rows), :], sems.at[sl])

    def tail_copies(fn):
        def per_expert(e, carry):
            def one(j, c):
                dst = pl.multiple_of(ttab_ref[0, 0, e] + j * RUN_ALIGN, RUN_ALIGN)
                fn(pltpu.make_async_copy(zero_ref, xs_ref.at[pl.ds(dst, RUN_ALIGN), :], tsem))
                return c
            lax.fori_loop(0, ttab_ref[0, 1, e], one, 0)
            return carry
        lax.fori_loop(0, N_EXPERTS, per_expert, 0)

    @pl.when(i == 0)
    def _():
        zero_ref[...] = jnp.zeros_like(zero_ref)
        tail_copies(lambda cp: cp.start())

    @pl.when(i > 0)
    def _():
        _wait_pieces(ptab_ref, lambda rows: run_copy(1 - slot, 0, 0, rows))

    _run_pieces(tab_ref, lambda dst, src: run_copy(slot, dst, src).start())

    @pl.when(i == last)
    def _():
        _wait_pieces(tab_ref, lambda rows: run_copy(slot, 0, 0, rows))
        tail_copies(lambda cp: cp.wait())


def _stage_rows(td):
    return MOE_TOPK * td + N_EXPERTS * RUN_ALIGN


def _dispatch(gtab, ttab, lpos, x1, *, td, cap):
    t = x1.shape[0]
    smem_tab = lambda imap: pl.BlockSpec((1, SUBLANES, LANES), imap, memory_space=pltpu.SMEM)
    return pl.pallas_call(
        functools.partial(_dispatch_kernel, td=td),
        grid=(t // td,),
        in_specs=[smem_tab(lambda i: (i, 0, 0)), smem_tab(lambda i: (jnp.maximum(i - 1, 0), 0, 0)),
                  smem_tab(lambda i: (0, 0, 0)),
                  pl.BlockSpec((MOE_TOPK, td), lambda i: (0, i)),
                  pl.BlockSpec((td, D_MODEL), lambda i: (i, 0))],
        out_specs=pl.BlockSpec(memory_space=pl.ANY),
        out_shape=jax.ShapeDtypeStruct((cap, D_MODEL), F32),
        scratch_shapes=[pltpu.VMEM((2, _stage_rows(td), D_MODEL), F32), pltpu.VMEM((RUN_ALIGN, D_MODEL), F32),
                        pltpu.SemaphoreType.DMA((2,)), pltpu.SemaphoreType.DMA(())],
        compiler_params=_cparams("arbitrary"),
        name="dispatch",
    )(gtab, gtab, ttab, lpos, x1)


def _expert_kernel(bexp_ref, nused_ref, xs_ref, w1_ref, w3_ref, w2_ref, y_ref):
    del bexp_ref
    i = pl.program_id(0)

    @pl.when(i < nused_ref[0])
    def _():
        xb = xs_ref[...].astype(BF)
        h1 = jnp.dot(xb, w1_ref[0], preferred_element_type=F32)
        h3 = jnp.dot(xb, w3_ref[0], preferred_element_type=F32)
        hid = (_silu(h1) * h3).astype(BF)
        y_ref[...] = jnp.dot(hid, w2_ref[0], preferred_element_type=F32)

    @pl.when(i >= nused_ref[0])
    def _():
        y_ref[...] = jnp.zeros_like(y_ref)


def _expert(bexp, nused, xs, w1, w3, w2):
    cap = xs.shape[0]
    nb = cap // EXPERT_BLOCK
    last = lambda i, nu: jnp.minimum(i, nu[0] - 1)
    grid_spec = pltpu.PrefetchScalarGridSpec(
        num_scalar_prefetch=2,
        grid=(nb,),
        in_specs=[pl.BlockSpec((EXPERT_BLOCK, D_MODEL), lambda i, be, nu: (last(i, nu), 0)),
                  pl.BlockSpec((1, D_MODEL, MOE_FF), lambda i, be, nu: (be[last(i, nu)], 0, 0)),
                  pl.BlockSpec((1, D_MODEL, MOE_FF), lambda i, be, nu: (be[last(i, nu)], 0, 0)),
                  pl.BlockSpec((1, MOE_FF, D_MODEL), lambda i, be, nu: (be[last(i, nu)], 0, 0))],
        out_specs=pl.BlockSpec((EXPERT_BLOCK, D_MODEL), lambda i, be, nu: (i, 0)),
    )
    return pl.pallas_call(
        _expert_kernel,
        grid_spec=grid_spec,
        out_shape=jax.ShapeDtypeStruct((cap, D_MODEL), F32),
        compiler_params=_cparams("arbitrary"),
        name="expert",
    )(bexp, nused, xs, w1, w3, w2)


def _combine_kernel(tab_ref, ntab_ref, lpos_ref, lposr_ref, wr_ref, x_ref, lg_ref, lb_ref, yb_ref, o_ref, stage_ref,
                    sems,
                    *, tc):
    i = pl.program_id(0)
    last = pl.num_programs(0) - 1
    slot = lax.rem(i, 2)

    def run_copy(sl, src, dst, rows=RUN_ALIGN):
        return pltpu.make_async_copy(yb_ref.at[pl.ds(src, rows), :],
                                     stage_ref.at[sl, pl.ds(dst, rows), :], sems.at[sl])

    @pl.when(i == 0)
    def _():
        stage_ref[...] = jnp.zeros_like(stage_ref)
        _run_pieces(tab_ref, lambda src, dst: run_copy(slot, src, dst).start())

    @pl.when(i < last)
    def _():
        _run_pieces(ntab_ref, lambda src, dst: run_copy(1 - slot, src, dst).start())

    _wait_pieces(tab_ref, lambda rows: run_copy(slot, 0, 0, rows))
    rows = stage_ref.shape[1]
    r = lax.broadcasted_iota(I32, (rows, tc), 0)
    row_w = jnp.sum(jnp.where(r == lposr_ref[0:1, :], wr_ref[0:1, :], 0.0)
                    + jnp.where(r == lposr_ref[1:2, :], wr_ref[1:2, :], 0.0), axis=1, keepdims=True)
    staged = (stage_ref[slot] * row_w).astype(BF)
    c = lax.broadcasted_iota(I32, (tc, rows), 1)
    pick = jnp.logical_or(c == lpos_ref[:, 0:1], c == lpos_ref[:, 1:2]).astype(BF)
    ffn = jnp.dot(pick, staged, preferred_element_type=F32)
    o_ref[...] = _layer_norm(DN_ALPHA * x_ref[...] + ffn, lg_ref[...], lb_ref[...])


def _combine(gtab, lpos, wts, x1, lg, lb, yb, *, tc):
    t = x1.shape[0]
    n_tiles = t // tc
    full = lambda a: pl.BlockSpec(a.shape, lambda i: (0,) * a.ndim)
    smem_tab = lambda imap: pl.BlockSpec((1, SUBLANES, LANES), imap, memory_space=pltpu.SMEM)
    return pl.pallas_call(
        functools.partial(_combine_kernel, tc=tc),
        grid=(n_tiles,),
        in_specs=[smem_tab(lambda i: (i, 0, 0)), smem_tab(lambda i: (jnp.minimum(i + 1, n_tiles - 1), 0, 0)),
                  pl.BlockSpec((tc, MOE_TOPK), lambda i: (i, 0)),
                  pl.BlockSpec((MOE_TOPK, tc), lambda i: (0, i)),
                  pl.BlockSpec((MOE_TOPK, tc), lambda i: (0, i)),
                  pl.BlockSpec((tc, D_MODEL), lambda i: (i, 0)),
                  full(lg), full(lb),
                  pl.BlockSpec(memory_space=pl.ANY)],
        out_specs=pl.BlockSpec((tc, D_MODEL), lambda i: (i, 0)),
        out_shape=jax.ShapeDtypeStruct((t, D_MODEL), F32),
        scratch_shapes=[pltpu.VMEM((2, _stage_rows(tc), D_MODEL), F32), pltpu.SemaphoreType.DMA((2,))],
        compiler_params=_cparams("arbitrary"),
        name="combine",
    )(gtab, gtab, lpos.T, lpos, wts, x1, lg, lb, yb)


def _tile(n, want):
    while n % want:
        want //= 2
    return want


def _mixer(x2d, b, l, w_in, rel_bias, conv_w, a_log, dt_bias, norm_g, pool_w, pool_scale,
           w_br_a, w_br_b, w_br_c, gate_b, w_out, ln_g, ln_b, wr, rb):
    t = x2d.shape[0]
    c0 = 3 * A_WIDTH
    c1 = c0 + 3 * B_WIDTH
    c2 = c1 + 2 * B_HEADS
    c3 = c2 + B_WIDTH
    c4 = c3 + C_WIDTH
    wbf = w_in.astype(BF)
    wbd = jnp.pad(wbf[:, c1:c2], ((0, 0), (0, LANES - 2 * B_HEADS)))
    qkva, qkvb, bd, go, oc = _proj(
        x2d, wbf[:, :c0], wbf[:, c0:c1], wbd, wbf[:, c2:c3], wbf[:, c3:c4],
        pool_w.astype(BF), pool_scale.reshape(1, C_WIDTH), seq=l, tm=_tile(l, 1024))

    oa = _attn(qkva.reshape(b, l, 3 * A_WIDTH), _attn_bias_table(rel_bias)).reshape(t, A_WIDTH)

    lane_pad = lambda v: jnp.pad(v.reshape(1, B_HEADS), ((0, 0), (B_HEADS, LANES - 2 * B_HEADS)))
    ob = _gdn(qkvb.reshape(b, l, 3 * B_WIDTH), bd.reshape(b, l, LANES), go.reshape(b, l, B_WIDTH),
              conv_w, lane_pad(a_log), lane_pad(dt_bias), norm_g.reshape(1, B_HEAD_DIM)).reshape(t, B_WIDTH)

    wrh, wrl = _split_bf16(wr)
    return _merge(x2d, oa, ob, oc, wbf[:, c4:], gate_b, w_br_a.astype(BF), w_br_b.astype(BF),
                  w_br_c.astype(BF), w_out.astype(BF), ln_g.reshape(1, -1), ln_b.reshape(1, -1),
                  wrh, wrl, rb, tm=_tile(t, 1024))


def _moe(x1, lgt, w1, w3, w2, ln_g, ln_b):
    t = x1.shape[0]
    tile = _tile(t, MOE_TILE)
    max_rows = t * MOE_TOPK + (t // tile) * N_EXPERTS * (RUN_ALIGN - 1)
    n_blocks = -(-max_rows // EXPERT_BLOCK) + N_EXPERTS
    cap = n_blocks * EXPERT_BLOCK
    wts, lpos, tab, cnt = _route(lgt, tn=tile)
    gtab, ttab, bexp, nused = _layout(cnt, tab, n_blocks=n_blocks)
    xs = _dispatch(gtab, ttab, lpos, x1, td=tile, cap=cap)
    yb = _expert(bexp.reshape(n_blocks), nused[0, :1], xs, w1.astype(BF), w3.astype(BF), w2.astype(BF))
    return _combine(gtab, lpos, wts, x1, ln_g.reshape(1, -1), ln_b.reshape(1, -1), yb, tc=tile)


def kernel(x, w_in, attn_rel_bias, gdn_conv_w, gdn_a_log, gdn_dt_bias, gdn_norm_g, pool_w, pool_scale,
           w_branch_a, w_branch_b, w_branch_c, gate_b, w_out, ln1_g, ln1_b, router_group_w,
           router_group_b, router_expert_w, router_expert_b, moe_w1, moe_w3, moe_w2, ln2_g, ln2_b):
    b, l, d = x.shape
    x2d = x.reshape(b * l, d)
    for i in range(DEPTH):
        wr = jnp.concatenate([router_expert_w[i].T, router_group_w[i].T,
                              jnp.zeros((ROUTER_ROWS - N_EXPERTS - MOE_GROUPS, d), F32)], axis=0)
        rb = jnp.concatenate([router_expert_b[i], router_group_b[i],
                              jnp.zeros((ROUTER_ROWS - N_EXPERTS - MOE_GROUPS,), F32)]).reshape(ROUTER_ROWS, 1)
        x1, lgt = _mixer(x2d, b, l, w_in[i], attn_rel_bias[i], gdn_conv_w[i], gdn_a_log[i], gdn_dt_bias[i],
                         gdn_norm_g[i], pool_w[i], pool_scale[i], w_branch_a[i], w_branch_b[i],
                         w_branch_c[i], gate_b[i], w_out[i], ln1_g[i], ln1_b[i], wr, rb)
        x2d = _moe(x1, lgt, moe_w1[i], moe_w3[i], moe_w2[i], ln2_g[i], ln2_b[i])
    return x2d.reshape(b, l, d)
```

```python
import functools

import jax
import jax.numpy as jnp
from jax import lax
from jax.experimental import pallas as pl
from jax.experimental.pallas import tpu as pltpu

BF = jnp.bfloat16
F32 = jnp.float32
I32 = jnp.int32

D_MODEL = 1024
DEPTH = 2
CHUNK = 64

A_HEADS = 8
A_HEAD_DIM = 64
A_WIDTH = A_HEADS * A_HEAD_DIM
A_LEFT_CHUNKS = 8
A_REL_CLIP = 256

B_HEADS = 4
B_HEAD_DIM = 128
B_WIDTH = B_HEADS * B_HEAD_DIM
B_CONV = 4

C_WINDOWS = (2, 4, 8, 16)
C_GROUPS = 4
C_GROUP_DIM = 128
C_WIDTH = C_GROUPS * C_GROUP_DIM

N_BRANCH = 3
MOE_GROUPS = 4
MOE_PER_GROUP = 8
N_EXPERTS = MOE_GROUPS * MOE_PER_GROUP
MOE_TOPK = 2
MOE_FF = 512

DN_ALPHA = (2 * DEPTH) ** 0.25
LN_EPS = 1e-5
RMS_EPS = 1e-6

LANES = 128
SUBLANES = 8
VMEM_LIMIT_BYTES = 56 * 1024 * 1024

QBLOCK = 2 * CHUNK
KBAND = (A_LEFT_CHUNKS + 2) * CHUNK
NEG_BIG = -1e30

DENSE_TILE = 1024
ROUTER_ROWS = 40
EXPERT_BLOCK = 512
MOE_TILE = 512
RUN_ALIGN = 8


def _cparams(*sem):
    return pltpu.CompilerParams(dimension_semantics=sem, vmem_limit_bytes=VMEM_LIMIT_BYTES)


def _split_bf16(a):
    hi = a.astype(BF)
    lo = (a - hi.astype(F32)).astype(BF)
    return hi, lo


def _sigmoid(x):
    return 1.0 / (1.0 + jnp.exp(-x))


def _silu(x):
    return x * _sigmoid(x)


def _layer_norm(z, g, b):
    mu = jnp.mean(z, axis=-1, keepdims=True)
    zc = z - mu
    var = jnp.mean(zc * zc, axis=-1, keepdims=True)
    return zc * lax.rsqrt(var + LN_EPS) * g + b


POOL_TAIL = 16


def _proj_kernel(x_ref, wa_ref, wb_ref, wbd_ref, wgo_ref, wc_ref, pw_ref, ps_ref,
                 qkva_ref, qkvb_ref, bd_ref, go_ref, oc_ref, tail_ref, *, tiles_per_seq, tm):
    it = pl.program_id(0) % tiles_per_seq
    xb = x_ref[...].astype(BF)
    qkva_ref[...] = jnp.dot(xb, wa_ref[...], preferred_element_type=F32).astype(BF)
    qkvb_ref[...] = jnp.dot(xb, wb_ref[...], preferred_element_type=F32).astype(BF)
    bd_ref[...] = jnp.dot(xb, wbd_ref[...], preferred_element_type=F32)
    go_ref[...] = jnp.dot(xb, wgo_ref[...], preferred_element_type=F32).astype(BF)
    u = jnp.dot(xb, wc_ref[...], preferred_element_type=F32)

    @pl.when(it == 0)
    def _():
        tail_ref[...] = jnp.zeros_like(tail_ref)

    ext = jnp.concatenate([tail_ref[...], u], axis=0)
    tail_ref[...] = u[tm - POOL_TAIL:, :]
    pos = (it * tm + 1 + lax.broadcasted_iota(I32, (tm, 1), 0)).astype(F32)
    s = ext
    for gi, win in enumerate(C_WINDOWS):
        if gi:
            s = s[:, C_GROUP_DIM:]
        s = s + pltpu.roll(s, win // 2, 0)
        lo_c, hi_c = gi * C_GROUP_DIM, (gi + 1) * C_GROUP_DIM
        cnt = jnp.minimum(pos, float(win))
        pooled = s[POOL_TAIL:, :C_GROUP_DIM] / cnt - u[:, lo_c:hi_c]
        mixed = jnp.dot(pooled.astype(BF), pw_ref[gi], preferred_element_type=F32)
        oc_ref[:, lo_c:hi_c] = (mixed * ps_ref[:, lo_c:hi_c]).astype(BF)


def _proj(x2d, wa, wb, wbd, wgo, wc, pw, ps, *, seq, tm):
    t = x2d.shape[0]
    full = lambda a: pl.BlockSpec(a.shape, lambda i: (0,) * a.ndim)
    row = lambda n: pl.BlockSpec((tm, n), lambda i: (i, 0))
    return pl.pallas_call(
        functools.partial(_proj_kernel, tiles_per_seq=seq // tm, tm=tm),
        grid=(t // tm,),
        in_specs=[row(D_MODEL), full(wa), full(wb), full(wbd), full(wgo), full(wc), full(pw), full(ps)],
        out_specs=[row(3 * A_WIDTH), row(3 * B_WIDTH), row(LANES), row(B_WIDTH), row(C_WIDTH)],
        out_shape=[jax.ShapeDtypeStruct((t, 3 * A_WIDTH), BF),
                   jax.ShapeDtypeStruct((t, 3 * B_WIDTH), BF),
                   jax.ShapeDtypeStruct((t, LANES), F32),
                   jax.ShapeDtypeStruct((t, B_WIDTH), BF),
                   jax.ShapeDtypeStruct((t, C_WIDTH), BF)],
        scratch_shapes=[pltpu.VMEM((POOL_TAIL, C_WIDTH), F32)],
        compiler_params=_cparams("arbitrary"),
        name="proj",
    )(x2d, wa, wb, wbd, wgo, wc, pw, ps)


def _attn_bias_table(rel_bias):
    h = rel_bias.shape[0]
    pad = A_LEFT_CHUNKS * CHUNK
    width = KBAND + pad
    n = width + QBLOCK - 1
    far = jnp.repeat(rel_bias[:, -1:], QBLOCK - 1 + pad - A_REL_CLIP, axis=1)
    near = jnp.repeat(rel_bias[:, :1], n - far.shape[1] - rel_bias.shape[1], axis=1)
    w = jnp.concatenate([far, rel_bias[:, ::-1], near], axis=1).astype(F32)
    w = jnp.roll(w, -(QBLOCK - 1), axis=1)
    toep = jnp.tile(w, (1, QBLOCK))[:, :QBLOCK * (n - 1)].reshape(h, QBLOCK, n - 1)[:, :, :width]
    r = jnp.arange(QBLOCK)[:, None]
    m = jnp.arange(width)[None, :]
    kc, qc = m // CHUNK, r // CHUNK
    valid = (kc >= qc) & (kc <= qc + A_LEFT_CHUNKS)
    tab = jnp.where(valid[None], toep, NEG_BIG)
    return tab.reshape(h, QBLOCK, width // LANES, LANES).transpose(0, 2, 1, 3)


ATTN_QBLOCKS_PER_STEP = 4


def _attn_kernel(q_ref, k_ref, v_ref, tab_ref, o_ref):
    first_blk = A_LEFT_CHUNKS * CHUNK // QBLOCK
    lane = lax.broadcasted_iota(I32, (QBLOCK, LANES), 1)
    lo = lane < A_HEAD_DIM
    nkb = KBAND // LANES
    items = []
    for sub in range(ATTN_QBLOCKS_PER_STEP):
        j = pl.program_id(1) * ATTN_QBLOCKS_PER_STEP + sub
        kstart = pl.multiple_of(jnp.maximum(j - first_blk, 0) * QBLOCK, QBLOCK)
        offb = jnp.maximum(first_blk - j, 0)
        for hp in range(A_HEADS // 2):
            items.append((slice(sub * QBLOCK, (sub + 1) * QBLOCK), kstart, offb, hp,
                          slice(hp * LANES, (hp + 1) * LANES)))
    scores = []
    for rows, kstart, offb, hp, cols in items:
        qp = q_ref[0, rows, cols] * jnp.asarray(A_HEAD_DIM ** -0.5, BF)
        zero = jnp.zeros_like(qp)
        q2 = jnp.concatenate([jnp.where(lo, qp, zero), jnp.where(lo, zero, qp)], axis=0)
        kp = k_ref[0, pl.ds(kstart, KBAND), cols]
        scores.append(lax.dot_general(q2, kp, (((1,), (1,)), ((), ())), preferred_element_type=F32))
    probs, denoms = [], []
    for (rows, kstart, offb, hp, cols), sc in zip(items, scores):
        bias = jnp.concatenate(
            [jnp.concatenate([tab_ref[2 * hp + par, offb + i] for i in range(nkb)], axis=1) for par in range(2)],
            axis=0)
        s = sc + bias
        p = jnp.exp(s - jnp.max(s, axis=-1, keepdims=True))
        denoms.append(jnp.sum(p, axis=-1, keepdims=True))
        probs.append(p.astype(BF))
    for (rows, kstart, offb, hp, cols), p, den in zip(items, probs, denoms):
        vp = v_ref[0, pl.ds(kstart, KBAND), cols]
        o2 = jnp.dot(p, vp, preferred_element_type=F32) / den
        o_ref[0, rows, cols] = jnp.where(lo, o2[:QBLOCK], o2[QBLOCK:]).astype(BF)


def _attn(qkva, tab):
    b, l, _ = qkva.shape
    step_rows = ATTN_QBLOCKS_PER_STEP * QBLOCK
    return pl.pallas_call(
        _attn_kernel,
        grid=(b, l // step_rows),
        in_specs=[pl.BlockSpec((1, step_rows, A_WIDTH), lambda bi, j: (bi, j, 0)),
                  pl.BlockSpec((1, l, A_WIDTH), lambda bi, j: (bi, 0, 1)),
                  pl.BlockSpec((1, l, A_WIDTH), lambda bi, j: (bi, 0, 2)),
                  pl.BlockSpec(tab.shape, lambda bi, j: (0, 0, 0, 0))],
        out_specs=pl.BlockSpec((1, step_rows, A_WIDTH), lambda bi, j: (bi, j, 0)),
        out_shape=jax.ShapeDtypeStruct((b, l, A_WIDTH), BF),
        compiler_params=_cparams("arbitrary", "arbitrary"),
        name="attn",
    )(qkva, qkva, qkva, tab)


CONV_HIST = 16
GDN_PREP_UNROLL = 4


def _gdn_kernel(qkv_ref, bd_ref, go_ref, cw_ref, alog_ref, dtb_ref, ng_ref, o_ref,
                u_ref, wq_ref, qkkd_ref, egl_ref, state_ref, *, n_chunks):
    rt = lax.broadcasted_iota(I32, (CHUNK, CHUNK), 0)
    ct = lax.broadcasted_iota(I32, (CHUNK, CHUNK), 1)
    tril = rt >= ct
    strict = rt > ct
    row_c = lax.broadcasted_iota(I32, (CHUNK, LANES), 0)
    eye_f = (rt == ct).astype(F32)
    neg_rate = -jnp.exp(alog_ref[...])
    sr = lax.broadcasted_iota(I32, (B_CONV * CHUNK, CONV_HIST + CHUNK), 0)
    sc = lax.broadcasted_iota(I32, (B_CONV * CHUNK, CONV_HIST + CHUNK), 1)
    shift_sel = (sc == CONV_HIST + jnp.bitwise_and(sr, CHUNK - 1)
                 - jnp.right_shift(sr, CHUNK.bit_length() - 1)).astype(BF)

    def heads(fn):
        return jnp.concatenate([fn(h, slice(h * B_HEAD_DIM, (h + 1) * B_HEAD_DIM)) for h in range(B_HEADS)],
                               axis=1)

    def prep(c):
        r0 = pl.multiple_of(c * CHUNK, CHUNK)
        cur = qkv_ref[0, pl.ds(r0, CHUNK), :]
        h0 = pl.multiple_of(jnp.maximum(r0 - CONV_HIST, 0), CONV_HIST)
        hist = qkv_ref[0, pl.ds(h0, CONV_HIST), :]
        hist = jnp.where(c > 0, hist, jnp.zeros_like(hist))
        taps = jnp.dot(shift_sel, jnp.concatenate([hist, cur], axis=0), preferred_element_type=F32)
        acc = taps[:CHUNK] * cw_ref[B_CONV - 1:B_CONV, :]
        for back in range(1, B_CONV):
            tap = B_CONV - 1 - back
            acc = acc + taps[back * CHUNK:(back + 1) * CHUNK] * cw_ref[tap:tap + 1, :]
        act = _silu(acc)
        q_all, k_all, v_all = act[:, :B_WIDTH], act[:, B_WIDTH:2 * B_WIDTH], act[:, 2 * B_WIDTH:]

        bd = bd_ref[0, pl.ds(r0, CHUNK), :]
        beta_all = _sigmoid(bd)
        z = bd + dtb_ref[...]
        softplus = jnp.maximum(z, 0.0) + jnp.log1p(jnp.exp(-jnp.abs(z)))
        g_all = neg_rate * softplus
        gc_all = g_all
        for sh in (1, 2, 4, 8, 16, 32):
            gc_all = gc_all + jnp.where(row_c >= sh, pltpu.roll(gc_all, sh, 0), 0.0)
        egc_all = jnp.exp(gc_all)
        glast = gc_all[CHUNK - 1:CHUNK, :]
        ekd_all = jnp.exp(glast - gc_all)
        egl = jnp.exp(glast)
        col = lambda m, h: m[:, B_HEADS + h:B_HEADS + h + 1]

        def l2n(m):
            return m * lax.rsqrt(jnp.sum(m * m, axis=-1, keepdims=True) + RMS_EPS)

        qn = heads(lambda h, hs: l2n(q_all[:, hs]) * (B_HEAD_DIM ** -0.5))
        kn = heads(lambda h, hs: l2n(k_all[:, hs]))
        kbeta = heads(lambda h, hs: kn[:, hs] * beta_all[:, h:h + 1])
        vbeta = heads(lambda h, hs: v_all[:, hs] * beta_all[:, h:h + 1])
        kbe = heads(lambda h, hs: kbeta[:, hs] * col(egc_all, h))
        qe = heads(lambda h, hs: qn[:, hs] * col(egc_all, h))
        chains = []
        for h in range(B_HEADS):
            hs = slice(h * B_HEAD_DIM, (h + 1) * B_HEAD_DIM)
            gcc = jnp.broadcast_to(col(gc_all, h), (CHUNK, CHUNK))
            gcr = jnp.sum(gcc * eye_f, axis=0, keepdims=True)
            chains.append(dict(
                c=c, h=h, hs=hs,
                decay=jnp.exp(jnp.where(tril, gcc - gcr, -jnp.inf)),
                kq=jnp.concatenate([kbeta[:, hs], qn[:, hs]], axis=0).astype(BF),
                kn=kn[:, hs].astype(BF),
                rhs=jnp.concatenate([vbeta[:, hs], kbe[:, hs]], axis=1).astype(BF),
                kd=kn[:, hs] * col(ekd_all, h)))
        egl_row = heads(lambda h, hs: jnp.broadcast_to(col(egl, h), (1, B_HEAD_DIM)))
        return chains, qe.astype(BF), egl_row

    nt = lambda a, b: lax.dot_general(a, b, (((1,), (1,)), ((), ())), preferred_element_type=F32)
    mm = lambda a, b: jnp.dot(a, b, preferred_element_type=F32)

    def prep_levels(i):
        fronts = [prep(i * GDN_PREP_UNROLL + g) for g in range(GDN_PREP_UNROLL)]
        ch = [x for f in fronts for x in f[0]]
        aq = [nt(x["kq"], x["kn"]) for x in ch]
        kdt = [x["kd"].T for x in ch]
        yield
        qk = [(a[CHUNK:] * x["decay"]).astype(BF) for x, a in zip(ch, aq)]
        pw = [-jnp.where(strict, a[:CHUNK] * x["decay"], 0.0) for x, a in zip(ch, aq)]
        tmat = [eye_f + p for p in pw]
        pwb = [p.astype(BF) for p in pw]
        pw = [mm(p, p) for p in pwb]
        yield
        n_sq = CHUNK.bit_length() - 2
        for lvl in range(n_sq):
            pwb = [p.astype(BF) for p in pw]
            if lvl + 1 < n_sq:
                both = [mm(jnp.concatenate([t.astype(BF), p], axis=0), p) for t, p in zip(tmat, pwb)]
                yield
                tmat = [t + r[:CHUNK] for t, r in zip(tmat, both)]
                pw = [r[CHUNK:] for r in both]
            else:
                last = [mm(t.astype(BF), p) for t, p in zip(tmat, pwb)]
                yield
                tmat = [t + r for t, r in zip(tmat, last)]
        uw = [mm(t.astype(BF), x["rhs"]) for t, x in zip(tmat, ch)]
        yield
        for x, r, qkx, kt in zip(ch, uw, qk, kdt):
            u_ref[x["c"], :, x["hs"]] = r[:, :B_HEAD_DIM]
            wq_ref[x["c"], :CHUNK, x["hs"]] = r[:, B_HEAD_DIM:].astype(BF)
            qkkd_ref[x["c"], x["h"], :CHUNK, :] = qkx
            qkkd_ref[x["c"], x["h"], CHUNK:, :] = kt.astype(BF)
        for g, (_, qe_b, egl_row) in enumerate(fronts):
            wq_ref[i * GDN_PREP_UNROLL + g, CHUNK:, :] = qe_b
            egl_ref[i * GDN_PREP_UNROLL + g] = egl_row

    def step_levels(c):
        r0 = pl.multiple_of(c * CHUNK, CHUNK)
        st = state_ref[...]
        stb = st.astype(BF)
        hss = [slice(h * B_HEAD_DIM, (h + 1) * B_HEAD_DIM) for h in range(B_HEADS)]
        r1 = [mm(wq_ref[c, :, hs], stb[:, hs]) for hs in hss]
        yield
        v_new = [(u_ref[c, :, hs] - r[:CHUNK]).astype(BF) for hs, r in zip(hss, r1)]
        r2 = [mm(qkkd_ref[c, h], v_new[h]) for h in range(B_HEADS)]
        yield
        o_all = jnp.concatenate([a[CHUNK:] + b[:CHUNK] for a, b in zip(r1, r2)], axis=1)
        state_ref[...] = st * egl_ref[c] + jnp.concatenate([b[CHUNK:] for b in r2], axis=1)
        gate = go_ref[0, pl.ds(r0, CHUNK), :].astype(F32)
        o_n = heads(lambda h, hs: o_all[:, hs]
                    * lax.rsqrt(jnp.mean(o_all[:, hs] * o_all[:, hs], axis=-1, keepdims=True) + RMS_EPS)
                    * ng_ref[...])
        o_ref[0, pl.ds(r0, CHUNK), :] = (o_n * _silu(gate)).astype(BF)

    def steps_of(i):
        for g in range(GDN_PREP_UNROLL):
            yield from step_levels(i * GDN_PREP_UNROLL + g)

    def run_interleaved(*gens):
        live = list(gens)
        while live:
            for g in list(live):
                if next(g, StopIteration) is StopIteration:
                    live.remove(g)

    n_groups = n_chunks // GDN_PREP_UNROLL
    state_ref[...] = jnp.zeros_like(state_ref)
    run_interleaved(prep_levels(0))

    def body(i, carry):
        run_interleaved(steps_of(i - 1), prep_levels(i))
        return carry

    lax.fori_loop(1, n_groups, body, 0)
    run_interleaved(steps_of(n_groups - 1))


def _gdn(qkvb, bd, go, cw, alog, dtb, ng):
    b, l, _ = qkvb.shape
    nc = l // CHUNK
    full = lambda a: pl.BlockSpec(a.shape, lambda bi: (0,) * a.ndim)
    seq = lambda n: pl.BlockSpec((1, l, n), lambda bi: (bi, 0, 0))
    return pl.pallas_call(
        functools.partial(_gdn_kernel, n_chunks=nc),
        grid=(b,),
        in_specs=[seq(3 * B_WIDTH), seq(LANES), seq(B_WIDTH), full(cw), full(alog), full(dtb), full(ng)],
        out_specs=seq(B_WIDTH),
        out_shape=jax.ShapeDtypeStruct((b, l, B_WIDTH), BF),
        scratch_shapes=[pltpu.VMEM((nc, CHUNK, B_WIDTH), F32),
                        pltpu.VMEM((nc, 2 * CHUNK, B_WIDTH), BF),
                        pltpu.VMEM((nc, B_HEADS, CHUNK + B_HEAD_DIM, CHUNK), BF),
                        pltpu.VMEM((nc, 1, B_WIDTH), F32),
                        pltpu.VMEM((B_HEAD_DIM, B_WIDTH), F32)],
        compiler_params=_cparams("arbitrary"),
        name="gdn",
    )(qkvb, bd, go, cw, alog, dtb, ng)


def _merge_kernel(x_ref, oa_ref, ob_ref, oc_ref, wg_ref, gb_ref, wa_ref, wb_ref, wc_ref, wo_ref,
                  lg_ref, lb_ref, wrh_ref, wrl_ref, rb_ref, x1_ref, wts_ref, lpos_ref, tab_ref, cnt_ref, run_ref):
    x = x_ref[...]
    xb = x.astype(BF)
    y = None
    for i, (o_ref, w_ref) in enumerate(((oa_ref, wa_ref), (ob_ref, wb_ref), (oc_ref, wc_ref))):
        gl = jnp.dot(xb, wg_ref[:, i * D_MODEL:(i + 1) * D_MODEL], preferred_element_type=F32)
        gate = _sigmoid(gl + gb_ref[i:i + 1, :])
        br = gate * jnp.dot(o_ref[...], w_ref[...], preferred_element_type=F32)
        y = br if y is None else y + br
    mix = jnp.dot(y.astype(BF), wo_ref[...], preferred_element_type=F32)
    x1 = _layer_norm(DN_ALPHA * x + mix, lg_ref[...], lb_ref[...])
    x1_ref[...] = x1
    xh, xl = _split_bf16(x1)
    nt = lambda a, b: lax.dot_general(a, b, (((1,), (1,)), ((), ())), preferred_element_type=F32)
    lgt = nt(wrh_ref[...], xh) + (nt(wrh_ref[...], xl) + nt(wrl_ref[...], xh)) + rb_ref[...]

    @pl.when(pl.program_id(0) == 0)
    def _():
        run_ref[...] = jnp.zeros_like(run_ref)

    run = run_ref[...]
    for s in range(lgt.shape[1] // MOE_TILE):
        cols = slice(s * MOE_TILE, (s + 1) * MOE_TILE)
        wts_ref[:, cols], lpos_ref[:, cols], tab_ref[s], run = _route_tile(lgt[:, cols], run)
    run_ref[...] = run
    cnt_ref[...] = jnp.broadcast_to(run, cnt_ref.shape)


def _merge(x2d, oa, ob, oc, wg, gb, wa, wb, wc, wo, lg, lb, wrh, wrl, rb, *, tm):
    t = x2d.shape[0]
    full = lambda a: pl.BlockSpec(a.shape, lambda i: (0,) * a.ndim)
    row = lambda n: pl.BlockSpec((tm, n), lambda i: (i, 0))
    tok = lambda: pl.BlockSpec((MOE_TOPK, tm), lambda i: (0, i))
    assert tm % MOE_TILE == 0, (tm, MOE_TILE)
    return pl.pallas_call(
        _merge_kernel,
        grid=(t // tm,),
        in_specs=[row(D_MODEL), row(A_WIDTH), row(B_WIDTH), row(C_WIDTH), full(wg), full(gb), full(wa),
                  full(wb), full(wc), full(wo), full(lg), full(lb), full(wrh), full(wrl), full(rb)],
        out_specs=[row(D_MODEL), tok(), tok(), pl.BlockSpec((tm // MOE_TILE, SUBLANES, LANES), lambda i: (i, 0, 0)),
                   pl.BlockSpec((N_EXPERTS, LANES), lambda i: (0, 0))],
        out_shape=[jax.ShapeDtypeStruct((t, D_MODEL), F32),
                   jax.ShapeDtypeStruct((MOE_TOPK, t), F32),
                   jax.ShapeDtypeStruct((MOE_TOPK, t), I32),
                   jax.ShapeDtypeStruct((t // MOE_TILE, SUBLANES, LANES), I32),
                   jax.ShapeDtypeStruct((N_EXPERTS, LANES), F32)],
        scratch_shapes=[pltpu.VMEM((N_EXPERTS, 1), F32)],
        compiler_params=_cparams("arbitrary"),
        name="merge",
    )(x2d, oa, ob, oc, wg, gb, wa, wb, wc, wo, lg, lb, wrh, wrl, rb)


def _route_tile(lgt, run):
    tn = lgt.shape[1]
    le = lgt[0:N_EXPERTS, :]
    lg = lgt[N_EXPERTS:N_EXPERTS + MOE_GROUPS, :]
    gi = lax.broadcasted_iota(I32, (MOE_GROUPS, tn), 0).astype(F32)
    ei = lax.broadcasted_iota(I32, (N_EXPERTS, tn), 0).astype(F32)
    eg = jnp.right_shift(lax.broadcasted_iota(I32, (N_EXPERTS, tn), 0),
                         MOE_PER_GROUP.bit_length() - 1).astype(F32)
    mg = jnp.max(lg, axis=0, keepdims=True)
    gsel = jnp.min(jnp.where(lg == mg, gi, float(MOE_GROUPS)), axis=0, keepdims=True)
    p_top = 1.0 / jnp.sum(jnp.exp(lg - mg), axis=0, keepdims=True)
    l1 = jnp.where(eg == gsel, le, -jnp.inf)
    m1 = jnp.max(l1, axis=0, keepdims=True)
    i1 = jnp.min(jnp.where(l1 == m1, ei, float(N_EXPERTS)), axis=0, keepdims=True)
    l2 = jnp.where(ei == i1, -jnp.inf, l1)
    m2 = jnp.max(l2, axis=0, keepdims=True)
    i2 = jnp.min(jnp.where(l2 == m2, ei, float(N_EXPERTS)), axis=0, keepdims=True)
    e2 = jnp.exp(m2 - m1)
    den = 1.0 + e2
    wts = jnp.concatenate([p_top / den, p_top * (e2 / den)], axis=0)

    before = (lax.broadcasted_iota(I32, (tn, tn), 0) < lax.broadcasted_iota(I32, (tn, tn), 1)).astype(BF)
    oh = [ei == ik for ik in (i1, i2)]
    prefix = [jnp.dot(o.astype(BF), before, preferred_element_type=F32) for o in oh]
    cnt_k = [jnp.sum(o.astype(F32), axis=1, keepdims=True) for o in oh]
    cnt = cnt_k[0] + cnt_k[1]
    cnt_al = jnp.floor((cnt + (RUN_ALIGN - 1)) * (1.0 / RUN_ALIGN)) * RUN_ALIGN
    er = lax.broadcasted_iota(I32, (N_EXPERTS, N_EXPERTS), 0)
    ec = lax.broadcasted_iota(I32, (N_EXPERTS, N_EXPERTS), 1)
    lstart = jnp.dot((ec < er).astype(BF), jnp.broadcast_to(cnt_al, (N_EXPERTS, LANES)).astype(BF),
                     preferred_element_type=F32)[:, 0:1]
    first = [0.0, cnt_k[0]]
    pick = lambda k: jnp.sum(jnp.where(oh[k], prefix[k] + first[k] + lstart, 0.0), axis=0, keepdims=True)
    lpos = jnp.concatenate([pick(0), pick(1)], axis=0).astype(I32)
    eye_l = (lax.broadcasted_iota(I32, (N_EXPERTS, LANES), 0) == lax.broadcasted_iota(I32, (N_EXPERTS, LANES), 1))
    as_row = lambda colv: jnp.sum(jnp.where(eye_l, colv, 0.0), axis=0, keepdims=True)
    pieces = jnp.broadcast_to(jnp.sum(cnt_al, axis=0, keepdims=True) * (1.0 / RUN_ALIGN), (1, LANES))
    tab = jnp.concatenate([as_row(run), as_row(cnt), as_row(lstart), pieces,
                           jnp.zeros((SUBLANES - 4, LANES), F32)], axis=0).astype(I32)
    return wts, lpos, tab, run + cnt_al


def _layout_kernel(cnt_ref, tab_ref, gtab_ref, ttab_ref, bexp_ref, nused_ref, *, n_blocks):
    cnt = cnt_ref[...].astype(I32)
    shift = EXPERT_BLOCK.bit_length() - 1
    padded_blocks = jnp.right_shift(cnt + (EXPERT_BLOCK - 1), shift)
    er = lax.broadcasted_iota(I32, (N_EXPERTS, LANES), 0)
    el = lax.broadcasted_iota(I32, (N_EXPERTS, LANES), 1)
    start = jnp.zeros((N_EXPERTS, LANES), I32)
    for e in range(N_EXPERTS - 1):
        start = start + jnp.where(er > e, padded_blocks[e:e + 1, :], 0)
    end = start + padded_blocks
    as_row = lambda m: jnp.sum(jnp.where(er == el, m, 0), axis=0, keepdims=True)
    tab = tab_ref[...]
    first_row = lax.broadcasted_iota(I32, tab.shape, 1) == 0
    gtab_ref[...] = tab + jnp.where(first_row, as_row(start * EXPERT_BLOCK)[None], 0)
    tail_rows = padded_blocks * EXPERT_BLOCK - cnt
    ttab_ref[...] = jnp.concatenate(
        [as_row(start * EXPERT_BLOCK + cnt), as_row(jnp.right_shift(tail_rows, RUN_ALIGN.bit_length() - 1)),
         jnp.zeros((SUBLANES - 2, LANES), I32)], axis=0)[None]
    blk = lax.broadcasted_iota(I32, (1, n_blocks), 1)
    owner = jnp.zeros((1, n_blocks), I32)
    for e in range(N_EXPERTS):
        owner = owner + (blk >= end[e:e + 1, 0:1]).astype(I32)
    bexp_ref[...] = jnp.minimum(owner, N_EXPERTS - 1)
    nused_ref[...] = end[N_EXPERTS - 1:N_EXPERTS, :]


def _layout(cnt, tab, *, n_blocks):
    full = lambda shape: pl.BlockSpec(shape, lambda i: (0,) * len(shape))
    return pl.pallas_call(
        functools.partial(_layout_kernel, n_blocks=n_blocks),
        grid=(1,),
        in_specs=[full(cnt.shape), full(tab.shape)],
        out_specs=[full(tab.shape), full((1, SUBLANES, LANES)), full((1, n_blocks)), full((1, LANES))],
        out_shape=[jax.ShapeDtypeStruct(tab.shape, I32),
                   jax.ShapeDtypeStruct((1, SUBLANES, LANES), I32),
                   jax.ShapeDtypeStruct((1, n_blocks), I32),
                   jax.ShapeDtypeStruct((1, LANES), I32)],
        compiler_params=_cparams("arbitrary"),
        name="layout",
    )(cnt, tab)


def _run_pieces(tab_ref, fn):
    for e in range(N_EXPERTS):
        sorted0, staging0 = tab_ref[0, 0, e], tab_ref[0, 2, e]
        n_pieces = (tab_ref[0, 1, e] + (RUN_ALIGN - 1)) // RUN_ALIGN

        def one(j, c, sorted0=sorted0, staging0=staging0):
            fn(pl.multiple_of(sorted0 + j * RUN_ALIGN, RUN_ALIGN), pl.multiple_of(staging0 + j * RUN_ALIGN, RUN_ALIGN))
            return c

        lax.fori_loop(0, n_pieces, one, 0)


WAIT_GROUPS = (64, 8, 1)


def _wait_pieces(tab_ref, make_copy):
    total = tab_ref[0, 3, 0]
    counts = (total // WAIT_GROUPS[0], (total // WAIT_GROUPS[1]) % (WAIT_GROUPS[0] // WAIT_GROUPS[1]),
              total % WAIT_GROUPS[1])
    for group, count in zip(WAIT_GROUPS, counts):
        def one(j, c, group=group):
            make_copy(group * RUN_ALIGN).wait()
            return c
        lax.fori_loop(0, count, one, 0)


def _dispatch_kernel(tab_ref, ptab_ref, ttab_ref, lpos_ref, x_ref, xs_ref, stage_ref, zero_ref, sems, tsem,
                     *, td):
    i = pl.program_id(0)
    last = pl.num_programs(0) - 1
    slot = lax.rem(i, 2)
    rows = stage_ref.shape[1]
    r = lax.broadcasted_iota(I32, (rows, td), 0)
    sel = jnp.logical_or(r == lpos_ref[0:1, :], r == lpos_ref[1:2, :]).astype(BF)
    stage_ref[slot] = jnp.dot(sel, x_ref[...].astype(BF), preferred_element_type=F32)

    def run_copy(sl, dst, src, rows=RUN_ALIGN):
        return pltpu.make_async_copy(stage_ref.at[sl, pl.ds(src, rows), :],
                                     xs_ref.at[pl.ds(dst, rows), :], sems.at[sl])

    def tail_copies(fn):
        def per_expert(e, carry):
            def one(j, c):
                dst = pl.multiple_of(ttab_ref[0, 0, e] + j * RUN_ALIGN, RUN_ALIGN)
                fn(pltpu.make_async_copy(zero_ref, xs_ref.at[pl.ds(dst, RUN_ALIGN), :], tsem))
                return c
            lax.fori_loop(0, ttab_ref[0, 1, e], one, 0)
            return carry
        lax.fori_loop(0, N_EXPERTS, per_expert, 0)

    @pl.when(i == 0)
    def _():
        zero_ref[...] = jnp.zeros_like(zero_ref)
        tail_copies(lambda cp: cp.start())

    @pl.when(i > 0)
    def _():
        _wait_pieces(ptab_ref, lambda rows: run_copy(1 - slot, 0, 0, rows))

    _run_pieces(tab_ref, lambda dst, src: run_copy(slot, dst, src).start())

    @pl.when(i == last)
    def _():
        _wait_pieces(tab_ref, lambda rows: run_copy(slot, 0, 0, rows))
        tail_copies(lambda cp: cp.wait())


def _stage_rows(td):
    return MOE_TOPK * td + N_EXPERTS * RUN_ALIGN


def _dispatch(gtab, ttab, lpos, x1, *, td, cap):
    t = x1.shape[0]
    smem_tab = lambda imap: pl.BlockSpec((1, SUBLANES, LANES), imap, memory_space=pltpu.SMEM)
    return pl.pallas_call(
        functools.partial(_dispatch_kernel, td=td),
        grid=(t // td,),
        in_specs=[smem_tab(lambda i: (i, 0, 0)), smem_tab(lambda i: (jnp.maximum(i - 1, 0), 0, 0)),
                  smem_tab(lambda i: (0, 0, 0)),
                  pl.BlockSpec((MOE_TOPK, td), lambda i: (0, i)),
                  pl.BlockSpec((td, D_MODEL), lambda i: (i, 0))],
        out_specs=pl.BlockSpec(memory_space=pl.ANY),
        out_shape=jax.ShapeDtypeStruct((cap, D_MODEL), F32),
        scratch_shapes=[pltpu.VMEM((2, _stage_rows(td), D_MODEL), F32), pltpu.VMEM((RUN_ALIGN, D_MODEL), F32),
                        pltpu.SemaphoreType.DMA((2,)), pltpu.SemaphoreType.DMA(())],
        compiler_params=_cparams("arbitrary"),
        name="dispatch",
    )(gtab, gtab, ttab, lpos, x1)


def _expert_kernel(bexp_ref, nused_ref, xs_ref, w1_ref, w3_ref, w2_ref, y_ref):
    del bexp_ref
    i = pl.program_id(0)

    @pl.when(i < nused_ref[0])
    def _():
        xb = xs_ref[...].astype(BF)
        h1 = jnp.dot(xb, w1_ref[0], preferred_element_type=F32)
        h3 = jnp.dot(xb, w3_ref[0], preferred_element_type=F32)
        hid = (_silu(h1) * h3).astype(BF)
        y_ref[...] = jnp.dot(hid, w2_ref[0], preferred_element_type=F32)

    @pl.when(i >= nused_ref[0])
    def _():
        y_ref[...] = jnp.zeros_like(y_ref)


def _expert(bexp, nused, xs, w1, w3, w2):
    cap = xs.shape[0]
    nb = cap // EXPERT_BLOCK
    last = lambda i, nu: jnp.minimum(i, nu[0] - 1)
    grid_spec = pltpu.PrefetchScalarGridSpec(
        num_scalar_prefetch=2,
        grid=(nb,),
        in_specs=[pl.BlockSpec((EXPERT_BLOCK, D_MODEL), lambda i, be, nu: (last(i, nu), 0)),
                  pl.BlockSpec((1, D_MODEL, MOE_FF), lambda i, be, nu: (be[last(i, nu)], 0, 0)),
                  pl.BlockSpec((1, D_MODEL, MOE_FF), lambda i, be, nu: (be[last(i, nu)], 0, 0)),
                  pl.BlockSpec((1, MOE_FF, D_MODEL), lambda i, be, nu: (be[last(i, nu)], 0, 0))],
        out_specs=pl.BlockSpec((EXPERT_BLOCK, D_MODEL), lambda i, be, nu: (jnp.where(i < nu[0], i, nb - 1), 0)),
    )
    return pl.pallas_call(
        _expert_kernel,
        grid_spec=grid_spec,
        out_shape=jax.ShapeDtypeStruct((cap, D_MODEL), F32),
        compiler_params=_cparams("arbitrary"),
        name="expert",
    )(bexp, nused, xs, w1, w3, w2)


def _combine_kernel(tab_ref, ntab_ref, lpos_ref, lposr_ref, wr_ref, x_ref, lg_ref, lb_ref, yb_ref, o_ref, stage_ref,
                    sems,
                    *, tc):
    i = pl.program_id(0)
    last = pl.num_programs(0) - 1
    slot = lax.rem(i, 2)

    def run_copy(sl, src, dst, rows=RUN_ALIGN):
        return pltpu.make_async_copy(yb_ref.at[pl.ds(src, rows), :],
                                     stage_ref.at[sl, pl.ds(dst, rows), :], sems.at[sl])

    @pl.when(i == 0)
    def _():
        stage_ref[...] = jnp.zeros_like(stage_ref)
        _run_pieces(tab_ref, lambda src, dst: run_copy(slot, src, dst).start())

    @pl.when(i < last)
    def _():
        _run_pieces(ntab_ref, lambda src, dst: run_copy(1 - slot, src, dst).start())

    _wait_pieces(tab_ref, lambda rows: run_copy(slot, 0, 0, rows))
    rows = stage_ref.shape[1]
    r = lax.broadcasted_iota(I32, (rows, tc), 0)
    row_w = jnp.sum(jnp.where(r == lposr_ref[0:1, :], wr_ref[0:1, :], 0.0)
                    + jnp.where(r == lposr_ref[1:2, :], wr_ref[1:2, :], 0.0), axis=1, keepdims=True)
    staged = (stage_ref[slot] * row_w).astype(BF)
    c = lax.broadcasted_iota(I32, (tc, rows), 1)
    pick = jnp.logical_or(c == lpos_ref[:, 0:1], c == lpos_ref[:, 1:2]).astype(BF)
    ffn = jnp.dot(pick, staged, preferred_element_type=F32)
    o_ref[...] = _layer_norm(DN_ALPHA * x_ref[...] + ffn, lg_ref[...], lb_ref[...])


def _combine(gtab, lpos, wts, x1, lg, lb, yb, *, tc):
    t = x1.shape[0]
    n_tiles = t // tc
    full = lambda a: pl.BlockSpec(a.shape, lambda i: (0,) * a.ndim)
    smem_tab = lambda imap: pl.BlockSpec((1, SUBLANES, LANES), imap, memory_space=pltpu.SMEM)
    return pl.pallas_call(
        functools.partial(_combine_kernel, tc=tc),
        grid=(n_tiles,),
        in_specs=[smem_tab(lambda i: (i, 0, 0)), smem_tab(lambda i: (jnp.minimum(i + 1, n_tiles - 1), 0, 0)),
                  pl.BlockSpec((tc, MOE_TOPK), lambda i: (i, 0)),
                  pl.BlockSpec((MOE_TOPK, tc), lambda i: (0, i)),
                  pl.BlockSpec((MOE_TOPK, tc), lambda i: (0, i)),
                  pl.BlockSpec((tc, D_MODEL), lambda i: (i, 0)),
                  full(lg), full(lb),
                  pl.BlockSpec(memory_space=pl.ANY)],
        out_specs=pl.BlockSpec((tc, D_MODEL), lambda i: (i, 0)),
        out_shape=jax.ShapeDtypeStruct((t, D_MODEL), F32),
        scratch_shapes=[pltpu.VMEM((2, _stage_rows(tc), D_MODEL), F32), pltpu.SemaphoreType.DMA((2,))],
        compiler_params=_cparams("arbitrary"),
        name="combine",
    )(gtab, gtab, lpos.T, lpos, wts, x1, lg, lb, yb)


def _tile(n, want):
    while n % want:
        want //= 2
    return want


def _mixer(x2d, b, l, w_in, rel_bias, conv_w, a_log, dt_bias, norm_g, pool_w, pool_scale,
           w_br_a, w_br_b, w_br_c, gate_b, w_out, ln_g, ln_b, wr, rb):
    t = x2d.shape[0]
    c0 = 3 * A_WIDTH
    c1 = c0 + 3 * B_WIDTH
    c2 = c1 + 2 * B_HEADS
    c3 = c2 + B_WIDTH
    c4 = c3 + C_WIDTH
    wbf = w_in.astype(BF)
    wbd = jnp.pad(wbf[:, c1:c2], ((0, 0), (0, LANES - 2 * B_HEADS)))
    qkva, qkvb, bd, go, oc = _proj(
        x2d, wbf[:, :c0], wbf[:, c0:c1], wbd, wbf[:, c2:c3], wbf[:, c3:c4],
        pool_w.astype(BF), pool_scale.reshape(1, C_WIDTH), seq=l, tm=_tile(l, DENSE_TILE))

    oa = _attn(qkva.reshape(b, l, 3 * A_WIDTH), _attn_bias_table(rel_bias)).reshape(t, A_WIDTH)

    lane_pad = lambda v: jnp.pad(v.reshape(1, B_HEADS), ((0, 0), (B_HEADS, LANES - 2 * B_HEADS)))
    ob = _gdn(qkvb.reshape(b, l, 3 * B_WIDTH), bd.reshape(b, l, LANES), go.reshape(b, l, B_WIDTH),
              conv_w, lane_pad(a_log), lane_pad(dt_bias), norm_g.reshape(1, B_HEAD_DIM)).reshape(t, B_WIDTH)

    wrh, wrl = _split_bf16(wr)
    return _merge(x2d, oa, ob, oc, wbf[:, c4:], gate_b, w_br_a.astype(BF), w_br_b.astype(BF),
                  w_br_c.astype(BF), w_out.astype(BF), ln_g.reshape(1, -1), ln_b.reshape(1, -1),
                  wrh, wrl, rb, tm=_tile(t, DENSE_TILE))


def _moe(x1, wts, lpos, tab, cnt, w1, w3, w2, ln_g, ln_b):
    t = x1.shape[0]
    tile = MOE_TILE
    max_rows = t * MOE_TOPK + (t // tile) * N_EXPERTS * (RUN_ALIGN - 1)
    n_blocks = -(-max_rows // EXPERT_BLOCK) + N_EXPERTS
    cap = n_blocks * EXPERT_BLOCK
    gtab, ttab, bexp, nused = _layout(cnt, tab, n_blocks=n_blocks)
    xs = _dispatch(gtab, ttab, lpos, x1, td=tile, cap=cap)
    yb = _expert(bexp.reshape(n_blocks), nused[0, :1], xs, w1.astype(BF), w3.astype(BF), w2.astype(BF))
    return _combine(gtab, lpos, wts, x1, ln_g.reshape(1, -1), ln_b.reshape(1, -1), yb, tc=tile)


def kernel(x, w_in, attn_rel_bias, gdn_conv_w, gdn_a_log, gdn_dt_bias, gdn_norm_g, pool_w, pool_scale,
           w_branch_a, w_branch_b, w_branch_c, gate_b, w_out, ln1_g, ln1_b, router_group_w,
           router_group_b, router_expert_w, router_expert_b, moe_w1, moe_w3, moe_w2, ln2_g, ln2_b):
    b, l, d = x.shape
    x2d = x.reshape(b * l, d)
    for i in range(DEPTH):
        wr = jnp.concatenate([router_expert_w[i].T, router_group_w[i].T,
                              jnp.zeros((ROUTER_ROWS - N_EXPERTS - MOE_GROUPS, d), F32)], axis=0)
        rb = jnp.concatenate([router_expert_b[i], router_group_b[i],
                              jnp.zeros((ROUTER_ROWS - N_EXPERTS - MOE_GROUPS,), F32)]).reshape(ROUTER_ROWS, 1)
        x1, *routing = _mixer(x2d, b, l, w_in[i], attn_rel_bias[i], gdn_conv_w[i], gdn_a_log[i], gdn_dt_bias[i],
                         gdn_norm_g[i], pool_w[i], pool_scale[i], w_branch_a[i], w_branch_b[i],
                         w_branch_c[i], gate_b[i], w_out[i], ln1_g[i], ln1_b[i], wr, rb)
        x2d = _moe(x1, *routing, moe_w1[i], moe_w3[i], moe_w2[i], ln2_g[i], ln2_b[i])
    return x2d.reshape(b, l, d)
```

```python
import functools

import jax
import jax.numpy as jnp
from jax import lax
from jax.experimental import pallas as pl
from jax.experimental.pallas import tpu as pltpu

BF = jnp.bfloat16
F32 = jnp.float32
I32 = jnp.int32

D_MODEL = 1024
DEPTH = 2
CHUNK = 64

A_HEADS = 8
A_HEAD_DIM = 64
A_WIDTH = A_HEADS * A_HEAD_DIM
A_LEFT_CHUNKS = 8
A_REL_CLIP = 256

B_HEADS = 4
B_HEAD_DIM = 128
B_WIDTH = B_HEADS * B_HEAD_DIM
B_CONV = 4

C_WINDOWS = (2, 4, 8, 16)
C_GROUPS = 4
C_GROUP_DIM = 128
C_WIDTH = C_GROUPS * C_GROUP_DIM

N_BRANCH = 3
MOE_GROUPS = 4
MOE_PER_GROUP = 8
N_EXPERTS = MOE_GROUPS * MOE_PER_GROUP
MOE_TOPK = 2
MOE_FF = 512

DN_ALPHA = (2 * DEPTH) ** 0.25
LN_EPS = 1e-5
RMS_EPS = 1e-6

LANES = 128
SUBLANES = 8
VMEM_LIMIT_BYTES = 56 * 1024 * 1024

QBLOCK = 2 * CHUNK
KBAND = (A_LEFT_CHUNKS + 2) * CHUNK
NEG_BIG = -1e30

DENSE_TILE = 1024
ROUTER_ROWS = 40
EXPERT_BLOCK = 512
MOE_TILE = 512
RUN_ALIGN = 8


def _cparams(*sem):
    return pltpu.CompilerParams(dimension_semantics=sem, vmem_limit_bytes=VMEM_LIMIT_BYTES)


def _split_bf16(a):
    hi = a.astype(BF)
    lo = (a - hi.astype(F32)).astype(BF)
    return hi, lo


def _sigmoid(x):
    return 1.0 / (1.0 + jnp.exp(-x))


def _silu(x):
    return x * _sigmoid(x)


def _layer_norm(z, g, b):
    mu = jnp.mean(z, axis=-1, keepdims=True)
    zc = z - mu
    var = jnp.mean(zc * zc, axis=-1, keepdims=True)
    return zc * lax.rsqrt(var + LN_EPS) * g + b


POOL_TAIL = 16


def _proj_kernel(x_ref, wa_ref, wb_ref, wbd_ref, wgo_ref, wc_ref, pw_ref, ps_ref,
                 qkva_ref, qkvb_ref, bd_ref, go_ref, oc_ref, tail_ref, *, tiles_per_seq, tm):
    it = pl.program_id(0) % tiles_per_seq
    xb = x_ref[...].astype(BF)
    qkva_ref[...] = jnp.dot(xb, wa_ref[...], preferred_element_type=F32).astype(BF)
    qkvb_ref[...] = jnp.dot(xb, wb_ref[...], preferred_element_type=F32).astype(BF)
    bd_ref[...] = jnp.dot(xb, wbd_ref[...], preferred_element_type=F32)
    go_ref[...] = jnp.dot(xb, wgo_ref[...], preferred_element_type=F32).astype(BF)
    u = jnp.dot(xb, wc_ref[...], preferred_element_type=F32)

    @pl.when(it == 0)
    def _():
        tail_ref[...] = jnp.zeros_like(tail_ref)

    ext = jnp.concatenate([tail_ref[...], u], axis=0)
    tail_ref[...] = u[tm - POOL_TAIL:, :]
    pos = (it * tm + 1 + lax.broadcasted_iota(I32, (tm, 1), 0)).astype(F32)
    s = ext
    for gi, win in enumerate(C_WINDOWS):
        if gi:
            s = s[:, C_GROUP_DIM:]
        s = s + pltpu.roll(s, win // 2, 0)
        lo_c, hi_c = gi * C_GROUP_DIM, (gi + 1) * C_GROUP_DIM
        cnt = jnp.minimum(pos, float(win))
        pooled = s[POOL_TAIL:, :C_GROUP_DIM] / cnt - u[:, lo_c:hi_c]
        mixed = jnp.dot(pooled.astype(BF), pw_ref[gi], preferred_element_type=F32)
        oc_ref[:, lo_c:hi_c] = (mixed * ps_ref[:, lo_c:hi_c]).astype(BF)


def _proj(x2d, wa, wb, wbd, wgo, wc, pw, ps, *, seq, tm):
    t = x2d.shape[0]
    full = lambda a: pl.BlockSpec(a.shape, lambda i: (0,) * a.ndim)
    row = lambda n: pl.BlockSpec((tm, n), lambda i: (i, 0))
    return pl.pallas_call(
        functools.partial(_proj_kernel, tiles_per_seq=seq // tm, tm=tm),
        grid=(t // tm,),
        in_specs=[row(D_MODEL), full(wa), full(wb), full(wbd), full(wgo), full(wc), full(pw), full(ps)],
        out_specs=[row(3 * A_WIDTH), row(3 * B_WIDTH), row(LANES), row(B_WIDTH), row(C_WIDTH)],
        out_shape=[jax.ShapeDtypeStruct((t, 3 * A_WIDTH), BF),
                   jax.ShapeDtypeStruct((t, 3 * B_WIDTH), BF),
                   jax.ShapeDtypeStruct((t, LANES), F32),
                   jax.ShapeDtypeStruct((t, B_WIDTH), BF),
                   jax.ShapeDtypeStruct((t, C_WIDTH), BF)],
        scratch_shapes=[pltpu.VMEM((POOL_TAIL, C_WIDTH), F32)],
        compiler_params=_cparams("arbitrary"),
        name="proj",
    )(x2d, wa, wb, wbd, wgo, wc, pw, ps)


def _attn_bias_table(rel_bias):
    h = rel_bias.shape[0]
    pad = A_LEFT_CHUNKS * CHUNK
    width = KBAND + pad
    n = width + QBLOCK - 1
    far = jnp.repeat(rel_bias[:, -1:], QBLOCK - 1 + pad - A_REL_CLIP, axis=1)
    near = jnp.repeat(rel_bias[:, :1], n - far.shape[1] - rel_bias.shape[1], axis=1)
    w = jnp.concatenate([far, rel_bias[:, ::-1], near], axis=1).astype(F32)
    w = jnp.roll(w, -(QBLOCK - 1), axis=1)
    toep = jnp.tile(w, (1, QBLOCK))[:, :QBLOCK * (n - 1)].reshape(h, QBLOCK, n - 1)[:, :, :width]
    r = jnp.arange(QBLOCK)[:, None]
    m = jnp.arange(width)[None, :]
    kc, qc = m // CHUNK, r // CHUNK
    valid = (kc >= qc) & (kc <= qc + A_LEFT_CHUNKS)
    tab = jnp.where(valid[None], toep, NEG_BIG)
    return tab.reshape(h, QBLOCK, width // LANES, LANES).transpose(0, 2, 1, 3)


ATTN_QBLOCKS_PER_STEP = 4


def _attn_kernel(q_ref, k_ref, v_ref, tab_ref, o_ref):
    first_blk = A_LEFT_CHUNKS * CHUNK // QBLOCK
    lane = lax.broadcasted_iota(I32, (QBLOCK, LANES), 1)
    lo = lane < A_HEAD_DIM
    nkb = KBAND // LANES
    items = []
    for sub in range(ATTN_QBLOCKS_PER_STEP):
        j = pl.program_id(1) * ATTN_QBLOCKS_PER_STEP + sub
        kstart = pl.multiple_of(jnp.maximum(j - first_blk, 0) * QBLOCK, QBLOCK)
        offb = jnp.maximum(first_blk - j, 0)
        for hp in range(A_HEADS // 2):
            items.append((slice(sub * QBLOCK, (sub + 1) * QBLOCK), kstart, offb, hp,
                          slice(hp * LANES, (hp + 1) * LANES)))
    scores = []
    for rows, kstart, offb, hp, cols in items:
        qp = q_ref[0, rows, cols] * jnp.asarray(A_HEAD_DIM ** -0.5, BF)
        zero = jnp.zeros_like(qp)
        q2 = jnp.concatenate([jnp.where(lo, qp, zero), jnp.where(lo, zero, qp)], axis=0)
        kp = k_ref[0, pl.ds(kstart, KBAND), cols]
        scores.append(lax.dot_general(q2, kp, (((1,), (1,)), ((), ())), preferred_element_type=F32))
    probs, denoms = [], []
    for (rows, kstart, offb, hp, cols), sc in zip(items, scores):
        bias = jnp.concatenate(
            [jnp.concatenate([tab_ref[2 * hp + par, offb + i] for i in range(nkb)], axis=1) for par in range(2)],
            axis=0)
        s = sc + bias
        p = jnp.exp(s - jnp.max(s, axis=-1, keepdims=True))
        denoms.append(jnp.sum(p, axis=-1, keepdims=True))
        probs.append(p.astype(BF))
    for (rows, kstart, offb, hp, cols), p, den in zip(items, probs, denoms):
        vp = v_ref[0, pl.ds(kstart, KBAND), cols]
        o2 = jnp.dot(p, vp, preferred_element_type=F32) / den
        o_ref[0, rows, cols] = jnp.where(lo, o2[:QBLOCK], o2[QBLOCK:]).astype(BF)


def _attn(qkva, tab):
    b, l, _ = qkva.shape
    step_rows = ATTN_QBLOCKS_PER_STEP * QBLOCK
    return pl.pallas_call(
        _attn_kernel,
        grid=(b, l // step_rows),
        in_specs=[pl.BlockSpec((1, step_rows, A_WIDTH), lambda bi, j: (bi, j, 0)),
                  pl.BlockSpec((1, l, A_WIDTH), lambda bi, j: (bi, 0, 1)),
                  pl.BlockSpec((1, l, A_WIDTH), lambda bi, j: (bi, 0, 2)),
                  pl.BlockSpec(tab.shape, lambda bi, j: (0, 0, 0, 0))],
        out_specs=pl.BlockSpec((1, step_rows, A_WIDTH), lambda bi, j: (bi, j, 0)),
        out_shape=jax.ShapeDtypeStruct((b, l, A_WIDTH), BF),
        compiler_params=_cparams("arbitrary", "arbitrary"),
        name="attn",
    )(qkva, qkva, qkva, tab)


CONV_HIST = 16
GDN_PREP_UNROLL = 4


def _gdn_kernel(qkv_ref, bd_ref, go_ref, cw_ref, alog_ref, dtb_ref, ng_ref, o_ref,
                u_ref, wq_ref, qkkd_ref, egl_ref, state_ref, *, n_chunks):
    rt = lax.broadcasted_iota(I32, (CHUNK, CHUNK), 0)
    ct = lax.broadcasted_iota(I32, (CHUNK, CHUNK), 1)
    tril = rt >= ct
    strict = rt > ct
    row_c = lax.broadcasted_iota(I32, (CHUNK, LANES), 0)
    eye_f = (rt == ct).astype(F32)
    neg_rate = -jnp.exp(alog_ref[...])
    sr = lax.broadcasted_iota(I32, (B_CONV * CHUNK, CONV_HIST + CHUNK), 0)
    sc = lax.broadcasted_iota(I32, (B_CONV * CHUNK, CONV_HIST + CHUNK), 1)
    shift_sel = (sc == CONV_HIST + jnp.bitwise_and(sr, CHUNK - 1)
                 - jnp.right_shift(sr, CHUNK.bit_length() - 1)).astype(BF)

    def heads(fn):
        return jnp.concatenate([fn(h, slice(h * B_HEAD_DIM, (h + 1) * B_HEAD_DIM)) for h in range(B_HEADS)],
                               axis=1)

    def prep(c):
        r0 = pl.multiple_of(c * CHUNK, CHUNK)
        cur = qkv_ref[0, pl.ds(r0, CHUNK), :]
        h0 = pl.multiple_of(jnp.maximum(r0 - CONV_HIST, 0), CONV_HIST)
        hist = qkv_ref[0, pl.ds(h0, CONV_HIST), :]
        hist = jnp.where(c > 0, hist, jnp.zeros_like(hist))
        taps = jnp.dot(shift_sel, jnp.concatenate([hist, cur], axis=0), preferred_element_type=F32)
        acc = taps[:CHUNK] * cw_ref[B_CONV - 1:B_CONV, :]
        for back in range(1, B_CONV):
            tap = B_CONV - 1 - back
            acc = acc + taps[back * CHUNK:(back + 1) * CHUNK] * cw_ref[tap:tap + 1, :]
        act = _silu(acc)
        q_all, k_all, v_all = act[:, :B_WIDTH], act[:, B_WIDTH:2 * B_WIDTH], act[:, 2 * B_WIDTH:]

        bd = bd_ref[0, pl.ds(r0, CHUNK), :]
        beta_all = _sigmoid(bd)
        z = bd + dtb_ref[...]
        softplus = jnp.maximum(z, 0.0) + jnp.log1p(jnp.exp(-jnp.abs(z)))
        g_all = neg_rate * softplus
        gc_all = g_all
        for sh in (1, 2, 4, 8, 16, 32):
            gc_all = gc_all + jnp.where(row_c >= sh, pltpu.roll(gc_all, sh, 0), 0.0)
        egc_all = jnp.exp(gc_all)
        glast = gc_all[CHUNK - 1:CHUNK, :]
        ekd_all = jnp.exp(glast - gc_all)
        egl = jnp.exp(glast)
        col = lambda m, h: m[:, B_HEADS + h:B_HEADS + h + 1]

        def l2n(m):
            return m * lax.rsqrt(jnp.sum(m * m, axis=-1, keepdims=True) + RMS_EPS)

        qn = heads(lambda h, hs: l2n(q_all[:, hs]) * (B_HEAD_DIM ** -0.5))
        kn = heads(lambda h, hs: l2n(k_all[:, hs]))
        kbeta = heads(lambda h, hs: kn[:, hs] * beta_all[:, h:h + 1])
        vbeta = heads(lambda h, hs: v_all[:, hs] * beta_all[:, h:h + 1])
        kbe = heads(lambda h, hs: kbeta[:, hs] * col(egc_all, h))
        qe = heads(lambda h, hs: qn[:, hs] * col(egc_all, h))
        chains = []
        for h in range(B_HEADS):
            hs = slice(h * B_HEAD_DIM, (h + 1) * B_HEAD_DIM)
            gcc = jnp.broadcast_to(col(gc_all, h), (CHUNK, CHUNK))
            gcr = jnp.sum(gcc * eye_f, axis=0, keepdims=True)
            chains.append(dict(
                c=c, h=h, hs=hs,
                decay=jnp.exp(jnp.where(tril, gcc - gcr, -jnp.inf)),
                kq=jnp.concatenate([kbeta[:, hs], qn[:, hs]], axis=0).astype(BF),
                kn=kn[:, hs].astype(BF),
                rhs=jnp.concatenate([vbeta[:, hs], kbe[:, hs]], axis=1).astype(BF),
                kd=kn[:, hs] * col(ekd_all, h)))
        egl_row = heads(lambda h, hs: jnp.broadcast_to(col(egl, h), (1, B_HEAD_DIM)))
        return chains, qe.astype(BF), egl_row

    nt = lambda a, b: lax.dot_general(a, b, (((1,), (1,)), ((), ())), preferred_element_type=F32)
    mm = lambda a, b: jnp.dot(a, b, preferred_element_type=F32)

    def prep_levels(i):
        fronts = [prep(i * GDN_PREP_UNROLL + g) for g in range(GDN_PREP_UNROLL)]
        ch = [x for f in fronts for x in f[0]]
        aq = [nt(x["kq"], x["kn"]) for x in ch]
        kdt = [x["kd"].T for x in ch]
        yield
        qk = [(a[CHUNK:] * x["decay"]).astype(BF) for x, a in zip(ch, aq)]
        pw = [-jnp.where(strict, a[:CHUNK] * x["decay"], 0.0) for x, a in zip(ch, aq)]
        tmat = [eye_f + p for p in pw]
        pwb = [p.astype(BF) for p in pw]
        pw = [mm(p, p) for p in pwb]
        yield
        n_sq = CHUNK.bit_length() - 2
        for lvl in range(n_sq):
            pwb = [p.astype(BF) for p in pw]
            if lvl + 1 < n_sq:
                both = [mm(jnp.concatenate([t.astype(BF), p], axis=0), p) for t, p in zip(tmat, pwb)]
                yield
                tmat = [t + r[:CHUNK] for t, r in zip(tmat, both)]
                pw = [r[CHUNK:] for r in both]
            else:
                last = [mm(t.astype(BF), p) for t, p in zip(tmat, pwb)]
                yield
                tmat = [t + r for t, r in zip(tmat, last)]
        uw = [mm(t.astype(BF), x["rhs"]) for t, x in zip(tmat, ch)]
        yield
        for x, r, qkx, kt in zip(ch, uw, qk, kdt):
            u_ref[x["c"], :, x["hs"]] = r[:, :B_HEAD_DIM]
            wq_ref[x["c"], :CHUNK, x["hs"]] = r[:, B_HEAD_DIM:].astype(BF)
            qkkd_ref[x["c"], x["h"], :CHUNK, :] = qkx
            qkkd_ref[x["c"], x["h"], CHUNK:, :] = kt.astype(BF)
        for g, (_, qe_b, egl_row) in enumerate(fronts):
            wq_ref[i * GDN_PREP_UNROLL + g, CHUNK:, :] = qe_b
            egl_ref[i * GDN_PREP_UNROLL + g] = egl_row

    def step_levels(c):
        r0 = pl.multiple_of(c * CHUNK, CHUNK)
        st = state_ref[...]
        stb = st.astype(BF)
        hss = [slice(h * B_HEAD_DIM, (h + 1) * B_HEAD_DIM) for h in range(B_HEADS)]
        r1 = [mm(wq_ref[c, :, hs], stb[:, hs]) for hs in hss]
        yield
        v_new = [(u_ref[c, :, hs] - r[:CHUNK]).astype(BF) for hs, r in zip(hss, r1)]
        r2 = [mm(qkkd_ref[c, h], v_new[h]) for h in range(B_HEADS)]
        yield
        o_all = jnp.concatenate([a[CHUNK:] + b[:CHUNK] for a, b in zip(r1, r2)], axis=1)
        state_ref[...] = st * egl_ref[c] + jnp.concatenate([b[CHUNK:] for b in r2], axis=1)
        gate = go_ref[0, pl.ds(r0, CHUNK), :].astype(F32)
        o_n = heads(lambda h, hs: o_all[:, hs]
                    * lax.rsqrt(jnp.mean(o_all[:, hs] * o_all[:, hs], axis=-1, keepdims=True) + RMS_EPS)
                    * ng_ref[...])
        o_ref[0, pl.ds(r0, CHUNK), :] = (o_n * _silu(gate)).astype(BF)

    def steps_of(i):
        for g in range(GDN_PREP_UNROLL):
            yield from step_levels(i * GDN_PREP_UNROLL + g)

    def run_interleaved(*gens):
        live = list(gens)
        while live:
            for g in list(live):
                if next(g, StopIteration) is StopIteration:
                    live.remove(g)

    n_groups = n_chunks // GDN_PREP_UNROLL
    state_ref[...] = jnp.zeros_like(state_ref)
    run_interleaved(prep_levels(0))

    def body(i, carry):
        run_interleaved(steps_of(i - 1), prep_levels(i))
        return carry

    lax.fori_loop(1, n_groups, body, 0)
    run_interleaved(steps_of(n_groups - 1))


def _gdn(qkvb, bd, go, cw, alog, dtb, ng):
    b, l, _ = qkvb.shape
    nc = l // CHUNK
    full = lambda a: pl.BlockSpec(a.shape, lambda bi: (0,) * a.ndim)
    seq = lambda n: pl.BlockSpec((1, l, n), lambda bi: (bi, 0, 0))
    return pl.pallas_call(
        functools.partial(_gdn_kernel, n_chunks=nc),
        grid=(b,),
        in_specs=[seq(3 * B_WIDTH), seq(LANES), seq(B_WIDTH), full(cw), full(alog), full(dtb), full(ng)],
        out_specs=seq(B_WIDTH),
        out_shape=jax.ShapeDtypeStruct((b, l, B_WIDTH), BF),
        scratch_shapes=[pltpu.VMEM((nc, CHUNK, B_WIDTH), F32),
                        pltpu.VMEM((nc, 2 * CHUNK, B_WIDTH), BF),
                        pltpu.VMEM((nc, B_HEADS, CHUNK + B_HEAD_DIM, CHUNK), BF),
                        pltpu.VMEM((nc, 1, B_WIDTH), F32),
                        pltpu.VMEM((B_HEAD_DIM, B_WIDTH), F32)],
        compiler_params=_cparams("arbitrary"),
        name="gdn",
    )(qkvb, bd, go, cw, alog, dtb, ng)


def _merge_kernel(x_ref, oa_ref, ob_ref, oc_ref, wg_ref, gb_ref, wa_ref, wb_ref, wc_ref, wo_ref,
                  lg_ref, lb_ref, wrh_ref, wrl_ref, rb_ref, x1_ref, wts_ref, lpos_ref, tab_ref, cnt_ref, run_ref):
    x = x_ref[...]
    xb = x.astype(BF)
    y = None
    for i, (o_ref, w_ref) in enumerate(((oa_ref, wa_ref), (ob_ref, wb_ref), (oc_ref, wc_ref))):
        gl = jnp.dot(xb, wg_ref[:, i * D_MODEL:(i + 1) * D_MODEL], preferred_element_type=F32)
        gate = _sigmoid(gl + gb_ref[i:i + 1, :])
        br = gate * jnp.dot(o_ref[...], w_ref[...], preferred_element_type=F32)
        y = br if y is None else y + br
    mix = jnp.dot(y.astype(BF), wo_ref[...], preferred_element_type=F32)
    x1 = _layer_norm(DN_ALPHA * x + mix, lg_ref[...], lb_ref[...])
    x1_ref[...] = x1
    xh, xl = _split_bf16(x1)
    nt = lambda a, b: lax.dot_general(a, b, (((1,), (1,)), ((), ())), preferred_element_type=F32)
    lgt = nt(wrh_ref[...], xh) + (nt(wrh_ref[...], xl) + nt(wrl_ref[...], xh)) + rb_ref[...]

    @pl.when(pl.program_id(0) == 0)
    def _():
        run_ref[...] = jnp.zeros_like(run_ref)

    run = run_ref[...]
    for s in range(lgt.shape[1] // MOE_TILE):
        cols = slice(s * MOE_TILE, (s + 1) * MOE_TILE)
        wts_ref[:, cols], lpos_ref[:, cols], tab_ref[s], run = _route_tile(lgt[:, cols], run)
    run_ref[...] = run
    cnt_ref[...] = jnp.broadcast_to(run, cnt_ref.shape)


def _merge(x2d, oa, ob, oc, wg, gb, wa, wb, wc, wo, lg, lb, wrh, wrl, rb, *, tm):
    t = x2d.shape[0]
    full = lambda a: pl.BlockSpec(a.shape, lambda i: (0,) * a.ndim)
    row = lambda n: pl.BlockSpec((tm, n), lambda i: (i, 0))
    tok = lambda: pl.BlockSpec((MOE_TOPK, tm), lambda i: (0, i))
    assert tm % MOE_TILE == 0, (tm, MOE_TILE)
    return pl.pallas_call(
        _merge_kernel,
        grid=(t // tm,),
        in_specs=[row(D_MODEL), row(A_WIDTH), row(B_WIDTH), row(C_WIDTH), full(wg), full(gb), full(wa),
                  full(wb), full(wc), full(wo), full(lg), full(lb), full(wrh), full(wrl), full(rb)],
        out_specs=[row(D_MODEL), tok(), tok(), pl.BlockSpec((tm // MOE_TILE, SUBLANES, LANES), lambda i: (i, 0, 0)),
                   pl.BlockSpec((N_EXPERTS, LANES), lambda i: (0, 0))],
        out_shape=[jax.ShapeDtypeStruct((t, D_MODEL), F32),
                   jax.ShapeDtypeStruct((MOE_TOPK, t), F32),
                   jax.ShapeDtypeStruct((MOE_TOPK, t), I32),
                   jax.ShapeDtypeStruct((t // MOE_TILE, SUBLANES, LANES), I32),
                   jax.ShapeDtypeStruct((N_EXPERTS, LANES), F32)],
        scratch_shapes=[pltpu.VMEM((N_EXPERTS, 1), F32)],
        compiler_params=_cparams("arbitrary"),
        name="merge",
    )(x2d, oa, ob, oc, wg, gb, wa, wb, wc, wo, lg, lb, wrh, wrl, rb)


def _route_tile(lgt, run):
    tn = lgt.shape[1]
    le = lgt[0:N_EXPERTS, :]
    lg = lgt[N_EXPERTS:N_EXPERTS + MOE_GROUPS, :]
    gi = lax.broadcasted_iota(I32, (MOE_GROUPS, tn), 0).astype(F32)
    ei = lax.broadcasted_iota(I32, (N_EXPERTS, tn), 0).astype(F32)
    eg = jnp.right_shift(lax.broadcasted_iota(I32, (N_EXPERTS, tn), 0),
                         MOE_PER_GROUP.bit_length() - 1).astype(F32)
    mg = jnp.max(lg, axis=0, keepdims=True)
    gsel = jnp.min(jnp.where(lg == mg, gi, float(MOE_GROUPS)), axis=0, keepdims=True)
    p_top = 1.0 / jnp.sum(jnp.exp(lg - mg), axis=0, keepdims=True)
    l1 = jnp.where(eg == gsel, le, -jnp.inf)
    m1 = jnp.max(l1, axis=0, keepdims=True)
    i1 = jnp.min(jnp.where(l1 == m1, ei, float(N_EXPERTS)), axis=0, keepdims=True)
    l2 = jnp.where(ei == i1, -jnp.inf, l1)
    m2 = jnp.max(l2, axis=0, keepdims=True)
    i2 = jnp.min(jnp.where(l2 == m2, ei, float(N_EXPERTS)), axis=0, keepdims=True)
    e2 = jnp.exp(m2 - m1)
    den = 1.0 + e2
    wts = jnp.concatenate([p_top / den, p_top * (e2 / den)], axis=0)

    before = (lax.broadcasted_iota(I32, (tn, tn), 0) < lax.broadcasted_iota(I32, (tn, tn), 1)).astype(BF)
    oh = [ei == ik for ik in (i1, i2)]
    prefix = [jnp.dot(o.astype(BF), before, preferred_element_type=F32) for o in oh]
    cnt_k = [jnp.sum(o.astype(F32), axis=1, keepdims=True) for o in oh]
    cnt = cnt_k[0] + cnt_k[1]
    cnt_al = jnp.floor((cnt + (RUN_ALIGN - 1)) * (1.0 / RUN_ALIGN)) * RUN_ALIGN
    er = lax.broadcasted_iota(I32, (N_EXPERTS, N_EXPERTS), 0)
    ec = lax.broadcasted_iota(I32, (N_EXPERTS, N_EXPERTS), 1)
    lstart = jnp.dot((ec < er).astype(BF), jnp.broadcast_to(cnt_al, (N_EXPERTS, LANES)).astype(BF),
                     preferred_element_type=F32)[:, 0:1]
    first = [0.0, cnt_k[0]]
    pick = lambda k: jnp.sum(jnp.where(oh[k], prefix[k] + first[k] + lstart, 0.0), axis=0, keepdims=True)
    lpos = jnp.concatenate([pick(0), pick(1)], axis=0).astype(I32)
    eye_l = (lax.broadcasted_iota(I32, (N_EXPERTS, LANES), 0) == lax.broadcasted_iota(I32, (N_EXPERTS, LANES), 1))
    as_row = lambda colv: jnp.sum(jnp.where(eye_l, colv, 0.0), axis=0, keepdims=True)
    pieces = jnp.broadcast_to(jnp.sum(cnt_al, axis=0, keepdims=True) * (1.0 / RUN_ALIGN), (1, LANES))
    tab = jnp.concatenate([as_row(run), as_row(cnt), as_row(lstart), pieces,
                           jnp.zeros((SUBLANES - 4, LANES), F32)], axis=0).astype(I32)
    return wts, lpos, tab, run + cnt_al


def _layout_kernel(cnt_ref, tab_ref, gtab_ref, ttab_ref, bexp_ref, nused_ref, *, n_blocks):
    cnt = cnt_ref[...].astype(I32)
    shift = EXPERT_BLOCK.bit_length() - 1
    padded_blocks = jnp.right_shift(cnt + (EXPERT_BLOCK - 1), shift)
    er = lax.broadcasted_iota(I32, (N_EXPERTS, LANES), 0)
    el = lax.broadcasted_iota(I32, (N_EXPERTS, LANES), 1)
    start = jnp.zeros((N_EXPERTS, LANES), I32)
    for e in range(N_EXPERTS - 1):
        start = start + jnp.where(er > e, padded_blocks[e:e + 1, :], 0)
    end = start + padded_blocks
    as_row = lambda m: jnp.sum(jnp.where(er == el, m, 0), axis=0, keepdims=True)
    tab = tab_ref[...]
    first_row = lax.broadcasted_iota(I32, tab.shape, 1) == 0
    gtab_ref[...] = tab + jnp.where(first_row, as_row(start * EXPERT_BLOCK)[None], 0)
    tail_rows = padded_blocks * EXPERT_BLOCK - cnt
    end_all = end[N_EXPERTS - 1:N_EXPERTS, :]
    spare = lax.broadcasted_iota(I32, (1, LANES), 1) == N_EXPERTS
    ttab_ref[...] = jnp.concatenate(
        [jnp.where(spare, end_all * EXPERT_BLOCK, as_row(start * EXPERT_BLOCK + cnt)),
         jnp.where(spare, n_blocks - end_all, as_row(jnp.right_shift(tail_rows, RUN_ALIGN.bit_length() - 1))),
         jnp.zeros((SUBLANES - 2, LANES), I32)], axis=0)[None]
    blk = lax.broadcasted_iota(I32, (1, n_blocks), 1)
    owner = jnp.zeros((1, n_blocks), I32)
    for e in range(N_EXPERTS):
        owner = owner + (blk >= end[e:e + 1, 0:1]).astype(I32)
    bexp_ref[...] = jnp.minimum(owner, N_EXPERTS - 1)
    nused_ref[...] = end[N_EXPERTS - 1:N_EXPERTS, :]


def _layout(cnt, tab, *, n_blocks):
    full = lambda shape: pl.BlockSpec(shape, lambda i: (0,) * len(shape))
    return pl.pallas_call(
        functools.partial(_layout_kernel, n_blocks=n_blocks),
        grid=(1,),
        in_specs=[full(cnt.shape), full(tab.shape)],
        out_specs=[full(tab.shape), full((1, SUBLANES, LANES)), full((1, n_blocks)), full((1, LANES))],
        out_shape=[jax.ShapeDtypeStruct(tab.shape, I32),
                   jax.ShapeDtypeStruct((1, SUBLANES, LANES), I32),
                   jax.ShapeDtypeStruct((1, n_blocks), I32),
                   jax.ShapeDtypeStruct((1, LANES), I32)],
        compiler_params=_cparams("arbitrary"),
        name="layout",
    )(cnt, tab)


def _run_pieces(tab_ref, fn):
    for e in range(N_EXPERTS):
        sorted0, staging0 = tab_ref[0, 0, e], tab_ref[0, 2, e]
        n_pieces = (tab_ref[0, 1, e] + (RUN_ALIGN - 1)) // RUN_ALIGN

        def one(j, c, sorted0=sorted0, staging0=staging0):
            fn(pl.multiple_of(sorted0 + j * RUN_ALIGN, RUN_ALIGN), pl.multiple_of(staging0 + j * RUN_ALIGN, RUN_ALIGN))
            return c

        lax.fori_loop(0, n_pieces, one, 0)


WAIT_GROUPS = (64, 8, 1)


def _wait_pieces(tab_ref, make_copy):
    total = tab_ref[0, 3, 0]
    counts = (total // WAIT_GROUPS[0], (total // WAIT_GROUPS[1]) % (WAIT_GROUPS[0] // WAIT_GROUPS[1]),
              total % WAIT_GROUPS[1])
    for group, count in zip(WAIT_GROUPS, counts):
        def one(j, c, group=group):
            make_copy(group * RUN_ALIGN).wait()
            return c
        lax.fori_loop(0, count, one, 0)


def _dispatch_kernel(tab_ref, ptab_ref, ttab_ref, lpos_ref, x_ref, xs_ref, stage_ref, zero_ref, sems, tsem,
                     *, td):
    i = pl.program_id(0)
    last = pl.num_programs(0) - 1
    slot = lax.rem(i, 2)
    rows = stage_ref.shape[1]
    r = lax.broadcasted_iota(I32, (rows, td), 0)
    sel = jnp.logical_or(r == lpos_ref[0:1, :], r == lpos_ref[1:2, :]).astype(BF)
    stage_ref[slot] = jnp.dot(sel, x_ref[...].astype(BF), preferred_element_type=F32)

    def run_copy(sl, dst, src, rows=RUN_ALIGN):
        return pltpu.make_async_copy(stage_ref.at[sl, pl.ds(src, rows), :],
                                     xs_ref.at[pl.ds(dst, rows), :], sems.at[sl])

    def tail_copies(fn):
        def per_expert(e, carry):
            def one(j, c):
                dst = pl.multiple_of(ttab_ref[0, 0, e] + j * RUN_ALIGN, RUN_ALIGN)
                fn(pltpu.make_async_copy(zero_ref.at[pl.ds(0, RUN_ALIGN), :],
                                         xs_ref.at[pl.ds(dst, RUN_ALIGN), :], tsem))
                return c
            lax.fori_loop(0, ttab_ref[0, 1, e], one, 0)
            return carry
        lax.fori_loop(0, N_EXPERTS, per_expert, 0)

        def spare_block(j, c):
            dst = pl.multiple_of(ttab_ref[0, 0, N_EXPERTS] + j * EXPERT_BLOCK, EXPERT_BLOCK)
            fn(pltpu.make_async_copy(zero_ref, xs_ref.at[pl.ds(dst, EXPERT_BLOCK), :], tsem))
            return c
        lax.fori_loop(0, ttab_ref[0, 1, N_EXPERTS], spare_block, 0)

    @pl.when(i == 0)
    def _():
        zero_ref[...] = jnp.zeros_like(zero_ref)
        tail_copies(lambda cp: cp.start())

    @pl.when(i > 0)
    def _():
        _wait_pieces(ptab_ref, lambda rows: run_copy(1 - slot, 0, 0, rows))

    _run_pieces(tab_ref, lambda dst, src: run_copy(slot, dst, src).start())

    @pl.when(i == last)
    def _():
        _wait_pieces(tab_ref, lambda rows: run_copy(slot, 0, 0, rows))
        tail_copies(lambda cp: cp.wait())


def _stage_rows(td):
    return MOE_TOPK * td + N_EXPERTS * RUN_ALIGN


def _dispatch(gtab, ttab, lpos, x1, *, td, cap):
    t = x1.shape[0]
    smem_tab = lambda imap: pl.BlockSpec((1, SUBLANES, LANES), imap, memory_space=pltpu.SMEM)
    return pl.pallas_call(
        functools.partial(_dispatch_kernel, td=td),
        grid=(t // td,),
        in_specs=[smem_tab(lambda i: (i, 0, 0)), smem_tab(lambda i: (jnp.maximum(i - 1, 0), 0, 0)),
                  smem_tab(lambda i: (0, 0, 0)),
                  pl.BlockSpec((MOE_TOPK, td), lambda i: (0, i)),
                  pl.BlockSpec((td, D_MODEL), lambda i: (i, 0))],
        out_specs=pl.BlockSpec(memory_space=pl.ANY),
        out_shape=jax.ShapeDtypeStruct((cap, D_MODEL), F32),
        scratch_shapes=[pltpu.VMEM((2, _stage_rows(td), D_MODEL), F32), pltpu.VMEM((EXPERT_BLOCK, D_MODEL), F32),
                        pltpu.SemaphoreType.DMA((2,)), pltpu.SemaphoreType.DMA(())],
        compiler_params=_cparams("arbitrary"),
        name="dispatch",
    )(gtab, gtab, ttab, lpos, x1)


def _expert_kernel(bexp_ref, nused_ref, xs_ref, w1_ref, w3_ref, w2_ref, y_ref):
    del bexp_ref
    i = pl.program_id(0)

    @pl.when(i < nused_ref[0])
    def _():
        xb = xs_ref[...].astype(BF)
        h1 = jnp.dot(xb, w1_ref[0], preferred_element_type=F32)
        h3 = jnp.dot(xb, w3_ref[0], preferred_element_type=F32)
        hid = (_silu(h1) * h3).astype(BF)
        y_ref[...] = jnp.dot(hid, w2_ref[0], preferred_element_type=F32)

    @pl.when(i >= nused_ref[0])
    def _():
        y_ref[...] = jnp.zeros_like(y_ref)


def _expert(bexp, nused, xs, w1, w3, w2):
    cap = xs.shape[0]
    nb = cap // EXPERT_BLOCK
    last = lambda i, nu: jnp.minimum(i, nu[0] - 1)
    grid_spec = pltpu.PrefetchScalarGridSpec(
        num_scalar_prefetch=2,
        grid=(nb,),
        in_specs=[pl.BlockSpec((EXPERT_BLOCK, D_MODEL), lambda i, be, nu: (last(i, nu), 0)),
                  pl.BlockSpec((1, D_MODEL, MOE_FF), lambda i, be, nu: (be[last(i, nu)], 0, 0)),
                  pl.BlockSpec((1, D_MODEL, MOE_FF), lambda i, be, nu: (be[last(i, nu)], 0, 0)),
                  pl.BlockSpec((1, MOE_FF, D_MODEL), lambda i, be, nu: (be[last(i, nu)], 0, 0))],
        out_specs=pl.BlockSpec((EXPERT_BLOCK, D_MODEL), lambda i, be, nu: (i, 0)),
    )
    return pl.pallas_call(
        _expert_kernel,
        grid_spec=grid_spec,
        out_shape=jax.ShapeDtypeStruct((cap, D_MODEL), F32),
        compiler_params=_cparams("arbitrary"),
        name="expert",
    )(bexp, nused, xs, w1, w3, w2)


def _combine_kernel(tab_ref, ntab_ref, lpos_ref, lposr_ref, wr_ref, x_ref, lg_ref, lb_ref, yb_ref, o_ref, stage_ref,
                    sems,
                    *, tc):
    i = pl.program_id(0)
    last = pl.num_programs(0) - 1
    slot = lax.rem(i, 2)

    def run_copy(sl, src, dst, rows=RUN_ALIGN):
        return pltpu.make_async_copy(yb_ref.at[pl.ds(src, rows), :],
                                     stage_ref.at[sl, pl.ds(dst, rows), :], sems.at[sl])

    @pl.when(i == 0)
    def _():
        stage_ref[...] = jnp.zeros_like(stage_ref)
        _run_pieces(tab_ref, lambda src, dst: run_copy(slot, src, dst).start())

    @pl.when(i < last)
    def _():
        _run_pieces(ntab_ref, lambda src, dst: run_copy(1 - slot, src, dst).start())

    _wait_pieces(tab_ref, lambda rows: run_copy(slot, 0, 0, rows))
    rows = stage_ref.shape[1]
    r = lax.broadcasted_iota(I32, (rows, tc), 0)
    row_w = jnp.sum(jnp.where(r == lposr_ref[0:1, :], wr_ref[0:1, :], 0.0)
                    + jnp.where(r == lposr_ref[1:2, :], wr_ref[1:2, :], 0.0), axis=1, keepdims=True)
    staged = (stage_ref[slot] * row_w).astype(BF)
    c = lax.broadcasted_iota(I32, (tc, rows), 1)
    pick = jnp.logical_or(c == lpos_ref[:, 0:1], c == lpos_ref[:, 1:2]).astype(BF)
    ffn = jnp.dot(pick, staged, preferred_element_type=F32)
    o_ref[...] = _layer_norm(DN_ALPHA * x_ref[...] + ffn, lg_ref[...], lb_ref[...])


def _combine(gtab, lpos, wts, x1, lg, lb, yb, *, tc):
    t = x1.shape[0]
    n_tiles = t // tc
    full = lambda a: pl.BlockSpec(a.shape, lambda i: (0,) * a.ndim)
    smem_tab = lambda imap: pl.BlockSpec((1, SUBLANES, LANES), imap, memory_space=pltpu.SMEM)
    return pl.pallas_call(
        functools.partial(_combine_kernel, tc=tc),
        grid=(n_tiles,),
        in_specs=[smem_tab(lambda i: (i, 0, 0)), smem_tab(lambda i: (jnp.minimum(i + 1, n_tiles - 1), 0, 0)),
                  pl.BlockSpec((tc, MOE_TOPK), lambda i: (i, 0)),
                  pl.BlockSpec((MOE_TOPK, tc), lambda i: (0, i)),
                  pl.BlockSpec((MOE_TOPK, tc), lambda i: (0, i)),
                  pl.BlockSpec((tc, D_MODEL), lambda i: (i, 0)),
                  full(lg), full(lb),
                  pl.BlockSpec(memory_space=pl.ANY)],
        out_specs=pl.BlockSpec((tc, D_MODEL), lambda i: (i, 0)),
        out_shape=jax.ShapeDtypeStruct((t, D_MODEL), F32),
        scratch_shapes=[pltpu.VMEM((2, _stage_rows(tc), D_MODEL), F32), pltpu.SemaphoreType.DMA((2,))],
        compiler_params=_cparams("arbitrary"),
        name="combine",
    )(gtab, gtab, lpos.T, lpos, wts, x1, lg, lb, yb)


def _tile(n, want):
    while n % want:
        want //= 2
    return want


def _mixer(x2d, b, l, w_in, rel_bias, conv_w, a_log, dt_bias, norm_g, pool_w, pool_scale,
           w_br_a, w_br_b, w_br_c, gate_b, w_out, ln_g, ln_b, wr, rb):
    t = x2d.shape[0]
    c0 = 3 * A_WIDTH
    c1 = c0 + 3 * B_WIDTH
    c2 = c1 + 2 * B_HEADS
    c3 = c2 + B_WIDTH
    c4 = c3 + C_WIDTH
    wbf = w_in.astype(BF)
    wbd = jnp.pad(wbf[:, c1:c2], ((0, 0), (0, LANES - 2 * B_HEADS)))
    qkva, qkvb, bd, go, oc = _proj(
        x2d, wbf[:, :c0], wbf[:, c0:c1], wbd, wbf[:, c2:c3], wbf[:, c3:c4],
        pool_w.astype(BF), pool_scale.reshape(1, C_WIDTH), seq=l, tm=_tile(l, DENSE_TILE))

    oa = _attn(qkva.reshape(b, l, 3 * A_WIDTH), _attn_bias_table(rel_bias)).reshape(t, A_WIDTH)

    lane_pad = lambda v: jnp.pad(v.reshape(1, B_HEADS), ((0, 0), (B_HEADS, LANES - 2 * B_HEADS)))
    ob = _gdn(qkvb.reshape(b, l, 3 * B_WIDTH), bd.reshape(b, l, LANES), go.reshape(b, l, B_WIDTH),
              conv_w, lane_pad(a_log), lane_pad(dt_bias), norm_g.reshape(1, B_HEAD_DIM)).reshape(t, B_WIDTH)

    wrh, wrl = _split_bf16(wr)
    return _merge(x2d, oa, ob, oc, wbf[:, c4:], gate_b, w_br_a.astype(BF), w_br_b.astype(BF),
                  w_br_c.astype(BF), w_out.astype(BF), ln_g.reshape(1, -1), ln_b.reshape(1, -1),
                  wrh, wrl, rb, tm=_tile(t, DENSE_TILE))


def _moe(x1, wts, lpos, tab, cnt, w1, w3, w2, ln_g, ln_b):
    t = x1.shape[0]
    tile = MOE_TILE
    max_rows = t * MOE_TOPK + (t // tile) * N_EXPERTS * (RUN_ALIGN - 1)
    n_blocks = -(-max_rows // EXPERT_BLOCK) + N_EXPERTS
    cap = n_blocks * EXPERT_BLOCK
    gtab, ttab, bexp, nused = _layout(cnt, tab, n_blocks=n_blocks)
    xs = _dispatch(gtab, ttab, lpos, x1, td=tile, cap=cap)
    yb = _expert(bexp.reshape(n_blocks), nused[0, :1], xs, w1.astype(BF), w3.astype(BF), w2.astype(BF))
    return _combine(gtab, lpos, wts, x1, ln_g.reshape(1, -1), ln_b.reshape(1, -1), yb, tc=tile)


def kernel(x, w_in, attn_rel_bias, gdn_conv_w, gdn_a_log, gdn_dt_bias, gdn_norm_g, pool_w, pool_scale,
           w_branch_a, w_branch_b, w_branch_c, gate_b, w_out, ln1_g, ln1_b, router_group_w,
           router_group_b, router_expert_w, router_expert_b, moe_w1, moe_w3, moe_w2, ln2_g, ln2_b):
    b, l, d = x.shape
    x2d = x.reshape(b * l, d)
    for i in range(DEPTH):
        wr = jnp.concatenate([router_expert_w[i].T, router_group_w[i].T,
                              jnp.zeros((ROUTER_ROWS - N_EXPERTS - MOE_GROUPS, d), F32)], axis=0)
        rb = jnp.concatenate([router_expert_b[i], router_group_b[i],
                              jnp.zeros((ROUTER_ROWS - N_EXPERTS - MOE_GROUPS,), F32)]).reshape(ROUTER_ROWS, 1)
        x1, *routing = _mixer(x2d, b, l, w_in[i], attn_rel_bias[i], gdn_conv_w[i], gdn_a_log[i], gdn_dt_bias[i],
                         gdn_norm_g[i], pool_w[i], pool_scale[i], w_branch_a[i], w_branch_b[i],
                         w_branch_c[i], gate_b[i], w_out[i], ln1_g[i], ln1_b[i], wr, rb)
        x2d = _moe(x1, *routing, moe_w1[i], moe_w3[i], moe_w2[i], ln2_g[i], ln2_b[i])
    return x2d.reshape(b, l, d)
```

```python
import functools

import jax
import jax.numpy as jnp
from jax import lax
from jax.experimental import pallas as pl
from jax.experimental.pallas import tpu as pltpu

BF = jnp.bfloat16
F32 = jnp.float32
I32 = jnp.int32

D_MODEL = 1024
DEPTH = 2
CHUNK = 64

A_HEADS = 8
A_HEAD_DIM = 64
A_WIDTH = A_HEADS * A_HEAD_DIM
A_LEFT_CHUNKS = 8
A_REL_CLIP = 256

B_HEADS = 4
B_HEAD_DIM = 128
B_WIDTH = B_HEADS * B_HEAD_DIM
B_CONV = 4

C_WINDOWS = (2, 4, 8, 16)
C_GROUPS = 4
C_GROUP_DIM = 128
C_WIDTH = C_GROUPS * C_GROUP_DIM

N_BRANCH = 3
MOE_GROUPS = 4
MOE_PER_GROUP = 8
N_EXPERTS = MOE_GROUPS * MOE_PER_GROUP
MOE_TOPK = 2
MOE_FF = 512

DN_ALPHA = (2 * DEPTH) ** 0.25
LN_EPS = 1e-5
RMS_EPS = 1e-6

LANES = 128
SUBLANES = 8
VMEM_LIMIT_BYTES = 56 * 1024 * 1024

QBLOCK = 2 * CHUNK
KBAND = (A_LEFT_CHUNKS + 2) * CHUNK
NEG_BIG = -1e30

DENSE_TILE = 1024
ROUTER_ROWS = 40
EXPERT_BLOCK = 512
MOE_TILE = 512
RUN_ALIGN = 8


def _cparams(*sem):
    return pltpu.CompilerParams(dimension_semantics=sem, vmem_limit_bytes=VMEM_LIMIT_BYTES)


def _split_bf16(a):
    hi = a.astype(BF)
    lo = (a - hi.astype(F32)).astype(BF)
    return hi, lo


def _sigmoid(x):
    return 1.0 / (1.0 + jnp.exp(-x))


def _silu(x):
    return x * _sigmoid(x)


def _layer_norm(z, g, b):
    mu = jnp.mean(z, axis=-1, keepdims=True)
    zc = z - mu
    var = jnp.mean(zc * zc, axis=-1, keepdims=True)
    return zc * lax.rsqrt(var + LN_EPS) * g + b


POOL_TAIL = 16


def _proj_kernel(x_ref, wa_ref, wb_ref, wbd_ref, wgo_ref, wc_ref, pw_ref, ps_ref,
                 qkva_ref, qkvb_ref, bd_ref, go_ref, oc_ref, tail_ref, *, tiles_per_seq, tm):
    it = pl.program_id(0) % tiles_per_seq
    xb = x_ref[...].astype(BF)
    qkva_ref[...] = jnp.dot(xb, wa_ref[...], preferred_element_type=F32).astype(BF)
    qkvb_ref[...] = jnp.dot(xb, wb_ref[...], preferred_element_type=F32).astype(BF)
    bd_ref[...] = jnp.dot(xb, wbd_ref[...], preferred_element_type=F32)
    go_ref[...] = jnp.dot(xb, wgo_ref[...], preferred_element_type=F32).astype(BF)
    u = jnp.dot(xb, wc_ref[...], preferred_element_type=F32)

    @pl.when(it == 0)
    def _():
        tail_ref[...] = jnp.zeros_like(tail_ref)

    ext = jnp.concatenate([tail_ref[...], u], axis=0)
    tail_ref[...] = u[tm - POOL_TAIL:, :]
    pos = (it * tm + 1 + lax.broadcasted_iota(I32, (tm, 1), 0)).astype(F32)
    s = ext
    for gi, win in enumerate(C_WINDOWS):
        if gi:
            s = s[:, C_GROUP_DIM:]
        s = s + pltpu.roll(s, win // 2, 0)
        lo_c, hi_c = gi * C_GROUP_DIM, (gi + 1) * C_GROUP_DIM
        cnt = jnp.minimum(pos, float(win))
        pooled = s[POOL_TAIL:, :C_GROUP_DIM] / cnt - u[:, lo_c:hi_c]
        mixed = jnp.dot(pooled.astype(BF), pw_ref[gi], preferred_element_type=F32)
        oc_ref[:, lo_c:hi_c] = (mixed * ps_ref[:, lo_c:hi_c]).astype(BF)


def _proj(x2d, wa, wb, wbd, wgo, wc, pw, ps, *, seq, tm):
    t = x2d.shape[0]
    full = lambda a: pl.BlockSpec(a.shape, lambda i: (0,) * a.ndim)
    row = lambda n: pl.BlockSpec((tm, n), lambda i: (i, 0))
    return pl.pallas_call(
        functools.partial(_proj_kernel, tiles_per_seq=seq // tm, tm=tm),
        grid=(t // tm,),
        in_specs=[row(D_MODEL), full(wa), full(wb), full(wbd), full(wgo), full(wc), full(pw), full(ps)],
        out_specs=[row(3 * A_WIDTH), row(3 * B_WIDTH), row(LANES), row(B_WIDTH), row(C_WIDTH)],
        out_shape=[jax.ShapeDtypeStruct((t, 3 * A_WIDTH), BF),
                   jax.ShapeDtypeStruct((t, 3 * B_WIDTH), BF),
                   jax.ShapeDtypeStruct((t, LANES), F32),
                   jax.ShapeDtypeStruct((t, B_WIDTH), BF),
                   jax.ShapeDtypeStruct((t, C_WIDTH), BF)],
        scratch_shapes=[pltpu.VMEM((POOL_TAIL, C_WIDTH), F32)],
        compiler_params=_cparams("arbitrary"),
        name="proj",
    )(x2d, wa, wb, wbd, wgo, wc, pw, ps)


def _attn_bias_table(rel_bias):
    h = rel_bias.shape[0]
    pad = A_LEFT_CHUNKS * CHUNK
    width = KBAND + pad
    n = width + QBLOCK - 1
    far = jnp.repeat(rel_bias[:, -1:], QBLOCK - 1 + pad - A_REL_CLIP, axis=1)
    near = jnp.repeat(rel_bias[:, :1], n - far.shape[1] - rel_bias.shape[1], axis=1)
    w = jnp.concatenate([far, rel_bias[:, ::-1], near], axis=1).astype(F32)
    w = jnp.roll(w, -(QBLOCK - 1), axis=1)
    toep = jnp.tile(w, (1, QBLOCK))[:, :QBLOCK * (n - 1)].reshape(h, QBLOCK, n - 1)[:, :, :width]
    r = jnp.arange(QBLOCK)[:, None]
    m = jnp.arange(width)[None, :]
    kc, qc = m // CHUNK, r // CHUNK
    valid = (kc >= qc) & (kc <= qc + A_LEFT_CHUNKS)
    tab = jnp.where(valid[None], toep, NEG_BIG)
    return tab.reshape(h, QBLOCK, width // LANES, LANES).transpose(0, 2, 1, 3)


ATTN_QBLOCKS_PER_STEP = 8


def _attn_kernel(q_ref, k_ref, v_ref, tab_ref, o_ref):
    first_blk = A_LEFT_CHUNKS * CHUNK // QBLOCK
    lane = lax.broadcasted_iota(I32, (QBLOCK, LANES), 1)
    lo = lane < A_HEAD_DIM
    nkb = KBAND // LANES
    items = []
    for sub in range(ATTN_QBLOCKS_PER_STEP):
        j = pl.program_id(1) * ATTN_QBLOCKS_PER_STEP + sub
        kstart = pl.multiple_of(jnp.maximum(j - first_blk, 0) * QBLOCK, QBLOCK)
        offb = jnp.maximum(first_blk - j, 0)
        for hp in range(A_HEADS // 2):
            items.append((slice(sub * QBLOCK, (sub + 1) * QBLOCK), kstart, offb, hp,
                          slice(hp * LANES, (hp + 1) * LANES)))
    scores = []
    for rows, kstart, offb, hp, cols in items:
        qp = q_ref[0, rows, cols] * jnp.asarray(A_HEAD_DIM ** -0.5, BF)
        zero = jnp.zeros_like(qp)
        q2 = jnp.concatenate([jnp.where(lo, qp, zero), jnp.where(lo, zero, qp)], axis=0)
        kp = k_ref[0, pl.ds(kstart, KBAND), cols]
        scores.append(lax.dot_general(q2, kp, (((1,), (1,)), ((), ())), preferred_element_type=F32))
    probs, denoms = [], []
    for (rows, kstart, offb, hp, cols), sc in zip(items, scores):
        bias = jnp.concatenate(
            [jnp.concatenate([tab_ref[2 * hp + par, offb + i] for i in range(nkb)], axis=1) for par in range(2)],
            axis=0)
        s = sc + bias
        p = jnp.exp(s - jnp.max(s, axis=-1, keepdims=True))
        denoms.append(jnp.sum(p, axis=-1, keepdims=True))
        probs.append(p.astype(BF))
    for (rows, kstart, offb, hp, cols), p, den in zip(items, probs, denoms):
        vp = v_ref[0, pl.ds(kstart, KBAND), cols]
        o2 = jnp.dot(p, vp, preferred_element_type=F32) / den
        o_ref[0, rows, cols] = jnp.where(lo, o2[:QBLOCK], o2[QBLOCK:]).astype(BF)


def _attn(qkva, tab):
    b, l, _ = qkva.shape
    step_rows = ATTN_QBLOCKS_PER_STEP * QBLOCK
    return pl.pallas_call(
        _attn_kernel,
        grid=(b, l // step_rows),
        in_specs=[pl.BlockSpec((1, step_rows, A_WIDTH), lambda bi, j: (bi, j, 0)),
                  pl.BlockSpec((1, l, A_WIDTH), lambda bi, j: (bi, 0, 1)),
                  pl.BlockSpec((1, l, A_WIDTH), lambda bi, j: (bi, 0, 2)),
                  pl.BlockSpec(tab.shape, lambda bi, j: (0, 0, 0, 0))],
        out_specs=pl.BlockSpec((1, step_rows, A_WIDTH), lambda bi, j: (bi, j, 0)),
        out_shape=jax.ShapeDtypeStruct((b, l, A_WIDTH), BF),
        compiler_params=_cparams("arbitrary", "arbitrary"),
        name="attn",
    )(qkva, qkva, qkva, tab)


CONV_HIST = 16
GDN_PREP_UNROLL = 4


def _gdn_kernel(qkv_ref, bd_ref, go_ref, cw_ref, alog_ref, dtb_ref, ng_ref, o_ref,
                u_ref, wq_ref, qkkd_ref, egl_ref, state_ref, *, n_chunks):
    rt = lax.broadcasted_iota(I32, (CHUNK, CHUNK), 0)
    ct = lax.broadcasted_iota(I32, (CHUNK, CHUNK), 1)
    tril = rt >= ct
    strict = rt > ct
    row_c = lax.broadcasted_iota(I32, (CHUNK, LANES), 0)
    eye_f = (rt == ct).astype(F32)
    neg_rate = -jnp.exp(alog_ref[...])
    sr = lax.broadcasted_iota(I32, (B_CONV * CHUNK, CONV_HIST + CHUNK), 0)
    sc = lax.broadcasted_iota(I32, (B_CONV * CHUNK, CONV_HIST + CHUNK), 1)
    shift_sel = (sc == CONV_HIST + jnp.bitwise_and(sr, CHUNK - 1)
                 - jnp.right_shift(sr, CHUNK.bit_length() - 1)).astype(BF)

    def heads(fn):
        return jnp.concatenate([fn(h, slice(h * B_HEAD_DIM, (h + 1) * B_HEAD_DIM)) for h in range(B_HEADS)],
                               axis=1)

    def prep(c):
        r0 = pl.multiple_of(c * CHUNK, CHUNK)
        cur = qkv_ref[0, pl.ds(r0, CHUNK), :]
        h0 = pl.multiple_of(jnp.maximum(r0 - CONV_HIST, 0), CONV_HIST)
        hist = qkv_ref[0, pl.ds(h0, CONV_HIST), :]
        hist = jnp.where(c > 0, hist, jnp.zeros_like(hist))
        taps = jnp.dot(shift_sel, jnp.concatenate([hist, cur], axis=0), preferred_element_type=F32)
        acc = taps[:CHUNK] * cw_ref[B_CONV - 1:B_CONV, :]
        for back in range(1, B_CONV):
            tap = B_CONV - 1 - back
            acc = acc + taps[back * CHUNK:(back + 1) * CHUNK] * cw_ref[tap:tap + 1, :]
        act = _silu(acc)
        q_all, k_all, v_all = act[:, :B_WIDTH], act[:, B_WIDTH:2 * B_WIDTH], act[:, 2 * B_WIDTH:]

        bd = bd_ref[0, pl.ds(r0, CHUNK), :]
        beta_all = _sigmoid(bd)
        z = bd + dtb_ref[...]
        softplus = jnp.maximum(z, 0.0) + jnp.log1p(jnp.exp(-jnp.abs(z)))
        g_all = neg_rate * softplus
        gc_all = g_all
        for sh in (1, 2, 4, 8, 16, 32):
            gc_all = gc_all + jnp.where(row_c >= sh, pltpu.roll(gc_all, sh, 0), 0.0)
        egc_all = jnp.exp(gc_all)
        glast = gc_all[CHUNK - 1:CHUNK, :]
        ekd_all = jnp.exp(glast - gc_all)
        egl = jnp.exp(glast)
        col = lambda m, h: m[:, B_HEADS + h:B_HEADS + h + 1]

        def l2n(m):
            return m * lax.rsqrt(jnp.sum(m * m, axis=-1, keepdims=True) + RMS_EPS)

        qn = heads(lambda h, hs: l2n(q_all[:, hs]) * (B_HEAD_DIM ** -0.5))
        kn = heads(lambda h, hs: l2n(k_all[:, hs]))
        kbeta = heads(lambda h, hs: kn[:, hs] * beta_all[:, h:h + 1])
        vbeta = heads(lambda h, hs: v_all[:, hs] * beta_all[:, h:h + 1])
        kbe = heads(lambda h, hs: kbeta[:, hs] * col(egc_all, h))
        qe = heads(lambda h, hs: qn[:, hs] * col(egc_all, h))
        chains = []
        for h in range(B_HEADS):
            hs = slice(h * B_HEAD_DIM, (h + 1) * B_HEAD_DIM)
            gcc = jnp.broadcast_to(col(gc_all, h), (CHUNK, CHUNK))
            gcr = jnp.sum(gcc * eye_f, axis=0, keepdims=True)
            chains.append(dict(
                c=c, h=h, hs=hs,
                decay=jnp.exp(jnp.where(tril, gcc - gcr, -jnp.inf)),
                kq=jnp.concatenate([kbeta[:, hs], qn[:, hs]], axis=0).astype(BF),
                kn=kn[:, hs].astype(BF),
                rhs=jnp.concatenate([vbeta[:, hs], kbe[:, hs]], axis=1).astype(BF),
                kd=kn[:, hs] * col(ekd_all, h)))
        egl_row = heads(lambda h, hs: jnp.broadcast_to(col(egl, h), (1, B_HEAD_DIM)))
        return chains, qe.astype(BF), egl_row

    nt = lambda a, b: lax.dot_general(a, b, (((1,), (1,)), ((), ())), preferred_element_type=F32)
    mm = lambda a, b: jnp.dot(a, b, preferred_element_type=F32)

    def prep_levels(i):
        fronts = [prep(i * GDN_PREP_UNROLL + g) for g in range(GDN_PREP_UNROLL)]
        ch = [x for f in fronts for x in f[0]]
        aq = [nt(x["kq"], x["kn"]) for x in ch]
        kdt = [x["kd"].T for x in ch]
        yield
        qk = [(a[CHUNK:] * x["decay"]).astype(BF) for x, a in zip(ch, aq)]
        pw = [-jnp.where(strict, a[:CHUNK] * x["decay"], 0.0) for x, a in zip(ch, aq)]
        tmat = [eye_f + p for p in pw]
        pwb = [p.astype(BF) for p in pw]
        pw = [mm(p, p) for p in pwb]
        yield
        n_sq = CHUNK.bit_length() - 2
        for lvl in range(n_sq):
            pwb = [p.astype(BF) for p in pw]
            if lvl + 1 < n_sq:
                both = [mm(jnp.concatenate([t.astype(BF), p], axis=0), p) for t, p in zip(tmat, pwb)]
                yield
                tmat = [t + r[:CHUNK] for t, r in zip(tmat, both)]
                pw = [r[CHUNK:] for r in both]
            else:
                last = [mm(t.astype(BF), p) for t, p in zip(tmat, pwb)]
                yield
                tmat = [t + r for t, r in zip(tmat, last)]
        uw = [mm(t.astype(BF), x["rhs"]) for t, x in zip(tmat, ch)]
        yield
        for x, r, qkx, kt in zip(ch, uw, qk, kdt):
            u_ref[x["c"], :, x["hs"]] = r[:, :B_HEAD_DIM]
            wq_ref[x["c"], :CHUNK, x["hs"]] = r[:, B_HEAD_DIM:].astype(BF)
            qkkd_ref[x["c"], x["h"], :CHUNK, :] = qkx
            qkkd_ref[x["c"], x["h"], CHUNK:, :] = kt.astype(BF)
        for g, (_, qe_b, egl_row) in enumerate(fronts):
            wq_ref[i * GDN_PREP_UNROLL + g, CHUNK:, :] = qe_b
            egl_ref[i * GDN_PREP_UNROLL + g] = egl_row

    def step_levels(c):
        r0 = pl.multiple_of(c * CHUNK, CHUNK)
        st = state_ref[...]
        stb = st.astype(BF)
        hss = [slice(h * B_HEAD_DIM, (h + 1) * B_HEAD_DIM) for h in range(B_HEADS)]
        r1 = [mm(wq_ref[c, :, hs], stb[:, hs]) for hs in hss]
        yield
        v_new = [(u_ref[c, :, hs] - r[:CHUNK]).astype(BF) for hs, r in zip(hss, r1)]
        r2 = [mm(qkkd_ref[c, h], v_new[h]) for h in range(B_HEADS)]
        yield
        o_all = jnp.concatenate([a[CHUNK:] + b[:CHUNK] for a, b in zip(r1, r2)], axis=1)
        state_ref[...] = st * egl_ref[c] + jnp.concatenate([b[CHUNK:] for b in r2], axis=1)
        gate = go_ref[0, pl.ds(r0, CHUNK), :].astype(F32)
        o_n = heads(lambda h, hs: o_all[:, hs]
                    * lax.rsqrt(jnp.mean(o_all[:, hs] * o_all[:, hs], axis=-1, keepdims=True) + RMS_EPS)
                    * ng_ref[...])
        o_ref[0, pl.ds(r0, CHUNK), :] = (o_n * _silu(gate)).astype(BF)

    def steps_of(i):
        for g in range(GDN_PREP_UNROLL):
            yield from step_levels(i * GDN_PREP_UNROLL + g)

    def run_interleaved(*gens):
        live = list(gens)
        while live:
            for g in list(live):
                if next(g, StopIteration) is StopIteration:
                    live.remove(g)

    n_groups = n_chunks // GDN_PREP_UNROLL
    state_ref[...] = jnp.zeros_like(state_ref)
    run_interleaved(prep_levels(0))

    def body(i, carry):
        run_interleaved(steps_of(i - 1), prep_levels(i))
        return carry

    lax.fori_loop(1, n_groups, body, 0)
    run_interleaved(steps_of(n_groups - 1))


def _gdn(qkvb, bd, go, cw, alog, dtb, ng):
    b, l, _ = qkvb.shape
    nc = l // CHUNK
    full = lambda a: pl.BlockSpec(a.shape, lambda bi: (0,) * a.ndim)
    seq = lambda n: pl.BlockSpec((1, l, n), lambda bi: (bi, 0, 0))
    return pl.pallas_call(
        functools.partial(_gdn_kernel, n_chunks=nc),
        grid=(b,),
        in_specs=[seq(3 * B_WIDTH), seq(LANES), seq(B_WIDTH), full(cw), full(alog), full(dtb), full(ng)],
        out_specs=seq(B_WIDTH),
        out_shape=jax.ShapeDtypeStruct((b, l, B_WIDTH), BF),
        scratch_shapes=[pltpu.VMEM((nc, CHUNK, B_WIDTH), F32),
                        pltpu.VMEM((nc, 2 * CHUNK, B_WIDTH), BF),
                        pltpu.VMEM((nc, B_HEADS, CHUNK + B_HEAD_DIM, CHUNK), BF),
                        pltpu.VMEM((nc, 1, B_WIDTH), F32),
                        pltpu.VMEM((B_HEAD_DIM, B_WIDTH), F32)],
        compiler_params=_cparams("arbitrary"),
        name="gdn",
    )(qkvb, bd, go, cw, alog, dtb, ng)


def _merge_kernel(x_ref, oa_ref, ob_ref, oc_ref, wg_ref, gb_ref, wa_ref, wb_ref, wc_ref, wo_ref,
                  lg_ref, lb_ref, wrh_ref, wrl_ref, rb_ref, x1_ref, wts_ref, lpos_ref, tab_ref, cnt_ref, run_ref):
    x = x_ref[...]
    xb = x.astype(BF)
    y = None
    for i, (o_ref, w_ref) in enumerate(((oa_ref, wa_ref), (ob_ref, wb_ref), (oc_ref, wc_ref))):
        gl = jnp.dot(xb, wg_ref[:, i * D_MODEL:(i + 1) * D_MODEL], preferred_element_type=F32)
        gate = _sigmoid(gl + gb_ref[i:i + 1, :])
        br = gate * jnp.dot(o_ref[...], w_ref[...], preferred_element_type=F32)
        y = br if y is None else y + br
    mix = jnp.dot(y.astype(BF), wo_ref[...], preferred_element_type=F32)
    x1 = _layer_norm(DN_ALPHA * x + mix, lg_ref[...], lb_ref[...])
    x1_ref[...] = x1
    xh, xl = _split_bf16(x1)
    nt = lambda a, b: lax.dot_general(a, b, (((1,), (1,)), ((), ())), preferred_element_type=F32)
    lgt = nt(wrh_ref[...], xh) + (nt(wrh_ref[...], xl) + nt(wrl_ref[...], xh)) + rb_ref[...]

    @pl.when(pl.program_id(0) == 0)
    def _():
        run_ref[...] = jnp.zeros_like(run_ref)

    run = run_ref[...]
    for s in range(lgt.shape[1] // MOE_TILE):
        cols = slice(s * MOE_TILE, (s + 1) * MOE_TILE)
        wts_ref[:, cols], lpos_ref[:, cols], tab_ref[s], run = _route_tile(lgt[:, cols], run)
    run_ref[...] = run
    cnt_ref[...] = jnp.broadcast_to(run, cnt_ref.shape)


def _merge(x2d, oa, ob, oc, wg, gb, wa, wb, wc, wo, lg, lb, wrh, wrl, rb, *, tm):
    t = x2d.shape[0]
    full = lambda a: pl.BlockSpec(a.shape, lambda i: (0,) * a.ndim)
    row = lambda n: pl.BlockSpec((tm, n), lambda i: (i, 0))
    tok = lambda: pl.BlockSpec((MOE_TOPK, tm), lambda i: (0, i))
    assert tm % MOE_TILE == 0, (tm, MOE_TILE)
    return pl.pallas_call(
        _merge_kernel,
        grid=(t // tm,),
        in_specs=[row(D_MODEL), row(A_WIDTH), row(B_WIDTH), row(C_WIDTH), full(wg), full(gb), full(wa),
                  full(wb), full(wc), full(wo), full(lg), full(lb), full(wrh), full(wrl), full(rb)],
        out_specs=[row(D_MODEL), tok(), tok(), pl.BlockSpec((tm // MOE_TILE, SUBLANES, LANES), lambda i: (i, 0, 0)),
                   pl.BlockSpec((N_EXPERTS, LANES), lambda i: (0, 0))],
        out_shape=[jax.ShapeDtypeStruct((t, D_MODEL), F32),
                   jax.ShapeDtypeStruct((MOE_TOPK, t), F32),
                   jax.ShapeDtypeStruct((MOE_TOPK, t), I32),
                   jax.ShapeDtypeStruct((t // MOE_TILE, SUBLANES, LANES), I32),
                   jax.ShapeDtypeStruct((N_EXPERTS, LANES), F32)],
        scratch_shapes=[pltpu.VMEM((N_EXPERTS, 1), F32)],
        compiler_params=_cparams("arbitrary"),
        name="merge",
    )(x2d, oa, ob, oc, wg, gb, wa, wb, wc, wo, lg, lb, wrh, wrl, rb)


def _route_tile(lgt, run):
    tn = lgt.shape[1]
    le = lgt[0:N_EXPERTS, :]
    lg = lgt[N_EXPERTS:N_EXPERTS + MOE_GROUPS, :]
    gi = lax.broadcasted_iota(I32, (MOE_GROUPS, tn), 0).astype(F32)
    ei = lax.broadcasted_iota(I32, (N_EXPERTS, tn), 0).astype(F32)
    eg = jnp.right_shift(lax.broadcasted_iota(I32, (N_EXPERTS, tn), 0),
                         MOE_PER_GROUP.bit_length() - 1).astype(F32)
    mg = jnp.max(lg, axis=0, keepdims=True)
    gsel = jnp.min(jnp.where(lg == mg, gi, float(MOE_GROUPS)), axis=0, keepdims=True)
    p_top = 1.0 / jnp.sum(jnp.exp(lg - mg), axis=0, keepdims=True)
    l1 = jnp.where(eg == gsel, le, -jnp.inf)
    m1 = jnp.max(l1, axis=0, keepdims=True)
    i1 = jnp.min(jnp.where(l1 == m1, ei, float(N_EXPERTS)), axis=0, keepdims=True)
    l2 = jnp.where(ei == i1, -jnp.inf, l1)
    m2 = jnp.max(l2, axis=0, keepdims=True)
    i2 = jnp.min(jnp.where(l2 == m2, ei, float(N_EXPERTS)), axis=0, keepdims=True)
    e2 = jnp.exp(m2 - m1)
    den = 1.0 + e2
    wts = jnp.concatenate([p_top / den, p_top * (e2 / den)], axis=0)

    before = (lax.broadcasted_iota(I32, (tn, tn), 0) < lax.broadcasted_iota(I32, (tn, tn), 1)).astype(BF)
    oh = [ei == ik for ik in (i1, i2)]
    prefix = [jnp.dot(o.astype(BF), before, preferred_element_type=F32) for o in oh]
    cnt_k = [jnp.sum(o.astype(F32), axis=1, keepdims=True) for o in oh]
    cnt = cnt_k[0] + cnt_k[1]
    cnt_al = jnp.floor((cnt + (RUN_ALIGN - 1)) * (1.0 / RUN_ALIGN)) * RUN_ALIGN
    er = lax.broadcasted_iota(I32, (N_EXPERTS, N_EXPERTS), 0)
    ec = lax.broadcasted_iota(I32, (N_EXPERTS, N_EXPERTS), 1)
    lstart = jnp.dot((ec < er).astype(BF), jnp.broadcast_to(cnt_al, (N_EXPERTS, LANES)).astype(BF),
                     preferred_element_type=F32)[:, 0:1]
    first = [0.0, cnt_k[0]]
    pick = lambda k: jnp.sum(jnp.where(oh[k], prefix[k] + first[k] + lstart, 0.0), axis=0, keepdims=True)
    lpos = jnp.concatenate([pick(0), pick(1)], axis=0).astype(I32)
    eye_l = (lax.broadcasted_iota(I32, (N_EXPERTS, LANES), 0) == lax.broadcasted_iota(I32, (N_EXPERTS, LANES), 1))
    as_row = lambda colv: jnp.sum(jnp.where(eye_l, colv, 0.0), axis=0, keepdims=True)
    pieces = jnp.broadcast_to(jnp.sum(cnt_al, axis=0, keepdims=True) * (1.0 / RUN_ALIGN), (1, LANES))
    tab = jnp.concatenate([as_row(run), as_row(cnt), as_row(lstart), pieces,
                           jnp.zeros((SUBLANES - 4, LANES), F32)], axis=0).astype(I32)
    return wts, lpos, tab, run + cnt_al


def _layout_kernel(cnt_ref, tab_ref, gtab_ref, ttab_ref, bexp_ref, nused_ref, *, n_blocks):
    cnt = cnt_ref[...].astype(I32)
    shift = EXPERT_BLOCK.bit_length() - 1
    padded_blocks = jnp.right_shift(cnt + (EXPERT_BLOCK - 1), shift)
    er = lax.broadcasted_iota(I32, (N_EXPERTS, LANES), 0)
    el = lax.broadcasted_iota(I32, (N_EXPERTS, LANES), 1)
    start = jnp.zeros((N_EXPERTS, LANES), I32)
    for e in range(N_EXPERTS - 1):
        start = start + jnp.where(er > e, padded_blocks[e:e + 1, :], 0)
    end = start + padded_blocks
    as_row = lambda m: jnp.sum(jnp.where(er == el, m, 0), axis=0, keepdims=True)
    tab = tab_ref[...]
    first_row = lax.broadcasted_iota(I32, tab.shape, 1) == 0
    gtab_ref[...] = tab + jnp.where(first_row, as_row(start * EXPERT_BLOCK)[None], 0)
    tail_rows = padded_blocks * EXPERT_BLOCK - cnt
    end_all = end[N_EXPERTS - 1:N_EXPERTS, :]
    spare = lax.broadcasted_iota(I32, (1, LANES), 1) == N_EXPERTS
    ttab_ref[...] = jnp.concatenate(
        [jnp.where(spare, end_all * EXPERT_BLOCK, as_row(start * EXPERT_BLOCK + cnt)),
         jnp.where(spare, n_blocks - end_all, as_row(jnp.right_shift(tail_rows, RUN_ALIGN.bit_length() - 1))),
         jnp.zeros((SUBLANES - 2, LANES), I32)], axis=0)[None]
    blk = lax.broadcasted_iota(I32, (1, n_blocks), 1)
    owner = jnp.zeros((1, n_blocks), I32)
    for e in range(N_EXPERTS):
        owner = owner + (blk >= end[e:e + 1, 0:1]).astype(I32)
    bexp_ref[...] = jnp.minimum(owner, N_EXPERTS - 1)
    nused_ref[...] = end[N_EXPERTS - 1:N_EXPERTS, :]


def _layout(cnt, tab, *, n_blocks):
    full = lambda shape: pl.BlockSpec(shape, lambda i: (0,) * len(shape))
    return pl.pallas_call(
        functools.partial(_layout_kernel, n_blocks=n_blocks),
        grid=(1,),
        in_specs=[full(cnt.shape), full(tab.shape)],
        out_specs=[full(tab.shape), full((1, SUBLANES, LANES)), full((1, n_blocks)), full((1, LANES))],
        out_shape=[jax.ShapeDtypeStruct(tab.shape, I32),
                   jax.ShapeDtypeStruct((1, SUBLANES, LANES), I32),
                   jax.ShapeDtypeStruct((1, n_blocks), I32),
                   jax.ShapeDtypeStruct((1, LANES), I32)],
        compiler_params=_cparams("arbitrary"),
        name="layout",
    )(cnt, tab)


def _run_pieces(tab_ref, fn):
    for e in range(N_EXPERTS):
        sorted0, staging0 = tab_ref[0, 0, e], tab_ref[0, 2, e]
        n_pieces = (tab_ref[0, 1, e] + (RUN_ALIGN - 1)) // RUN_ALIGN

        def one(j, c, sorted0=sorted0, staging0=staging0):
            fn(pl.multiple_of(sorted0 + j * RUN_ALIGN, RUN_ALIGN), pl.multiple_of(staging0 + j * RUN_ALIGN, RUN_ALIGN))
            return c

        lax.fori_loop(0, n_pieces, one, 0)


WAIT_GROUPS = (64, 8, 1)


def _wait_pieces(tab_ref, make_copy):
    total = tab_ref[0, 3, 0]
    counts = (total // WAIT_GROUPS[0], (total // WAIT_GROUPS[1]) % (WAIT_GROUPS[0] // WAIT_GROUPS[1]),
              total % WAIT_GROUPS[1])
    for group, count in zip(WAIT_GROUPS, counts):
        def one(j, c, group=group):
            make_copy(group * RUN_ALIGN).wait()
            return c
        lax.fori_loop(0, count, one, 0)


def _dispatch_kernel(tab_ref, ptab_ref, ttab_ref, lpos_ref, x_ref, xs_ref, stage_ref, zero_ref, sems, tsem,
                     *, td):
    i = pl.program_id(0)
    last = pl.num_programs(0) - 1
    slot = lax.rem(i, 2)
    rows = stage_ref.shape[1]
    r = lax.broadcasted_iota(I32, (rows, td), 0)
    sel = jnp.logical_or(r == lpos_ref[0:1, :], r == lpos_ref[1:2, :]).astype(BF)
    stage_ref[slot] = jnp.dot(sel, x_ref[...].astype(BF), preferred_element_type=F32)

    def run_copy(sl, dst, src, rows=RUN_ALIGN):
        return pltpu.make_async_copy(stage_ref.at[sl, pl.ds(src, rows), :],
                                     xs_ref.at[pl.ds(dst, rows), :], sems.at[sl])

    def tail_copies(fn):
        def per_expert(e, carry):
            def one(j, c):
                dst = pl.multiple_of(ttab_ref[0, 0, e] + j * RUN_ALIGN, RUN_ALIGN)
                fn(pltpu.make_async_copy(zero_ref.at[pl.ds(0, RUN_ALIGN), :],
                                         xs_ref.at[pl.ds(dst, RUN_ALIGN), :], tsem))
                return c
            lax.fori_loop(0, ttab_ref[0, 1, e], one, 0)
            return carry
        lax.fori_loop(0, N_EXPERTS, per_expert, 0)

        def spare_block(j, c):
            dst = pl.multiple_of(ttab_ref[0, 0, N_EXPERTS] + j * EXPERT_BLOCK, EXPERT_BLOCK)
            fn(pltpu.make_async_copy(zero_ref, xs_ref.at[pl.ds(dst, EXPERT_BLOCK), :], tsem))
            return c
        lax.fori_loop(0, ttab_ref[0, 1, N_EXPERTS], spare_block, 0)

    @pl.when(i == 0)
    def _():
        zero_ref[...] = jnp.zeros_like(zero_ref)
        tail_copies(lambda cp: cp.start())

    @pl.when(i > 0)
    def _():
        _wait_pieces(ptab_ref, lambda rows: run_copy(1 - slot, 0, 0, rows))

    _run_pieces(tab_ref, lambda dst, src: run_copy(slot, dst, src).start())

    @pl.when(i == last)
    def _():
        _wait_pieces(tab_ref, lambda rows: run_copy(slot, 0, 0, rows))
        tail_copies(lambda cp: cp.wait())


def _stage_rows(td):
    return MOE_TOPK * td + N_EXPERTS * RUN_ALIGN


def _dispatch(gtab, ttab, lpos, x1, *, td, cap):
    t = x1.shape[0]
    smem_tab = lambda imap: pl.BlockSpec((1, SUBLANES, LANES), imap, memory_space=pltpu.SMEM)
    return pl.pallas_call(
        functools.partial(_dispatch_kernel, td=td),
        grid=(t // td,),
        in_specs=[smem_tab(lambda i: (i, 0, 0)), smem_tab(lambda i: (jnp.maximum(i - 1, 0), 0, 0)),
                  smem_tab(lambda i: (0, 0, 0)),
                  pl.BlockSpec((MOE_TOPK, td), lambda i: (0, i)),
                  pl.BlockSpec((td, D_MODEL), lambda i: (i, 0))],
        out_specs=pl.BlockSpec(memory_space=pl.ANY),
        out_shape=jax.ShapeDtypeStruct((cap, D_MODEL), F32),
        scratch_shapes=[pltpu.VMEM((2, _stage_rows(td), D_MODEL), F32), pltpu.VMEM((EXPERT_BLOCK, D_MODEL), F32),
                        pltpu.SemaphoreType.DMA((2,)), pltpu.SemaphoreType.DMA(())],
        compiler_params=_cparams("arbitrary"),
        name="dispatch",
    )(gtab, gtab, ttab, lpos, x1)


def _expert_kernel(bexp_ref, nused_ref, xs_ref, w1_ref, w3_ref, w2_ref, y_ref):
    del bexp_ref
    i = pl.program_id(0)

    @pl.when(i < nused_ref[0])
    def _():
        xb = xs_ref[...].astype(BF)
        h1 = jnp.dot(xb, w1_ref[0], preferred_element_type=F32)
        h3 = jnp.dot(xb, w3_ref[0], preferred_element_type=F32)
        hid = (_silu(h1) * h3).astype(BF)
        y_ref[...] = jnp.dot(hid, w2_ref[0], preferred_element_type=F32)

    @pl.when(i >= nused_ref[0])
    def _():
        y_ref[...] = jnp.zeros_like(y_ref)


def _expert(bexp, nused, xs, w1, w3, w2):
    cap = xs.shape[0]
    nb = cap // EXPERT_BLOCK
    last = lambda i, nu: jnp.minimum(i, nu[0] - 1)
    grid_spec = pltpu.PrefetchScalarGridSpec(
        num_scalar_prefetch=2,
        grid=(nb,),
        in_specs=[pl.BlockSpec((EXPERT_BLOCK, D_MODEL), lambda i, be, nu: (last(i, nu), 0)),
                  pl.BlockSpec((1, D_MODEL, MOE_FF), lambda i, be, nu: (be[last(i, nu)], 0, 0)),
                  pl.BlockSpec((1, D_MODEL, MOE_FF), lambda i, be, nu: (be[last(i, nu)], 0, 0)),
                  pl.BlockSpec((1, MOE_FF, D_MODEL), lambda i, be, nu: (be[last(i, nu)], 0, 0))],
        out_specs=pl.BlockSpec((EXPERT_BLOCK, D_MODEL), lambda i, be, nu: (i, 0)),
    )
    return pl.pallas_call(
        _expert_kernel,
        grid_spec=grid_spec,
        out_shape=jax.ShapeDtypeStruct((cap, D_MODEL), F32),
        compiler_params=_cparams("arbitrary"),
        name="expert",
    )(bexp, nused, xs, w1, w3, w2)


def _combine_kernel(tab_ref, ntab_ref, lpos_ref, lposr_ref, wr_ref, x_ref, lg_ref, lb_ref, yb_ref, o_ref, stage_ref,
                    sems,
                    *, tc):
    i = pl.program_id(0)
    last = pl.num_programs(0) - 1
    slot = lax.rem(i, 2)

    def run_copy(sl, src, dst, rows=RUN_ALIGN):
        return pltpu.make_async_copy(yb_ref.at[pl.ds(src, rows), :],
                                     stage_ref.at[sl, pl.ds(dst, rows), :], sems.at[sl])

    @pl.when(i == 0)
    def _():
        stage_ref[...] = jnp.zeros_like(stage_ref)
        _run_pieces(tab_ref, lambda src, dst: run_copy(slot, src, dst).start())

    @pl.when(i < last)
    def _():
        _run_pieces(ntab_ref, lambda src, dst: run_copy(1 - slot, src, dst).start())

    _wait_pieces(tab_ref, lambda rows: run_copy(slot, 0, 0, rows))
    rows = stage_ref.shape[1]
    r = lax.broadcasted_iota(I32, (rows, tc), 0)
    row_w = jnp.sum(jnp.where(r == lposr_ref[0:1, :], wr_ref[0:1, :], 0.0)
                    + jnp.where(r == lposr_ref[1:2, :], wr_ref[1:2, :], 0.0), axis=1, keepdims=True)
    staged = (stage_ref[slot] * row_w).astype(BF)
    c = lax.broadcasted_iota(I32, (tc, rows), 1)
    pick = jnp.logical_or(c == lpos_ref[:, 0:1], c == lpos_ref[:, 1:2]).astype(BF)
    ffn = jnp.dot(pick, staged, preferred_element_type=F32)
    o_ref[...] = _layer_norm(DN_ALPHA * x_ref[...] + ffn, lg_ref[...], lb_ref[...])


def _combine(gtab, lpos, wts, x1, lg, lb, yb, *, tc):
    t = x1.shape[0]
    n_tiles = t // tc
    full = lambda a: pl.BlockSpec(a.shape, lambda i: (0,) * a.ndim)
    smem_tab = lambda imap: pl.BlockSpec((1, SUBLANES, LANES), imap, memory_space=pltpu.SMEM)
    return pl.pallas_call(
        functools.partial(_combine_kernel, tc=tc),
        grid=(n_tiles,),
        in_specs=[smem_tab(lambda i: (i, 0, 0)), smem_tab(lambda i: (jnp.minimum(i + 1, n_tiles - 1), 0, 0)),
                  pl.BlockSpec((tc, MOE_TOPK), lambda i: (i, 0)),
                  pl.BlockSpec((MOE_TOPK, tc), lambda i: (0, i)),
                  pl.BlockSpec((MOE_TOPK, tc), lambda i: (0, i)),
                  pl.BlockSpec((tc, D_MODEL), lambda i: (i, 0)),
                  full(lg), full(lb),
                  pl.BlockSpec(memory_space=pl.ANY)],
        out_specs=pl.BlockSpec((tc, D_MODEL), lambda i: (i, 0)),
        out_shape=jax.ShapeDtypeStruct((t, D_MODEL), F32),
        scratch_shapes=[pltpu.VMEM((2, _stage_rows(tc), D_MODEL), F32), pltpu.SemaphoreType.DMA((2,))],
        compiler_params=_cparams("arbitrary"),
        name="combine",
    )(gtab, gtab, lpos.T, lpos, wts, x1, lg, lb, yb)


def _tile(n, want):
    while n % want:
        want //= 2
    return want


def _mixer(x2d, b, l, w_in, rel_bias, conv_w, a_log, dt_bias, norm_g, pool_w, pool_scale,
           w_br_a, w_br_b, w_br_c, gate_b, w_out, ln_g, ln_b, wr, rb):
    t = x2d.shape[0]
    c0 = 3 * A_WIDTH
    c1 = c0 + 3 * B_WIDTH
    c2 = c1 + 2 * B_HEADS
    c3 = c2 + B_WIDTH
    c4 = c3 + C_WIDTH
    wbf = w_in.astype(BF)
    wbd = jnp.pad(wbf[:, c1:c2], ((0, 0), (0, LANES - 2 * B_HEADS)))
    qkva, qkvb, bd, go, oc = _proj(
        x2d, wbf[:, :c0], wbf[:, c0:c1], wbd, wbf[:, c2:c3], wbf[:, c3:c4],
        pool_w.astype(BF), pool_scale.reshape(1, C_WIDTH), seq=l, tm=_tile(l, DENSE_TILE))

    oa = _attn(qkva.reshape(b, l, 3 * A_WIDTH), _attn_bias_table(rel_bias)).reshape(t, A_WIDTH)

    lane_pad = lambda v: jnp.pad(v.reshape(1, B_HEADS), ((0, 0), (B_HEADS, LANES - 2 * B_HEADS)))
    ob = _gdn(qkvb.reshape(b, l, 3 * B_WIDTH), bd.reshape(b, l, LANES), go.reshape(b, l, B_WIDTH),
              conv_w, lane_pad(a_log), lane_pad(dt_bias), norm_g.reshape(1, B_HEAD_DIM)).reshape(t, B_WIDTH)

    wrh, wrl = _split_bf16(wr)
    return _merge(x2d, oa, ob, oc, wbf[:, c4:], gate_b, w_br_a.astype(BF), w_br_b.astype(BF),
                  w_br_c.astype(BF), w_out.astype(BF), ln_g.reshape(1, -1), ln_b.reshape(1, -1),
                  wrh, wrl, rb, tm=_tile(t, DENSE_TILE))


def _moe(x1, wts, lpos, tab, cnt, w1, w3, w2, ln_g, ln_b):
    t = x1.shape[0]
    tile = MOE_TILE
    max_rows = t * MOE_TOPK + (t // tile) * N_EXPERTS * (RUN_ALIGN - 1)
    n_blocks = -(-max_rows // EXPERT_BLOCK) + N_EXPERTS
    cap = n_blocks * EXPERT_BLOCK
    gtab, ttab, bexp, nused = _layout(cnt, tab, n_blocks=n_blocks)
    xs = _dispatch(gtab, ttab, lpos, x1, td=tile, cap=cap)
    yb = _expert(bexp.reshape(n_blocks), nused[0, :1], xs, w1.astype(BF), w3.astype(BF), w2.astype(BF))
    return _combine(gtab, lpos, wts, x1, ln_g.reshape(1, -1), ln_b.reshape(1, -1), yb, tc=tile)


def kernel(x, w_in, attn_rel_bias, gdn_conv_w, gdn_a_log, gdn_dt_bias, gdn_norm_g, pool_w, pool_scale,
           w_branch_a, w_branch_b, w_branch_c, gate_b, w_out, ln1_g, ln1_b, router_group_w,
           router_group_b, router_expert_w, router_expert_b, moe_w1, moe_w3, moe_w2, ln2_g, ln2_b):
    b, l, d = x.shape
    x2d = x.reshape(b * l, d)
    for i in range(DEPTH):
        wr = jnp.concatenate([router_expert_w[i].T, router_group_w[i].T,
                              jnp.zeros((ROUTER_ROWS - N_EXPERTS - MOE_GROUPS, d), F32)], axis=0)
        rb = jnp.concatenate([router_expert_b[i], router_group_b[i],
                              jnp.zeros((ROUTER_ROWS - N_EXPERTS - MOE_GROUPS,), F32)]).reshape(ROUTER_ROWS, 1)
        x1, *routing = _mixer(x2d, b, l, w_in[i], attn_rel_bias[i], gdn_conv_w[i], gdn_a_log[i], gdn_dt_bias[i],
                         gdn_norm_g[i], pool_w[i], pool_scale[i], w_branch_a[i], w_branch_b[i],
                         w_branch_c[i], gate_b[i], w_out[i], ln1_g[i], ln1_b[i], wr, rb)
        x2d = _moe(x1, *routing, moe_w1[i], moe_w3[i], moe_w2[i], ln2_g[i], ln2_b[i])
    return x2d.reshape(b, l, d)
```

```python
import functools

import jax
import jax.numpy as jnp
from jax import lax
from jax.experimental import pallas as pl
from jax.experimental.pallas import tpu as pltpu

BF = jnp.bfloat16
F32 = jnp.float32
I32 = jnp.int32

D_MODEL = 1024
DEPTH = 2
CHUNK = 64

A_HEADS = 8
A_HEAD_DIM = 64
A_WIDTH = A_HEADS * A_HEAD_DIM
A_LEFT_CHUNKS = 8
A_REL_CLIP = 256

B_HEADS = 4
B_HEAD_DIM = 128
B_WIDTH = B_HEADS * B_HEAD_DIM
B_CONV = 4

C_WINDOWS = (2, 4, 8, 16)
C_GROUPS = 4
C_GROUP_DIM = 128
C_WIDTH = C_GROUPS * C_GROUP_DIM

N_BRANCH = 3
MOE_GROUPS = 4
MOE_PER_GROUP = 8
N_EXPERTS = MOE_GROUPS * MOE_PER_GROUP
MOE_TOPK = 2
MOE_FF = 512

DN_ALPHA = (2 * DEPTH) ** 0.25
LN_EPS = 1e-5
RMS_EPS = 1e-6

LANES = 128
SUBLANES = 8
VMEM_LIMIT_BYTES = 56 * 1024 * 1024

QBLOCK = 2 * CHUNK
KBAND = (A_LEFT_CHUNKS + 2) * CHUNK
NEG_BIG = -1e30

DENSE_TILE = 1024
ROUTER_ROWS = 40
EXPERT_BLOCK = 512
MOE_TILE = 512
RUN_ALIGN = 8


def _cparams(*sem):
    return pltpu.CompilerParams(dimension_semantics=sem, vmem_limit_bytes=VMEM_LIMIT_BYTES)


def _split_bf16(a):
    hi = a.astype(BF)
    lo = (a - hi.astype(F32)).astype(BF)
    return hi, lo


def _sigmoid(x):
    return 1.0 / (1.0 + jnp.exp(-x))


def _silu(x):
    return x * _sigmoid(x)


def _layer_norm(z, g, b):
    mu = jnp.mean(z, axis=-1, keepdims=True)
    zc = z - mu
    var = jnp.mean(zc * zc, axis=-1, keepdims=True)
    return zc * lax.rsqrt(var + LN_EPS) * g + b


POOL_TAIL = 16


def _proj_kernel(x_ref, wa_ref, wb_ref, wbd_ref, wgo_ref, wc_ref, pw_ref, ps_ref,
                 qkva_ref, qkvb_ref, bd_ref, go_ref, oc_ref, tail_ref, *, tiles_per_seq, tm):
    it = pl.program_id(0) % tiles_per_seq
    xb = x_ref[...].astype(BF)
    qkva_ref[...] = jnp.dot(xb, wa_ref[...], preferred_element_type=F32).astype(BF)
    qkvb_ref[...] = jnp.dot(xb, wb_ref[...], preferred_element_type=F32).astype(BF)
    bd_ref[...] = jnp.dot(xb, wbd_ref[...], preferred_element_type=F32)
    go_ref[...] = jnp.dot(xb, wgo_ref[...], preferred_element_type=F32).astype(BF)
    u = jnp.dot(xb, wc_ref[...], preferred_element_type=F32)

    @pl.when(it == 0)
    def _():
        tail_ref[...] = jnp.zeros_like(tail_ref)

    ext = jnp.concatenate([tail_ref[...], u], axis=0)
    tail_ref[...] = u[tm - POOL_TAIL:, :]
    pos = (it * tm + 1 + lax.broadcasted_iota(I32, (tm, 1), 0)).astype(F32)
    s = ext
    for gi, win in enumerate(C_WINDOWS):
        if gi:
            s = s[:, C_GROUP_DIM:]
        s = s + pltpu.roll(s, win // 2, 0)
        lo_c, hi_c = gi * C_GROUP_DIM, (gi + 1) * C_GROUP_DIM
        cnt = jnp.minimum(pos, float(win))
        pooled = s[POOL_TAIL:, :C_GROUP_DIM] / cnt - u[:, lo_c:hi_c]
        mixed = jnp.dot(pooled.astype(BF), pw_ref[gi], preferred_element_type=F32)
        oc_ref[:, lo_c:hi_c] = (mixed * ps_ref[:, lo_c:hi_c]).astype(BF)


def _proj(x2d, wa, wb, wbd, wgo, wc, pw, ps, *, seq, tm):
    t = x2d.shape[0]
    full = lambda a: pl.BlockSpec(a.shape, lambda i: (0,) * a.ndim)
    row = lambda n: pl.BlockSpec((tm, n), lambda i: (i, 0))
    return pl.pallas_call(
        functools.partial(_proj_kernel, tiles_per_seq=seq // tm, tm=tm),
        grid=(t // tm,),
        in_specs=[row(D_MODEL), full(wa), full(wb), full(wbd), full(wgo), full(wc), full(pw), full(ps)],
        out_specs=[row(3 * A_WIDTH), row(3 * B_WIDTH), row(LANES), row(B_WIDTH), row(C_WIDTH)],
        out_shape=[jax.ShapeDtypeStruct((t, 3 * A_WIDTH), BF),
                   jax.ShapeDtypeStruct((t, 3 * B_WIDTH), BF),
                   jax.ShapeDtypeStruct((t, LANES), F32),
                   jax.ShapeDtypeStruct((t, B_WIDTH), BF),
                   jax.ShapeDtypeStruct((t, C_WIDTH), BF)],
        scratch_shapes=[pltpu.VMEM((POOL_TAIL, C_WIDTH), F32)],
        compiler_params=_cparams("arbitrary"),
        name="proj",
    )(x2d, wa, wb, wbd, wgo, wc, pw, ps)


def _attn_bias_table(rel_bias):
    h = rel_bias.shape[0]
    pad = A_LEFT_CHUNKS * CHUNK
    width = KBAND + pad
    n = width + QBLOCK - 1
    far = jnp.repeat(rel_bias[:, -1:], QBLOCK - 1 + pad - A_REL_CLIP, axis=1)
    near = jnp.repeat(rel_bias[:, :1], n - far.shape[1] - rel_bias.shape[1], axis=1)
    w = jnp.concatenate([far, rel_bias[:, ::-1], near], axis=1).astype(F32)
    w = jnp.roll(w, -(QBLOCK - 1), axis=1)
    toep = jnp.tile(w, (1, QBLOCK))[:, :QBLOCK * (n - 1)].reshape(h, QBLOCK, n - 1)[:, :, :width]
    r = jnp.arange(QBLOCK)[:, None]
    m = jnp.arange(width)[None, :]
    kc, qc = m // CHUNK, r // CHUNK
    valid = (kc >= qc) & (kc <= qc + A_LEFT_CHUNKS)
    tab = jnp.where(valid[None], toep, NEG_BIG)
    return tab.reshape(h, QBLOCK, width // LANES, LANES).transpose(0, 2, 1, 3)


ATTN_QBLOCKS_PER_STEP = 8


def _attn_kernel(q_ref, k_ref, v_ref, tab_ref, o_ref):
    first_blk = A_LEFT_CHUNKS * CHUNK // QBLOCK
    lane = lax.broadcasted_iota(I32, (QBLOCK, LANES), 1)
    lo = lane < A_HEAD_DIM
    nkb = KBAND // LANES
    items = []
    for sub in range(ATTN_QBLOCKS_PER_STEP):
        j = pl.program_id(1) * ATTN_QBLOCKS_PER_STEP + sub
        kstart = pl.multiple_of(jnp.maximum(j - first_blk, 0) * QBLOCK, QBLOCK)
        offb = jnp.maximum(first_blk - j, 0)
        for hp in range(A_HEADS // 2):
            items.append((slice(sub * QBLOCK, (sub + 1) * QBLOCK), kstart, offb, hp,
                          slice(hp * LANES, (hp + 1) * LANES)))
    scores = []
    for rows, kstart, offb, hp, cols in items:
        qp = q_ref[0, rows, cols] * jnp.asarray(A_HEAD_DIM ** -0.5, BF)
        zero = jnp.zeros_like(qp)
        q2 = jnp.concatenate([jnp.where(lo, qp, zero), jnp.where(lo, zero, qp)], axis=0)
        kp = k_ref[0, pl.ds(kstart, KBAND), cols]
        scores.append(lax.dot_general(q2, kp, (((1,), (1,)), ((), ())), preferred_element_type=F32))
    probs, denoms = [], []
    for (rows, kstart, offb, hp, cols), sc in zip(items, scores):
        bias = jnp.concatenate(
            [jnp.concatenate([tab_ref[2 * hp + par, offb + i] for i in range(nkb)], axis=1) for par in range(2)],
            axis=0)
        s = sc + bias
        p = jnp.exp(s - jnp.max(s, axis=-1, keepdims=True))
        denoms.append(jnp.sum(p, axis=-1, keepdims=True))
        probs.append(p.astype(BF))
    for (rows, kstart, offb, hp, cols), p, den in zip(items, probs, denoms):
        vp = v_ref[0, pl.ds(kstart, KBAND), cols]
        o2 = jnp.dot(p, vp, preferred_element_type=F32) / den
        o_ref[0, rows, cols] = jnp.where(lo, o2[:QBLOCK], o2[QBLOCK:]).astype(BF)


def _attn(qkva, tab):
    b, l, _ = qkva.shape
    step_rows = ATTN_QBLOCKS_PER_STEP * QBLOCK
    return pl.pallas_call(
        _attn_kernel,
        grid=(b, l // step_rows),
        in_specs=[pl.BlockSpec((1, step_rows, A_WIDTH), lambda bi, j: (bi, j, 0)),
                  pl.BlockSpec((1, l, A_WIDTH), lambda bi, j: (bi, 0, 1)),
                  pl.BlockSpec((1, l, A_WIDTH), lambda bi, j: (bi, 0, 2)),
                  pl.BlockSpec(tab.shape, lambda bi, j: (0, 0, 0, 0))],
        out_specs=pl.BlockSpec((1, step_rows, A_WIDTH), lambda bi, j: (bi, j, 0)),
        out_shape=jax.ShapeDtypeStruct((b, l, A_WIDTH), BF),
        compiler_params=_cparams("arbitrary", "arbitrary"),
        name="attn",
    )(qkva, qkva, qkva, tab)


CONV_HIST = 16
GDN_PREP_UNROLL = 4


def _gdn_kernel(qkv_ref, bd_ref, go_ref, cw_ref, alog_ref, dtb_ref, ng_ref, o_ref,
                u_ref, wq_ref, qkkd_ref, egl_ref, state_ref, *, n_chunks):
    rt = lax.broadcasted_iota(I32, (CHUNK, CHUNK), 0)
    ct = lax.broadcasted_iota(I32, (CHUNK, CHUNK), 1)
    tril = rt >= ct
    strict = rt > ct
    row_c = lax.broadcasted_iota(I32, (CHUNK, LANES), 0)
    eye_f = (rt == ct).astype(F32)
    neg_rate = -jnp.exp(alog_ref[...])
    sr = lax.broadcasted_iota(I32, (B_CONV * CHUNK, CONV_HIST + CHUNK), 0)
    sc = lax.broadcasted_iota(I32, (B_CONV * CHUNK, CONV_HIST + CHUNK), 1)
    shift_sel = (sc == CONV_HIST + jnp.bitwise_and(sr, CHUNK - 1)
                 - jnp.right_shift(sr, CHUNK.bit_length() - 1)).astype(BF)

    def heads(fn):
        return jnp.concatenate([fn(h, slice(h * B_HEAD_DIM, (h + 1) * B_HEAD_DIM)) for h in range(B_HEADS)],
                               axis=1)

    def prep(c):
        r0 = pl.multiple_of(c * CHUNK, CHUNK)
        cur = qkv_ref[0, pl.ds(r0, CHUNK), :]
        h0 = pl.multiple_of(jnp.maximum(r0 - CONV_HIST, 0), CONV_HIST)
        hist = qkv_ref[0, pl.ds(h0, CONV_HIST), :]
        hist = jnp.where(c > 0, hist, jnp.zeros_like(hist))
        taps = jnp.dot(shift_sel, jnp.concatenate([hist, cur], axis=0), preferred_element_type=F32)
        acc = taps[:CHUNK] * cw_ref[B_CONV - 1:B_CONV, :]
        for back in range(1, B_CONV):
            tap = B_CONV - 1 - back
            acc = acc + taps[back * CHUNK:(back + 1) * CHUNK] * cw_ref[tap:tap + 1, :]
        act = _silu(acc)
        q_all, k_all, v_all = act[:, :B_WIDTH], act[:, B_WIDTH:2 * B_WIDTH], act[:, 2 * B_WIDTH:]

        bd = bd_ref[0, pl.ds(r0, CHUNK), :]
        beta_all = _sigmoid(bd)
        z = bd + dtb_ref[...]
        softplus = jnp.maximum(z, 0.0) + jnp.log1p(jnp.exp(-jnp.abs(z)))
        g_all = neg_rate * softplus
        gc_all = g_all
        for sh in (1, 2, 4, 8, 16, 32):
            gc_all = gc_all + jnp.where(row_c >= sh, pltpu.roll(gc_all, sh, 0), 0.0)
        egc_all = jnp.exp(gc_all)
        glast = gc_all[CHUNK - 1:CHUNK, :]
        ekd_all = jnp.exp(glast - gc_all)
        egl = jnp.exp(glast)
        col = lambda m, h: m[:, B_HEADS + h:B_HEADS + h + 1]

        def l2n(m):
            return m * lax.rsqrt(jnp.sum(m * m, axis=-1, keepdims=True) + RMS_EPS)

        qn = heads(lambda h, hs: l2n(q_all[:, hs]) * (B_HEAD_DIM ** -0.5))
        kn = heads(lambda h, hs: l2n(k_all[:, hs]))
        kbeta = heads(lambda h, hs: kn[:, hs] * beta_all[:, h:h + 1])
        vbeta = heads(lambda h, hs: v_all[:, hs] * beta_all[:, h:h + 1])
        kbe = heads(lambda h, hs: kbeta[:, hs] * col(egc_all, h))
        qe = heads(lambda h, hs: qn[:, hs] * col(egc_all, h))
        chains = []
        for h in range(B_HEADS):
            hs = slice(h * B_HEAD_DIM, (h + 1) * B_HEAD_DIM)
            gcc = jnp.broadcast_to(col(gc_all, h), (CHUNK, CHUNK))
            gcr = jnp.sum(gcc * eye_f, axis=0, keepdims=True)
            chains.append(dict(
                c=c, h=h, hs=hs,
                decay=jnp.exp(jnp.where(tril, gcc - gcr, -jnp.inf)),
                kq=jnp.concatenate([kbeta[:, hs], qn[:, hs]], axis=0).astype(BF),
                kn=kn[:, hs].astype(BF),
                rhs=jnp.concatenate([vbeta[:, hs], kbe[:, hs]], axis=1).astype(BF),
                kd=kn[:, hs] * col(ekd_all, h)))
        egl_row = heads(lambda h, hs: jnp.broadcast_to(col(egl, h), (1, B_HEAD_DIM)))
        return chains, qe.astype(BF), egl_row

    nt = lambda a, b: lax.dot_general(a, b, (((1,), (1,)), ((), ())), preferred_element_type=F32)
    mm = lambda a, b: jnp.dot(a, b, preferred_element_type=F32)

    def prep_levels(i):
        fronts = [prep(i * GDN_PREP_UNROLL + g) for g in range(GDN_PREP_UNROLL)]
        ch = [x for f in fronts for x in f[0]]
        aq = [nt(x["kq"], x["kn"]) for x in ch]
        kdt = [x["kd"].T for x in ch]
        yield
        qk = [(a[CHUNK:] * x["decay"]).astype(BF) for x, a in zip(ch, aq)]
        pw = [-jnp.where(strict, a[:CHUNK] * x["decay"], 0.0) for x, a in zip(ch, aq)]
        tmat = [eye_f + p for p in pw]
        pwb = [p.astype(BF) for p in pw]
        pw = [mm(p, p) for p in pwb]
        yield
        n_sq = CHUNK.bit_length() - 2
        for lvl in range(n_sq):
            pwb = [p.astype(BF) for p in pw]
            if lvl + 1 < n_sq:
                both = [mm(jnp.concatenate([t.astype(BF), p], axis=0), p) for t, p in zip(tmat, pwb)]
                yield
                tmat = [t + r[:CHUNK] for t, r in zip(tmat, both)]
                pw = [r[CHUNK:] for r in both]
            else:
                last = [mm(t.astype(BF), p) for t, p in zip(tmat, pwb)]
                yield
                tmat = [t + r for t, r in zip(tmat, last)]
        uw = [mm(t.astype(BF), x["rhs"]) for t, x in zip(tmat, ch)]
        yield
        for x, r, qkx, kt in zip(ch, uw, qk, kdt):
            u_ref[x["c"], :, x["hs"]] = r[:, :B_HEAD_DIM]
            wq_ref[x["c"], :CHUNK, x["hs"]] = r[:, B_HEAD_DIM:].astype(BF)
            qkkd_ref[x["c"], x["h"], :CHUNK, :] = qkx
            qkkd_ref[x["c"], x["h"], CHUNK:, :] = kt.astype(BF)
        for g, (_, qe_b, egl_row) in enumerate(fronts):
            wq_ref[i * GDN_PREP_UNROLL + g, CHUNK:, :] = qe_b
            egl_ref[i * GDN_PREP_UNROLL + g] = egl_row

    def step_levels(c):
        r0 = pl.multiple_of(c * CHUNK, CHUNK)
        st = state_ref[...]
        stb = st.astype(BF)
        hss = [slice(h * B_HEAD_DIM, (h + 1) * B_HEAD_DIM) for h in range(B_HEADS)]
        r1 = [mm(wq_ref[c, :, hs], stb[:, hs]) for hs in hss]
        yield
        v_new = [(u_ref[c, :, hs] - r[:CHUNK]).astype(BF) for hs, r in zip(hss, r1)]
        r2 = [mm(qkkd_ref[c, h], v_new[h]) for h in range(B_HEADS)]
        yield
        o_all = jnp.concatenate([a[CHUNK:] + b[:CHUNK] for a, b in zip(r1, r2)], axis=1)
        state_ref[...] = st * egl_ref[c] + jnp.concatenate([b[CHUNK:] for b in r2], axis=1)
        gate = go_ref[0, pl.ds(r0, CHUNK), :].astype(F32)
        o_n = heads(lambda h, hs: o_all[:, hs]
                    * lax.rsqrt(jnp.mean(o_all[:, hs] * o_all[:, hs], axis=-1, keepdims=True) + RMS_EPS)
                    * ng_ref[...])
        o_ref[0, pl.ds(r0, CHUNK), :] = (o_n * _silu(gate)).astype(BF)

    def steps_of(i):
        for g in range(GDN_PREP_UNROLL):
            yield from step_levels(i * GDN_PREP_UNROLL + g)

    def run_interleaved(*gens):
        live = list(gens)
        while live:
            for g in list(live):
                if next(g, StopIteration) is StopIteration:
                    live.remove(g)

    n_groups = n_chunks // GDN_PREP_UNROLL
    state_ref[...] = jnp.zeros_like(state_ref)
    run_interleaved(prep_levels(0))

    def body(i, carry):
        run_interleaved(steps_of(i - 1), prep_levels(i))
        return carry

    lax.fori_loop(1, n_groups, body, 0)
    run_interleaved(steps_of(n_groups - 1))


def _gdn(qkvb, bd, go, cw, alog, dtb, ng):
    b, l, _ = qkvb.shape
    nc = l // CHUNK
    full = lambda a: pl.BlockSpec(a.shape, lambda bi: (0,) * a.ndim)
    seq = lambda n: pl.BlockSpec((1, l, n), lambda bi: (bi, 0, 0))
    return pl.pallas_call(
        functools.partial(_gdn_kernel, n_chunks=nc),
        grid=(b,),
        in_specs=[seq(3 * B_WIDTH), seq(LANES), seq(B_WIDTH), full(cw), full(alog), full(dtb), full(ng)],
        out_specs=seq(B_WIDTH),
        out_shape=jax.ShapeDtypeStruct((b, l, B_WIDTH), BF),
        scratch_shapes=[pltpu.VMEM((nc, CHUNK, B_WIDTH), F32),
                        pltpu.VMEM((nc, 2 * CHUNK, B_WIDTH), BF),
                        pltpu.VMEM((nc, B_HEADS, CHUNK + B_HEAD_DIM, CHUNK), BF),
                        pltpu.VMEM((nc, 1, B_WIDTH), F32),
                        pltpu.VMEM((B_HEAD_DIM, B_WIDTH), F32)],
        compiler_params=_cparams("arbitrary"),
        name="gdn",
    )(qkvb, bd, go, cw, alog, dtb, ng)


def _merge_kernel(x_ref, oa_ref, ob_ref, oc_ref, wg_ref, gb_ref, wa_ref, wb_ref, wc_ref, wo_ref,
                  lg_ref, lb_ref, wrh_ref, wrl_ref, rb_ref, x1_ref, wts_ref, lpos_ref, tab_ref, cnt_ref, run_ref):
    x = x_ref[...]
    xb = x.astype(BF)
    y = None
    for i, (o_ref, w_ref) in enumerate(((oa_ref, wa_ref), (ob_ref, wb_ref), (oc_ref, wc_ref))):
        gl = jnp.dot(xb, wg_ref[:, i * D_MODEL:(i + 1) * D_MODEL], preferred_element_type=F32)
        gate = _sigmoid(gl + gb_ref[i:i + 1, :])
        br = gate * jnp.dot(o_ref[...], w_ref[...], preferred_element_type=F32)
        y = br if y is None else y + br
    mix = jnp.dot(y.astype(BF), wo_ref[...], preferred_element_type=F32)
    x1 = _layer_norm(DN_ALPHA * x + mix, lg_ref[...], lb_ref[...])
    x1_ref[...] = x1
    xh, xl = _split_bf16(x1)
    nt = lambda a, b: lax.dot_general(a, b, (((1,), (1,)), ((), ())), preferred_element_type=F32)
    lgt = nt(wrh_ref[...], xh) + (nt(wrh_ref[...], xl) + nt(wrl_ref[...], xh)) + rb_ref[...]

    @pl.when(pl.program_id(0) == 0)
    def _():
        run_ref[...] = jnp.zeros_like(run_ref)

    run = run_ref[...]
    for s in range(lgt.shape[1] // MOE_TILE):
        cols = slice(s * MOE_TILE, (s + 1) * MOE_TILE)
        wts_ref[:, cols], lpos_ref[:, cols], tab_ref[s], run = _route_tile(lgt[:, cols], run)
    run_ref[...] = run
    cnt_ref[...] = jnp.broadcast_to(run, cnt_ref.shape)


def _merge(x2d, oa, ob, oc, wg, gb, wa, wb, wc, wo, lg, lb, wrh, wrl, rb, *, tm):
    t = x2d.shape[0]
    full = lambda a: pl.BlockSpec(a.shape, lambda i: (0,) * a.ndim)
    row = lambda n: pl.BlockSpec((tm, n), lambda i: (i, 0))
    tok = lambda: pl.BlockSpec((MOE_TOPK, tm), lambda i: (0, i))
    assert tm % MOE_TILE == 0, (tm, MOE_TILE)
    return pl.pallas_call(
        _merge_kernel,
        grid=(t // tm,),
        in_specs=[row(D_MODEL), row(A_WIDTH), row(B_WIDTH), row(C_WIDTH), full(wg), full(gb), full(wa),
                  full(wb), full(wc), full(wo), full(lg), full(lb), full(wrh), full(wrl), full(rb)],
        out_specs=[row(D_MODEL), tok(), tok(), pl.BlockSpec((tm // MOE_TILE, SUBLANES, LANES), lambda i: (i, 0, 0)),
                   pl.BlockSpec((N_EXPERTS, LANES), lambda i: (0, 0))],
        out_shape=[jax.ShapeDtypeStruct((t, D_MODEL), F32),
                   jax.ShapeDtypeStruct((MOE_TOPK, t), F32),
                   jax.ShapeDtypeStruct((MOE_TOPK, t), I32),
                   jax.ShapeDtypeStruct((t // MOE_TILE, SUBLANES, LANES), I32),
                   jax.ShapeDtypeStruct((N_EXPERTS, LANES), F32)],
        scratch_shapes=[pltpu.VMEM((N_EXPERTS, 1), F32)],
        compiler_params=_cparams("arbitrary"),
        name="merge",
    )(x2d, oa, ob, oc, wg, gb, wa, wb, wc, wo, lg, lb, wrh, wrl, rb)


def _route_tile(lgt, run):
    tn = lgt.shape[1]
    le = lgt[0:N_EXPERTS, :]
    lg = lgt[N_EXPERTS:N_EXPERTS + MOE_GROUPS, :]
    gi = lax.broadcasted_iota(I32, (MOE_GROUPS, tn), 0).astype(F32)
    ei = lax.broadcasted_iota(I32, (N_EXPERTS, tn), 0).astype(F32)
    eg = jnp.right_shift(lax.broadcasted_iota(I32, (N_EXPERTS, tn), 0),
                         MOE_PER_GROUP.bit_length() - 1).astype(F32)
    mg = jnp.max(lg, axis=0, keepdims=True)
    gsel = jnp.min(jnp.where(lg == mg, gi, float(MOE_GROUPS)), axis=0, keepdims=True)
    p_top = 1.0 / jnp.sum(jnp.exp(lg - mg), axis=0, keepdims=True)
    l1 = jnp.where(eg == gsel, le, -jnp.inf)
    m1 = jnp.max(l1, axis=0, keepdims=True)
    i1 = jnp.min(jnp.where(l1 == m1, ei, float(N_EXPERTS)), axis=0, keepdims=True)
    l2 = jnp.where(ei == i1, -jnp.inf, l1)
    m2 = jnp.max(l2, axis=0, keepdims=True)
    i2 = jnp.min(jnp.where(l2 == m2, ei, float(N_EXPERTS)), axis=0, keepdims=True)
    e2 = jnp.exp(m2 - m1)
    den = 1.0 + e2
    wts = jnp.concatenate([p_top / den, p_top * (e2 / den)], axis=0)

    before = (lax.broadcasted_iota(I32, (tn, tn), 0) < lax.broadcasted_iota(I32, (tn, tn), 1)).astype(BF)
    oh = [ei == ik for ik in (i1, i2)]
    prefix = [jnp.dot(o.astype(BF), before, preferred_element_type=F32) for o in oh]
    cnt_k = [jnp.sum(o.astype(F32), axis=1, keepdims=True) for o in oh]
    cnt = cnt_k[0] + cnt_k[1]
    cnt_al = jnp.floor((cnt + (RUN_ALIGN - 1)) * (1.0 / RUN_ALIGN)) * RUN_ALIGN
    er = lax.broadcasted_iota(I32, (N_EXPERTS, N_EXPERTS), 0)
    ec = lax.broadcasted_iota(I32, (N_EXPERTS, N_EXPERTS), 1)
    lstart = jnp.dot((ec < er).astype(BF), jnp.broadcast_to(cnt_al, (N_EXPERTS, LANES)).astype(BF),
                     preferred_element_type=F32)[:, 0:1]
    first = [0.0, cnt_k[0]]
    pick = lambda k: jnp.sum(jnp.where(oh[k], prefix[k] + first[k] + lstart, 0.0), axis=0, keepdims=True)
    lpos = jnp.concatenate([pick(0), pick(1)], axis=0).astype(I32)
    eye_l = (lax.broadcasted_iota(I32, (N_EXPERTS, LANES), 0) == lax.broadcasted_iota(I32, (N_EXPERTS, LANES), 1))
    as_row = lambda colv: jnp.sum(jnp.where(eye_l, colv, 0.0), axis=0, keepdims=True)
    pieces = jnp.broadcast_to(jnp.sum(cnt_al, axis=0, keepdims=True) * (1.0 / RUN_ALIGN), (1, LANES))
    tab = jnp.concatenate([as_row(run), as_row(cnt), as_row(lstart), pieces,
                           jnp.zeros((SUBLANES - 4, LANES), F32)], axis=0).astype(I32)
    return wts, lpos, tab, run + cnt_al


def _layout_kernel(cnt_ref, tab_ref, gtab_ref, ttab_ref, bexp_ref, nused_ref, *, n_blocks):
    cnt = cnt_ref[...].astype(I32)
    shift = EXPERT_BLOCK.bit_length() - 1
    padded_blocks = jnp.right_shift(cnt + (EXPERT_BLOCK - 1), shift)
    er = lax.broadcasted_iota(I32, (N_EXPERTS, LANES), 0)
    el = lax.broadcasted_iota(I32, (N_EXPERTS, LANES), 1)
    start = jnp.zeros((N_EXPERTS, LANES), I32)
    for e in range(N_EXPERTS - 1):
        start = start + jnp.where(er > e, padded_blocks[e:e + 1, :], 0)
    end = start + padded_blocks
    as_row = lambda m: jnp.sum(jnp.where(er == el, m, 0), axis=0, keepdims=True)
    tab = tab_ref[...]
    first_row = lax.broadcasted_iota(I32, tab.shape, 1) == 0
    gtab_ref[...] = tab + jnp.where(first_row, as_row(start * EXPERT_BLOCK)[None], 0)
    tail_rows = padded_blocks * EXPERT_BLOCK - cnt
    end_all = end[N_EXPERTS - 1:N_EXPERTS, :]
    spare = lax.broadcasted_iota(I32, (1, LANES), 1) == N_EXPERTS
    ttab_ref[...] = jnp.concatenate(
        [jnp.where(spare, end_all * EXPERT_BLOCK, as_row(start * EXPERT_BLOCK + cnt)),
         jnp.where(spare, n_blocks - end_all, as_row(jnp.right_shift(tail_rows, RUN_ALIGN.bit_length() - 1))),
         jnp.zeros((SUBLANES - 2, LANES), I32)], axis=0)[None]
    blk = lax.broadcasted_iota(I32, (1, n_blocks), 1)
    owner = jnp.zeros((1, n_blocks), I32)
    for e in range(N_EXPERTS):
        owner = owner + (blk >= end[e:e + 1, 0:1]).astype(I32)
    bexp_ref[...] = jnp.minimum(owner, N_EXPERTS - 1)
    nused_ref[...] = end[N_EXPERTS - 1:N_EXPERTS, :]


def _layout(cnt, tab, *, n_blocks):
    full = lambda shape: pl.BlockSpec(shape, lambda i: (0,) * len(shape))
    return pl.pallas_call(
        functools.partial(_layout_kernel, n_blocks=n_blocks),
        grid=(1,),
        in_specs=[full(cnt.shape), full(tab.shape)],
        out_specs=[full(tab.shape), full((1, SUBLANES, LANES)), full((1, n_blocks)), full((1, LANES))],
        out_shape=[jax.ShapeDtypeStruct(tab.shape, I32),
                   jax.ShapeDtypeStruct((1, SUBLANES, LANES), I32),
                   jax.ShapeDtypeStruct((1, n_blocks), I32),
                   jax.ShapeDtypeStruct((1, LANES), I32)],
        compiler_params=_cparams("arbitrary"),
        name="layout",
    )(cnt, tab)


def _run_pieces(tab_ref, fn):
    for e in range(N_EXPERTS):
        sorted0, staging0 = tab_ref[0, 0, e], tab_ref[0, 2, e]
        n_pieces = (tab_ref[0, 1, e] + (RUN_ALIGN - 1)) // RUN_ALIGN

        def one(j, c, sorted0=sorted0, staging0=staging0):
            fn(pl.multiple_of(sorted0 + j * RUN_ALIGN, RUN_ALIGN), pl.multiple_of(staging0 + j * RUN_ALIGN, RUN_ALIGN))
            return c

        lax.fori_loop(0, n_pieces, one, 0)


WAIT_GROUPS = (64, 8, 1)


def _wait_pieces(tab_ref, make_copy):
    total = tab_ref[0, 3, 0]
    counts = (total // WAIT_GROUPS[0], (total // WAIT_GROUPS[1]) % (WAIT_GROUPS[0] // WAIT_GROUPS[1]),
              total % WAIT_GROUPS[1])
    for group, count in zip(WAIT_GROUPS, counts):
        def one(j, c, group=group):
            make_copy(group * RUN_ALIGN).wait()
            return c
        lax.fori_loop(0, count, one, 0)


def _dispatch_kernel(tab_ref, ptab_ref, ttab_ref, lpos_ref, x_ref, xs_ref, stage_ref, zero_ref, sems, tsem,
                     *, td):
    i = pl.program_id(0)
    last = pl.num_programs(0) - 1
    slot = lax.rem(i, 2)
    rows = stage_ref.shape[1]
    r = lax.broadcasted_iota(I32, (rows, td), 0)
    sel = jnp.logical_or(r == lpos_ref[0:1, :], r == lpos_ref[1:2, :]).astype(BF)
    stage_ref[slot] = jnp.dot(sel, x_ref[...].astype(BF), preferred_element_type=F32)

    def run_copy(sl, dst, src, rows=RUN_ALIGN):
        return pltpu.make_async_copy(stage_ref.at[sl, pl.ds(src, rows), :],
                                     xs_ref.at[pl.ds(dst, rows), :], sems.at[sl])

    def tail_copies(fn):
        def per_expert(e, carry):
            def one(j, c):
                dst = pl.multiple_of(ttab_ref[0, 0, e] + j * RUN_ALIGN, RUN_ALIGN)
                fn(pltpu.make_async_copy(zero_ref.at[pl.ds(0, RUN_ALIGN), :],
                                         xs_ref.at[pl.ds(dst, RUN_ALIGN), :], tsem))
                return c
            lax.fori_loop(0, ttab_ref[0, 1, e], one, 0)
            return carry
        lax.fori_loop(0, N_EXPERTS, per_expert, 0)

    def spare_copies(fn, first, stride):
        n_spare = ttab_ref[0, 1, N_EXPERTS]

        def one(j, c):
            dst = pl.multiple_of(ttab_ref[0, 0, N_EXPERTS] + (first + j * stride) * EXPERT_BLOCK, EXPERT_BLOCK)
            fn(pltpu.make_async_copy(zero_ref, xs_ref.at[pl.ds(dst, EXPERT_BLOCK), :], tsem))
            return c
        lax.fori_loop(0, (jnp.maximum(n_spare - first, 0) + stride - 1) // stride, one, 0)

    @pl.when(i == 0)
    def _():
        zero_ref[...] = jnp.zeros_like(zero_ref)
        tail_copies(lambda cp: cp.start())

    @pl.when(i > 0)
    def _():
        _wait_pieces(ptab_ref, lambda rows: run_copy(1 - slot, 0, 0, rows))

    _run_pieces(tab_ref, lambda dst, src: run_copy(slot, dst, src).start())
    spare_copies(lambda cp: cp.start(), i, last + 1)

    @pl.when(i == last)
    def _():
        _wait_pieces(tab_ref, lambda rows: run_copy(slot, 0, 0, rows))
        tail_copies(lambda cp: cp.wait())
        spare_copies(lambda cp: cp.wait(), 0, 1)


def _stage_rows(td):
    return MOE_TOPK * td + N_EXPERTS * RUN_ALIGN


def _dispatch(gtab, ttab, lpos, x1, *, td, cap):
    t = x1.shape[0]
    smem_tab = lambda imap: pl.BlockSpec((1, SUBLANES, LANES), imap, memory_space=pltpu.SMEM)
    return pl.pallas_call(
        functools.partial(_dispatch_kernel, td=td),
        grid=(t // td,),
        in_specs=[smem_tab(lambda i: (i, 0, 0)), smem_tab(lambda i: (jnp.maximum(i - 1, 0), 0, 0)),
                  smem_tab(lambda i: (0, 0, 0)),
                  pl.BlockSpec((MOE_TOPK, td), lambda i: (0, i)),
                  pl.BlockSpec((td, D_MODEL), lambda i: (i, 0))],
        out_specs=pl.BlockSpec(memory_space=pl.ANY),
        out_shape=jax.ShapeDtypeStruct((cap, D_MODEL), F32),
        scratch_shapes=[pltpu.VMEM((2, _stage_rows(td), D_MODEL), F32), pltpu.VMEM((EXPERT_BLOCK, D_MODEL), F32),
                        pltpu.SemaphoreType.DMA((2,)), pltpu.SemaphoreType.DMA(())],
        compiler_params=_cparams("arbitrary"),
        name="dispatch",
    )(gtab, gtab, ttab, lpos, x1)


def _expert_kernel(bexp_ref, nused_ref, xs_ref, w1_ref, w3_ref, w2_ref, y_ref):
    del bexp_ref
    i = pl.program_id(0)

    @pl.when(i < nused_ref[0])
    def _():
        xb = xs_ref[...].astype(BF)
        h1 = jnp.dot(xb, w1_ref[0], preferred_element_type=F32)
        h3 = jnp.dot(xb, w3_ref[0], preferred_element_type=F32)
        hid = (_silu(h1) * h3).astype(BF)
        y_ref[...] = jnp.dot(hid, w2_ref[0], preferred_element_type=F32)

    @pl.when(i >= nused_ref[0])
    def _():
        y_ref[...] = jnp.zeros_like(y_ref)


def _expert(bexp, nused, xs, w1, w3, w2):
    cap = xs.shape[0]
    nb = cap // EXPERT_BLOCK
    last = lambda i, nu: jnp.minimum(i, nu[0] - 1)
    grid_spec = pltpu.PrefetchScalarGridSpec(
        num_scalar_prefetch=2,
        grid=(nb,),
        in_specs=[pl.BlockSpec((EXPERT_BLOCK, D_MODEL), lambda i, be, nu: (last(i, nu), 0)),
                  pl.BlockSpec((1, D_MODEL, MOE_FF), lambda i, be, nu: (be[last(i, nu)], 0, 0)),
                  pl.BlockSpec((1, D_MODEL, MOE_FF), lambda i, be, nu: (be[last(i, nu)], 0, 0)),
                  pl.BlockSpec((1, MOE_FF, D_MODEL), lambda i, be, nu: (be[last(i, nu)], 0, 0))],
        out_specs=pl.BlockSpec((EXPERT_BLOCK, D_MODEL), lambda i, be, nu: (i, 0)),
    )
    return pl.pallas_call(
        _expert_kernel,
        grid_spec=grid_spec,
        out_shape=jax.ShapeDtypeStruct((cap, D_MODEL), F32),
        compiler_params=_cparams("arbitrary"),
        name="expert",
    )(bexp, nused, xs, w1, w3, w2)


def _combine_kernel(tab_ref, ntab_ref, lpos_ref, lposr_ref, wr_ref, x_ref, lg_ref, lb_ref, yb_ref, o_ref, stage_ref,
                    sems,
                    *, tc):
    i = pl.program_id(0)
    last = pl.num_programs(0) - 1
    slot = lax.rem(i, 2)

    def run_copy(sl, src, dst, rows=RUN_ALIGN):
        return pltpu.make_async_copy(yb_ref.at[pl.ds(src, rows), :],
                                     stage_ref.at[sl, pl.ds(dst, rows), :], sems.at[sl])

    @pl.when(i == 0)
    def _():
        stage_ref[...] = jnp.zeros_like(stage_ref)
        _run_pieces(tab_ref, lambda src, dst: run_copy(slot, src, dst).start())

    @pl.when(i < last)
    def _():
        _run_pieces(ntab_ref, lambda src, dst: run_copy(1 - slot, src, dst).start())

    _wait_pieces(tab_ref, lambda rows: run_copy(slot, 0, 0, rows))
    rows = stage_ref.shape[1]
    r = lax.broadcasted_iota(I32, (rows, tc), 0)
    row_w = jnp.sum(jnp.where(r == lposr_ref[0:1, :], wr_ref[0:1, :], 0.0)
                    + jnp.where(r == lposr_ref[1:2, :], wr_ref[1:2, :], 0.0), axis=1, keepdims=True)
    staged = (stage_ref[slot] * row_w).astype(BF)
    c = lax.broadcasted_iota(I32, (tc, rows), 1)
    pick = jnp.logical_or(c == lpos_ref[:, 0:1], c == lpos_ref[:, 1:2]).astype(BF)
    ffn = jnp.dot(pick, staged, preferred_element_type=F32)
    o_ref[...] = _layer_norm(DN_ALPHA * x_ref[...] + ffn, lg_ref[...], lb_ref[...])


def _combine(gtab, lpos, wts, x1, lg, lb, yb, *, tc):
    t = x1.shape[0]
    n_tiles = t // tc
    full = lambda a: pl.BlockSpec(a.shape, lambda i: (0,) * a.ndim)
    smem_tab = lambda imap: pl.BlockSpec((1, SUBLANES, LANES), imap, memory_space=pltpu.SMEM)
    return pl.pallas_call(
        functools.partial(_combine_kernel, tc=tc),
        grid=(n_tiles,),
        in_specs=[smem_tab(lambda i: (i, 0, 0)), smem_tab(lambda i: (jnp.minimum(i + 1, n_tiles - 1), 0, 0)),
                  pl.BlockSpec((tc, MOE_TOPK), lambda i: (i, 0)),
                  pl.BlockSpec((MOE_TOPK, tc), lambda i: (0, i)),
                  pl.BlockSpec((MOE_TOPK, tc), lambda i: (0, i)),
                  pl.BlockSpec((tc, D_MODEL), lambda i: (i, 0)),
                  full(lg), full(lb),
                  pl.BlockSpec(memory_space=pl.ANY)],
        out_specs=pl.BlockSpec((tc, D_MODEL), lambda i: (i, 0)),
        out_shape=jax.ShapeDtypeStruct((t, D_MODEL), F32),
        scratch_shapes=[pltpu.VMEM((2, _stage_rows(tc), D_MODEL), F32), pltpu.SemaphoreType.DMA((2,))],
        compiler_params=_cparams("arbitrary"),
        name="combine",
    )(gtab, gtab, lpos.T, lpos, wts, x1, lg, lb, yb)


def _tile(n, want):
    while n % want:
        want //= 2
    return want


def _mixer(x2d, b, l, w_in, rel_bias, conv_w, a_log, dt_bias, norm_g, pool_w, pool_scale,
           w_br_a, w_br_b, w_br_c, gate_b, w_out, ln_g, ln_b, wr, rb):
    t = x2d.shape[0]
    c0 = 3 * A_WIDTH
    c1 = c0 + 3 * B_WIDTH
    c2 = c1 + 2 * B_HEADS
    c3 = c2 + B_WIDTH
    c4 = c3 + C_WIDTH
    wbf = w_in.astype(BF)
    wbd = jnp.pad(wbf[:, c1:c2], ((0, 0), (0, LANES - 2 * B_HEADS)))
    qkva, qkvb, bd, go, oc = _proj(
        x2d, wbf[:, :c0], wbf[:, c0:c1], wbd, wbf[:, c2:c3], wbf[:, c3:c4],
        pool_w.astype(BF), pool_scale.reshape(1, C_WIDTH), seq=l, tm=_tile(l, DENSE_TILE))

    oa = _attn(qkva.reshape(b, l, 3 * A_WIDTH), _attn_bias_table(rel_bias)).reshape(t, A_WIDTH)

    lane_pad = lambda v: jnp.pad(v.reshape(1, B_HEADS), ((0, 0), (B_HEADS, LANES - 2 * B_HEADS)))
    ob = _gdn(qkvb.reshape(b, l, 3 * B_WIDTH), bd.reshape(b, l, LANES), go.reshape(b, l, B_WIDTH),
              conv_w, lane_pad(a_log), lane_pad(dt_bias), norm_g.reshape(1, B_HEAD_DIM)).reshape(t, B_WIDTH)

    wrh, wrl = _split_bf16(wr)
    return _merge(x2d, oa, ob, oc, wbf[:, c4:], gate_b, w_br_a.astype(BF), w_br_b.astype(BF),
                  w_br_c.astype(BF), w_out.astype(BF), ln_g.reshape(1, -1), ln_b.reshape(1, -1),
                  wrh, wrl, rb, tm=_tile(t, DENSE_TILE))


def _moe(x1, wts, lpos, tab, cnt, w1, w3, w2, ln_g, ln_b):
    t = x1.shape[0]
    tile = MOE_TILE
    max_rows = t * MOE_TOPK + (t // tile) * N_EXPERTS * (RUN_ALIGN - 1)
    n_blocks = -(-max_rows // EXPERT_BLOCK) + N_EXPERTS
    cap = n_blocks * EXPERT_BLOCK
    gtab, ttab, bexp, nused = _layout(cnt, tab, n_blocks=n_blocks)
    xs = _dispatch(gtab, ttab, lpos, x1, td=tile, cap=cap)
    yb = _expert(bexp.reshape(n_blocks), nused[0, :1], xs, w1.astype(BF), w3.astype(BF), w2.astype(BF))
    return _combine(gtab, lpos, wts, x1, ln_g.reshape(1, -1), ln_b.reshape(1, -1), yb, tc=tile)


def kernel(x, w_in, attn_rel_bias, gdn_conv_w, gdn_a_log, gdn_dt_bias, gdn_norm_g, pool_w, pool_scale,
           w_branch_a, w_branch_b, w_branch_c, gate_b, w_out, ln1_g, ln1_b, router_group_w,
           router_group_b, router_expert_w, router_expert_b, moe_w1, moe_w3, moe_w2, ln2_g, ln2_b):
    b, l, d = x.shape
    x2d = x.reshape(b * l, d)
    for i in range(DEPTH):
        wr = jnp.concatenate([router_expert_w[i].T, router_group_w[i].T,
                              jnp.zeros((ROUTER_ROWS - N_EXPERTS - MOE_GROUPS, d), F32)], axis=0)
        rb = jnp.concatenate([router_expert_b[i], router_group_b[i],
                              jnp.zeros((ROUTER_ROWS - N_EXPERTS - MOE_GROUPS,), F32)]).reshape(ROUTER_ROWS, 1)
        x1, *routing = _mixer(x2d, b, l, w_in[i], attn_rel_bias[i], gdn_conv_w[i], gdn_a_log[i], gdn_dt_bias[i],
                         gdn_norm_g[i], pool_w[i], pool_scale[i], w_branch_a[i], w_branch_b[i],
                         w_branch_c[i], gate_b[i], w_out[i], ln1_g[i], ln1_b[i], wr, rb)
        x2d = _moe(x1, *routing, moe_w1[i], moe_w3[i], moe_w2[i], ln2_g[i], ln2_b[i])
    return x2d.reshape(b, l, d)
```

```python
import functools

import jax
import jax.numpy as jnp
from jax import lax
from jax.experimental import pallas as pl
from jax.experimental.pallas import tpu as pltpu

BF = jnp.bfloat16
F32 = jnp.float32
I32 = jnp.int32

D_MODEL = 1024
DEPTH = 2
CHUNK = 64

A_HEADS = 8
A_HEAD_DIM = 64
A_WIDTH = A_HEADS * A_HEAD_DIM
A_LEFT_CHUNKS = 8
A_REL_CLIP = 256

B_HEADS = 4
B_HEAD_DIM = 128
B_WIDTH = B_HEADS * B_HEAD_DIM
B_CONV = 4

C_WINDOWS = (2, 4, 8, 16)
C_GROUPS = 4
C_GROUP_DIM = 128
C_WIDTH = C_GROUPS * C_GROUP_DIM

N_BRANCH = 3
MOE_GROUPS = 4
MOE_PER_GROUP = 8
N_EXPERTS = MOE_GROUPS * MOE_PER_GROUP
MOE_TOPK = 2
MOE_FF = 512

DN_ALPHA = (2 * DEPTH) ** 0.25
LN_EPS = 1e-5
RMS_EPS = 1e-6

LANES = 128
SUBLANES = 8
VMEM_LIMIT_BYTES = 56 * 1024 * 1024

QBLOCK = 2 * CHUNK
KBAND = (A_LEFT_CHUNKS + 2) * CHUNK
NEG_BIG = -1e30

DENSE_TILE = 1024
ROUTER_ROWS = 40
EXPERT_BLOCK = 512
MOE_TILE = 512
RUN_ALIGN = 8


def _cparams(*sem):
    return pltpu.CompilerParams(dimension_semantics=sem, vmem_limit_bytes=VMEM_LIMIT_BYTES)


def _split_bf16(a):
    hi = a.astype(BF)
    lo = (a - hi.astype(F32)).astype(BF)
    return hi, lo


def _sigmoid(x):
    return 1.0 / (1.0 + jnp.exp(-x))


def _silu(x):
    return x * _sigmoid(x)


def _layer_norm(z, g, b):
    mu = jnp.mean(z, axis=-1, keepdims=True)
    zc = z - mu
    var = jnp.mean(zc * zc, axis=-1, keepdims=True)
    return zc * lax.rsqrt(var + LN_EPS) * g + b


POOL_TAIL = 16


def _proj_kernel(x_ref, wa_ref, wb_ref, wbd_ref, wgo_ref, wc_ref, pw_ref, ps_ref,
                 qkva_ref, qkvb_ref, bd_ref, go_ref, oc_ref, tail_ref, *, tiles_per_seq, tm):
    it = pl.program_id(0) % tiles_per_seq
    xb = x_ref[...].astype(BF)
    qkva_ref[...] = jnp.dot(xb, wa_ref[...], preferred_element_type=F32).astype(BF)
    qkvb_ref[...] = jnp.dot(xb, wb_ref[...], preferred_element_type=F32).astype(BF)
    bd_ref[...] = jnp.dot(xb, wbd_ref[...], preferred_element_type=F32)
    go_ref[...] = jnp.dot(xb, wgo_ref[...], preferred_element_type=F32).astype(BF)
    u = jnp.dot(xb, wc_ref[...], preferred_element_type=F32)

    @pl.when(it == 0)
    def _():
        tail_ref[...] = jnp.zeros_like(tail_ref)

    ext = jnp.concatenate([tail_ref[...], u], axis=0)
    tail_ref[...] = u[tm - POOL_TAIL:, :]
    pos = (it * tm + 1 + lax.broadcasted_iota(I32, (tm, 1), 0)).astype(F32)
    s = ext
    for gi, win in enumerate(C_WINDOWS):
        if gi:
            s = s[:, C_GROUP_DIM:]
        s = s + pltpu.roll(s, win // 2, 0)
        lo_c, hi_c = gi * C_GROUP_DIM, (gi + 1) * C_GROUP_DIM
        cnt = jnp.minimum(pos, float(win))
        pooled = s[POOL_TAIL:, :C_GROUP_DIM] / cnt - u[:, lo_c:hi_c]
        mixed = jnp.dot(pooled.astype(BF), pw_ref[gi], preferred_element_type=F32)
        oc_ref[:, lo_c:hi_c] = (mixed * ps_ref[:, lo_c:hi_c]).astype(BF)


def _proj(x2d, wa, wb, wbd, wgo, wc, pw, ps, *, seq, tm):
    t = x2d.shape[0]
    full = lambda a: pl.BlockSpec(a.shape, lambda i: (0,) * a.ndim)
    row = lambda n: pl.BlockSpec((tm, n), lambda i: (i, 0))
    return pl.pallas_call(
        functools.partial(_proj_kernel, tiles_per_seq=seq // tm, tm=tm),
        grid=(t // tm,),
        in_specs=[row(D_MODEL), full(wa), full(wb), full(wbd), full(wgo), full(wc), full(pw), full(ps)],
        out_specs=[row(3 * A_WIDTH), row(3 * B_WIDTH), row(LANES), row(B_WIDTH), row(C_WIDTH)],
        out_shape=[jax.ShapeDtypeStruct((t, 3 * A_WIDTH), BF),
                   jax.ShapeDtypeStruct((t, 3 * B_WIDTH), BF),
                   jax.ShapeDtypeStruct((t, LANES), F32),
                   jax.ShapeDtypeStruct((t, B_WIDTH), BF),
                   jax.ShapeDtypeStruct((t, C_WIDTH), BF)],
        scratch_shapes=[pltpu.VMEM((POOL_TAIL, C_WIDTH), F32)],
        compiler_params=_cparams("arbitrary"),
        name="proj",
    )(x2d, wa, wb, wbd, wgo, wc, pw, ps)


def _attn_bias_table(rel_bias):
    h = rel_bias.shape[0]
    pad = A_LEFT_CHUNKS * CHUNK
    width = KBAND + pad
    n = width + QBLOCK - 1
    far = jnp.repeat(rel_bias[:, -1:], QBLOCK - 1 + pad - A_REL_CLIP, axis=1)
    near = jnp.repeat(rel_bias[:, :1], n - far.shape[1] - rel_bias.shape[1], axis=1)
    w = jnp.concatenate([far, rel_bias[:, ::-1], near], axis=1).astype(F32)
    w = jnp.roll(w, -(QBLOCK - 1), axis=1)
    toep = jnp.tile(w, (1, QBLOCK))[:, :QBLOCK * (n - 1)].reshape(h, QBLOCK, n - 1)[:, :, :width]
    r = jnp.arange(QBLOCK)[:, None]
    m = jnp.arange(width)[None, :]
    kc, qc = m // CHUNK, r // CHUNK
    valid = (kc >= qc) & (kc <= qc + A_LEFT_CHUNKS)
    tab = jnp.where(valid[None], toep, NEG_BIG)
    return tab.reshape(h, QBLOCK, width // LANES, LANES).transpose(0, 2, 1, 3)


ATTN_QBLOCKS_PER_STEP = 8


def _attn_kernel(q_ref, k_ref, v_ref, tab_ref, o_ref):
    first_blk = A_LEFT_CHUNKS * CHUNK // QBLOCK
    lane = lax.broadcasted_iota(I32, (QBLOCK, LANES), 1)
    lo = lane < A_HEAD_DIM
    nkb = KBAND // LANES
    items = []
    for sub in range(ATTN_QBLOCKS_PER_STEP):
        j = pl.program_id(1) * ATTN_QBLOCKS_PER_STEP + sub
        kstart = pl.multiple_of(jnp.maximum(j - first_blk, 0) * QBLOCK, QBLOCK)
        offb = jnp.maximum(first_blk - j, 0)
        for hp in range(A_HEADS // 2):
            items.append((slice(sub * QBLOCK, (sub + 1) * QBLOCK), kstart, offb, hp,
                          slice(hp * LANES, (hp + 1) * LANES)))
    scores = []
    for rows, kstart, offb, hp, cols in items:
        qp = q_ref[0, rows, cols] * jnp.asarray(A_HEAD_DIM ** -0.5, BF)
        zero = jnp.zeros_like(qp)
        q2 = jnp.concatenate([jnp.where(lo, qp, zero), jnp.where(lo, zero, qp)], axis=0)
        kp = k_ref[0, pl.ds(kstart, KBAND), cols]
        scores.append(lax.dot_general(q2, kp, (((1,), (1,)), ((), ())), preferred_element_type=F32))
    probs, denoms = [], []
    for (rows, kstart, offb, hp, cols), sc in zip(items, scores):
        bias = jnp.concatenate(
            [jnp.concatenate([tab_ref[2 * hp + par, offb + i] for i in range(nkb)], axis=1) for par in range(2)],
            axis=0)
        s = sc + bias
        p = jnp.exp(s - jnp.max(s, axis=-1, keepdims=True))
        denoms.append(jnp.sum(p, axis=-1, keepdims=True))
        probs.append(p.astype(BF))
    for (rows, kstart, offb, hp, cols), p, den in zip(items, probs, denoms):
        vp = v_ref[0, pl.ds(kstart, KBAND), cols]
        o2 = jnp.dot(p, vp, preferred_element_type=F32) / den
        o_ref[0, rows, cols] = jnp.where(lo, o2[:QBLOCK], o2[QBLOCK:]).astype(BF)


def _attn(qkva, tab):
    b, l, _ = qkva.shape
    step_rows = ATTN_QBLOCKS_PER_STEP * QBLOCK
    return pl.pallas_call(
        _attn_kernel,
        grid=(b, l // step_rows),
        in_specs=[pl.BlockSpec((1, step_rows, A_WIDTH), lambda bi, j: (bi, j, 0)),
                  pl.BlockSpec((1, l, A_WIDTH), lambda bi, j: (bi, 0, 1)),
                  pl.BlockSpec((1, l, A_WIDTH), lambda bi, j: (bi, 0, 2)),
                  pl.BlockSpec(tab.shape, lambda bi, j: (0, 0, 0, 0))],
        out_specs=pl.BlockSpec((1, step_rows, A_WIDTH), lambda bi, j: (bi, j, 0)),
        out_shape=jax.ShapeDtypeStruct((b, l, A_WIDTH), BF),
        compiler_params=_cparams("arbitrary", "arbitrary"),
        name="attn",
    )(qkva, qkva, qkva, tab)


CONV_HIST = 16
GDN_PREP_UNROLL = 4


def _gdn_kernel(qkv_ref, bd_ref, go_ref, cw_ref, alog_ref, dtb_ref, ng_ref, o_ref,
                u_ref, wq_ref, qkkd_ref, egl_ref, state_ref, *, n_chunks):
    rt = lax.broadcasted_iota(I32, (CHUNK, CHUNK), 0)
    ct = lax.broadcasted_iota(I32, (CHUNK, CHUNK), 1)
    tril = rt >= ct
    strict = rt > ct
    row_c = lax.broadcasted_iota(I32, (CHUNK, LANES), 0)
    eye_f = (rt == ct).astype(F32)
    neg_rate = -jnp.exp(alog_ref[...])
    sr = lax.broadcasted_iota(I32, (B_CONV * CHUNK, CONV_HIST + CHUNK), 0)
    sc = lax.broadcasted_iota(I32, (B_CONV * CHUNK, CONV_HIST + CHUNK), 1)
    shift_sel = (sc == CONV_HIST + jnp.bitwise_and(sr, CHUNK - 1)
                 - jnp.right_shift(sr, CHUNK.bit_length() - 1)).astype(BF)

    def heads(fn):
        return jnp.concatenate([fn(h, slice(h * B_HEAD_DIM, (h + 1) * B_HEAD_DIM)) for h in range(B_HEADS)],
                               axis=1)

    def prep(c):
        r0 = pl.multiple_of(c * CHUNK, CHUNK)
        cur = qkv_ref[0, pl.ds(r0, CHUNK), :]
        h0 = pl.multiple_of(jnp.maximum(r0 - CONV_HIST, 0), CONV_HIST)
        hist = qkv_ref[0, pl.ds(h0, CONV_HIST), :]
        hist = jnp.where(c > 0, hist, jnp.zeros_like(hist))
        taps = jnp.dot(shift_sel, jnp.concatenate([hist, cur], axis=0), preferred_element_type=F32)
        acc = taps[:CHUNK] * cw_ref[B_CONV - 1:B_CONV, :]
        for back in range(1, B_CONV):
            tap = B_CONV - 1 - back
            acc = acc + taps[back * CHUNK:(back + 1) * CHUNK] * cw_ref[tap:tap + 1, :]
        act = _silu(acc)
        q_all, k_all, v_all = act[:, :B_WIDTH], act[:, B_WIDTH:2 * B_WIDTH], act[:, 2 * B_WIDTH:]

        bd = bd_ref[0, pl.ds(r0, CHUNK), :]
        beta_all = _sigmoid(bd)
        z = bd + dtb_ref[...]
        softplus = jnp.maximum(z, 0.0) + jnp.log1p(jnp.exp(-jnp.abs(z)))
        g_all = neg_rate * softplus
        gc_all = g_all
        for sh in (1, 2, 4, 8, 16, 32):
            gc_all = gc_all + jnp.where(row_c >= sh, pltpu.roll(gc_all, sh, 0), 0.0)
        egc_all = jnp.exp(gc_all)
        glast = gc_all[CHUNK - 1:CHUNK, :]
        ekd_all = jnp.exp(glast - gc_all)
        egl = jnp.exp(glast)
        col = lambda m, h: m[:, B_HEADS + h:B_HEADS + h + 1]

        def l2n(m):
            return m * lax.rsqrt(jnp.sum(m * m, axis=-1, keepdims=True) + RMS_EPS)

        qn = heads(lambda h, hs: l2n(q_all[:, hs]) * (B_HEAD_DIM ** -0.5))
        kn = heads(lambda h, hs: l2n(k_all[:, hs]))
        kbeta = heads(lambda h, hs: kn[:, hs] * beta_all[:, h:h + 1])
        vbeta = heads(lambda h, hs: v_all[:, hs] * beta_all[:, h:h + 1])
        kbe = heads(lambda h, hs: kbeta[:, hs] * col(egc_all, h))
        qe = heads(lambda h, hs: qn[:, hs] * col(egc_all, h))
        chains = []
        for h in range(B_HEADS):
            hs = slice(h * B_HEAD_DIM, (h + 1) * B_HEAD_DIM)
            gcc = jnp.broadcast_to(col(gc_all, h), (CHUNK, CHUNK))
            gcr = jnp.sum(gcc * eye_f, axis=0, keepdims=True)
            chains.append(dict(
                c=c, h=h, hs=hs,
                decay=jnp.exp(jnp.where(tril, gcc - gcr, -jnp.inf)),
                kq=jnp.concatenate([kbeta[:, hs], qn[:, hs]], axis=0).astype(BF),
                kn=kn[:, hs].astype(BF),
                rhs=jnp.concatenate([vbeta[:, hs], kbe[:, hs]], axis=1).astype(BF),
                kd=kn[:, hs] * col(ekd_all, h)))
        egl_row = heads(lambda h, hs: jnp.broadcast_to(col(egl, h), (1, B_HEAD_DIM)))
        return chains, qe.astype(BF), egl_row

    nt = lambda a, b: lax.dot_general(a, b, (((1,), (1,)), ((), ())), preferred_element_type=F32)
    mm = lambda a, b: jnp.dot(a, b, preferred_element_type=F32)

    def prep_levels(i):
        fronts = [prep(i * GDN_PREP_UNROLL + g) for g in range(GDN_PREP_UNROLL)]
        ch = [x for f in fronts for x in f[0]]
        aq = [nt(x["kq"], x["kn"]) for x in ch]
        kdt = [x["kd"].T for x in ch]
        yield
        qk = [(a[CHUNK:] * x["decay"]).astype(BF) for x, a in zip(ch, aq)]
        pw = [-jnp.where(strict, a[:CHUNK] * x["decay"], 0.0) for x, a in zip(ch, aq)]
        tmat = [eye_f + p for p in pw]
        pwb = [p.astype(BF) for p in pw]
        pw = [mm(p, p) for p in pwb]
        yield
        n_sq = CHUNK.bit_length() - 2
        for lvl in range(n_sq):
            pwb = [p.astype(BF) for p in pw]
            if lvl + 1 < n_sq:
                both = [mm(jnp.concatenate([t.astype(BF), p], axis=0), p) for t, p in zip(tmat, pwb)]
                yield
                tmat = [t + r[:CHUNK] for t, r in zip(tmat, both)]
                pw = [r[CHUNK:] for r in both]
            else:
                last = [mm(t.astype(BF), p) for t, p in zip(tmat, pwb)]
                yield
                tmat = [t + r for t, r in zip(tmat, last)]
        uw = [mm(t.astype(BF), x["rhs"]) for t, x in zip(tmat, ch)]
        yield
        for x, r, qkx, kt in zip(ch, uw, qk, kdt):
            u_ref[x["c"], :, x["hs"]] = r[:, :B_HEAD_DIM]
            wq_ref[x["c"], :CHUNK, x["hs"]] = r[:, B_HEAD_DIM:].astype(BF)
            qkkd_ref[x["c"], x["h"], :CHUNK, :] = qkx
            qkkd_ref[x["c"], x["h"], CHUNK:, :] = kt.astype(BF)
        for g, (_, qe_b, egl_row) in enumerate(fronts):
            wq_ref[i * GDN_PREP_UNROLL + g, CHUNK:, :] = qe_b
            egl_ref[i * GDN_PREP_UNROLL + g] = egl_row

    def step_levels(c):
        r0 = pl.multiple_of(c * CHUNK, CHUNK)
        st = state_ref[...]
        stb = st.astype(BF)
        hss = [slice(h * B_HEAD_DIM, (h + 1) * B_HEAD_DIM) for h in range(B_HEADS)]
        r1 = [mm(wq_ref[c, :, hs], stb[:, hs]) for hs in hss]
        yield
        v_new = [(u_ref[c, :, hs] - r[:CHUNK]).astype(BF) for hs, r in zip(hss, r1)]
        r2 = [mm(qkkd_ref[c, h], v_new[h]) for h in range(B_HEADS)]
        yield
        o_all = jnp.concatenate([a[CHUNK:] + b[:CHUNK] for a, b in zip(r1, r2)], axis=1)
        state_ref[...] = st * egl_ref[c] + jnp.concatenate([b[CHUNK:] for b in r2], axis=1)
        gate = go_ref[0, pl.ds(r0, CHUNK), :].astype(F32)
        o_n = heads(lambda h, hs: o_all[:, hs]
                    * lax.rsqrt(jnp.mean(o_all[:, hs] * o_all[:, hs], axis=-1, keepdims=True) + RMS_EPS)
                    * ng_ref[...])
        o_ref[0, pl.ds(r0, CHUNK), :] = (o_n * _silu(gate)).astype(BF)

    def steps_of(i):
        for g in range(GDN_PREP_UNROLL):
            yield from step_levels(i * GDN_PREP_UNROLL + g)

    def run_interleaved(*gens):
        live = list(gens)
        while live:
            for g in list(live):
                if next(g, StopIteration) is StopIteration:
                    live.remove(g)

    n_groups = n_chunks // GDN_PREP_UNROLL
    state_ref[...] = jnp.zeros_like(state_ref)
    run_interleaved(prep_levels(0))

    def body(i, carry):
        run_interleaved(steps_of(i - 1), prep_levels(i))
        return carry

    lax.fori_loop(1, n_groups, body, 0)
    run_interleaved(steps_of(n_groups - 1))


def _gdn(qkvb, bd, go, cw, alog, dtb, ng):
    b, l, _ = qkvb.shape
    nc = l // CHUNK
    full = lambda a: pl.BlockSpec(a.shape, lambda bi: (0,) * a.ndim)
    seq = lambda n: pl.BlockSpec((1, l, n), lambda bi: (bi, 0, 0))
    return pl.pallas_call(
        functools.partial(_gdn_kernel, n_chunks=nc),
        grid=(b,),
        in_specs=[seq(3 * B_WIDTH), seq(LANES), seq(B_WIDTH), full(cw), full(alog), full(dtb), full(ng)],
        out_specs=seq(B_WIDTH),
        out_shape=jax.ShapeDtypeStruct((b, l, B_WIDTH), BF),
        scratch_shapes=[pltpu.VMEM((nc, CHUNK, B_WIDTH), F32),
                        pltpu.VMEM((nc, 2 * CHUNK, B_WIDTH), BF),
                        pltpu.VMEM((nc, B_HEADS, CHUNK + B_HEAD_DIM, CHUNK), BF),
                        pltpu.VMEM((nc, 1, B_WIDTH), F32),
                        pltpu.VMEM((B_HEAD_DIM, B_WIDTH), F32)],
        compiler_params=_cparams("arbitrary"),
        name="gdn",
    )(qkvb, bd, go, cw, alog, dtb, ng)


def _merge_kernel(x_ref, oa_ref, ob_ref, oc_ref, wg_ref, gb_ref, wa_ref, wb_ref, wc_ref, wo_ref,
                  lg_ref, lb_ref, wrh_ref, wrl_ref, rb_ref, x1_ref, wts_ref, lpos_ref, tab_ref, cnt_ref, run_ref):
    x = x_ref[...]
    xb = x.astype(BF)
    y = None
    for i, (o_ref, w_ref) in enumerate(((oa_ref, wa_ref), (ob_ref, wb_ref), (oc_ref, wc_ref))):
        gl = jnp.dot(xb, wg_ref[:, i * D_MODEL:(i + 1) * D_MODEL], preferred_element_type=F32)
        gate = _sigmoid(gl + gb_ref[i:i + 1, :])
        br = gate * jnp.dot(o_ref[...], w_ref[...], preferred_element_type=F32)
        y = br if y is None else y + br
    mix = jnp.dot(y.astype(BF), wo_ref[...], preferred_element_type=F32)
    x1 = _layer_norm(DN_ALPHA * x + mix, lg_ref[...], lb_ref[...])
    x1_ref[...] = x1
    xh, xl = _split_bf16(x1)
    nt = lambda a, b: lax.dot_general(a, b, (((1,), (1,)), ((), ())), preferred_element_type=F32)
    lgt = nt(wrh_ref[...], xh) + (nt(wrh_ref[...], xl) + nt(wrl_ref[...], xh)) + rb_ref[...]

    @pl.when(pl.program_id(0) == 0)
    def _():
        run_ref[...] = jnp.zeros_like(run_ref)

    run = run_ref[...]
    for s in range(lgt.shape[1] // MOE_TILE):
        cols = slice(s * MOE_TILE, (s + 1) * MOE_TILE)
        wts_ref[:, cols], lpos_ref[:, cols], tab_ref[s], run = _route_tile(lgt[:, cols], run)
    run_ref[...] = run
    cnt_ref[...] = jnp.broadcast_to(run, cnt_ref.shape)


def _merge(x2d, oa, ob, oc, wg, gb, wa, wb, wc, wo, lg, lb, wrh, wrl, rb, *, tm):
    t = x2d.shape[0]
    full = lambda a: pl.BlockSpec(a.shape, lambda i: (0,) * a.ndim)
    row = lambda n: pl.BlockSpec((tm, n), lambda i: (i, 0))
    tok = lambda: pl.BlockSpec((MOE_TOPK, tm), lambda i: (0, i))
    assert tm % MOE_TILE == 0, (tm, MOE_TILE)
    return pl.pallas_call(
        _merge_kernel,
        grid=(t // tm,),
        in_specs=[row(D_MODEL), row(A_WIDTH), row(B_WIDTH), row(C_WIDTH), full(wg), full(gb), full(wa),
                  full(wb), full(wc), full(wo), full(lg), full(lb), full(wrh), full(wrl), full(rb)],
        out_specs=[row(D_MODEL), tok(), tok(), pl.BlockSpec((tm // MOE_TILE, SUBLANES, LANES), lambda i: (i, 0, 0)),
                   pl.BlockSpec((N_EXPERTS, LANES), lambda i: (0, 0))],
        out_shape=[jax.ShapeDtypeStruct((t, D_MODEL), F32),
                   jax.ShapeDtypeStruct((MOE_TOPK, t), F32),
                   jax.ShapeDtypeStruct((MOE_TOPK, t), I32),
                   jax.ShapeDtypeStruct((t // MOE_TILE, SUBLANES, LANES), I32),
                   jax.ShapeDtypeStruct((N_EXPERTS, LANES), F32)],
        scratch_shapes=[pltpu.VMEM((N_EXPERTS, 1), F32)],
        compiler_params=_cparams("arbitrary"),
        name="merge",
    )(x2d, oa, ob, oc, wg, gb, wa, wb, wc, wo, lg, lb, wrh, wrl, rb)


def _route_tile(lgt, run):
    tn = lgt.shape[1]
    le = lgt[0:N_EXPERTS, :]
    lg = lgt[N_EXPERTS:N_EXPERTS + MOE_GROUPS, :]
    gi = lax.broadcasted_iota(I32, (MOE_GROUPS, tn), 0).astype(F32)
    ei = lax.broadcasted_iota(I32, (N_EXPERTS, tn), 0).astype(F32)
    eg = jnp.right_shift(lax.broadcasted_iota(I32, (N_EXPERTS, tn), 0),
                         MOE_PER_GROUP.bit_length() - 1).astype(F32)
    mg = jnp.max(lg, axis=0, keepdims=True)
    gsel = jnp.min(jnp.where(lg == mg, gi, float(MOE_GROUPS)), axis=0, keepdims=True)
    p_top = 1.0 / jnp.sum(jnp.exp(lg - mg), axis=0, keepdims=True)
    l1 = jnp.where(eg == gsel, le, -jnp.inf)
    m1 = jnp.max(l1, axis=0, keepdims=True)
    i1 = jnp.min(jnp.where(l1 == m1, ei, float(N_EXPERTS)), axis=0, keepdims=True)
    l2 = jnp.where(ei == i1, -jnp.inf, l1)
    m2 = jnp.max(l2, axis=0, keepdims=True)
    i2 = jnp.min(jnp.where(l2 == m2, ei, float(N_EXPERTS)), axis=0, keepdims=True)
    e2 = jnp.exp(m2 - m1)
    den = 1.0 + e2
    wts = jnp.concatenate([p_top / den, p_top * (e2 / den)], axis=0)

    before = (lax.broadcasted_iota(I32, (tn, tn), 0) < lax.broadcasted_iota(I32, (tn, tn), 1)).astype(BF)
    oh = [ei == ik for ik in (i1, i2)]
    prefix = [jnp.dot(o.astype(BF), before, preferred_element_type=F32) for o in oh]
    cnt_k = [jnp.sum(o.astype(F32), axis=1, keepdims=True) for o in oh]
    cnt = cnt_k[0] + cnt_k[1]
    cnt_al = jnp.floor((cnt + (RUN_ALIGN - 1)) * (1.0 / RUN_ALIGN)) * RUN_ALIGN
    er = lax.broadcasted_iota(I32, (N_EXPERTS, N_EXPERTS), 0)
    ec = lax.broadcasted_iota(I32, (N_EXPERTS, N_EXPERTS), 1)
    lstart = jnp.dot((ec < er).astype(BF), jnp.broadcast_to(cnt_al, (N_EXPERTS, LANES)).astype(BF),
                     preferred_element_type=F32)[:, 0:1]
    first = [0.0, cnt_k[0]]
    pick = lambda k: jnp.sum(jnp.where(oh[k], prefix[k] + first[k] + lstart, 0.0), axis=0, keepdims=True)
    lpos = jnp.concatenate([pick(0), pick(1)], axis=0).astype(I32)
    eye_l = (lax.broadcasted_iota(I32, (N_EXPERTS, LANES), 0) == lax.broadcasted_iota(I32, (N_EXPERTS, LANES), 1))
    as_row = lambda colv: jnp.sum(jnp.where(eye_l, colv, 0.0), axis=0, keepdims=True)
    pieces = jnp.broadcast_to(jnp.sum(cnt_al, axis=0, keepdims=True) * (1.0 / RUN_ALIGN), (1, LANES))
    tab = jnp.concatenate([as_row(run), as_row(cnt), as_row(lstart), pieces,
                           jnp.zeros((SUBLANES - 4, LANES), F32)], axis=0).astype(I32)
    return wts, lpos, tab, run + cnt_al


def _layout_kernel(cnt_ref, tab_ref, gtab_ref, ttab_ref, bexp_ref, nused_ref, *, n_blocks):
    cnt = cnt_ref[...].astype(I32)
    shift = EXPERT_BLOCK.bit_length() - 1
    padded_blocks = jnp.right_shift(cnt + (EXPERT_BLOCK - 1), shift)
    er = lax.broadcasted_iota(I32, (N_EXPERTS, LANES), 0)
    el = lax.broadcasted_iota(I32, (N_EXPERTS, LANES), 1)
    start = jnp.zeros((N_EXPERTS, LANES), I32)
    for e in range(N_EXPERTS - 1):
        start = start + jnp.where(er > e, padded_blocks[e:e + 1, :], 0)
    end = start + padded_blocks
    as_row = lambda m: jnp.sum(jnp.where(er == el, m, 0), axis=0, keepdims=True)
    tab = tab_ref[...]
    first_row = lax.broadcasted_iota(I32, tab.shape, 1) == 0
    gtab_ref[...] = tab + jnp.where(first_row, as_row(start * EXPERT_BLOCK)[None], 0)
    tail_rows = padded_blocks * EXPERT_BLOCK - cnt
    end_all = end[N_EXPERTS - 1:N_EXPERTS, :]
    spare = lax.broadcasted_iota(I32, (1, LANES), 1) == N_EXPERTS
    ttab_ref[...] = jnp.concatenate(
        [jnp.where(spare, end_all * EXPERT_BLOCK, as_row(start * EXPERT_BLOCK + cnt)),
         jnp.where(spare, n_blocks - end_all, as_row(jnp.right_shift(tail_rows, RUN_ALIGN.bit_length() - 1))),
         jnp.zeros((SUBLANES - 2, LANES), I32)], axis=0)[None]
    blk = lax.broadcasted_iota(I32, (1, n_blocks), 1)
    owner = jnp.zeros((1, n_blocks), I32)
    for e in range(N_EXPERTS):
        owner = owner + (blk >= end[e:e + 1, 0:1]).astype(I32)
    bexp_ref[...] = jnp.minimum(owner, N_EXPERTS - 1)
    nused_ref[...] = end[N_EXPERTS - 1:N_EXPERTS, :]


def _layout(cnt, tab, *, n_blocks):
    full = lambda shape: pl.BlockSpec(shape, lambda i: (0,) * len(shape))
    return pl.pallas_call(
        functools.partial(_layout_kernel, n_blocks=n_blocks),
        grid=(1,),
        in_specs=[full(cnt.shape), full(tab.shape)],
        out_specs=[full(tab.shape), full((1, SUBLANES, LANES)), full((1, n_blocks)), full((1, LANES))],
        out_shape=[jax.ShapeDtypeStruct(tab.shape, I32),
                   jax.ShapeDtypeStruct((1, SUBLANES, LANES), I32),
                   jax.ShapeDtypeStruct((1, n_blocks), I32),
                   jax.ShapeDtypeStruct((1, LANES), I32)],
        compiler_params=_cparams("arbitrary"),
        name="layout",
    )(cnt, tab)


def _run_pieces(tab_ref, fn):
    for e in range(N_EXPERTS):
        sorted0, staging0 = tab_ref[0, 0, e], tab_ref[0, 2, e]
        n_pieces = (tab_ref[0, 1, e] + (RUN_ALIGN - 1)) // RUN_ALIGN

        def one(j, c, sorted0=sorted0, staging0=staging0):
            fn(pl.multiple_of(sorted0 + j * RUN_ALIGN, RUN_ALIGN), pl.multiple_of(staging0 + j * RUN_ALIGN, RUN_ALIGN))
            return c

        lax.fori_loop(0, n_pieces, one, 0)


WAIT_GROUPS = (64, 8, 1)


def _wait_pieces(tab_ref, make_copy):
    total = tab_ref[0, 3, 0]
    counts = (total // WAIT_GROUPS[0], (total // WAIT_GROUPS[1]) % (WAIT_GROUPS[0] // WAIT_GROUPS[1]),
              total % WAIT_GROUPS[1])
    for group, count in zip(WAIT_GROUPS, counts):
        def one(j, c, group=group):
            make_copy(group * RUN_ALIGN).wait()
            return c
        lax.fori_loop(0, count, one, 0)


def _dispatch_kernel(tab_ref, ptab_ref, ttab_ref, lpos_ref, x_ref, xs_ref, stage_ref, zero_ref, sems, tsem,
                     *, td):
    i = pl.program_id(0)
    last = pl.num_programs(0) - 1
    slot = lax.rem(i, 2)
    rows = stage_ref.shape[1]
    r = lax.broadcasted_iota(I32, (rows, td), 0)
    sel = jnp.logical_or(r == lpos_ref[0:1, :], r == lpos_ref[1:2, :]).astype(BF)
    stage_ref[slot] = jnp.dot(sel, x_ref[...].astype(BF), preferred_element_type=F32)

    def run_copy(sl, dst, src, rows=RUN_ALIGN):
        return pltpu.make_async_copy(stage_ref.at[sl, pl.ds(src, rows), :],
                                     xs_ref.at[pl.ds(dst, rows), :], sems.at[sl])

    def tail_copies(fn, first, stride):
        def per_expert(k, carry):
            e = first + k * stride

            def one(j, c):
                dst = pl.multiple_of(ttab_ref[0, 0, e] + j * RUN_ALIGN, RUN_ALIGN)
                fn(pltpu.make_async_copy(zero_ref.at[pl.ds(0, RUN_ALIGN), :],
                                         xs_ref.at[pl.ds(dst, RUN_ALIGN), :], tsem))
                return c
            lax.fori_loop(0, ttab_ref[0, 1, e], one, 0)
            return carry
        lax.fori_loop(0, (jnp.maximum(N_EXPERTS - first, 0) + stride - 1) // stride, per_expert, 0)

    def spare_copies(fn, first, stride):
        n_spare = ttab_ref[0, 1, N_EXPERTS]

        def one(j, c):
            dst = pl.multiple_of(ttab_ref[0, 0, N_EXPERTS] + (first + j * stride) * EXPERT_BLOCK, EXPERT_BLOCK)
            fn(pltpu.make_async_copy(zero_ref, xs_ref.at[pl.ds(dst, EXPERT_BLOCK), :], tsem))
            return c
        lax.fori_loop(0, (jnp.maximum(n_spare - first, 0) + stride - 1) // stride, one, 0)

    @pl.when(i == 0)
    def _():
        zero_ref[...] = jnp.zeros_like(zero_ref)

    @pl.when(i > 0)
    def _():
        _wait_pieces(ptab_ref, lambda rows: run_copy(1 - slot, 0, 0, rows))

    _run_pieces(tab_ref, lambda dst, src: run_copy(slot, dst, src).start())
    tail_copies(lambda cp: cp.start(), i, last + 1)
    spare_copies(lambda cp: cp.start(), i, last + 1)

    @pl.when(i == last)
    def _():
        _wait_pieces(tab_ref, lambda rows: run_copy(slot, 0, 0, rows))
        tail_copies(lambda cp: cp.wait(), 0, 1)
        spare_copies(lambda cp: cp.wait(), 0, 1)


def _stage_rows(td):
    return MOE_TOPK * td + N_EXPERTS * RUN_ALIGN


def _dispatch(gtab, ttab, lpos, x1, *, td, cap):
    t = x1.shape[0]
    smem_tab = lambda imap: pl.BlockSpec((1, SUBLANES, LANES), imap, memory_space=pltpu.SMEM)
    return pl.pallas_call(
        functools.partial(_dispatch_kernel, td=td),
        grid=(t // td,),
        in_specs=[smem_tab(lambda i: (i, 0, 0)), smem_tab(lambda i: (jnp.maximum(i - 1, 0), 0, 0)),
                  smem_tab(lambda i: (0, 0, 0)),
                  pl.BlockSpec((MOE_TOPK, td), lambda i: (0, i)),
                  pl.BlockSpec((td, D_MODEL), lambda i: (i, 0))],
        out_specs=pl.BlockSpec(memory_space=pl.ANY),
        out_shape=jax.ShapeDtypeStruct((cap, D_MODEL), F32),
        scratch_shapes=[pltpu.VMEM((2, _stage_rows(td), D_MODEL), F32), pltpu.VMEM((EXPERT_BLOCK, D_MODEL), F32),
                        pltpu.SemaphoreType.DMA((2,)), pltpu.SemaphoreType.DMA(())],
        compiler_params=_cparams("arbitrary"),
        name="dispatch",
    )(gtab, gtab, ttab, lpos, x1)


def _expert_kernel(bexp_ref, nused_ref, xs_ref, w1_ref, w3_ref, w2_ref, y_ref):
    del bexp_ref
    i = pl.program_id(0)

    @pl.when(i < nused_ref[0])
    def _():
        xb = xs_ref[...].astype(BF)
        h1 = jnp.dot(xb, w1_ref[0], preferred_element_type=F32)
        h3 = jnp.dot(xb, w3_ref[0], preferred_element_type=F32)
        hid = (_silu(h1) * h3).astype(BF)
        y_ref[...] = jnp.dot(hid, w2_ref[0], preferred_element_type=F32)

    @pl.when(i >= nused_ref[0])
    def _():
        y_ref[...] = jnp.zeros_like(y_ref)


def _expert(bexp, nused, xs, w1, w3, w2):
    cap = xs.shape[0]
    nb = cap // EXPERT_BLOCK
    last = lambda i, nu: jnp.minimum(i, nu[0] - 1)
    grid_spec = pltpu.PrefetchScalarGridSpec(
        num_scalar_prefetch=2,
        grid=(nb,),
        in_specs=[pl.BlockSpec((EXPERT_BLOCK, D_MODEL), lambda i, be, nu: (last(i, nu), 0)),
                  pl.BlockSpec((1, D_MODEL, MOE_FF), lambda i, be, nu: (be[last(i, nu)], 0, 0)),
                  pl.BlockSpec((1, D_MODEL, MOE_FF), lambda i, be, nu: (be[last(i, nu)], 0, 0)),
                  pl.BlockSpec((1, MOE_FF, D_MODEL), lambda i, be, nu: (be[last(i, nu)], 0, 0))],
        out_specs=pl.BlockSpec((EXPERT_BLOCK, D_MODEL), lambda i, be, nu: (i, 0)),
    )
    return pl.pallas_call(
        _expert_kernel,
        grid_spec=grid_spec,
        out_shape=jax.ShapeDtypeStruct((cap, D_MODEL), F32),
        compiler_params=_cparams("arbitrary"),
        name="expert",
    )(bexp, nused, xs, w1, w3, w2)


def _combine_kernel(tab_ref, ntab_ref, lpos_ref, lposr_ref, wr_ref, x_ref, lg_ref, lb_ref, yb_ref, o_ref, stage_ref,
                    sems,
                    *, tc):
    i = pl.program_id(0)
    last = pl.num_programs(0) - 1
    slot = lax.rem(i, 2)

    def run_copy(sl, src, dst, rows=RUN_ALIGN):
        return pltpu.make_async_copy(yb_ref.at[pl.ds(src, rows), :],
                                     stage_ref.at[sl, pl.ds(dst, rows), :], sems.at[sl])

    @pl.when(i == 0)
    def _():
        stage_ref[...] = jnp.zeros_like(stage_ref)
        _run_pieces(tab_ref, lambda src, dst: run_copy(slot, src, dst).start())

    @pl.when(i < last)
    def _():
        _run_pieces(ntab_ref, lambda src, dst: run_copy(1 - slot, src, dst).start())

    _wait_pieces(tab_ref, lambda rows: run_copy(slot, 0, 0, rows))
    rows = stage_ref.shape[1]
    r = lax.broadcasted_iota(I32, (rows, tc), 0)
    row_w = jnp.sum(jnp.where(r == lposr_ref[0:1, :], wr_ref[0:1, :], 0.0)
                    + jnp.where(r == lposr_ref[1:2, :], wr_ref[1:2, :], 0.0), axis=1, keepdims=True)
    staged = (stage_ref[slot] * row_w).astype(BF)
    c = lax.broadcasted_iota(I32, (tc, rows), 1)
    pick = jnp.logical_or(c == lpos_ref[:, 0:1], c == lpos_ref[:, 1:2]).astype(BF)
    ffn = jnp.dot(pick, staged, preferred_element_type=F32)
    o_ref[...] = _layer_norm(DN_ALPHA * x_ref[...] + ffn, lg_ref[...], lb_ref[...])


def _combine(gtab, lpos, wts, x1, lg, lb, yb, *, tc):
    t = x1.shape[0]
    n_tiles = t // tc
    full = lambda a: pl.BlockSpec(a.shape, lambda i: (0,) * a.ndim)
    smem_tab = lambda imap: pl.BlockSpec((1, SUBLANES, LANES), imap, memory_space=pltpu.SMEM)
    return pl.pallas_call(
        functools.partial(_combine_kernel, tc=tc),
        grid=(n_tiles,),
        in_specs=[smem_tab(lambda i: (i, 0, 0)), smem_tab(lambda i: (jnp.minimum(i + 1, n_tiles - 1), 0, 0)),
                  pl.BlockSpec((tc, MOE_TOPK), lambda i: (i, 0)),
                  pl.BlockSpec((MOE_TOPK, tc), lambda i: (0, i)),
                  pl.BlockSpec((MOE_TOPK, tc), lambda i: (0, i)),
                  pl.BlockSpec((tc, D_MODEL), lambda i: (i, 0)),
                  full(lg), full(lb),
                  pl.BlockSpec(memory_space=pl.ANY)],
        out_specs=pl.BlockSpec((tc, D_MODEL), lambda i: (i, 0)),
        out_shape=jax.ShapeDtypeStruct((t, D_MODEL), F32),
        scratch_shapes=[pltpu.VMEM((2, _stage_rows(tc), D_MODEL), F32), pltpu.SemaphoreType.DMA((2,))],
        compiler_params=_cparams("arbitrary"),
        name="combine",
    )(gtab, gtab, lpos.T, lpos, wts, x1, lg, lb, yb)


def _tile(n, want):
    while n % want:
        want //= 2
    return want


def _mixer(x2d, b, l, w_in, rel_bias, conv_w, a_log, dt_bias, norm_g, pool_w, pool_scale,
           w_br_a, w_br_b, w_br_c, gate_b, w_out, ln_g, ln_b, wr, rb):
    t = x2d.shape[0]
    c0 = 3 * A_WIDTH
    c1 = c0 + 3 * B_WIDTH
    c2 = c1 + 2 * B_HEADS
    c3 = c2 + B_WIDTH
    c4 = c3 + C_WIDTH
    wbf = w_in.astype(BF)
    wbd = jnp.pad(wbf[:, c1:c2], ((0, 0), (0, LANES - 2 * B_HEADS)))
    qkva, qkvb, bd, go, oc = _proj(
        x2d, wbf[:, :c0], wbf[:, c0:c1], wbd, wbf[:, c2:c3], wbf[:, c3:c4],
        pool_w.astype(BF), pool_scale.reshape(1, C_WIDTH), seq=l, tm=_tile(l, DENSE_TILE))

    oa = _attn(qkva.reshape(b, l, 3 * A_WIDTH), _attn_bias_table(rel_bias)).reshape(t, A_WIDTH)

    lane_pad = lambda v: jnp.pad(v.reshape(1, B_HEADS), ((0, 0), (B_HEADS, LANES - 2 * B_HEADS)))
    ob = _gdn(qkvb.reshape(b, l, 3 * B_WIDTH), bd.reshape(b, l, LANES), go.reshape(b, l, B_WIDTH),
              conv_w, lane_pad(a_log), lane_pad(dt_bias), norm_g.reshape(1, B_HEAD_DIM)).reshape(t, B_WIDTH)

    wrh, wrl = _split_bf16(wr)
    return _merge(x2d, oa, ob, oc, wbf[:, c4:], gate_b, w_br_a.astype(BF), w_br_b.astype(BF),
                  w_br_c.astype(BF), w_out.astype(BF), ln_g.reshape(1, -1), ln_b.reshape(1, -1),
                  wrh, wrl, rb, tm=_tile(t, DENSE_TILE))


def _moe(x1, wts, lpos, tab, cnt, w1, w3, w2, ln_g, ln_b):
    t = x1.shape[0]
    tile = MOE_TILE
    max_rows = t * MOE_TOPK + (t // tile) * N_EXPERTS * (RUN_ALIGN - 1)
    n_blocks = -(-max_rows // EXPERT_BLOCK) + N_EXPERTS
    cap = n_blocks * EXPERT_BLOCK
    gtab, ttab, bexp, nused = _layout(cnt, tab, n_blocks=n_blocks)
    xs = _dispatch(gtab, ttab, lpos, x1, td=tile, cap=cap)
    yb = _expert(bexp.reshape(n_blocks), nused[0, :1], xs, w1.astype(BF), w3.astype(BF), w2.astype(BF))
    return _combine(gtab, lpos, wts, x1, ln_g.reshape(1, -1), ln_b.reshape(1, -1), yb, tc=tile)


def kernel(x, w_in, attn_rel_bias, gdn_conv_w, gdn_a_log, gdn_dt_bias, gdn_norm_g, pool_w, pool_scale,
           w_branch_a, w_branch_b, w_branch_c, gate_b, w_out, ln1_g, ln1_b, router_group_w,
           router_group_b, router_expert_w, router_expert_b, moe_w1, moe_w3, moe_w2, ln2_g, ln2_b):
    b, l, d = x.shape
    x2d = x.reshape(b * l, d)
    for i in range(DEPTH):
        wr = jnp.concatenate([router_expert_w[i].T, router_group_w[i].T,
                              jnp.zeros((ROUTER_ROWS - N_EXPERTS - MOE_GROUPS, d), F32)], axis=0)
        rb = jnp.concatenate([router_expert_b[i], router_group_b[i],
                              jnp.zeros((ROUTER_ROWS - N_EXPERTS - MOE_GROUPS,), F32)]).reshape(ROUTER_ROWS, 1)
        x1, *routing = _mixer(x2d, b, l, w_in[i], attn_rel_bias[i], gdn_conv_w[i], gdn_a_log[i], gdn_dt_bias[i],
                         gdn_norm_g[i], pool_w[i], pool_scale[i], w_branch_a[i], w_branch_b[i],
                         w_branch_c[i], gate_b[i], w_out[i], ln1_g[i], ln1_b[i], wr, rb)
        x2d = _moe(x1, *routing, moe_w1[i], moe_w3[i], moe_w2[i], ln2_g[i], ln2_b[i])
    return x2d.reshape(b, l, d)
```
